```python
import math
import jax
import jax.numpy as jnp
from jax import lax
import numpy as np

D_MODEL = 1024
BATCH = 16
SEQ = 2048
DEPTH = 1

ATT_PATTERNS = ((128, 1), (512, 4), (2048, 16))
N_ATT_GROUPS = 3
HEADS_PER_GROUP = 4
ATT_HEADS = N_ATT_GROUPS * HEADS_PER_GROUP
HEAD_DIM = 64
ATT_WIDTH = ATT_HEADS * HEAD_DIM
ATT_OUT_WIDTH = HEADS_PER_GROUP * HEAD_DIM
REL_BUCKETS = 32
REL_MAX_DIST = 2048
RWKV_HEADS = 8
RWKV_HEAD_DIM = 64
RWKV_WIDTH = RWKV_HEADS * RWKV_HEAD_DIM
DECAY_LORA = 64
ICLR_LORA = 64
GATE_LORA = 160
RWKV_SHIFT_WIDTH = 3 * RWKV_WIDTH + DECAY_LORA + ICLR_LORA + GATE_LORA
GN_EPS = 64e-5
N_BRANCHES = 2
IN_WIDTH = 3 * ATT_WIDTH + RWKV_SHIFT_WIDTH + N_BRANCHES * D_MODEL
N_EXPERT_GROUPS = 4
EXPERTS_PER_GROUP = 8
N_EXPERTS = N_EXPERT_GROUPS * EXPERTS_PER_GROUP
TOP_K = 2
D_EXPERT = 512
ROW_BLOCK = 256
NORM_EPS = 1e-6
NEG_INF = -1e30

kernel_name = 'hybrid_dilated_rwkv7_hmoe_block'


def rms_norm(x, gain, eps=NORM_EPS):
    xf = x.astype(jnp.float32)
    y = xf * lax.rsqrt(jnp.mean(xf * xf, axis=-1, keepdims=True) + eps)
    return (y * gain.astype(jnp.float32)).astype(x.dtype)


def t5_causal_bucket(dist):
    max_exact = REL_BUCKETS // 2
    d = jnp.maximum(dist.astype(jnp.float32), 1.0)
    large = max_exact + (jnp.log(d / max_exact) / math.log(REL_MAX_DIST / max_exact)
                         * (REL_BUCKETS - max_exact)).astype(jnp.int32)
    large = jnp.minimum(large, REL_BUCKETS - 1)
    return jnp.where(dist < max_exact, dist, large)


def dilated_window_attention(q, k, v, bias_table, window, dilation):
    B, S, H, E = q.shape
    L = S // dilation
    W = window // dilation
    n_blk = -(-L // W)
    Lp = n_blk * W

    def to_sub(t):
        return t.reshape(B, L, dilation, H, E).transpose(0, 2, 3, 1, 4)

    qb = jnp.pad(to_sub(q), ((0, 0),) * 3 + ((0, Lp - L), (0, 0))).reshape(B, dilation, H, n_blk, W, E)
    kv_pad = ((0, 0),) * 3 + ((W, Lp - L), (0, 0))

    def band(t):
        t = jnp.pad(to_sub(t), kv_pad)
        prev = t[:, :, :, :Lp].reshape(B, dilation, H, n_blk, W, E)
        cur = t[:, :, :, W:].reshape(B, dilation, H, n_blk, W, E)
        return jnp.concatenate([prev, cur], axis=4)

    kb, vb = band(k), band(v)
    qi = jnp.arange(W)[:, None]
    kj = jnp.arange(2 * W)[None, :]
    rel = qi + W - kj
    key_pos = (jnp.arange(n_blk) * W - W)[:, None, None] + kj[None]
    valid = (rel >= 0)[None] & (rel <= W)[None] & (key_pos >= 0)
    bias = bias_table[t5_causal_bucket(jnp.clip(rel, 0, W) * dilation)]
    bias = jnp.transpose(bias, (2, 0, 1)).astype(jnp.float32)[:, None]
    s = jnp.einsum('bdhnqe,bdhnke->bdhnqk', qb, kb).astype(jnp.float32) * (E ** -0.5) + bias
    s = jnp.where(valid, s, NEG_INF)
    lse = jax.nn.logsumexp(s, axis=-1)
    p = jnp.exp(s - lse[..., None])
    o = jnp.einsum('bdhnqk,bdhnke->bdhnqe', p.astype(v.dtype), vb)

    def from_sub(t):
        t = t.reshape((B, dilation, H, Lp) + t.shape[5:])[:, :, :, :L]
        t = jnp.moveaxis(t, 3, 1)
        return t.reshape((B, S, H) + t.shape[4:])

    return from_sub(o), from_sub(lse)


def wkv7_scan(r, decay, k, v, kk, a):
    B, S, H, N = r.shape

    def step(state, inp):
        r_t, w_t, k_t, v_t, kk_t, b_t = inp
        sa = jnp.einsum('bhvk,bhk->bhv', state, -kk_t)
        state = (state * w_t[:, :, None, :] + sa[..., None] * b_t[:, :, None, :]
                 + v_t[..., None] * k_t[:, :, None, :])
        return state, jnp.einsum('bhvk,bhk->bhv', state, r_t)

    xs = tuple(jnp.moveaxis(t, 1, 0) for t in (r, decay, k, v, kk, kk * a))
    _, out = lax.scan(step, jnp.zeros((B, H, N, N), jnp.float32), xs)
    return jnp.moveaxis(out, 0, 1)


def rwkv7_time_mix(p, shift_mu, w0, w_up, a0, a_up, g_up, k_k, k_a, r_k, ln_w, ln_b, w_rwkv_out):
    B, S, _ = p.shape
    H, N, C = RWKV_HEADS, RWKV_HEAD_DIM, RWKV_WIDTH
    pf = p.astype(jnp.float32)
    prev = jnp.pad(pf, ((0, 0), (1, 0), (0, 0)))[:, :-1]
    pm = pf + (prev - pf) * shift_mu
    r, k, v, xw, xa, xg = jnp.split(pm, [C, 2 * C, 3 * C, 3 * C + DECAY_LORA,
                                          3 * C + DECAY_LORA + ICLR_LORA], axis=-1)
    w_log = -jax.nn.softplus(-(w0 + jnp.tanh(xw) @ w_up)) - 0.5
    decay = jnp.exp(-jnp.exp(w_log))
    a = jax.nn.sigmoid(a0 + xa @ a_up)
    g = jax.nn.sigmoid(xg) @ g_up
    hd = lambda t: t.reshape(B, S, H, N)
    kk = hd(k * k_k)
    kk = kk / jnp.maximum(jnp.sqrt(jnp.sum(kk * kk, axis=-1, keepdims=True)), 1e-12)
    k = k * (1.0 + (a - 1.0) * k_a)
    o = wkv7_scan(hd(r), hd(decay), hd(k), hd(v), kk, hd(a))
    mu = jnp.mean(o, axis=-1, keepdims=True)
    var = jnp.mean(jnp.square(o - mu), axis=-1, keepdims=True)
    o = ((o - mu) * lax.rsqrt(var + GN_EPS)).reshape(B, S, C) * ln_w + ln_b
    bonus = jnp.sum(hd(r) * hd(k) * r_k, axis=-1, keepdims=True) * hd(v)
    o = (o + bonus.reshape(B, S, C)) * g
    return o.astype(p.dtype) @ w_rwkv_out


def hierarchical_moe(h, w_group_router, b_group_router, w_expert_router, b_expert_router,
                     w_expert_gate, w_expert_up, w_expert_down):
    B, S, D = h.shape
    T = B * S
    hf = h.reshape(T, D)
    grp_prob = jax.nn.softmax((hf @ w_group_router).astype(jnp.float32) + b_group_router, axis=-1)
    grp_p, grp_idx = lax.top_k(grp_prob, 1)
    exp_logits = ((hf @ w_expert_router).astype(jnp.float32) + b_expert_router).reshape(
        T, N_EXPERT_GROUPS, EXPERTS_PER_GROUP)
    sel_logits = jnp.take_along_axis(exp_logits, grp_idx[:, :, None], axis=1)[:, 0]
    top_p, top_i = lax.top_k(jax.nn.softmax(sel_logits, axis=-1), TOP_K)
    gates = grp_p * top_p / jnp.sum(top_p, axis=-1, keepdims=True)
    expert_id = grp_idx * EXPERTS_PER_GROUP + top_i

    A = T * TOP_K
    eid = expert_id.reshape(A).astype(jnp.int32)
    tok = jnp.arange(A, dtype=jnp.int32) // TOP_K
    gate = gates.reshape(A)
    order = jnp.argsort(eid)
    se = eid[order]
    counts = jnp.bincount(eid, length=N_EXPERTS)
    starts = jnp.cumsum(counts) - counts
    pcounts = (counts + ROW_BLOCK - 1) // ROW_BLOCK * ROW_BLOCK
    pends = jnp.cumsum(pcounts)
    pstarts = pends - pcounts
    dest = pstarts[se] + jnp.arange(A, dtype=jnp.int32) - starts[se]
    n_blocks = -(-A // ROW_BLOCK) + N_EXPERTS
    n_rows = n_blocks * ROW_BLOCK
    row_tok = jnp.full((n_rows,), T, jnp.int32).at[dest].set(tok[order])
    row_gate = jnp.zeros((n_rows,), jnp.float32).at[dest].set(gate[order])
    blk_expert = jnp.minimum(jnp.searchsorted(pends, jnp.arange(n_blocks) * ROW_BLOCK, side='right'),
                             N_EXPERTS - 1)
    h_pad = jnp.concatenate([hf, jnp.zeros((1, D), hf.dtype)], axis=0)
    xin = h_pad[row_tok].reshape(n_blocks, ROW_BLOCK, D)

    def expert_block(args):
        xb, e = args
        hid = jax.nn.silu(xb @ w_expert_gate[e]) * (xb @ w_expert_up[e])
        return hid @ w_expert_down[e]

    yb = lax.map(expert_block, (xin, blk_expert)).reshape(n_rows, D)
    y = jax.ops.segment_sum(yb.astype(jnp.float32) * row_gate[:, None], row_tok, num_segments=T + 1)[:T]
    return y.reshape(B, S, D).astype(h.dtype)


def hybrid_layer(x, norm1_gain, w_in, q_norm_gain, k_norm_gain, rel_bias_table, w_att_out,
                 shift_mu, w0, w_up, a0, a_up, g_up, k_k, k_a, r_k, ln_w, ln_b, w_rwkv_out,
                 w_out, norm2_gain, w_group_router, b_group_router, w_expert_router,
                 b_expert_router, w_expert_gate, w_expert_up, w_expert_down):
    B, S, D = x.shape
    h = rms_norm(x, norm1_gain)
    p = h @ w_in
    q, k, v, p_rwkv, p_gate = jnp.split(
        p, [ATT_WIDTH, 2 * ATT_WIDTH, 3 * ATT_WIDTH, 3 * ATT_WIDTH + RWKV_SHIFT_WIDTH], axis=-1)

    heads = lambda t: t.reshape(B, S, N_ATT_GROUPS, HEADS_PER_GROUP, HEAD_DIM)
    q = rms_norm(heads(q), q_norm_gain[:, None, :])
    k = rms_norm(heads(k), k_norm_gain[:, None, :])
    v = heads(v)
    outs, lses = [], []
    for g, (win, dil) in enumerate(ATT_PATTERNS):
        o_g, l_g = dilated_window_attention(
            q[:, :, g], k[:, :, g], v[:, :, g],
            rel_bias_table[:, g * HEADS_PER_GROUP:(g + 1) * HEADS_PER_GROUP], win, dil)
        outs.append(o_g)
        lses.append(l_g)
    wts = jax.nn.softmax(jnp.stack(lses, axis=0), axis=0)
    att = jnp.einsum('gbsh,gbshe->bshe', wts, jnp.stack(outs, axis=0).astype(jnp.float32))
    y_att = att.reshape(B, S, ATT_OUT_WIDTH).astype(x.dtype) @ w_att_out

    y_rwkv = rwkv7_time_mix(p_rwkv, shift_mu, w0, w_up, a0, a_up, g_up, k_k, k_a, r_k,
                            ln_w, ln_b, w_rwkv_out)

    gates = jax.nn.sigmoid(p_gate.astype(jnp.float32)).astype(x.dtype)
    mixed = gates[..., :D] * y_att + gates[..., D:] * y_rwkv
    x = x + mixed @ w_out

    x = x + hierarchical_moe(rms_norm(x, norm2_gain), w_group_router, b_group_router,
                             w_expert_router, b_expert_router, w_expert_gate, w_expert_up,
                             w_expert_down)
    return x


def setup_inputs(seed: int = 0) -> dict:
    key = jax.random.key(seed)
    ks = jax.random.split(key, 28)
    f32 = jnp.float32
    L, D = DEPTH, D_MODEL
    nrm = lambda k, shape, scale: scale * jax.random.normal(k, shape, f32)
    return {
        'x': nrm(ks[0], (BATCH, SEQ, D), 1.0),
        'norm1_gain': 1.0 + nrm(ks[1], (L, D), 0.05),
        'w_in': nrm(ks[2], (L, D, IN_WIDTH), D ** -0.5),
        'q_norm_gain': 1.0 + nrm(ks[3], (L, N_ATT_GROUPS, HEAD_DIM), 0.05),
        'k_norm_gain': 1.0 + nrm(ks[4], (L, N_ATT_GROUPS, HEAD_DIM), 0.05),
        'rel_bias_table': nrm(ks[5], (REL_BUCKETS, ATT_HEADS), 0.5),
        'w_att_out': nrm(ks[6], (L, ATT_OUT_WIDTH, D), ATT_OUT_WIDTH ** -0.5),
        'rwkv_shift_mu': jax.random.uniform(ks[7], (L, RWKV_SHIFT_WIDTH), f32),
        'rwkv_w0': jax.random.uniform(ks[8], (L, RWKV_WIDTH), f32, -6.0, 1.0),
        'rwkv_w_up': nrm(ks[9], (L, DECAY_LORA, RWKV_WIDTH), 0.5 * DECAY_LORA ** -0.5),
        'rwkv_a0': nrm(ks[10], (L, RWKV_WIDTH), 0.1),
        'rwkv_a_up': nrm(ks[11], (L, ICLR_LORA, RWKV_WIDTH), ICLR_LORA ** -0.5),
        'rwkv_g_up': nrm(ks[12], (L, GATE_LORA, RWKV_WIDTH), GATE_LORA ** -0.5),
        'rwkv_k_k': 0.85 + nrm(ks[13], (L, RWKV_WIDTH), 0.05),
        'rwkv_k_a': 1.0 + nrm(ks[14], (L, RWKV_WIDTH), 0.05),
        'rwkv_r_k': nrm(ks[15], (L, RWKV_HEADS, RWKV_HEAD_DIM), 0.1),
        'rwkv_ln_w': 1.0 + nrm(ks[16], (L, RWKV_WIDTH), 0.05),
        'rwkv_ln_b': nrm(ks[17], (L, RWKV_WIDTH), 0.01),
        'w_rwkv_out': nrm(ks[18], (L, RWKV_WIDTH, D), RWKV_WIDTH ** -0.5),
        'w_out': nrm(ks[19], (L, D, D), D ** -0.5),
        'norm2_gain': 1.0 + nrm(ks[20], (L, D), 0.05),
        'w_group_router': nrm(ks[21], (L, D, N_EXPERT_GROUPS), D ** -0.5),
        'b_group_router': nrm(ks[22], (L, N_EXPERT_GROUPS), 0.01),
        'w_expert_router': nrm(ks[23], (L, D, N_EXPERTS), D ** -0.5),
        'b_expert_router': nrm(ks[24], (L, N_EXPERTS), 0.01),
        'w_expert_gate': nrm(ks[25], (L, N_EXPERTS, D, D_EXPERT), D ** -0.5),
        'w_expert_up': nrm(ks[26], (L, N_EXPERTS, D, D_EXPERT), D ** -0.5),
        'w_expert_down': nrm(ks[27], (L, N_EXPERTS, D_EXPERT, D), D_EXPERT ** -0.5),
    }


def reference(x, norm1_gain, w_in, q_norm_gain, k_norm_gain, rel_bias_table, w_att_out,
              rwkv_shift_mu, rwkv_w0, rwkv_w_up, rwkv_a0, rwkv_a_up, rwkv_g_up, rwkv_k_k,
              rwkv_k_a, rwkv_r_k, rwkv_ln_w, rwkv_ln_b, w_rwkv_out, w_out, norm2_gain,
              w_group_router, b_group_router, w_expert_router, b_expert_router,
              w_expert_gate, w_expert_up, w_expert_down):
    for l in range(DEPTH):
        x = hybrid_layer(x, norm1_gain[l], w_in[l], q_norm_gain[l], k_norm_gain[l], rel_bias_table,
                         w_att_out[l], rwkv_shift_mu[l], rwkv_w0[l], rwkv_w_up[l], rwkv_a0[l],
                         rwkv_a_up[l], rwkv_g_up[l], rwkv_k_k[l], rwkv_k_a[l], rwkv_r_k[l],
                         rwkv_ln_w[l], rwkv_ln_b[l], w_rwkv_out[l], w_out[l], norm2_gain[l],
                         w_group_router[l], b_group_router[l], w_expert_router[l],
                         b_expert_router[l], w_expert_gate[l], w_expert_up[l], w_expert_down[l])
    return x
```

```python
import functools
import math

import jax
import jax.numpy as jnp
from jax import lax
from jax.experimental import pallas as pl
from jax.experimental.pallas import tpu as pltpu

F32 = jnp.float32
BF16 = jnp.bfloat16

D_MODEL = 1024
ATT_PATTERNS = ((128, 1), (512, 4), (2048, 16))
N_ATT_GROUPS = 3
HEADS_PER_GROUP = 4
ATT_HEADS = 12
HEAD_DIM = 64
ATT_WIDTH = 768
ATT_OUT_WIDTH = 256
REL_BUCKETS = 32
REL_MAX_DIST = 2048
RWKV_HEADS = 8
RWKV_HEAD_DIM = 64
RWKV_WIDTH = 512
DECAY_LORA = 64
ICLR_LORA = 64
GATE_LORA = 160
RWKV_SHIFT_WIDTH = 1824
GN_EPS = 64e-5
N_EXPERT_GROUPS = 4
EXPERTS_PER_GROUP = 8
N_EXPERTS = 32
TOP_K = 2
D_EXPERT = 512
NORM_EPS = 1e-6
NEG_INF = -1e30

LANES = 128
ATT_BLOCK = 128
WKV_CHUNK = 64
RWKV_PAD_WIDTH = 2048
P_WIDTH = 6400
QKV_COL_BLOCK = (RWKV_PAD_WIDTH + 2 * D_MODEL) // LANES
ROUTER_PAD = 128
MOE_ROWS = 256
VMEM_LIMIT = 56 * 1024 * 1024


def _sigmoid(x):
    return 1.0 / (1.0 + jnp.exp(-x))


def _dot(a, b):
    return jnp.dot(a.astype(BF16), b.astype(BF16), preferred_element_type=F32)


def _dot_t(a, b):
    return lax.dot_general(a.astype(BF16), b.astype(BF16), (((1,), (1,)), ((), ())),
                           preferred_element_type=F32)


def _split3(a):
    hi = a.astype(BF16)
    r1 = a - hi.astype(F32)
    mid = r1.astype(BF16)
    lo = (r1 - mid.astype(F32)).astype(BF16)
    return hi, mid, lo


def _split2(a):
    hi = a.astype(BF16)
    lo = (a - hi.astype(F32)).astype(BF16)
    return hi, lo


def _inproj_kernel(x_ref, g_ref, w_ref, o_ref, h_ref):
    @pl.when(pl.program_id(1) == 0)
    def _():
        x = x_ref[...]
        ms = jnp.mean(x * x, axis=-1, keepdims=True)
        h_ref[...] = (x * lax.rsqrt(ms + NORM_EPS) * g_ref[...]).astype(BF16)

    o_ref[...] = jnp.dot(h_ref[...], w_ref[...], preferred_element_type=F32)


def _inproj(x2, gain, w_all):
    T = x2.shape[0]
    tm, tn = 1024, 1280
    return pl.pallas_call(
        _inproj_kernel,
        out_shape=jax.ShapeDtypeStruct((T, P_WIDTH), F32),
        grid=(T // tm, P_WIDTH // tn),
        in_specs=[pl.BlockSpec((tm, D_MODEL), lambda i, j: (i, 0)),
                  pl.BlockSpec((1, D_MODEL), lambda i, j: (0, 0)),
                  pl.BlockSpec((D_MODEL, tn), lambda i, j: (0, j))],
        out_specs=pl.BlockSpec((tm, tn), lambda i, j: (i, j)),
        scratch_shapes=[pltpu.VMEM((tm, D_MODEL), BF16)],
        compiler_params=pltpu.CompilerParams(
            dimension_semantics=("arbitrary", "arbitrary"), vmem_limit_bytes=VMEM_LIMIT),
        name="inproj",
    )(x2, gain, w_all)


def _attn_kernel(q0, q1, q2, k0, k1, k2, v0, v1, v2, qg_ref, kg_ref, bias_ref, out_ref,
                 qn_s, kn_s, acc_s, m_s, l_s):
    S = out_ref.shape[1]
    q_refs, k_refs, v_refs = (q0, q1, q2), (k0, k1, k2), (v0, v1, v2)
    lane = lax.broadcasted_iota(jnp.int32, (1, LANES), 1)
    lo = lane < HEAD_DIM

    def pair_norm(x, gain):
        x2 = x * x
        sa = jnp.sum(jnp.where(lo, x2, 0.0), axis=-1, keepdims=True)
        sb = jnp.sum(jnp.where(lo, 0.0, x2), axis=-1, keepdims=True)
        ms = jnp.where(lo, sa, sb) * (1.0 / HEAD_DIM)
        return x * lax.rsqrt(ms + NORM_EPS) * gain

    rows_per = 256

    def norm_body(i, c):
        rs = pl.ds(pl.multiple_of(i * rows_per, rows_per), rows_per)
        for g in range(N_ATT_GROUPS):
            qn_s[g, rs, :] = pair_norm(q_refs[g][0, rs, :], qg_ref[g]) * (HEAD_DIM ** -0.5)
            kn_s[g, rs, :] = pair_norm(k_refs[g][0, rs, :], kg_ref[g])
        return c

    lax.fori_loop(0, S // rows_per, norm_body, 0)

    for g, (window, d) in enumerate(ATT_PATTERNS):
        n_blk = (S // d) // ATT_BLOCK
        has_prev = n_blk > 1

        def rows_at(start, d=d):
            if d == 1:
                return pl.ds(pl.multiple_of(start, ATT_BLOCK), ATT_BLOCK)
            return pl.ds(start, ATT_BLOCK, stride=d)

        def blk_body(idx, c, g=g, d=d, n_blk=n_blk, has_prev=has_prev, rows_at=rows_at):
            r = idx % d
            n = idx // d
            cur = rows_at(n * (ATT_BLOCK * d) + r)
            q = qn_s[g, cur, :]
            kc = kn_s[g, cur, :].astype(BF16)
            vc = v_refs[g][0, cur, :].astype(BF16)
            if has_prev:
                prv = rows_at(jnp.maximum(n - 1, 0) * (ATT_BLOCK * d) + r)
                kp = kn_s[g, prv, :].astype(BF16)
                vp = v_refs[g][0, prv, :].astype(BF16)
            outs, ms, ls = [], [], []
            for hh in range(2):
                qm = jnp.where(lo if hh == 0 else jnp.logical_not(lo), q, 0.0)
                s_c = _dot_t(qm, kc) + bias_ref[g, 0, hh, :, ATT_BLOCK:]
                m = jnp.max(s_c, axis=-1, keepdims=True)
                if has_prev:
                    s_p = _dot_t(qm, kp) + bias_ref[g, 0, hh, :, :ATT_BLOCK]
                    s_p = jnp.where(n > 0, s_p, NEG_INF)
                    m = jnp.maximum(m, jnp.max(s_p, axis=-1, keepdims=True))
                p_c = jnp.exp(s_c - m)
                l = jnp.sum(p_c, axis=-1, keepdims=True)
                o = _dot(p_c, vc)
                if has_prev:
                    p_p = jnp.exp(s_p - m)
                    l = l + jnp.sum(p_p, axis=-1, keepdims=True)
                    o = o + _dot(p_p, vp)
                outs.append(o)
                ms.append(m)
                ls.append(l)
            acc_s[g, cur, :] = jnp.where(lo, outs[0], outs[1])
            m_s[g, cur, :] = jnp.where(lo, ms[0], ms[1])
            l_s[g, cur, :] = jnp.where(lo, ls[0], ls[1])
            return c

        lax.fori_loop(0, d * n_blk, blk_body, 0)

    def comb_body(i, c):
        rs = pl.ds(pl.multiple_of(i * rows_per, rows_per), rows_per)
        m = jnp.maximum(jnp.maximum(m_s[0, rs, :], m_s[1, rs, :]), m_s[2, rs, :])
        num = jnp.zeros((rows_per, LANES), F32)
        den = jnp.zeros((rows_per, LANES), F32)
        for g in range(N_ATT_GROUPS):
            e = jnp.exp(m_s[g, rs, :] - m)
            num = num + e * acc_s[g, rs, :]
            den = den + e * l_s[g, rs, :]
        out_ref[0, rs, :] = num / den
        return c

    lax.fori_loop(0, S // rows_per, comb_body, 0)


def _attention(p3, q_gain, k_gain, bias):
    B, S, _ = p3.shape
    n_pairs = HEADS_PER_GROUP // 2

    def col_spec(base):
        return [pl.BlockSpec((1, S, LANES),
                             functools.partial(lambda b, sp, off: (b, 0, off + sp), off=base + g * n_pairs))
                for g in range(N_ATT_GROUPS)]

    n_qkv_blocks = ATT_WIDTH // LANES
    in_specs = (col_spec(QKV_COL_BLOCK) + col_spec(QKV_COL_BLOCK + n_qkv_blocks)
                + col_spec(QKV_COL_BLOCK + 2 * n_qkv_blocks)
                + [pl.BlockSpec((N_ATT_GROUPS, 1, LANES), lambda b, sp: (0, 0, 0)),
                   pl.BlockSpec((N_ATT_GROUPS, 1, LANES), lambda b, sp: (0, 0, 0)),
                   pl.BlockSpec((N_ATT_GROUPS, 1, 2, ATT_BLOCK, 2 * ATT_BLOCK),
                                lambda b, sp: (0, sp, 0, 0, 0))])
    return pl.pallas_call(
        _attn_kernel,
        out_shape=jax.ShapeDtypeStruct((B, S, ATT_OUT_WIDTH), F32),
        grid=(B, n_pairs),
        in_specs=in_specs,
        out_specs=pl.BlockSpec((1, S, LANES), lambda b, sp: (b, 0, sp)),
        scratch_shapes=[pltpu.VMEM((N_ATT_GROUPS, S, LANES), F32) for _ in range(5)],
        compiler_params=pltpu.CompilerParams(
            dimension_semantics=("arbitrary", "arbitrary"), vmem_limit_bytes=VMEM_LIMIT),
        name="dilated_attention",
    )(*([p3] * 9), q_gain, k_gain, bias)


def _t5_causal_bucket(dist):
    max_exact = REL_BUCKETS // 2
    d = jnp.maximum(dist.astype(F32), 1.0)
    large = max_exact + (jnp.log(d / max_exact) / math.log(REL_MAX_DIST / max_exact)
                         * (REL_BUCKETS - max_exact)).astype(jnp.int32)
    large = jnp.minimum(large, REL_BUCKETS - 1)
    return jnp.where(dist < max_exact, dist, large)


def _attention_bias(rel_bias_table):
    W = ATT_BLOCK
    qi = jnp.arange(W)[:, None]
    kj = jnp.arange(2 * W)[None, :]
    rel = qi + W - kj
    valid = (rel >= 0) & (rel <= W)
    per_group = []
    for g, (_, d) in enumerate(ATT_PATTERNS):
        tab = rel_bias_table[:, g * HEADS_PER_GROUP:(g + 1) * HEADS_PER_GROUP]
        bias = tab[_t5_causal_bucket(jnp.clip(rel, 0, W) * d)]
        bias = jnp.where(valid[:, :, None], bias.astype(F32), NEG_INF)
        per_group.append(jnp.transpose(bias, (2, 0, 1)).reshape(2, 2, W, 2 * W))
    return jnp.stack(per_group, axis=0)


def _rwkv_prep_kernel(p_ref, mu_ref, w0_ref, wup_ref, a0_ref, aup_ref, gup_ref, kk_ref, ka_ref,
                      r_o, lw_o, k_o, v_o, a_o, b_o, g_o, carry):
    tq = p_ref.shape[1]
    C = RWKV_WIDTH

    @pl.when(pl.program_id(1) == 0)
    def _():
        carry[...] = jnp.zeros_like(carry)

    p = p_ref[0]
    row = lax.broadcasted_iota(jnp.int32, (tq, 1), 0)
    prev = jnp.where(row == 0, carry[...], pltpu.roll(p, 1, axis=0))
    carry[...] = p[tq - 1:tq, :]
    pm = p + (prev - p) * mu_ref[...]
    r, k, v = pm[:, 0:C], pm[:, C:2 * C], pm[:, 2 * C:3 * C]
    xw = pm[:, 3 * C:3 * C + LANES]
    xa = pm[:, 3 * C + LANES:3 * C + 2 * LANES]
    xg = pm[:, 3 * C + 2 * LANES:]

    z = w0_ref[...] + _dot(jnp.tanh(xw), wup_ref[...])
    softplus_neg = jnp.maximum(-z, 0.0) + jnp.log(1.0 + jnp.exp(-jnp.abs(z)))
    lw = -jnp.exp(-softplus_neg - 0.5)
    a = _sigmoid(a0_ref[...] + _dot(xa, aup_ref[...]))
    g = _dot(_sigmoid(xg), gup_ref[...])

    lane = lax.broadcasted_iota(jnp.int32, (1, LANES), 1)
    lo = lane < RWKV_HEAD_DIM
    kk = k * kk_ref[...]
    kk_n = []
    for j in range(C // LANES):
        x = kk[:, j * LANES:(j + 1) * LANES]
        x2 = x * x
        sa = jnp.sum(jnp.where(lo, x2, 0.0), axis=-1, keepdims=True)
        sb = jnp.sum(jnp.where(lo, 0.0, x2), axis=-1, keepdims=True)
        nrm = jnp.maximum(jnp.sqrt(jnp.where(lo, sa, sb)), 1e-12)
        kk_n.append(x / nrm)
    kk = jnp.concatenate(kk_n, axis=-1)
    k2 = k * (1.0 + (a - 1.0) * ka_ref[...])
    b = kk * a
    for h in range(RWKV_HEADS):
        sl = slice(h * RWKV_HEAD_DIM, (h + 1) * RWKV_HEAD_DIM)
        r_o[0, h] = r[:, sl]
        lw_o[0, h] = lw[:, sl]
        k_o[0, h] = k2[:, sl]
        v_o[0, h] = v[:, sl]
        a_o[0, h] = -kk[:, sl]
        b_o[0, h] = b[:, sl]
        g_o[0, h] = g[:, sl]


def _rwkv_prep(p3, mu, w0, wup, a0, aup, gup, k_k, k_a):
    B, S, _ = p3.shape
    tq = 512
    C = RWKV_WIDTH
    hm = jax.ShapeDtypeStruct((B, RWKV_HEADS, S, RWKV_HEAD_DIM), F32)
    hm_spec = pl.BlockSpec((1, RWKV_HEADS, tq, RWKV_HEAD_DIM), lambda b, j: (b, 0, j, 0))
    full = lambda shape: pl.BlockSpec(shape, lambda b, j: (0,) * len(shape))
    return pl.pallas_call(
        _rwkv_prep_kernel,
        out_shape=[hm] * 7,
        grid=(B, S // tq),
        in_specs=[pl.BlockSpec((1, tq, RWKV_PAD_WIDTH), lambda b, j: (b, j, 0)),
                  full((1, RWKV_PAD_WIDTH)), full((1, C)), full((LANES, C)), full((1, C)),
                  full((LANES, C)), full((2 * LANES, C)), full((1, C)), full((1, C))],
        out_specs=[hm_spec] * 7,
        scratch_shapes=[pltpu.VMEM((1, RWKV_PAD_WIDTH), F32)],
        compiler_params=pltpu.CompilerParams(
            dimension_semantics=("arbitrary", "arbitrary"), vmem_limit_bytes=VMEM_LIMIT),
        name="rwkv_prep",
    )(p3, mu, w0, wup, a0, aup, gup, k_k, k_a)


def _bmm(a, b):
    return lax.dot_general(a.astype(BF16), b.astype(BF16), (((2,), (1,)), ((0,), (0,))),
                           preferred_element_type=F32)


def _bmm_nt(a, b):
    return lax.dot_general(a.astype(BF16), b.astype(BF16), (((2,), (2,)), ((0,), (0,))),
                           preferred_element_type=F32)


def _bmm_tn(a, b):
    return lax.dot_general(a.astype(BF16), b.astype(BF16), (((1,), (1,)), ((0,), (0,))),
                           preferred_element_type=F32)


def _bmm_hi(a, b):
    ah, al = _split2(a)
    bh, bl = _split2(b)
    return _bmm(ah, bh) + (_bmm(ah, bl) + _bmm(al, bh))


def _wkv_kernel(r_ref, lw_ref, k_ref, v_ref, a_ref, b_ref, g_ref, lnw_ref, lnb_ref, rk_ref,
                o_ref, state):
    H, N, C = RWKV_HEADS, RWKV_HEAD_DIM, WKV_CHUNK
    n_sub = r_ref.shape[2] // C

    @pl.when(pl.program_id(1) == 0)
    def _():
        state[...] = jnp.zeros_like(state)

    ri = lax.broadcasted_iota(jnp.int32, (1, C, C), 1)
    ci = lax.broadcasted_iota(jnp.int32, (1, C, C), 2)
    strict = ri > ci
    incl = ri >= ci
    tri = jnp.broadcast_to(jnp.where(incl, 1.0, 0.0).astype(BF16), (H, C, C))
    eye = jnp.where(ri == ci, 1.0, 0.0).astype(F32)

    def sub(i, c):
        rs = pl.ds(pl.multiple_of(i * C, C), C)
        r, lw, k, v = r_ref[0, :, rs, :], lw_ref[0, :, rs, :], k_ref[0, :, rs, :], v_ref[0, :, rs, :]
        a, b, g = a_ref[0, :, rs, :], b_ref[0, :, rs, :], g_ref[0, :, rs, :]
        hi, mid, lo = _split3(lw)
        cum = _bmm(tri, hi) + (_bmm(tri, mid) + _bmm(tri, lo))
        e_pos = jnp.exp(cum)
        e_neg = jnp.exp(-cum)
        at = a * jnp.exp(cum - lw)
        rt = r * e_pos
        bt = b * e_neg
        kt = k * e_neg
        a_ab = jnp.where(strict, _bmm_nt(at, bt), 0.0)
        a_ak = jnp.where(strict, _bmm_nt(at, kt), 0.0)
        a_rb = jnp.where(incl, _bmm_nt(rt, bt), 0.0)
        a_rk = jnp.where(incl, _bmm_nt(rt, kt), 0.0)
        inv = eye + a_ab
        pw = a_ab
        for _ in range(int(math.log2(C)) - 1):
            pw = _bmm_hi(pw, pw)
            inv = inv + _bmm_hi(inv, pw)
        s0 = state[...]
        u = _bmm_hi(inv, _bmm(at, s0) + _bmm(a_ak, v))
        o = _bmm(rt, s0) + _bmm(a_rb, u) + _bmm(a_rk, v)
        w_end = e_pos[:, C - 1:C, :]
        bw = bt * w_end
        kw = kt * w_end
        state[...] = (s0 * jnp.swapaxes(w_end, 1, 2) + _bmm_tn(bw, u) + _bmm_tn(kw, v))

        mu = jnp.mean(o, axis=-1, keepdims=True)
        var = jnp.mean(jnp.square(o - mu), axis=-1, keepdims=True)
        on = (o - mu) * lax.rsqrt(var + GN_EPS) * lnw_ref[...] + lnb_ref[...]
        bonus = jnp.sum(r * k * rk_ref[...], axis=-1, keepdims=True) * v
        o_ref[0, :, rs, :] = (on + bonus) * g
        return c

    lax.fori_loop(0, n_sub, sub, 0)


def _wkv(r, lw, k, v, a, b, g, ln_w, ln_b, r_k):
    B, H, S, N = r.shape
    tc = 256
    hm_spec = pl.BlockSpec((1, H, tc, N), lambda bb, j: (bb, 0, j, 0))
    par_spec = pl.BlockSpec((H, 1, N), lambda bb, j: (0, 0, 0))
    return pl.pallas_call(
        _wkv_kernel,
        out_shape=jax.ShapeDtypeStruct((B, H, S, N), F32),
        grid=(B, S // tc),
        in_specs=[hm_spec] * 7 + [par_spec] * 3,
        out_specs=hm_spec,
        scratch_shapes=[pltpu.VMEM((H, N, N), F32)],
        compiler_params=pltpu.CompilerParams(
            dimension_semantics=("arbitrary", "arbitrary"), vmem_limit_bytes=VMEM_LIMIT),
        name="wkv7",
    )(r, lw, k, v, a, b, g, ln_w, ln_b, r_k)


def _merge_kernel(att_ref, o_ref, gate_ref, x_ref, wa_ref, wr_ref, wo_ref, g2_ref, wrt_ref, brt_ref,
                  x1_ref, h2_ref, lg_ref):
    y_att = _dot(att_ref[0], wa_ref[...])
    y_rwkv = _dot(o_ref[0, 0], wr_ref[0])
    for h in range(1, RWKV_HEADS):
        y_rwkv = y_rwkv + _dot(o_ref[0, h], wr_ref[h])
    gates = _sigmoid(gate_ref[0])
    mixed = gates[:, :D_MODEL] * y_att + gates[:, D_MODEL:] * y_rwkv
    x1 = x_ref[0] + _dot(mixed, wo_ref[...])
    x1_ref[0] = x1
    ms = jnp.mean(x1 * x1, axis=-1, keepdims=True)
    h2 = x1 * lax.rsqrt(ms + NORM_EPS) * g2_ref[...]
    h2_ref[0] = h2.astype(BF16)
    hh, hl = _split2(h2)
    wh, wl = wrt_ref[0], wrt_ref[1]
    lg = (jnp.dot(hh, wh, preferred_element_type=F32)
          + (jnp.dot(hh, wl, preferred_element_type=F32) + jnp.dot(hl, wh, preferred_element_type=F32)))
    lg_ref[0] = lg + brt_ref[...]


def _merge(att, o_g, p3, x, w_att_out, w_rwkv_out, w_out, gain2, w_router, b_router):
    B, S, D = x.shape
    tm = 512
    full = lambda shape: pl.BlockSpec(shape, lambda b, j: (0,) * len(shape))
    return pl.pallas_call(
        _merge_kernel,
        out_shape=[jax.ShapeDtypeStruct((B, S, D), F32), jax.ShapeDtypeStruct((B, S, D), BF16),
                   jax.ShapeDtypeStruct((B, S, ROUTER_PAD), F32)],
        grid=(B, S // tm),
        in_specs=[pl.BlockSpec((1, tm, ATT_OUT_WIDTH), lambda b, j: (b, j, 0)),
                  pl.BlockSpec((1, RWKV_HEADS, tm, RWKV_HEAD_DIM), lambda b, j: (b, 0, j, 0)),
                  pl.BlockSpec((1, tm, 2 * D_MODEL), lambda b, j: (b, j, 1)),
                  pl.BlockSpec((1, tm, D), lambda b, j: (b, j, 0)),
                  full((ATT_OUT_WIDTH, D)), full((RWKV_HEADS, RWKV_HEAD_DIM, D)), full((D, D)),
                  full((1, D)), full((2, D, ROUTER_PAD)), full((1, ROUTER_PAD))],
        out_specs=[pl.BlockSpec((1, tm, D), lambda b, j: (b, j, 0)),
                   pl.BlockSpec((1, tm, D), lambda b, j: (b, j, 0)),
                   pl.BlockSpec((1, tm, ROUTER_PAD), lambda b, j: (b, j, 0))],
        compiler_params=pltpu.CompilerParams(
            dimension_semantics=("arbitrary", "arbitrary"), vmem_limit_bytes=VMEM_LIMIT),
        name="merge_outproj_router",
    )(att, o_g, p3, x, w_att_out, w_rwkv_out, w_out, gain2, w_router, b_router)


def _expert_kernel(be_ref, x_ref, gate_ref, wg_ref, wu_ref, wd_ref, y_ref):
    xb = x_ref[...]
    hg = jnp.dot(xb, wg_ref[0].astype(BF16), preferred_element_type=F32)
    hu = jnp.dot(xb, wu_ref[0].astype(BF16), preferred_element_type=F32)
    hid = hg * _sigmoid(hg) * hu
    y = jnp.dot(hid.astype(BF16), wd_ref[0].astype(BF16), preferred_element_type=F32)
    y_ref[...] = y * gate_ref[...]


def _experts(blk_expert, xin, row_gate, w_gate, w_up, w_down):
    n_rows, D = xin.shape
    n_blocks = n_rows // MOE_ROWS
    return pl.pallas_call(
        _expert_kernel,
        out_shape=jax.ShapeDtypeStruct((n_rows, D), F32),
        grid_spec=pltpu.PrefetchScalarGridSpec(
            num_scalar_prefetch=1,
            grid=(n_blocks,),
            in_specs=[pl.BlockSpec((MOE_ROWS, D), lambda i, be: (i, 0)),
                      pl.BlockSpec((MOE_ROWS, 1), lambda i, be: (i, 0)),
                      pl.BlockSpec((1, D, D_EXPERT), lambda i, be: (be[i], 0, 0)),
                      pl.BlockSpec((1, D, D_EXPERT), lambda i, be: (be[i], 0, 0)),
                      pl.BlockSpec((1, D_EXPERT, D), lambda i, be: (be[i], 0, 0))],
            out_specs=pl.BlockSpec((MOE_ROWS, D), lambda i, be: (i, 0))),
        compiler_params=pltpu.CompilerParams(
            dimension_semantics=("arbitrary",), vmem_limit_bytes=VMEM_LIMIT),
        name="moe_experts",
    )(blk_expert, xin, row_gate, w_gate, w_up, w_down)


def _route(logits, T):
    grp_prob = jax.nn.softmax(logits[:, :N_EXPERT_GROUPS], axis=-1)
    grp_p, grp_idx = lax.top_k(grp_prob, 1)
    exp_logits = logits[:, N_EXPERT_GROUPS:N_EXPERT_GROUPS + N_EXPERTS].reshape(
        T, N_EXPERT_GROUPS, EXPERTS_PER_GROUP)
    sel = jnp.take_along_axis(exp_logits, grp_idx[:, :, None], axis=1)[:, 0]
    top_p, top_i = lax.top_k(jax.nn.softmax(sel, axis=-1), TOP_K)
    gates = grp_p * top_p / jnp.sum(top_p, axis=-1, keepdims=True)
    expert_id = grp_idx * EXPERTS_PER_GROUP + top_i

    A = T * TOP_K
    eid = expert_id.reshape(A).astype(jnp.int32)
    tok = jnp.arange(A, dtype=jnp.int32) // TOP_K
    gate = gates.reshape(A)
    order = jnp.argsort(eid)
    se = eid[order]
    counts = jnp.bincount(eid, length=N_EXPERTS)
    starts = jnp.cumsum(counts) - counts
    pcounts = (counts + MOE_ROWS - 1) // MOE_ROWS * MOE_ROWS
    pends = jnp.cumsum(pcounts)
    pstarts = pends - pcounts
    dest = (pstarts[se] + jnp.arange(A, dtype=jnp.int32) - starts[se]).astype(jnp.int32)
    n_blocks = -(-A // MOE_ROWS) + N_EXPERTS
    n_rows = n_blocks * MOE_ROWS
    row_tok = jnp.full((n_rows,), T, jnp.int32).at[dest].set(tok[order])
    row_gate = jnp.zeros((n_rows,), F32).at[dest].set(gate[order])
    blk_expert = jnp.minimum(
        jnp.searchsorted(pends, jnp.arange(n_blocks) * MOE_ROWS, side='right'), N_EXPERTS - 1)
    dest_by_assign = jnp.zeros((A,), jnp.int32).at[order].set(dest)
    return row_tok, row_gate, blk_expert.astype(jnp.int32), dest_by_assign


def _pad_rows(w, rows):
    return jnp.pad(w, ((0, rows - w.shape[0]), (0, 0)))


def kernel(x, norm1_gain, w_in, q_norm_gain, k_norm_gain, rel_bias_table, w_att_out, rwkv_shift_mu,
           rwkv_w0, rwkv_w_up, rwkv_a0, rwkv_a_up, rwkv_g_up, rwkv_k_k, rwkv_k_a, rwkv_r_k,
           rwkv_ln_w, rwkv_ln_b, w_rwkv_out, w_out, norm2_gain, w_group_router, b_group_router,
           w_expert_router, b_expert_router, w_expert_gate, w_expert_up, w_expert_down):
    B, S, D = x.shape
    T = B * S
    C = RWKV_WIDTH
    bias = _attention_bias(rel_bias_table)
    for l in range(norm1_gain.shape[0]):
        wi = w_in[l]
        w_qkv = wi[:, :3 * ATT_WIDTH]
        w_rw = wi[:, 3 * ATT_WIDTH:3 * ATT_WIDTH + RWKV_SHIFT_WIDTH]
        w_gt = wi[:, 3 * ATT_WIDTH + RWKV_SHIFT_WIDTH:]
        pad_cols = lambda w, n: jnp.pad(w, ((0, 0), (0, n - w.shape[1])))
        o_w, o_a, o_g = 3 * C, 3 * C + DECAY_LORA, 3 * C + DECAY_LORA + ICLR_LORA

        def rw_layout(w):
            return jnp.concatenate([w[:, :o_w], pad_cols(w[:, o_w:o_a], LANES),
                                    pad_cols(w[:, o_a:o_g], LANES), pad_cols(w[:, o_g:], 2 * LANES)], axis=1)

        w_all = jnp.concatenate([rw_layout(w_rw), w_gt, w_qkv], axis=1).astype(BF16)
        mu = rw_layout(rwkv_shift_mu[l][None, :])

        p = _inproj(x.reshape(T, D), norm1_gain[l][None, :], w_all)
        p3 = p.reshape(B, S, P_WIDTH)

        tile2 = lambda gmat: jnp.tile(gmat, (1, 2))[:, None, :]
        att = _attention(p3, tile2(q_norm_gain[l]), tile2(k_norm_gain[l]), bias)

        row = lambda v: v[None, :]
        r, lw, k2, v, a_vec, b_vec, g = _rwkv_prep(
            p3, mu, row(rwkv_w0[l]), _pad_rows(rwkv_w_up[l], LANES).astype(BF16), row(rwkv_a0[l]),
            _pad_rows(rwkv_a_up[l], LANES).astype(BF16), _pad_rows(rwkv_g_up[l], 2 * LANES).astype(BF16),
            row(rwkv_k_k[l]), row(rwkv_k_a[l]))
        hp = lambda v: v.reshape(RWKV_HEADS, 1, RWKV_HEAD_DIM)
        o_g = _wkv(r, lw, k2, v, a_vec, b_vec, g, hp(rwkv_ln_w[l]), hp(rwkv_ln_b[l]), hp(rwkv_r_k[l]))

        w_router = jnp.concatenate([w_group_router[l], w_expert_router[l]], axis=1)
        w_router = jnp.pad(w_router, ((0, 0), (0, ROUTER_PAD - w_router.shape[1])))
        wr_hi = w_router.astype(BF16)
        wr_lo = (w_router - wr_hi.astype(F32)).astype(BF16)
        b_router = jnp.concatenate([b_group_router[l], b_expert_router[l]])
        b_router = jnp.pad(b_router, (0, ROUTER_PAD - b_router.shape[0]))[None, :]
        x1, h2, logits = _merge(
            att, o_g, p3, x, w_att_out[l].astype(BF16),
            w_rwkv_out[l].astype(BF16).reshape(RWKV_HEADS, RWKV_HEAD_DIM, D), w_out[l].astype(BF16),
            norm2_gain[l][None, :], jnp.stack([wr_hi, wr_lo]), b_router)

        row_tok, row_gate, blk_expert, dest = _route(logits.reshape(T, ROUTER_PAD), T)
        h2_pad = jnp.concatenate([h2.reshape(T, D), jnp.zeros((1, D), BF16)], axis=0)
        xin = h2_pad[row_tok]
        yb = _experts(blk_expert, xin, row_gate[:, None], w_expert_gate[l], w_expert_up[l],
                      w_expert_down[l])
        y = yb[dest[0::2]] + yb[dest[1::2]]
        x = x1 + y.reshape(B, S, D)
    return x
```

```python
import functools
import math

import jax
import jax.numpy as jnp
from jax import lax
from jax.experimental import pallas as pl
from jax.experimental.pallas import tpu as pltpu

F32 = jnp.float32
BF16 = jnp.bfloat16

D_MODEL = 1024
ATT_PATTERNS = ((128, 1), (512, 4), (2048, 16))
N_ATT_GROUPS = 3
HEADS_PER_GROUP = 4
ATT_HEADS = 12
HEAD_DIM = 64
ATT_WIDTH = 768
ATT_OUT_WIDTH = 256
REL_BUCKETS = 32
REL_MAX_DIST = 2048
RWKV_HEADS = 8
RWKV_HEAD_DIM = 64
RWKV_WIDTH = 512
DECAY_LORA = 64
ICLR_LORA = 64
GATE_LORA = 160
RWKV_SHIFT_WIDTH = 1824
GN_EPS = 64e-5
N_EXPERT_GROUPS = 4
EXPERTS_PER_GROUP = 8
N_EXPERTS = 32
TOP_K = 2
D_EXPERT = 512
NORM_EPS = 1e-6
NEG_INF = -1e30

LANES = 128
ATT_BLOCK = 128
WKV_CHUNK = 64
RWKV_PAD_WIDTH = 2048
P_WIDTH = 6400
QKV_COL_BLOCK = (RWKV_PAD_WIDTH + 2 * D_MODEL) // LANES
ROUTER_PAD = 128
MOE_ROWS = 256
VMEM_LIMIT = 56 * 1024 * 1024


def _sigmoid(x):
    return 1.0 / (1.0 + jnp.exp(-x))


def _dot(a, b):
    return jnp.dot(a.astype(BF16), b.astype(BF16), preferred_element_type=F32)


def _dot_t(a, b):
    return lax.dot_general(a.astype(BF16), b.astype(BF16), (((1,), (1,)), ((), ())),
                           preferred_element_type=F32)


def _split3(a):
    hi = a.astype(BF16)
    r1 = a - hi.astype(F32)
    mid = r1.astype(BF16)
    lo = (r1 - mid.astype(F32)).astype(BF16)
    return hi, mid, lo


def _split2(a):
    hi = a.astype(BF16)
    lo = (a - hi.astype(F32)).astype(BF16)
    return hi, lo


def _inproj_kernel(x_ref, g_ref, w_ref, o_ref, h_ref):
    @pl.when(pl.program_id(1) == 0)
    def _():
        x = x_ref[...]
        ms = jnp.mean(x * x, axis=-1, keepdims=True)
        h_ref[...] = (x * lax.rsqrt(ms + NORM_EPS) * g_ref[...]).astype(BF16)

    o_ref[...] = jnp.dot(h_ref[...], w_ref[...], preferred_element_type=F32)


def _inproj(x2, gain, w_all):
    T = x2.shape[0]
    tm, tn = 1024, 1280
    return pl.pallas_call(
        _inproj_kernel,
        out_shape=jax.ShapeDtypeStruct((T, P_WIDTH), F32),
        grid=(T // tm, P_WIDTH // tn),
        in_specs=[pl.BlockSpec((tm, D_MODEL), lambda i, j: (i, 0)),
                  pl.BlockSpec((1, D_MODEL), lambda i, j: (0, 0)),
                  pl.BlockSpec((D_MODEL, tn), lambda i, j: (0, j))],
        out_specs=pl.BlockSpec((tm, tn), lambda i, j: (i, j)),
        scratch_shapes=[pltpu.VMEM((tm, D_MODEL), BF16)],
        compiler_params=pltpu.CompilerParams(
            dimension_semantics=("arbitrary", "arbitrary"), vmem_limit_bytes=VMEM_LIMIT),
        name="inproj",
    )(x2, gain, w_all)


def _attn_kernel(q0, q1, q2, k0, k1, k2, v0, v1, v2, qg_ref, kg_ref, bias_ref, out_ref,
                 qn_s, kn_s, acc_s, m_s, l_s):
    S = out_ref.shape[1]
    q_refs, k_refs, v_refs = (q0, q1, q2), (k0, k1, k2), (v0, v1, v2)
    lane = lax.broadcasted_iota(jnp.int32, (1, LANES), 1)
    lo = lane < HEAD_DIM

    def pair_norm(x, gain):
        x2 = x * x
        sa = jnp.sum(jnp.where(lo, x2, 0.0), axis=-1, keepdims=True)
        sb = jnp.sum(jnp.where(lo, 0.0, x2), axis=-1, keepdims=True)
        ms = jnp.where(lo, sa, sb) * (1.0 / HEAD_DIM)
        return x * lax.rsqrt(ms + NORM_EPS) * gain

    rows_per = 256

    def norm_body(i, c):
        rs = pl.ds(pl.multiple_of(i * rows_per, rows_per), rows_per)
        for g in range(N_ATT_GROUPS):
            qn_s[g, rs, :] = pair_norm(q_refs[g][0, rs, :], qg_ref[g]) * (HEAD_DIM ** -0.5)
            kn_s[g, rs, :] = pair_norm(k_refs[g][0, rs, :], kg_ref[g])
        return c

    lax.fori_loop(0, S // rows_per, norm_body, 0)

    for g, (window, d) in enumerate(ATT_PATTERNS):
        n_blk = (S // d) // ATT_BLOCK
        has_prev = n_blk > 1

        def rows_at(start, d=d):
            if d == 1:
                return pl.ds(pl.multiple_of(start, ATT_BLOCK), ATT_BLOCK)
            return pl.ds(start, ATT_BLOCK, stride=d)

        def blk_body(idx, c, g=g, d=d, n_blk=n_blk, has_prev=has_prev, rows_at=rows_at):
            r = idx % d
            n = idx // d
            cur = rows_at(n * (ATT_BLOCK * d) + r)
            q = qn_s[g, cur, :]
            kc = kn_s[g, cur, :].astype(BF16)
            vc = v_refs[g][0, cur, :].astype(BF16)
            if has_prev:
                prv = rows_at(jnp.maximum(n - 1, 0) * (ATT_BLOCK * d) + r)
                kp = kn_s[g, prv, :].astype(BF16)
                vp = v_refs[g][0, prv, :].astype(BF16)
            outs, ms, ls = [], [], []
            for hh in range(2):
                qm = jnp.where(lo if hh == 0 else jnp.logical_not(lo), q, 0.0)
                s_c = _dot_t(qm, kc) + bias_ref[g, 0, hh, :, ATT_BLOCK:]
                m = jnp.max(s_c, axis=-1, keepdims=True)
                if has_prev:
                    s_p = _dot_t(qm, kp) + bias_ref[g, 0, hh, :, :ATT_BLOCK]
                    s_p = jnp.where(n > 0, s_p, NEG_INF)
                    m = jnp.maximum(m, jnp.max(s_p, axis=-1, keepdims=True))
                p_c = jnp.exp(s_c - m)
                l = jnp.sum(p_c, axis=-1, keepdims=True)
                o = _dot(p_c, vc)
                if has_prev:
                    p_p = jnp.exp(s_p - m)
                    l = l + jnp.sum(p_p, axis=-1, keepdims=True)
                    o = o + _dot(p_p, vp)
                outs.append(o)
                ms.append(m)
                ls.append(l)
            acc_s[g, cur, :] = jnp.where(lo, outs[0], outs[1])
            m_s[g, cur, :] = jnp.where(lo, ms[0], ms[1])
            l_s[g, cur, :] = jnp.where(lo, ls[0], ls[1])
            return c

        lax.fori_loop(0, d * n_blk, blk_body, 0)

    def comb_body(i, c):
        rs = pl.ds(pl.multiple_of(i * rows_per, rows_per), rows_per)
        m = jnp.maximum(jnp.maximum(m_s[0, rs, :], m_s[1, rs, :]), m_s[2, rs, :])
        num = jnp.zeros((rows_per, LANES), F32)
        den = jnp.zeros((rows_per, LANES), F32)
        for g in range(N_ATT_GROUPS):
            e = jnp.exp(m_s[g, rs, :] - m)
            num = num + e * acc_s[g, rs, :]
            den = den + e * l_s[g, rs, :]
        out_ref[0, rs, :] = num / den
        return c

    lax.fori_loop(0, S // rows_per, comb_body, 0)


def _attention(p3, q_gain, k_gain, bias):
    B, S, _ = p3.shape
    n_pairs = HEADS_PER_GROUP // 2

    def col_spec(base):
        return [pl.BlockSpec((1, S, LANES),
                             functools.partial(lambda b, sp, off: (b, 0, off + sp), off=base + g * n_pairs))
                for g in range(N_ATT_GROUPS)]

    n_qkv_blocks = ATT_WIDTH // LANES
    in_specs = (col_spec(QKV_COL_BLOCK) + col_spec(QKV_COL_BLOCK + n_qkv_blocks)
                + col_spec(QKV_COL_BLOCK + 2 * n_qkv_blocks)
                + [pl.BlockSpec((N_ATT_GROUPS, 1, LANES), lambda b, sp: (0, 0, 0)),
                   pl.BlockSpec((N_ATT_GROUPS, 1, LANES), lambda b, sp: (0, 0, 0)),
                   pl.BlockSpec((N_ATT_GROUPS, 1, 2, ATT_BLOCK, 2 * ATT_BLOCK),
                                lambda b, sp: (0, sp, 0, 0, 0))])
    return pl.pallas_call(
        _attn_kernel,
        out_shape=jax.ShapeDtypeStruct((B, S, ATT_OUT_WIDTH), F32),
        grid=(B, n_pairs),
        in_specs=in_specs,
        out_specs=pl.BlockSpec((1, S, LANES), lambda b, sp: (b, 0, sp)),
        scratch_shapes=[pltpu.VMEM((N_ATT_GROUPS, S, LANES), F32) for _ in range(5)],
        compiler_params=pltpu.CompilerParams(
            dimension_semantics=("arbitrary", "arbitrary"), vmem_limit_bytes=VMEM_LIMIT),
        name="dilated_attention",
    )(*([p3] * 9), q_gain, k_gain, bias)


def _t5_causal_bucket(dist):
    max_exact = REL_BUCKETS // 2
    d = jnp.maximum(dist.astype(F32), 1.0)
    large = max_exact + (jnp.log(d / max_exact) / math.log(REL_MAX_DIST / max_exact)
                         * (REL_BUCKETS - max_exact)).astype(jnp.int32)
    large = jnp.minimum(large, REL_BUCKETS - 1)
    return jnp.where(dist < max_exact, dist, large)


def _attention_bias(rel_bias_table):
    W = ATT_BLOCK
    qi = jnp.arange(W)[:, None]
    kj = jnp.arange(2 * W)[None, :]
    rel = qi + W - kj
    valid = (rel >= 0) & (rel <= W)
    per_group = []
    for g, (_, d) in enumerate(ATT_PATTERNS):
        tab = rel_bias_table[:, g * HEADS_PER_GROUP:(g + 1) * HEADS_PER_GROUP]
        bias = tab[_t5_causal_bucket(jnp.clip(rel, 0, W) * d)]
        bias = jnp.where(valid[:, :, None], bias.astype(F32), NEG_INF)
        per_group.append(jnp.transpose(bias, (2, 0, 1)).reshape(2, 2, W, 2 * W))
    return jnp.stack(per_group, axis=0)


def _rwkv_prep_kernel(p_ref, mu_ref, w0_ref, wup_ref, a0_ref, aup_ref, gup_ref, kk_ref, ka_ref,
                      r_o, lw_o, k_o, v_o, a_o, b_o, g_o, carry):
    tq = p_ref.shape[1]
    C = RWKV_WIDTH

    @pl.when(pl.program_id(1) == 0)
    def _():
        carry[...] = jnp.zeros_like(carry)

    p = p_ref[0]
    row = lax.broadcasted_iota(jnp.int32, (tq, 1), 0)
    prev = jnp.where(row == 0, carry[...], pltpu.roll(p, 1, axis=0))
    carry[...] = p[tq - 1:tq, :]
    pm = p + (prev - p) * mu_ref[...]
    r, k, v = pm[:, 0:C], pm[:, C:2 * C], pm[:, 2 * C:3 * C]
    xw = pm[:, 3 * C:3 * C + LANES]
    xa = pm[:, 3 * C + LANES:3 * C + 2 * LANES]
    xg = pm[:, 3 * C + 2 * LANES:]

    z = w0_ref[...] + _dot(jnp.tanh(xw), wup_ref[...])
    softplus_neg = jnp.maximum(-z, 0.0) + jnp.log(1.0 + jnp.exp(-jnp.abs(z)))
    lw = -jnp.exp(-softplus_neg - 0.5)
    a = _sigmoid(a0_ref[...] + _dot(xa, aup_ref[...]))
    g = _dot(_sigmoid(xg), gup_ref[...])

    lane = lax.broadcasted_iota(jnp.int32, (1, LANES), 1)
    lo = lane < RWKV_HEAD_DIM
    kk = k * kk_ref[...]
    kk_n = []
    for j in range(C // LANES):
        x = kk[:, j * LANES:(j + 1) * LANES]
        x2 = x * x
        sa = jnp.sum(jnp.where(lo, x2, 0.0), axis=-1, keepdims=True)
        sb = jnp.sum(jnp.where(lo, 0.0, x2), axis=-1, keepdims=True)
        nrm = jnp.maximum(jnp.sqrt(jnp.where(lo, sa, sb)), 1e-12)
        kk_n.append(x / nrm)
    kk = jnp.concatenate(kk_n, axis=-1)
    k2 = k * (1.0 + (a - 1.0) * ka_ref[...])
    b = kk * a
    for h in range(RWKV_HEADS):
        sl = slice(h * RWKV_HEAD_DIM, (h + 1) * RWKV_HEAD_DIM)
        r_o[0, h] = r[:, sl]
        lw_o[0, h] = lw[:, sl]
        k_o[0, h] = k2[:, sl]
        v_o[0, h] = v[:, sl]
        a_o[0, h] = -kk[:, sl]
        b_o[0, h] = b[:, sl]
        g_o[0, h] = g[:, sl]


def _rwkv_prep(p3, mu, w0, wup, a0, aup, gup, k_k, k_a):
    B, S, _ = p3.shape
    tq = 512
    C = RWKV_WIDTH
    hm = jax.ShapeDtypeStruct((B, RWKV_HEADS, S, RWKV_HEAD_DIM), F32)
    hm_spec = pl.BlockSpec((1, RWKV_HEADS, tq, RWKV_HEAD_DIM), lambda b, j: (b, 0, j, 0))
    full = lambda shape: pl.BlockSpec(shape, lambda b, j: (0,) * len(shape))
    return pl.pallas_call(
        _rwkv_prep_kernel,
        out_shape=[hm] * 7,
        grid=(B, S // tq),
        in_specs=[pl.BlockSpec((1, tq, RWKV_PAD_WIDTH), lambda b, j: (b, j, 0)),
                  full((1, RWKV_PAD_WIDTH)), full((1, C)), full((LANES, C)), full((1, C)),
                  full((LANES, C)), full((2 * LANES, C)), full((1, C)), full((1, C))],
        out_specs=[hm_spec] * 7,
        scratch_shapes=[pltpu.VMEM((1, RWKV_PAD_WIDTH), F32)],
        compiler_params=pltpu.CompilerParams(
            dimension_semantics=("arbitrary", "arbitrary"), vmem_limit_bytes=VMEM_LIMIT),
        name="rwkv_prep",
    )(p3, mu, w0, wup, a0, aup, gup, k_k, k_a)


def _bmm(a, b):
    return lax.dot_general(a.astype(BF16), b.astype(BF16), (((2,), (1,)), ((0,), (0,))),
                           preferred_element_type=F32)


def _bmm_nt(a, b):
    return lax.dot_general(a.astype(BF16), b.astype(BF16), (((2,), (2,)), ((0,), (0,))),
                           preferred_element_type=F32)


def _bmm_tn(a, b):
    return lax.dot_general(a.astype(BF16), b.astype(BF16), (((1,), (1,)), ((0,), (0,))),
                           preferred_element_type=F32)


def _bmm_hi(a, b):
    ah, al = _split2(a)
    bh, bl = _split2(b)
    return _bmm(ah, bh) + (_bmm(ah, bl) + _bmm(al, bh))


def _wkv_kernel(r_ref, lw_ref, k_ref, v_ref, a_ref, b_ref, g_ref, lnw_ref, lnb_ref, rk_ref,
                o_ref, state):
    H, N, C = RWKV_HEADS, RWKV_HEAD_DIM, WKV_CHUNK
    n_sub = r_ref.shape[2] // C

    @pl.when(pl.program_id(1) == 0)
    def _():
        state[...] = jnp.zeros_like(state)

    ri = lax.broadcasted_iota(jnp.int32, (1, C, C), 1)
    ci = lax.broadcasted_iota(jnp.int32, (1, C, C), 2)
    strict = ri > ci
    incl = ri >= ci
    tri = jnp.broadcast_to(jnp.where(incl, 1.0, 0.0).astype(BF16), (H, C, C))
    eye = jnp.where(ri == ci, 1.0, 0.0).astype(F32)

    def sub(i, c):
        rs = pl.ds(pl.multiple_of(i * C, C), C)
        r, lw, k, v = r_ref[0, :, rs, :], lw_ref[0, :, rs, :], k_ref[0, :, rs, :], v_ref[0, :, rs, :]
        a, b, g = a_ref[0, :, rs, :], b_ref[0, :, rs, :], g_ref[0, :, rs, :]
        hi, mid, lo = _split3(lw)
        cum = _bmm(tri, hi) + (_bmm(tri, mid) + _bmm(tri, lo))
        e_pos = jnp.exp(cum)
        e_neg = jnp.exp(-cum)
        at = a * jnp.exp(cum - lw)
        rt = r * e_pos
        bt = b * e_neg
        kt = k * e_neg
        a_ab = jnp.where(strict, _bmm_nt(at, bt), 0.0)
        a_ak = jnp.where(strict, _bmm_nt(at, kt), 0.0)
        a_rb = jnp.where(incl, _bmm_nt(rt, bt), 0.0)
        a_rk = jnp.where(incl, _bmm_nt(rt, kt), 0.0)
        inv = eye + a_ab
        pw = a_ab
        for _ in range(int(math.log2(C)) - 1):
            pw = _bmm_hi(pw, pw)
            inv = inv + _bmm_hi(inv, pw)
        s0 = state[...]
        u = _bmm_hi(inv, _bmm(at, s0) + _bmm(a_ak, v))
        o = _bmm(rt, s0) + _bmm(a_rb, u) + _bmm(a_rk, v)
        w_end = e_pos[:, C - 1:C, :]
        bw = bt * w_end
        kw = kt * w_end
        state[...] = (s0 * jnp.swapaxes(w_end, 1, 2) + _bmm_tn(bw, u) + _bmm_tn(kw, v))

        mu = jnp.mean(o, axis=-1, keepdims=True)
        var = jnp.mean(jnp.square(o - mu), axis=-1, keepdims=True)
        on = (o - mu) * lax.rsqrt(var + GN_EPS) * lnw_ref[...] + lnb_ref[...]
        bonus = jnp.sum(r * k * rk_ref[...], axis=-1, keepdims=True) * v
        o_ref[0, :, rs, :] = (on + bonus) * g
        return c

    lax.fori_loop(0, n_sub, sub, 0)


def _wkv(r, lw, k, v, a, b, g, ln_w, ln_b, r_k):
    B, H, S, N = r.shape
    tc = 256
    hm_spec = pl.BlockSpec((1, H, tc, N), lambda bb, j: (bb, 0, j, 0))
    par_spec = pl.BlockSpec((H, 1, N), lambda bb, j: (0, 0, 0))
    return pl.pallas_call(
        _wkv_kernel,
        out_shape=jax.ShapeDtypeStruct((B, H, S, N), F32),
        grid=(B, S // tc),
        in_specs=[hm_spec] * 7 + [par_spec] * 3,
        out_specs=hm_spec,
        scratch_shapes=[pltpu.VMEM((H, N, N), F32)],
        compiler_params=pltpu.CompilerParams(
            dimension_semantics=("arbitrary", "arbitrary"), vmem_limit_bytes=VMEM_LIMIT),
        name="wkv7",
    )(r, lw, k, v, a, b, g, ln_w, ln_b, r_k)


def _merge_kernel(att_ref, o_ref, gate_ref, x_ref, wa_ref, wr_ref, wo_ref, g2_ref, wrt_ref, brt_ref,
                  x1_ref, h2_ref, lg_ref, cnt_ref):
    y_att = _dot(att_ref[0], wa_ref[...])
    y_rwkv = _dot(o_ref[0, 0], wr_ref[0])
    for h in range(1, RWKV_HEADS):
        y_rwkv = y_rwkv + _dot(o_ref[0, h], wr_ref[h])
    gates = _sigmoid(gate_ref[0])
    mixed = gates[:, :D_MODEL] * y_att + gates[:, D_MODEL:] * y_rwkv
    x1 = x_ref[0] + _dot(mixed, wo_ref[...])
    x1_ref[0] = x1
    ms = jnp.mean(x1 * x1, axis=-1, keepdims=True)
    h2 = x1 * lax.rsqrt(ms + NORM_EPS) * g2_ref[...]
    h2_ref[0] = h2.astype(BF16)
    hh, hl = _split2(h2)
    wh, wl = wrt_ref[0], wrt_ref[1]
    lg = (jnp.dot(hh, wh, preferred_element_type=F32)
          + (jnp.dot(hh, wl, preferred_element_type=F32) + jnp.dot(hl, wh, preferred_element_type=F32)))
    lg = lg + brt_ref[...]

    lane = lax.broadcasted_iota(jnp.int32, (1, ROUTER_PAD), 1)
    gmask = lane < N_EXPERT_GROUPS
    gl = jnp.where(gmask, lg, -jnp.inf)
    gmax = jnp.max(gl, axis=-1, keepdims=True)
    grp_idx = jnp.min(jnp.where(gl == gmax, lane, ROUTER_PAD), axis=-1, keepdims=True)
    grp_p = 1.0 / jnp.sum(jnp.where(gmask, jnp.exp(lg - gmax), 0.0), axis=-1, keepdims=True)
    e_lane = lane - N_EXPERT_GROUPS
    emask = ((e_lane >= 0) & (e_lane < N_EXPERTS)
             & (jnp.right_shift(e_lane, int(math.log2(EXPERTS_PER_GROUP))) == grp_idx))
    el = jnp.where(emask, lg, -jnp.inf)
    m1 = jnp.max(el, axis=-1, keepdims=True)
    i1 = jnp.min(jnp.where(el == m1, lane, ROUTER_PAD), axis=-1, keepdims=True)
    el2 = jnp.where(lane == i1, -jnp.inf, el)
    m2 = jnp.max(el2, axis=-1, keepdims=True)
    i2 = jnp.min(jnp.where(el2 == m2, lane, ROUTER_PAD), axis=-1, keepdims=True)
    e2 = jnp.exp(m2 - m1)
    gate1 = grp_p / (1.0 + e2)
    gate2 = gate1 * e2

    @pl.when((pl.program_id(0) == 0) & (pl.program_id(1) == 0))
    def _():
        cnt_ref[...] = jnp.zeros_like(cnt_ref)

    tm = lg.shape[0]
    oh = jnp.where((lane == i1) | (lane == i2), 1.0, 0.0)
    ti = lax.broadcasted_iota(jnp.int32, (tm, tm), 0)
    tj = lax.broadcasted_iota(jnp.int32, (tm, tm), 1)
    before = jnp.where(tj < ti, 1.0, 0.0).astype(BF16)
    pref = jnp.dot(before, oh.astype(BF16), preferred_element_type=F32) + cnt_ref[...]
    rank1 = jnp.sum(jnp.where(lane == i1, pref, 0.0), axis=-1, keepdims=True)
    rank2 = jnp.sum(jnp.where(lane == i2, pref, 0.0), axis=-1, keepdims=True)
    cnt_ref[...] = cnt_ref[...] + jnp.sum(oh, axis=0, keepdims=True)
    info = jnp.where(lane == 0, (i1 - N_EXPERT_GROUPS).astype(F32), 0.0)
    info = jnp.where(lane == 1, (i2 - N_EXPERT_GROUPS).astype(F32), info)
    info = jnp.where(lane == 2, rank1, info)
    info = jnp.where(lane == 3, rank2, info)
    info = jnp.where(lane == 4, gate1, info)
    info = jnp.where(lane == 5, gate2, info)
    lg_ref[0] = info


def _merge(att, o_g, p3, x, w_att_out, w_rwkv_out, w_out, gain2, w_router, b_router):
    B, S, D = x.shape
    tm = 512
    full = lambda shape: pl.BlockSpec(shape, lambda b, j: (0,) * len(shape))
    return pl.pallas_call(
        _merge_kernel,
        out_shape=[jax.ShapeDtypeStruct((B, S, D), F32), jax.ShapeDtypeStruct((B, S, D), BF16),
                   jax.ShapeDtypeStruct((B, S, ROUTER_PAD), F32),
                   jax.ShapeDtypeStruct((1, ROUTER_PAD), F32)],
        grid=(B, S // tm),
        in_specs=[pl.BlockSpec((1, tm, ATT_OUT_WIDTH), lambda b, j: (b, j, 0)),
                  pl.BlockSpec((1, RWKV_HEADS, tm, RWKV_HEAD_DIM), lambda b, j: (b, 0, j, 0)),
                  pl.BlockSpec((1, tm, 2 * D_MODEL), lambda b, j: (b, j, 1)),
                  pl.BlockSpec((1, tm, D), lambda b, j: (b, j, 0)),
                  full((ATT_OUT_WIDTH, D)), full((RWKV_HEADS, RWKV_HEAD_DIM, D)), full((D, D)),
                  full((1, D)), full((2, D, ROUTER_PAD)), full((1, ROUTER_PAD))],
        out_specs=[pl.BlockSpec((1, tm, D), lambda b, j: (b, j, 0)),
                   pl.BlockSpec((1, tm, D), lambda b, j: (b, j, 0)),
                   pl.BlockSpec((1, tm, ROUTER_PAD), lambda b, j: (b, j, 0)),
                   pl.BlockSpec((1, ROUTER_PAD), lambda b, j: (0, 0))],
        compiler_params=pltpu.CompilerParams(
            dimension_semantics=("arbitrary", "arbitrary"), vmem_limit_bytes=VMEM_LIMIT),
        name="merge_outproj_router",
    )(att, o_g, p3, x, w_att_out, w_rwkv_out, w_out, gain2, w_router, b_router)


def _expert_kernel(be_ref, nb_ref, x_ref, wg_ref, wu_ref, wd_ref, y_ref):
    @pl.when(pl.program_id(0) < nb_ref[0])
    def _():
        xb = x_ref[...]
        hg = jnp.dot(xb, wg_ref[0].astype(BF16), preferred_element_type=F32)
        hu = jnp.dot(xb, wu_ref[0].astype(BF16), preferred_element_type=F32)
        hid = hg * _sigmoid(hg) * hu
        y_ref[...] = jnp.dot(hid.astype(BF16), wd_ref[0].astype(BF16), preferred_element_type=F32)

    @pl.when(pl.program_id(0) >= nb_ref[0])
    def _():
        y_ref[...] = jnp.zeros_like(y_ref)


def _experts(blk_expert, n_used, xin, w_gate, w_up, w_down):
    n_rows, D = xin.shape
    n_blocks = n_rows // MOE_ROWS
    return pl.pallas_call(
        _expert_kernel,
        out_shape=jax.ShapeDtypeStruct((n_rows, D), F32),
        grid_spec=pltpu.PrefetchScalarGridSpec(
            num_scalar_prefetch=2,
            grid=(n_blocks,),
            in_specs=[pl.BlockSpec((MOE_ROWS, D), lambda i, be, nb: (i, 0)),
                      pl.BlockSpec((1, D, D_EXPERT), lambda i, be, nb: (be[i], 0, 0)),
                      pl.BlockSpec((1, D, D_EXPERT), lambda i, be, nb: (be[i], 0, 0)),
                      pl.BlockSpec((1, D_EXPERT, D), lambda i, be, nb: (be[i], 0, 0))],
            out_specs=pl.BlockSpec((MOE_ROWS, D), lambda i, be, nb: (i, 0))),
        compiler_params=pltpu.CompilerParams(
            dimension_semantics=("arbitrary",), vmem_limit_bytes=VMEM_LIMIT),
        name="moe_experts",
    )(blk_expert, n_used, xin, w_gate, w_up, w_down)


def _row_layout(info, counts, T):
    A = T * TOP_K
    eid = info[:, 0:2].astype(jnp.int32)
    rank = info[:, 2:4].astype(jnp.int32)
    gates = info[:, 4:6]
    counts = counts.astype(jnp.int32)
    pcounts = (counts + MOE_ROWS - 1) // MOE_ROWS * MOE_ROWS
    pends = jnp.cumsum(pcounts)
    pstarts = pends - pcounts
    dest = (pstarts[eid] + rank).astype(jnp.int32)
    n_blocks = -(-A // MOE_ROWS) + N_EXPERTS
    n_rows = n_blocks * MOE_ROWS
    tok = jnp.broadcast_to(jnp.arange(T, dtype=jnp.int32)[:, None], (T, TOP_K))
    row_tok = jnp.full((n_rows,), T, jnp.int32).at[dest.reshape(A)].set(tok.reshape(A))
    blk_start = jnp.arange(n_blocks, dtype=jnp.int32) * MOE_ROWS
    blk_expert = jnp.minimum(jnp.sum(blk_start[:, None] >= pends[None, :], axis=1), N_EXPERTS - 1)
    n_used = (pends[N_EXPERTS - 1] // MOE_ROWS).astype(jnp.int32)[None]
    return row_tok, blk_expert.astype(jnp.int32), n_used, dest, gates


def _pad_rows(w, rows):
    return jnp.pad(w, ((0, rows - w.shape[0]), (0, 0)))


def kernel(x, norm1_gain, w_in, q_norm_gain, k_norm_gain, rel_bias_table, w_att_out, rwkv_shift_mu,
           rwkv_w0, rwkv_w_up, rwkv_a0, rwkv_a_up, rwkv_g_up, rwkv_k_k, rwkv_k_a, rwkv_r_k,
           rwkv_ln_w, rwkv_ln_b, w_rwkv_out, w_out, norm2_gain, w_group_router, b_group_router,
           w_expert_router, b_expert_router, w_expert_gate, w_expert_up, w_expert_down):
    B, S, D = x.shape
    T = B * S
    C = RWKV_WIDTH
    bias = _attention_bias(rel_bias_table)
    for l in range(norm1_gain.shape[0]):
        wi = w_in[l]
        w_qkv = wi[:, :3 * ATT_WIDTH]
        w_rw = wi[:, 3 * ATT_WIDTH:3 * ATT_WIDTH + RWKV_SHIFT_WIDTH]
        w_gt = wi[:, 3 * ATT_WIDTH + RWKV_SHIFT_WIDTH:]
        pad_cols = lambda w, n: jnp.pad(w, ((0, 0), (0, n - w.shape[1])))
        o_w, o_a, o_g = 3 * C, 3 * C + DECAY_LORA, 3 * C + DECAY_LORA + ICLR_LORA

        def rw_layout(w):
            return jnp.concatenate([w[:, :o_w], pad_cols(w[:, o_w:o_a], LANES),
                                    pad_cols(w[:, o_a:o_g], LANES), pad_cols(w[:, o_g:], 2 * LANES)], axis=1)

        w_all = jnp.concatenate([rw_layout(w_rw), w_gt, w_qkv], axis=1).astype(BF16)
        mu = rw_layout(rwkv_shift_mu[l][None, :])

        p = _inproj(x.reshape(T, D), norm1_gain[l][None, :], w_all)
        p3 = p.reshape(B, S, P_WIDTH)

        tile2 = lambda gmat: jnp.tile(gmat, (1, 2))[:, None, :]
        att = _attention(p3, tile2(q_norm_gain[l]), tile2(k_norm_gain[l]), bias)

        row = lambda v: v[None, :]
        r, lw, k2, v, a_vec, b_vec, g = _rwkv_prep(
            p3, mu, row(rwkv_w0[l]), _pad_rows(rwkv_w_up[l], LANES).astype(BF16), row(rwkv_a0[l]),
            _pad_rows(rwkv_a_up[l], LANES).astype(BF16), _pad_rows(rwkv_g_up[l], 2 * LANES).astype(BF16),
            row(rwkv_k_k[l]), row(rwkv_k_a[l]))
        hp = lambda v: v.reshape(RWKV_HEADS, 1, RWKV_HEAD_DIM)
        o_g = _wkv(r, lw, k2, v, a_vec, b_vec, g, hp(rwkv_ln_w[l]), hp(rwkv_ln_b[l]), hp(rwkv_r_k[l]))

        w_router = jnp.concatenate([w_group_router[l], w_expert_router[l]], axis=1)
        w_router = jnp.pad(w_router, ((0, 0), (0, ROUTER_PAD - w_router.shape[1])))
        wr_hi = w_router.astype(BF16)
        wr_lo = (w_router - wr_hi.astype(F32)).astype(BF16)
        b_router = jnp.concatenate([b_group_router[l], b_expert_router[l]])
        b_router = jnp.pad(b_router, (0, ROUTER_PAD - b_router.shape[0]))[None, :]
        x1, h2, info, counts = _merge(
            att, o_g, p3, x, w_att_out[l].astype(BF16),
            w_rwkv_out[l].astype(BF16).reshape(RWKV_HEADS, RWKV_HEAD_DIM, D), w_out[l].astype(BF16),
            norm2_gain[l][None, :], jnp.stack([wr_hi, wr_lo]), b_router)

        row_tok, blk_expert, n_used, dest, gates = _row_layout(
            info.reshape(T, ROUTER_PAD), counts[0, N_EXPERT_GROUPS:N_EXPERT_GROUPS + N_EXPERTS], T)
        h2_pad = jnp.concatenate([h2.reshape(T, D), jnp.zeros((1, D), BF16)], axis=0)
        xin = h2_pad[row_tok]
        yb = _experts(blk_expert, n_used, xin, w_expert_gate[l], w_expert_up[l], w_expert_down[l])
        y = gates[:, 0:1] * yb[dest[:, 0]] + gates[:, 1:2] * yb[dest[:, 1]]
        x = x1 + y.reshape(B, S, D)
    return x
```

```python
import functools
import math

import jax
import jax.numpy as jnp
from jax import lax
from jax.experimental import pallas as pl
from jax.experimental.pallas import tpu as pltpu

F32 = jnp.float32
BF16 = jnp.bfloat16

D_MODEL = 1024
ATT_PATTERNS = ((128, 1), (512, 4), (2048, 16))
N_ATT_GROUPS = 3
HEADS_PER_GROUP = 4
ATT_HEADS = 12
HEAD_DIM = 64
ATT_WIDTH = 768
ATT_OUT_WIDTH = 256
REL_BUCKETS = 32
REL_MAX_DIST = 2048
RWKV_HEADS = 8
RWKV_HEAD_DIM = 64
RWKV_WIDTH = 512
DECAY_LORA = 64
ICLR_LORA = 64
GATE_LORA = 160
RWKV_SHIFT_WIDTH = 1824
GN_EPS = 64e-5
N_EXPERT_GROUPS = 4
EXPERTS_PER_GROUP = 8
N_EXPERTS = 32
TOP_K = 2
D_EXPERT = 512
NORM_EPS = 1e-6
NEG_INF = -1e30

LANES = 128
ATT_BLOCK = 128
WKV_CHUNK = 64
RWKV_PAD_WIDTH = 2048
P_WIDTH = 6400
QKV_COL_BLOCK = (RWKV_PAD_WIDTH + 2 * D_MODEL) // LANES
ROUTER_PAD = 128
MOE_ROWS = 256
VMEM_LIMIT = 56 * 1024 * 1024


def _sigmoid(x):
    return 1.0 / (1.0 + jnp.exp(-x))


def _dot(a, b):
    return jnp.dot(a.astype(BF16), b.astype(BF16), preferred_element_type=F32)


def _dot_t(a, b):
    return lax.dot_general(a.astype(BF16), b.astype(BF16), (((1,), (1,)), ((), ())),
                           preferred_element_type=F32)


def _split3(a):
    hi = a.astype(BF16)
    r1 = a - hi.astype(F32)
    mid = r1.astype(BF16)
    lo = (r1 - mid.astype(F32)).astype(BF16)
    return hi, mid, lo


def _split2(a):
    hi = a.astype(BF16)
    lo = (a - hi.astype(F32)).astype(BF16)
    return hi, lo


def _inproj_kernel(x_ref, g_ref, w_ref, o_ref, h_ref):
    @pl.when(pl.program_id(1) == 0)
    def _():
        x = x_ref[...]
        ms = jnp.mean(x * x, axis=-1, keepdims=True)
        h_ref[...] = (x * lax.rsqrt(ms + NORM_EPS) * g_ref[...]).astype(BF16)

    o_ref[...] = jnp.dot(h_ref[...], w_ref[...], preferred_element_type=F32)


def _inproj(x2, gain, w_all):
    T = x2.shape[0]
    tm, tn = 1024, 1280
    return pl.pallas_call(
        _inproj_kernel,
        out_shape=jax.ShapeDtypeStruct((T, P_WIDTH), F32),
        grid=(T // tm, P_WIDTH // tn),
        in_specs=[pl.BlockSpec((tm, D_MODEL), lambda i, j: (i, 0)),
                  pl.BlockSpec((1, D_MODEL), lambda i, j: (0, 0)),
                  pl.BlockSpec((D_MODEL, tn), lambda i, j: (0, j))],
        out_specs=pl.BlockSpec((tm, tn), lambda i, j: (i, j)),
        scratch_shapes=[pltpu.VMEM((tm, D_MODEL), BF16)],
        compiler_params=pltpu.CompilerParams(
            dimension_semantics=("arbitrary", "arbitrary"), vmem_limit_bytes=VMEM_LIMIT),
        name="inproj",
    )(x2, gain, w_all)


def _attn_kernel(q0, q1, q2, k0, k1, k2, v0, v1, v2, qg_ref, kg_ref, bias_ref, out_ref,
                 qn_s, kn_s, acc_s, m_s, l_s):
    S = out_ref.shape[1]
    q_refs, k_refs, v_refs = (q0, q1, q2), (k0, k1, k2), (v0, v1, v2)
    lane = lax.broadcasted_iota(jnp.int32, (1, LANES), 1)
    lo = lane < HEAD_DIM
    col = lax.broadcasted_iota(jnp.int32, (1, 2 * ATT_BLOCK), 1)

    def pair_norm(x, gain):
        x2 = x * x
        sa = jnp.sum(jnp.where(lo, x2, 0.0), axis=-1, keepdims=True)
        sb = jnp.sum(jnp.where(lo, 0.0, x2), axis=-1, keepdims=True)
        ms = jnp.where(lo, sa, sb) * (1.0 / HEAD_DIM)
        return x * lax.rsqrt(ms + NORM_EPS) * gain

    rows_per = 256

    def norm_body(i, c):
        rs = pl.ds(pl.multiple_of(i * rows_per, rows_per), rows_per)
        for g in range(N_ATT_GROUPS):
            qn_s[g, rs, :] = pair_norm(q_refs[g][0, rs, :], qg_ref[g]) * (HEAD_DIM ** -0.5)
            kn_s[g, rs, :] = pair_norm(k_refs[g][0, rs, :], kg_ref[g])
        return c

    lax.fori_loop(0, S // rows_per, norm_body, 0)

    for g, (window, d) in enumerate(ATT_PATTERNS):
        n_blk = (S // d) // ATT_BLOCK
        has_prev = n_blk > 1

        def rows_at(start, d=d):
            if d == 1:
                return pl.ds(pl.multiple_of(start, ATT_BLOCK), ATT_BLOCK)
            return pl.ds(start, ATT_BLOCK, stride=d)

        def blk_body(idx, c, g=g, d=d, has_prev=has_prev, rows_at=rows_at):
            r = idx % d
            n = idx // d
            cur = rows_at(n * (ATT_BLOCK * d) + r)
            q = qn_s[g, cur, :]
            q2 = jnp.concatenate([jnp.where(lo, q, 0.0), jnp.where(lo, 0.0, q)], axis=0)
            kc = kn_s[g, cur, :]
            vc = v_refs[g][0, cur, :]
            if has_prev:
                prv = rows_at(jnp.maximum(n - 1, 0) * (ATT_BLOCK * d) + r)
                kcat = jnp.concatenate([kn_s[g, prv, :], kc], axis=0)
                vcat = jnp.concatenate([v_refs[g][0, prv, :], vc], axis=0)
                s = _dot_t(q2, kcat) + bias_ref[g, 0]
                s = s + jnp.where(col < ATT_BLOCK, jnp.where(n == 0, NEG_INF, 0.0), 0.0)
            else:
                kcat, vcat = kc, vc
                s = _dot_t(q2, kcat) + bias_ref[g, 0, :, ATT_BLOCK:]
            m = jnp.max(s, axis=-1, keepdims=True)
            p = jnp.exp(s - m)
            l = jnp.sum(p, axis=-1, keepdims=True)
            o = _dot(p, vcat)
            acc_s[g, cur, :] = jnp.where(lo, o[:ATT_BLOCK], o[ATT_BLOCK:])
            m_s[g, cur, :] = jnp.where(lo, m[:ATT_BLOCK], m[ATT_BLOCK:])
            l_s[g, cur, :] = jnp.where(lo, l[:ATT_BLOCK], l[ATT_BLOCK:])
            return c

        lax.fori_loop(0, d * n_blk, blk_body, 0, unroll=2)

    def comb_body(i, c):
        rs = pl.ds(pl.multiple_of(i * rows_per, rows_per), rows_per)
        m = jnp.maximum(jnp.maximum(m_s[0, rs, :], m_s[1, rs, :]), m_s[2, rs, :])
        num = jnp.zeros((rows_per, LANES), F32)
        den = jnp.zeros((rows_per, LANES), F32)
        for g in range(N_ATT_GROUPS):
            e = jnp.exp(m_s[g, rs, :] - m)
            num = num + e * acc_s[g, rs, :]
            den = den + e * l_s[g, rs, :]
        out_ref[0, rs, :] = num / den
        return c

    lax.fori_loop(0, S // rows_per, comb_body, 0)


def _attention(p3, q_gain, k_gain, bias):
    B, S, _ = p3.shape
    n_pairs = HEADS_PER_GROUP // 2

    def col_spec(base):
        return [pl.BlockSpec((1, S, LANES),
                             functools.partial(lambda b, sp, off: (b, 0, off + sp), off=base + g * n_pairs))
                for g in range(N_ATT_GROUPS)]

    n_qkv_blocks = ATT_WIDTH // LANES
    in_specs = (col_spec(QKV_COL_BLOCK) + col_spec(QKV_COL_BLOCK + n_qkv_blocks)
                + col_spec(QKV_COL_BLOCK + 2 * n_qkv_blocks)
                + [pl.BlockSpec((N_ATT_GROUPS, 1, LANES), lambda b, sp: (0, 0, 0)),
                   pl.BlockSpec((N_ATT_GROUPS, 1, LANES), lambda b, sp: (0, 0, 0)),
                   pl.BlockSpec((N_ATT_GROUPS, 1, 2 * ATT_BLOCK, 2 * ATT_BLOCK),
                                lambda b, sp: (0, sp, 0, 0))])
    return pl.pallas_call(
        _attn_kernel,
        out_shape=jax.ShapeDtypeStruct((B, S, ATT_OUT_WIDTH), F32),
        grid=(B, n_pairs),
        in_specs=in_specs,
        out_specs=pl.BlockSpec((1, S, LANES), lambda b, sp: (b, 0, sp)),
        scratch_shapes=[pltpu.VMEM((N_ATT_GROUPS, S, LANES), F32) for _ in range(5)],
        compiler_params=pltpu.CompilerParams(
            dimension_semantics=("arbitrary", "arbitrary"), vmem_limit_bytes=VMEM_LIMIT),
        name="dilated_attention",
    )(*([p3] * 9), q_gain, k_gain, bias)


def _t5_causal_bucket(dist):
    max_exact = REL_BUCKETS // 2
    d = jnp.maximum(dist.astype(F32), 1.0)
    large = max_exact + (jnp.log(d / max_exact) / math.log(REL_MAX_DIST / max_exact)
                         * (REL_BUCKETS - max_exact)).astype(jnp.int32)
    large = jnp.minimum(large, REL_BUCKETS - 1)
    return jnp.where(dist < max_exact, dist, large)


def _attention_bias(rel_bias_table):
    W = ATT_BLOCK
    qi = jnp.arange(W)[:, None]
    kj = jnp.arange(2 * W)[None, :]
    rel = qi + W - kj
    valid = (rel >= 0) & (rel <= W)
    per_group = []
    for g, (_, d) in enumerate(ATT_PATTERNS):
        tab = rel_bias_table[:, g * HEADS_PER_GROUP:(g + 1) * HEADS_PER_GROUP]
        bucket = _t5_causal_bucket(jnp.clip(rel, 0, W) * d)
        onehot = (bucket[:, :, None] == jnp.arange(REL_BUCKETS)[None, None, :]).astype(F32)
        bias = jnp.einsum('qkb,bh->qkh', onehot, tab.astype(F32),
                          precision=lax.Precision.HIGHEST)
        bias = jnp.where(valid[:, :, None], bias, NEG_INF)
        per_group.append(jnp.transpose(bias, (2, 0, 1)).reshape(2, 2 * W, 2 * W))
    return jnp.stack(per_group, axis=0)


def _rwkv_prep_kernel(p_ref, mu_ref, w0_ref, wup_ref, a0_ref, aup_ref, gup_ref, kk_ref, ka_ref, rk_ref,
                      at_o, rt_o, bt_o, kt_o, bw_o, kw_o, v_o, bonus_o, g_o, wend_o, carry):
    tq = p_ref.shape[1]
    C = RWKV_WIDTH
    L = WKV_CHUNK

    @pl.when(pl.program_id(1) == 0)
    def _():
        carry[...] = jnp.zeros_like(carry)

    p = p_ref[0]
    row = lax.broadcasted_iota(jnp.int32, (tq, 1), 0)
    prev = jnp.where(row == 0, carry[...], pltpu.roll(p, 1, axis=0))
    carry[...] = p[tq - 1:tq, :]
    pm = p + (prev - p) * mu_ref[...]
    r, k, v = pm[:, 0:C], pm[:, C:2 * C], pm[:, 2 * C:3 * C]
    xw = pm[:, 3 * C:3 * C + LANES]
    xa = pm[:, 3 * C + LANES:3 * C + 2 * LANES]
    xg = pm[:, 3 * C + 2 * LANES:]

    z = w0_ref[...] + _dot(jnp.tanh(xw), wup_ref[...])
    softplus_neg = jnp.maximum(-z, 0.0) + jnp.log(1.0 + jnp.exp(-jnp.abs(z)))
    lw = -jnp.exp(-softplus_neg - 0.5)
    a = _sigmoid(a0_ref[...] + _dot(xa, aup_ref[...]))
    g = _dot(_sigmoid(xg), gup_ref[...])

    lane = lax.broadcasted_iota(jnp.int32, (1, LANES), 1)
    lo = lane < RWKV_HEAD_DIM

    def head_sums(x):
        sa = jnp.sum(jnp.where(lo, x, 0.0), axis=-1, keepdims=True)
        sb = jnp.sum(jnp.where(lo, 0.0, x), axis=-1, keepdims=True)
        return jnp.where(lo, sa, sb)

    k2 = k * (1.0 + (a - 1.0) * ka_ref[...])
    kk = k * kk_ref[...]
    rk = r * k2 * rk_ref[...]
    kk_n, bonus = [], []
    for j in range(C // LANES):
        sl = slice(j * LANES, (j + 1) * LANES)
        x = kk[:, sl]
        kk_n.append(x / jnp.maximum(jnp.sqrt(head_sums(x * x)), 1e-12))
        bonus.append(head_sums(rk[:, sl]) * v[:, sl])
    kk = jnp.concatenate(kk_n, axis=-1)
    bonus = jnp.concatenate(bonus, axis=-1)
    b = kk * a

    half = 256
    ri = lax.broadcasted_iota(jnp.int32, (half, half), 0)
    ci = lax.broadcasted_iota(jnp.int32, (half, half), 1)
    tri = jnp.where((jnp.right_shift(ri, 6) == jnp.right_shift(ci, 6)) & (ri >= ci), 1.0, 0.0).astype(BF16)
    cums = []
    for s in range(tq // half):
        hi, mid, lo3 = _split3(lw[s * half:(s + 1) * half, :])
        cums.append(jnp.dot(tri, hi, preferred_element_type=F32)
                    + (jnp.dot(tri, mid, preferred_element_type=F32)
                       + jnp.dot(tri, lo3, preferred_element_type=F32)))
    cum = jnp.concatenate(cums, axis=0)
    n_chunks = tq // L
    ends = [cum[c * L + L - 1:c * L + L, :] for c in range(n_chunks)]
    tot = jnp.concatenate([jnp.broadcast_to(e, (L, C)) for e in ends], axis=0)
    wend = jnp.exp(jnp.concatenate(ends, axis=0))

    e_pos = jnp.exp(cum)
    e_neg = jnp.exp(-cum)
    e_rem = jnp.exp(tot - cum)
    at = -kk * jnp.exp(cum - lw)
    rt = r * e_pos
    bt = b * e_neg
    kt = k2 * e_neg
    bw = b * e_rem
    kw = k2 * e_rem
    for h in range(RWKV_HEADS):
        sl = slice(h * RWKV_HEAD_DIM, (h + 1) * RWKV_HEAD_DIM)
        at_o[0, h] = at[:, sl].astype(BF16)
        rt_o[0, h] = rt[:, sl].astype(BF16)
        bt_o[0, h] = bt[:, sl].astype(BF16)
        kt_o[0, h] = kt[:, sl].astype(BF16)
        bw_o[0, h] = bw[:, sl].astype(BF16)
        kw_o[0, h] = kw[:, sl].astype(BF16)
        v_o[0, h] = v[:, sl].astype(BF16)
        bonus_o[0, h] = bonus[:, sl]
        g_o[0, h] = g[:, sl]
        wend_o[0, h] = wend[:, sl]


def _rwkv_prep(p3, mu, w0, wup, a0, aup, gup, k_k, k_a, r_k):
    B, S, _ = p3.shape
    tq = 512
    C = RWKV_WIDTH
    H, N = RWKV_HEADS, RWKV_HEAD_DIM
    hm = lambda dt: jax.ShapeDtypeStruct((B, H, S, N), dt)
    hm_spec = pl.BlockSpec((1, H, tq, N), lambda b, j: (b, 0, j, 0))
    full = lambda shape: pl.BlockSpec(shape, lambda b, j: (0,) * len(shape))
    return pl.pallas_call(
        _rwkv_prep_kernel,
        out_shape=[hm(BF16)] * 7 + [hm(F32)] * 2 + [jax.ShapeDtypeStruct((B, H, S // WKV_CHUNK, N), F32)],
        grid=(B, S // tq),
        in_specs=[pl.BlockSpec((1, tq, RWKV_PAD_WIDTH), lambda b, j: (b, j, 0)),
                  full((1, RWKV_PAD_WIDTH)), full((1, C)), full((LANES, C)), full((1, C)),
                  full((LANES, C)), full((2 * LANES, C)), full((1, C)), full((1, C)), full((1, C))],
        out_specs=[hm_spec] * 9 + [pl.BlockSpec((1, H, tq // WKV_CHUNK, N), lambda b, j: (b, 0, j, 0))],
        scratch_shapes=[pltpu.VMEM((1, RWKV_PAD_WIDTH), F32)],
        compiler_params=pltpu.CompilerParams(
            dimension_semantics=("arbitrary", "arbitrary"), vmem_limit_bytes=VMEM_LIMIT),
        name="rwkv_prep",
    )(p3, mu, w0, wup, a0, aup, gup, k_k, k_a, r_k)


def _bmm(a, b):
    return lax.dot_general(a.astype(BF16), b.astype(BF16), (((2,), (1,)), ((0,), (0,))),
                           preferred_element_type=F32)


def _bmm_nt(a, b):
    return lax.dot_general(a.astype(BF16), b.astype(BF16), (((2,), (2,)), ((0,), (0,))),
                           preferred_element_type=F32)


def _bmm_tn(a, b):
    return lax.dot_general(a.astype(BF16), b.astype(BF16), (((1,), (1,)), ((0,), (0,))),
                           preferred_element_type=F32)


def _wkv_kernel(at_ref, rt_ref, bt_ref, kt_ref, bw_ref, kw_ref, v_ref, bonus_ref, g_ref, wend_ref,
                lnw_ref, lnb_ref, o_ref, state):
    H, N, C = RWKV_HEADS, RWKV_HEAD_DIM, WKV_CHUNK
    n_sub = at_ref.shape[2] // C
    j = pl.program_id(1)

    @pl.when(j == 0)
    def _():
        state[...] = jnp.zeros_like(state)

    ri = lax.broadcasted_iota(jnp.int32, (1, 2 * C, 2 * C), 1)
    ci = lax.broadcasted_iota(jnp.int32, (1, 2 * C, 2 * C), 2)
    cm = jnp.bitwise_and(ci, C - 1)
    band = ((ri < C) & (ri > cm)) | ((ri >= C) & ((ri - C) >= cm))
    right = lax.broadcasted_iota(jnp.int32, (1, C, 2 * C), 2) >= C
    eye = jnp.where(lax.broadcasted_iota(jnp.int32, (1, C, C), 1)
                    == lax.broadcasted_iota(jnp.int32, (1, C, C), 2), 1.0, 0.0).astype(F32)

    def sub(i, c):
        rs = pl.ds(pl.multiple_of(i * C, C), C)
        v = v_ref[0, :, rs, :]
        ar = jnp.concatenate([at_ref[0, :, rs, :], rt_ref[0, :, rs, :]], axis=1)
        bk = jnp.concatenate([bt_ref[0, :, rs, :], kt_ref[0, :, rs, :]], axis=1)
        bkw = jnp.concatenate([bw_ref[0, :, rs, :], kw_ref[0, :, rs, :]], axis=1)
        gm = jnp.where(band, _bmm_nt(ar, bk), 0.0)
        a_ab = gm[:, :C, :C]
        inv = eye + a_ab
        pw = _bmm(a_ab, a_ab)
        for _ in range(int(math.log2(C)) - 2):
            prod = _bmm(jnp.concatenate([inv, pw], axis=1), pw)
            inv = inv + prod[:, :C, :]
            pw = prod[:, C:, :]
        inv = inv + _bmm(inv, pw)
        s0 = state[...]
        ar_s = _bmm(ar, s0)
        x = ar_s[:, :C, :] + _bmm(jnp.where(right, gm[:, :C, :], 0.0), jnp.concatenate([v, v], axis=1))
        u = _bmm(inv, x)
        uv = jnp.concatenate([u.astype(BF16), v], axis=1)
        o = ar_s[:, C:, :] + _bmm(gm[:, C:, :], uv)
        w_end = wend_ref[0, :, pl.ds(j * n_sub + i, 1), :]
        state[...] = s0 * jnp.swapaxes(w_end, 1, 2) + _bmm_tn(bkw, uv)

        mu = jnp.mean(o, axis=-1, keepdims=True)
        var = jnp.mean(jnp.square(o - mu), axis=-1, keepdims=True)
        on = (o - mu) * lax.rsqrt(var + GN_EPS) * lnw_ref[...] + lnb_ref[...]
        o_ref[0, :, rs, :] = (on + bonus_ref[0, :, rs, :]) * g_ref[0, :, rs, :]
        return c

    lax.fori_loop(0, n_sub, sub, 0)


def _wkv(at, rt, bt, kt, bw, kw, v, bonus, g, wend, ln_w, ln_b):
    B, H, S, N = at.shape
    tc = 512
    hm_spec = pl.BlockSpec((1, H, tc, N), lambda bb, j: (bb, 0, j, 0))
    par_spec = pl.BlockSpec((H, 1, N), lambda bb, j: (0, 0, 0))
    return pl.pallas_call(
        _wkv_kernel,
        out_shape=jax.ShapeDtypeStruct((B, H, S, N), F32),
        grid=(B, S // tc),
        in_specs=([hm_spec] * 9 + [pl.BlockSpec((1, H, S // WKV_CHUNK, N), lambda bb, j: (bb, 0, 0, 0))]
                  + [par_spec] * 2),
        out_specs=hm_spec,
        scratch_shapes=[pltpu.VMEM((H, N, N), F32)],
        compiler_params=pltpu.CompilerParams(
            dimension_semantics=("arbitrary", "arbitrary"), vmem_limit_bytes=VMEM_LIMIT),
        name="wkv7",
    )(at, rt, bt, kt, bw, kw, v, bonus, g, wend, ln_w, ln_b)


def _merge_kernel(att_ref, o_ref, gate_ref, x_ref, wa_ref, wr_ref, wo_ref, g2_ref, wrt_ref, brt_ref,
                  x1_ref, h2_ref, lg_ref, cnt_ref):
    y_att = _dot(att_ref[0], wa_ref[...])
    y_rwkv = _dot(o_ref[0, 0], wr_ref[0])
    for h in range(1, RWKV_HEADS):
        y_rwkv = y_rwkv + _dot(o_ref[0, h], wr_ref[h])
    gates = _sigmoid(gate_ref[0])
    mixed = gates[:, :D_MODEL] * y_att + gates[:, D_MODEL:] * y_rwkv
    x1 = x_ref[0] + _dot(mixed, wo_ref[...])
    x1_ref[0] = x1
    ms = jnp.mean(x1 * x1, axis=-1, keepdims=True)
    h2 = x1 * lax.rsqrt(ms + NORM_EPS) * g2_ref[...]
    hh, hl = _split2(h2)
    half = D_MODEL // 2
    rounded = hh.astype(F32)
    w_hi = pltpu.bitcast(rounded[:, :half], jnp.uint32)
    w_lo = pltpu.bitcast(rounded[:, half:], jnp.uint32)
    h2_ref[0] = w_hi | jnp.right_shift(w_lo, jnp.uint32(16))
    wh, wl = wrt_ref[0], wrt_ref[1]
    lg = (jnp.dot(hh, wh, preferred_element_type=F32)
          + (jnp.dot(hh, wl, preferred_element_type=F32) + jnp.dot(hl, wh, preferred_element_type=F32)))
    lg = lg + brt_ref[...]

    lane = lax.broadcasted_iota(jnp.int32, (1, ROUTER_PAD), 1)
    gmask = lane < N_EXPERT_GROUPS
    gl = jnp.where(gmask, lg, -jnp.inf)
    gmax = jnp.max(gl, axis=-1, keepdims=True)
    grp_idx = jnp.min(jnp.where(gl == gmax, lane, ROUTER_PAD), axis=-1, keepdims=True)
    grp_p = 1.0 / jnp.sum(jnp.where(gmask, jnp.exp(lg - gmax), 0.0), axis=-1, keepdims=True)
    e_lane = lane - N_EXPERT_GROUPS
    emask = ((e_lane >= 0) & (e_lane < N_EXPERTS)
             & (jnp.right_shift(e_lane, int(math.log2(EXPERTS_PER_GROUP))) == grp_idx))
    el = jnp.where(emask, lg, -jnp.inf)
    m1 = jnp.max(el, axis=-1, keepdims=True)
    i1 = jnp.min(jnp.where(el == m1, lane, ROUTER_PAD), axis=-1, keepdims=True)
    el2 = jnp.where(lane == i1, -jnp.inf, el)
    m2 = jnp.max(el2, axis=-1, keepdims=True)
    i2 = jnp.min(jnp.where(el2 == m2, lane, ROUTER_PAD), axis=-1, keepdims=True)
    e2 = jnp.exp(m2 - m1)
    gate1 = grp_p / (1.0 + e2)
    gate2 = gate1 * e2

    @pl.when((pl.program_id(0) == 0) & (pl.program_id(1) == 0))
    def _():
        cnt_ref[...] = jnp.zeros_like(cnt_ref)

    tm = lg.shape[0]
    oh = jnp.where((lane == i1) | (lane == i2), 1.0, 0.0)
    ti = lax.broadcasted_iota(jnp.int32, (tm, tm), 0)
    tj = lax.broadcasted_iota(jnp.int32, (tm, tm), 1)
    before = jnp.where(tj < ti, 1.0, 0.0).astype(BF16)
    pref = jnp.dot(before, oh.astype(BF16), preferred_element_type=F32) + cnt_ref[...]
    rank1 = jnp.sum(jnp.where(lane == i1, pref, 0.0), axis=-1, keepdims=True)
    rank2 = jnp.sum(jnp.where(lane == i2, pref, 0.0), axis=-1, keepdims=True)
    cnt_ref[...] = cnt_ref[...] + jnp.sum(oh, axis=0, keepdims=True)
    info = jnp.where(lane == 0, (i1 - N_EXPERT_GROUPS).astype(F32), 0.0)
    info = jnp.where(lane == 1, (i2 - N_EXPERT_GROUPS).astype(F32), info)
    info = jnp.where(lane == 2, rank1, info)
    info = jnp.where(lane == 3, rank2, info)
    info = jnp.where(lane == 4, gate1, info)
    info = jnp.where(lane == 5, gate2, info)
    lg_ref[0] = info


def _merge(att, o_g, p3, x, w_att_out, w_rwkv_out, w_out, gain2, w_router, b_router):
    B, S, D = x.shape
    tm = 512
    full = lambda shape: pl.BlockSpec(shape, lambda b, j: (0,) * len(shape))
    return pl.pallas_call(
        _merge_kernel,
        out_shape=[jax.ShapeDtypeStruct((B, S, D), F32), jax.ShapeDtypeStruct((B, S, D // 2), jnp.uint32),
                   jax.ShapeDtypeStruct((B, S, ROUTER_PAD), F32),
                   jax.ShapeDtypeStruct((1, ROUTER_PAD), F32)],
        grid=(B, S // tm),
        in_specs=[pl.BlockSpec((1, tm, ATT_OUT_WIDTH), lambda b, j: (b, j, 0)),
                  pl.BlockSpec((1, RWKV_HEADS, tm, RWKV_HEAD_DIM), lambda b, j: (b, 0, j, 0)),
                  pl.BlockSpec((1, tm, 2 * D_MODEL), lambda b, j: (b, j, 1)),
                  pl.BlockSpec((1, tm, D), lambda b, j: (b, j, 0)),
                  full((ATT_OUT_WIDTH, D)), full((RWKV_HEADS, RWKV_HEAD_DIM, D)), full((D, D)),
                  full((1, D)), full((2, D, ROUTER_PAD)), full((1, ROUTER_PAD))],
        out_specs=[pl.BlockSpec((1, tm, D), lambda b, j: (b, j, 0)),
                   pl.BlockSpec((1, tm, D // 2), lambda b, j: (b, j, 0)),
                   pl.BlockSpec((1, tm, ROUTER_PAD), lambda b, j: (b, j, 0)),
                   pl.BlockSpec((1, ROUTER_PAD), lambda b, j: (0, 0))],
        compiler_params=pltpu.CompilerParams(
            dimension_semantics=("arbitrary", "arbitrary"), vmem_limit_bytes=VMEM_LIMIT),
        name="merge_outproj_router",
    )(att, o_g, p3, x, w_att_out, w_rwkv_out, w_out, gain2, w_router, b_router)


def _expert_kernel(be_ref, nb_ref, x_ref, wg_ref, wu_ref, wd_ref, y_ref, wgu_s, wd_s):
    i = pl.program_id(0)

    @pl.when((i == 0) | (be_ref[i] != be_ref[jnp.maximum(i - 1, 0)]))
    def _():
        wgu_s[:, :D_EXPERT] = wg_ref[0].astype(BF16)
        wgu_s[:, D_EXPERT:] = wu_ref[0].astype(BF16)
        wd_s[...] = wd_ref[0].astype(BF16)

    @pl.when(i < nb_ref[0])
    def _():
        words = x_ref[...]
        x_a = pltpu.bitcast(words & jnp.uint32(0xFFFF0000), F32)
        x_b = pltpu.bitcast(jnp.left_shift(words, jnp.uint32(16)), F32)
        xb = jnp.concatenate([x_a, x_b], axis=-1).astype(BF16)
        h = jnp.dot(xb, wgu_s[...], preferred_element_type=F32)
        hg, hu = h[:, :D_EXPERT], h[:, D_EXPERT:]
        hid = hg * _sigmoid(hg) * hu
        y_ref[...] = jnp.dot(hid.astype(BF16), wd_s[...], preferred_element_type=F32)

    @pl.when(i >= nb_ref[0])
    def _():
        y_ref[...] = jnp.zeros_like(y_ref)


def _experts(blk_expert, n_used, xin, w_gate, w_up, w_down):
    n_rows = xin.shape[0]
    D = D_MODEL
    n_blocks = n_rows // MOE_ROWS
    return pl.pallas_call(
        _expert_kernel,
        out_shape=jax.ShapeDtypeStruct((n_rows, D), F32),
        grid_spec=pltpu.PrefetchScalarGridSpec(
            num_scalar_prefetch=2,
            grid=(n_blocks,),
            scratch_shapes=[pltpu.VMEM((D, 2 * D_EXPERT), BF16), pltpu.VMEM((D_EXPERT, D), BF16)],
            in_specs=[pl.BlockSpec((MOE_ROWS, D // 2), lambda i, be, nb: (i, 0)),
                      pl.BlockSpec((1, D, D_EXPERT), lambda i, be, nb: (be[i], 0, 0)),
                      pl.BlockSpec((1, D, D_EXPERT), lambda i, be, nb: (be[i], 0, 0)),
                      pl.BlockSpec((1, D_EXPERT, D), lambda i, be, nb: (be[i], 0, 0))],
            out_specs=pl.BlockSpec((MOE_ROWS, D), lambda i, be, nb: (i, 0))),
        compiler_params=pltpu.CompilerParams(
            dimension_semantics=("arbitrary",), vmem_limit_bytes=VMEM_LIMIT),
        name="moe_experts",
    )(blk_expert, n_used, xin, w_gate, w_up, w_down)


def _row_layout(info, counts, T):
    A = T * TOP_K
    eid = info[:, 0:2].astype(jnp.int32)
    rank = info[:, 2:4].astype(jnp.int32)
    gates = info[:, 4:6]
    counts = counts.astype(jnp.int32)
    pcounts = (counts + MOE_ROWS - 1) // MOE_ROWS * MOE_ROWS
    pends = jnp.cumsum(pcounts)
    pstarts = pends - pcounts
    dest = (pstarts[eid] + rank).astype(jnp.int32)
    n_blocks = -(-A // MOE_ROWS) + N_EXPERTS
    n_rows = n_blocks * MOE_ROWS
    tok = jnp.broadcast_to(jnp.arange(T, dtype=jnp.int32)[:, None], (T, TOP_K))
    row_tok = jnp.zeros((n_rows,), jnp.int32).at[dest.reshape(A)].set(tok.reshape(A))
    blk_start = jnp.arange(n_blocks, dtype=jnp.int32) * MOE_ROWS
    blk_expert = jnp.minimum(jnp.sum(blk_start[:, None] >= pends[None, :], axis=1), N_EXPERTS - 1)
    n_used = (pends[N_EXPERTS - 1] // MOE_ROWS).astype(jnp.int32)[None]
    return row_tok, blk_expert.astype(jnp.int32), n_used, dest, gates


def _pad_rows(w, rows):
    return jnp.pad(w, ((0, rows - w.shape[0]), (0, 0)))


def kernel(x, norm1_gain, w_in, q_norm_gain, k_norm_gain, rel_bias_table, w_att_out, rwkv_shift_mu,
           rwkv_w0, rwkv_w_up, rwkv_a0, rwkv_a_up, rwkv_g_up, rwkv_k_k, rwkv_k_a, rwkv_r_k,
           rwkv_ln_w, rwkv_ln_b, w_rwkv_out, w_out, norm2_gain, w_group_router, b_group_router,
           w_expert_router, b_expert_router, w_expert_gate, w_expert_up, w_expert_down):
    B, S, D = x.shape
    T = B * S
    C = RWKV_WIDTH
    bias = _attention_bias(rel_bias_table)
    for l in range(norm1_gain.shape[0]):
        wi = w_in[l]
        w_qkv = wi[:, :3 * ATT_WIDTH]
        w_rw = wi[:, 3 * ATT_WIDTH:3 * ATT_WIDTH + RWKV_SHIFT_WIDTH]
        w_gt = wi[:, 3 * ATT_WIDTH + RWKV_SHIFT_WIDTH:]
        pad_cols = lambda w, n: jnp.pad(w, ((0, 0), (0, n - w.shape[1])))
        o_w, o_a, o_g = 3 * C, 3 * C + DECAY_LORA, 3 * C + DECAY_LORA + ICLR_LORA

        def rw_layout(w):
            return jnp.concatenate([w[:, :o_w], pad_cols(w[:, o_w:o_a], LANES),
                                    pad_cols(w[:, o_a:o_g], LANES), pad_cols(w[:, o_g:], 2 * LANES)], axis=1)

        w_all = jnp.concatenate([rw_layout(w_rw), w_gt, w_qkv], axis=1).astype(BF16)
        mu = rw_layout(rwkv_shift_mu[l][None, :])

        p = _inproj(x.reshape(T, D), norm1_gain[l][None, :], w_all)
        p3 = p.reshape(B, S, P_WIDTH)

        tile2 = lambda gmat: jnp.tile(gmat, (1, 2))[:, None, :]
        att = _attention(p3, tile2(q_norm_gain[l]), tile2(k_norm_gain[l]), bias)

        row = lambda v: v[None, :]
        hp = lambda v: v.reshape(RWKV_HEADS, 1, RWKV_HEAD_DIM)
        prep = _rwkv_prep(
            p3, mu, row(rwkv_w0[l]), _pad_rows(rwkv_w_up[l], LANES).astype(BF16), row(rwkv_a0[l]),
            _pad_rows(rwkv_a_up[l], LANES).astype(BF16), _pad_rows(rwkv_g_up[l], 2 * LANES).astype(BF16),
            row(rwkv_k_k[l]), row(rwkv_k_a[l]), rwkv_r_k[l].reshape(1, C))
        o_g = _wkv(*prep, hp(rwkv_ln_w[l]), hp(rwkv_ln_b[l]))

        w_router = jnp.concatenate([w_group_router[l], w_expert_router[l]], axis=1)
        w_router = jnp.pad(w_router, ((0, 0), (0, ROUTER_PAD - w_router.shape[1])))
        wr_hi = w_router.astype(BF16)
        wr_lo = (w_router - wr_hi.astype(F32)).astype(BF16)
        b_router = jnp.concatenate([b_group_router[l], b_expert_router[l]])
        b_router = jnp.pad(b_router, (0, ROUTER_PAD - b_router.shape[0]))[None, :]
        x1, h2, info, counts = _merge(
            att, o_g, p3, x, w_att_out[l].astype(BF16),
            w_rwkv_out[l].astype(BF16).reshape(RWKV_HEADS, RWKV_HEAD_DIM, D), w_out[l].astype(BF16),
            norm2_gain[l][None, :], jnp.stack([wr_hi, wr_lo]), b_router)

        row_tok, blk_expert, n_used, dest, gates = _row_layout(
            info.reshape(T, ROUTER_PAD), counts[0, N_EXPERT_GROUPS:N_EXPERT_GROUPS + N_EXPERTS], T)
        xin = h2.reshape(T, D // 2)[row_tok]
        yb = _experts(blk_expert, n_used, xin, w_expert_gate[l], w_expert_up[l], w_expert_down[l])
        y = gates[:, 0:1] * yb[dest[:, 0]] + gates[:, 1:2] * yb[dest[:, 1]]
        x = x1 + y.reshape(B, S, D)
    return x
```

```python
import functools
import math

import jax
import jax.numpy as jnp
from jax import lax
from jax.experimental import pallas as pl
from jax.experimental.pallas import tpu as pltpu

F32 = jnp.float32
BF16 = jnp.bfloat16

D_MODEL = 1024
ATT_PATTERNS = ((128, 1), (512, 4), (2048, 16))
N_ATT_GROUPS = 3
HEADS_PER_GROUP = 4
ATT_HEADS = 12
HEAD_DIM = 64
ATT_WIDTH = 768
ATT_OUT_WIDTH = 256
REL_BUCKETS = 32
REL_MAX_DIST = 2048
RWKV_HEADS = 8
RWKV_HEAD_DIM = 64
RWKV_WIDTH = 512
DECAY_LORA = 64
ICLR_LORA = 64
GATE_LORA = 160
RWKV_SHIFT_WIDTH = 1824
GN_EPS = 64e-5
N_EXPERT_GROUPS = 4
EXPERTS_PER_GROUP = 8
N_EXPERTS = 32
TOP_K = 2
D_EXPERT = 512
NORM_EPS = 1e-6
NEG_INF = -1e30

LANES = 128
ATT_BLOCK = 128
WKV_CHUNK = 64
RWKV_PAD_WIDTH = 2048
P_WIDTH = 6400
QKV_COL_BLOCK = (RWKV_PAD_WIDTH + 2 * D_MODEL) // LANES
ROUTER_PAD = 128
MOE_ROWS = 256
VMEM_LIMIT = 56 * 1024 * 1024


def _sigmoid(x):
    return 1.0 / (1.0 + jnp.exp(-x))


def _dot(a, b):
    return jnp.dot(a.astype(BF16), b.astype(BF16), preferred_element_type=F32)


def _dot_t(a, b):
    return lax.dot_general(a.astype(BF16), b.astype(BF16), (((1,), (1,)), ((), ())),
                           preferred_element_type=F32)


def _split3(a):
    hi = a.astype(BF16)
    r1 = a - hi.astype(F32)
    mid = r1.astype(BF16)
    lo = (r1 - mid.astype(F32)).astype(BF16)
    return hi, mid, lo


def _split2(a):
    hi = a.astype(BF16)
    lo = (a - hi.astype(F32)).astype(BF16)
    return hi, lo


def _inproj_kernel(x_ref, g_ref, w_ref, o_ref, h_ref):
    @pl.when(pl.program_id(1) == 0)
    def _():
        x = x_ref[...]
        ms = jnp.mean(x * x, axis=-1, keepdims=True)
        h_ref[...] = (x * lax.rsqrt(ms + NORM_EPS) * g_ref[...]).astype(BF16)

    o_ref[...] = jnp.dot(h_ref[...], w_ref[...], preferred_element_type=F32)


def _inproj(x2, gain, w_all):
    T = x2.shape[0]
    tm, tn = 1024, 1280
    return pl.pallas_call(
        _inproj_kernel,
        out_shape=jax.ShapeDtypeStruct((T, P_WIDTH), F32),
        grid=(T // tm, P_WIDTH // tn),
        in_specs=[pl.BlockSpec((tm, D_MODEL), lambda i, j: (i, 0)),
                  pl.BlockSpec((1, D_MODEL), lambda i, j: (0, 0)),
                  pl.BlockSpec((D_MODEL, tn), lambda i, j: (0, j))],
        out_specs=pl.BlockSpec((tm, tn), lambda i, j: (i, j)),
        scratch_shapes=[pltpu.VMEM((tm, D_MODEL), BF16)],
        compiler_params=pltpu.CompilerParams(
            dimension_semantics=("arbitrary", "arbitrary"), vmem_limit_bytes=VMEM_LIMIT),
        name="inproj",
    )(x2, gain, w_all)


def _attn_kernel(q0, q1, q2, k0, k1, k2, v0, v1, v2, qg_ref, kg_ref, bias_ref, out_ref,
                 qn_s, kn_s, acc_s, m_s, l_s):
    S = out_ref.shape[1]
    q_refs, k_refs, v_refs = (q0, q1, q2), (k0, k1, k2), (v0, v1, v2)
    lane = lax.broadcasted_iota(jnp.int32, (1, LANES), 1)
    lo = lane < HEAD_DIM
    col = lax.broadcasted_iota(jnp.int32, (1, 2 * ATT_BLOCK), 1)

    same_head = (lax.broadcasted_iota(jnp.int32, (LANES, LANES), 0) < HEAD_DIM) == (
        lax.broadcasted_iota(jnp.int32, (LANES, LANES), 1) < HEAD_DIM)
    ones_bd = jnp.where(same_head, 1.0, 0.0).astype(BF16)

    def pair_norm(x, gain):
        hi, lo2 = _split2(x * x)
        ss = (jnp.dot(hi, ones_bd, preferred_element_type=F32)
              + jnp.dot(lo2, ones_bd, preferred_element_type=F32))
        return x * lax.rsqrt(ss * (1.0 / HEAD_DIM) + NORM_EPS) * gain

    rows_per = 256

    def norm_body(i, c):
        rs = pl.ds(pl.multiple_of(i * rows_per, rows_per), rows_per)
        for g in range(N_ATT_GROUPS):
            qn_s[g, rs, :] = pair_norm(q_refs[g][0, rs, :], qg_ref[g]) * (HEAD_DIM ** -0.5)
            kn_s[g, rs, :] = pair_norm(k_refs[g][0, rs, :], kg_ref[g])
        return c

    lax.fori_loop(0, S // rows_per, norm_body, 0)

    for g, (window, d) in enumerate(ATT_PATTERNS):
        n_blk = (S // d) // ATT_BLOCK
        has_prev = n_blk > 1

        def rows_at(start, d=d):
            if d == 1:
                return pl.ds(pl.multiple_of(start, ATT_BLOCK), ATT_BLOCK)
            return pl.ds(start, ATT_BLOCK, stride=d)

        def blk_body(idx, c, g=g, d=d, has_prev=has_prev, rows_at=rows_at):
            r = idx % d
            n = idx // d
            cur = rows_at(n * (ATT_BLOCK * d) + r)
            q = qn_s[g, cur, :]
            q2 = jnp.concatenate([jnp.where(lo, q, 0.0), jnp.where(lo, 0.0, q)], axis=0)
            kc = kn_s[g, cur, :]
            vc = v_refs[g][0, cur, :]
            if has_prev:
                prv = rows_at(jnp.maximum(n - 1, 0) * (ATT_BLOCK * d) + r)
                kcat = jnp.concatenate([kn_s[g, prv, :], kc], axis=0)
                vcat = jnp.concatenate([v_refs[g][0, prv, :], vc], axis=0)
                s = _dot_t(q2, kcat) + bias_ref[g, 0]
                s = s + jnp.where(col < ATT_BLOCK, jnp.where(n == 0, NEG_INF, 0.0), 0.0)
            else:
                kcat, vcat = kc, vc
                s = _dot_t(q2, kcat) + bias_ref[g, 0, :, ATT_BLOCK:]
            m = jnp.max(s, axis=-1, keepdims=True)
            p = jnp.exp(s - m)
            l = jnp.sum(p, axis=-1, keepdims=True)
            o = _dot(p, vcat)
            acc_s[g, cur, :] = jnp.where(lo, o[:ATT_BLOCK], o[ATT_BLOCK:])
            m_s[g, cur, :] = jnp.where(lo, m[:ATT_BLOCK], m[ATT_BLOCK:])
            l_s[g, cur, :] = jnp.where(lo, l[:ATT_BLOCK], l[ATT_BLOCK:])
            return c

        lax.fori_loop(0, d * n_blk, blk_body, 0, unroll=4)

    def comb_body(i, c):
        rs = pl.ds(pl.multiple_of(i * rows_per, rows_per), rows_per)
        m = jnp.maximum(jnp.maximum(m_s[0, rs, :], m_s[1, rs, :]), m_s[2, rs, :])
        num = jnp.zeros((rows_per, LANES), F32)
        den = jnp.zeros((rows_per, LANES), F32)
        for g in range(N_ATT_GROUPS):
            e = jnp.exp(m_s[g, rs, :] - m)
            num = num + e * acc_s[g, rs, :]
            den = den + e * l_s[g, rs, :]
        out_ref[0, rs, :] = num / den
        return c

    lax.fori_loop(0, S // rows_per, comb_body, 0)


def _attention(p3, q_gain, k_gain, bias):
    B, S, _ = p3.shape
    n_pairs = HEADS_PER_GROUP // 2

    def col_spec(base):
        return [pl.BlockSpec((1, S, LANES),
                             functools.partial(lambda b, sp, off: (b, 0, off + sp), off=base + g * n_pairs))
                for g in range(N_ATT_GROUPS)]

    n_qkv_blocks = ATT_WIDTH // LANES
    in_specs = (col_spec(QKV_COL_BLOCK) + col_spec(QKV_COL_BLOCK + n_qkv_blocks)
                + col_spec(QKV_COL_BLOCK + 2 * n_qkv_blocks)
                + [pl.BlockSpec((N_ATT_GROUPS, 1, LANES), lambda b, sp: (0, 0, 0)),
                   pl.BlockSpec((N_ATT_GROUPS, 1, LANES), lambda b, sp: (0, 0, 0)),
                   pl.BlockSpec((N_ATT_GROUPS, 1, 2 * ATT_BLOCK, 2 * ATT_BLOCK),
                                lambda b, sp: (0, sp, 0, 0))])
    return pl.pallas_call(
        _attn_kernel,
        out_shape=jax.ShapeDtypeStruct((B, S, ATT_OUT_WIDTH), F32),
        grid=(B, n_pairs),
        in_specs=in_specs,
        out_specs=pl.BlockSpec((1, S, LANES), lambda b, sp: (b, 0, sp)),
        scratch_shapes=[pltpu.VMEM((N_ATT_GROUPS, S, LANES), F32) for _ in range(5)],
        compiler_params=pltpu.CompilerParams(
            dimension_semantics=("arbitrary", "arbitrary"), vmem_limit_bytes=VMEM_LIMIT),
        name="dilated_attention",
    )(*([p3] * 9), q_gain, k_gain, bias)


def _t5_causal_bucket(dist):
    max_exact = REL_BUCKETS // 2
    d = jnp.maximum(dist.astype(F32), 1.0)
    large = max_exact + (jnp.log(d / max_exact) / math.log(REL_MAX_DIST / max_exact)
                         * (REL_BUCKETS - max_exact)).astype(jnp.int32)
    large = jnp.minimum(large, REL_BUCKETS - 1)
    return jnp.where(dist < max_exact, dist, large)


def _attention_bias(rel_bias_table):
    W = ATT_BLOCK
    qi = jnp.arange(W)[:, None]
    kj = jnp.arange(2 * W)[None, :]
    rel = qi + W - kj
    valid = (rel >= 0) & (rel <= W)
    per_group = []
    for g, (_, d) in enumerate(ATT_PATTERNS):
        tab = rel_bias_table[:, g * HEADS_PER_GROUP:(g + 1) * HEADS_PER_GROUP]
        bucket = _t5_causal_bucket(jnp.clip(rel, 0, W) * d)
        onehot = (bucket[:, :, None] == jnp.arange(REL_BUCKETS)[None, None, :]).astype(F32)
        bias = jnp.einsum('qkb,bh->qkh', onehot, tab.astype(F32),
                          precision=lax.Precision.HIGHEST)
        bias = jnp.where(valid[:, :, None], bias, NEG_INF)
        per_group.append(jnp.transpose(bias, (2, 0, 1)).reshape(2, 2 * W, 2 * W))
    return jnp.stack(per_group, axis=0)


def _rwkv_prep_kernel(p_ref, mu_ref, w0_ref, wup_ref, a0_ref, aup_ref, gup_ref, kk_ref, ka_ref, rk_ref,
                      at_o, rt_o, bt_o, kt_o, bw_o, kw_o, v_o, bonus_o, g_o, wend_o, carry):
    tq = p_ref.shape[1]
    C = RWKV_WIDTH
    L = WKV_CHUNK

    @pl.when(pl.program_id(1) == 0)
    def _():
        carry[...] = jnp.zeros_like(carry)

    p = p_ref[0]
    row = lax.broadcasted_iota(jnp.int32, (tq, 1), 0)
    prev = jnp.where(row == 0, carry[...], pltpu.roll(p, 1, axis=0))
    carry[...] = p[tq - 1:tq, :]
    pm = p + (prev - p) * mu_ref[...]
    r, k, v = pm[:, 0:C], pm[:, C:2 * C], pm[:, 2 * C:3 * C]
    xw = pm[:, 3 * C:3 * C + LANES]
    xa = pm[:, 3 * C + LANES:3 * C + 2 * LANES]
    xg = pm[:, 3 * C + 2 * LANES:]

    z = w0_ref[...] + _dot(jnp.tanh(xw), wup_ref[...])
    softplus_neg = jnp.maximum(-z, 0.0) + jnp.log(1.0 + jnp.exp(-jnp.abs(z)))
    lw = -jnp.exp(-softplus_neg - 0.5)
    a = _sigmoid(a0_ref[...] + _dot(xa, aup_ref[...]))
    g = _dot(_sigmoid(xg), gup_ref[...])

    hr = jnp.right_shift(lax.broadcasted_iota(jnp.int32, (C, C), 0), 6)
    hc = jnp.right_shift(lax.broadcasted_iota(jnp.int32, (C, C), 1), 6)
    ones_bd = jnp.where(hr == hc, 1.0, 0.0).astype(BF16)

    def head_sums(x):
        hi, lo2 = _split2(x)
        return (jnp.dot(hi, ones_bd, preferred_element_type=F32)
                + jnp.dot(lo2, ones_bd, preferred_element_type=F32))

    k2 = k * (1.0 + (a - 1.0) * ka_ref[...])
    kk = k * kk_ref[...]
    kk = kk / jnp.maximum(jnp.sqrt(head_sums(kk * kk)), 1e-12)
    bonus = head_sums(r * k2 * rk_ref[...]) * v
    b = kk * a

    half = 256
    ri = lax.broadcasted_iota(jnp.int32, (half, half), 0)
    ci = lax.broadcasted_iota(jnp.int32, (half, half), 1)
    tri = jnp.where((jnp.right_shift(ri, 6) == jnp.right_shift(ci, 6)) & (ri >= ci), 1.0, 0.0).astype(BF16)
    cums = []
    for s in range(tq // half):
        hi, mid, lo3 = _split3(lw[s * half:(s + 1) * half, :])
        cums.append(jnp.dot(tri, hi, preferred_element_type=F32)
                    + (jnp.dot(tri, mid, preferred_element_type=F32)
                       + jnp.dot(tri, lo3, preferred_element_type=F32)))
    cum = jnp.concatenate(cums, axis=0)
    n_chunks = tq // L
    ends = [cum[c * L + L - 1:c * L + L, :] for c in range(n_chunks)]
    tot = jnp.concatenate([jnp.broadcast_to(e, (L, C)) for e in ends], axis=0)
    wend = jnp.exp(jnp.concatenate(ends, axis=0))

    e_pos = jnp.exp(cum)
    e_neg = jnp.exp(-cum)
    e_rem = jnp.exp(tot - cum)
    at = -kk * jnp.exp(cum - lw)
    rt = r * e_pos
    bt = b * e_neg
    kt = k2 * e_neg
    bw = b * e_rem
    kw = k2 * e_rem
    at_o[0] = at.astype(BF16)
    rt_o[0] = rt.astype(BF16)
    bt_o[0] = bt.astype(BF16)
    kt_o[0] = kt.astype(BF16)
    bw_o[0] = bw.astype(BF16)
    kw_o[0] = kw.astype(BF16)
    v_o[0] = v.astype(BF16)
    bonus_o[0] = bonus
    g_o[0] = g
    wend_o[0] = wend


def _rwkv_prep(p3, mu, w0, wup, a0, aup, gup, k_k, k_a, r_k):
    B, S, _ = p3.shape
    tq = 512
    C = RWKV_WIDTH
    tm_shape = lambda dt: jax.ShapeDtypeStruct((B, S, C), dt)
    tm_spec = pl.BlockSpec((1, tq, C), lambda b, j: (b, j, 0))
    full = lambda shape: pl.BlockSpec(shape, lambda b, j: (0,) * len(shape))
    return pl.pallas_call(
        _rwkv_prep_kernel,
        out_shape=[tm_shape(BF16)] * 7 + [tm_shape(F32)] * 2 + [jax.ShapeDtypeStruct((B, S // WKV_CHUNK, C), F32)],
        grid=(B, S // tq),
        in_specs=[pl.BlockSpec((1, tq, RWKV_PAD_WIDTH), lambda b, j: (b, j, 0)),
                  full((1, RWKV_PAD_WIDTH)), full((1, C)), full((LANES, C)), full((1, C)),
                  full((LANES, C)), full((2 * LANES, C)), full((1, C)), full((1, C)), full((1, C))],
        out_specs=[tm_spec] * 9 + [pl.BlockSpec((1, tq // WKV_CHUNK, C), lambda b, j: (b, j, 0))],
        scratch_shapes=[pltpu.VMEM((1, RWKV_PAD_WIDTH), F32)],
        compiler_params=pltpu.CompilerParams(
            dimension_semantics=("arbitrary", "arbitrary"), vmem_limit_bytes=VMEM_LIMIT),
        name="rwkv_prep",
    )(p3, mu, w0, wup, a0, aup, gup, k_k, k_a, r_k)


def _dot_tn(a, b):
    return lax.dot_general(a.astype(BF16), b.astype(BF16), (((0,), (0,)), ((), ())),
                           preferred_element_type=F32)


WKV_GROUP = 4
WKV_GROUP_WIDTH = WKV_GROUP * RWKV_HEAD_DIM


def _wkv_kernel(at_ref, rt_ref, bt_ref, kt_ref, bw_ref, kw_ref, v_ref, bonus_ref, g_ref, wend_ref,
                lnw_ref, lnb_ref, o_ref, state, m_s, u0_s, o0_s, arb_s, p_s, c_s, sh_s):
    N, C, GW = RWKV_HEAD_DIM, WKV_CHUNK, WKV_GROUP_WIDTH
    n_groups = RWKV_WIDTH // GW
    n_sub = at_ref.shape[1] // C
    j = pl.program_id(1)

    @pl.when(j == 0)
    def _():
        state[...] = jnp.zeros_like(state)

    def iota(shape, dim):
        return lax.broadcasted_iota(jnp.int32, shape, dim)

    same_head = jnp.right_shift(iota((1, GW, GW), 1), 6) == jnp.right_shift(iota((1, GW, GW), 2), 6)
    ones_bd = jnp.where(same_head, 1.0, 0.0).astype(BF16)
    g_row = iota((1, 2 * C, 2 * GW), 1)
    g_col = jnp.bitwise_and(iota((1, 2 * C, 2 * GW), 2), N - 1)
    band = ((g_row < C) & (g_row > g_col)) | ((g_row >= C) & ((g_row - C) >= g_col))
    eye = jnp.where(iota((1, C, GW), 1) == jnp.bitwise_and(iota((1, C, GW), 2), N - 1), 1.0, 0.0)

    def bmm(a, b, dims):
        return lax.dot_general(a.astype(BF16), b.astype(BF16), (dims, ((0,), (0,))),
                               preferred_element_type=F32)

    nn, nt, tn = ((2,), (1,)), ((2,), (2,)), ((1,), (1,))

    def bd(x):
        xb = x.astype(BF16)
        return jnp.where(same_head, jnp.concatenate([xb] * WKV_GROUP, axis=1), jnp.zeros((), BF16))

    span = 4
    rows = span * C

    def stack(x):
        x3 = x.reshape(span, C, n_groups * GW)
        return jnp.concatenate([x3[:, :, q * GW:(q + 1) * GW] for q in range(n_groups)], axis=0)

    def unstack(y):
        return jnp.concatenate([y[q * span:(q + 1) * span].reshape(rows, GW) for q in range(n_groups)],
                               axis=1)

    for i in range(n_sub // span):
        rs = slice(i * rows, (i + 1) * rows)
        at = stack(at_ref[0, rs, :])
        ar = jnp.concatenate([at, stack(rt_ref[0, rs, :])], axis=1)
        bk_bd = jnp.concatenate([bd(stack(bt_ref[0, rs, :])), bd(stack(kt_ref[0, rs, :]))], axis=1)
        gm = jnp.where(band, bmm(ar, bk_bd, nt), 0.0)
        a_ab = gm[:, :C, :GW]
        inv = eye + a_ab
        pw = bmm(a_ab, bd(a_ab), nn)
        for _ in range(int(math.log2(C)) - 2):
            prod = bmm(jnp.concatenate([inv, pw], axis=1), bd(pw), nn)
            inv = inv + prod[:, :C]
            pw = prod[:, C:]
        inv = inv + bmm(inv, bd(pw), nn)
        v = stack(v_ref[0, rs, :])
        kv = bmm(gm[:, :, GW:], bd(v), nn)
        mu0 = bmm(inv, jnp.concatenate([bd(at), bd(kv[:, :C])], axis=2), nn)
        m = mu0[:, :, :GW].astype(BF16)
        u0 = mu0[:, :, GW:]
        bw = stack(bw_ref[0, rs, :])
        low_rank = jnp.where(same_head, bmm(m, bw, tn), 0.0)
        const = jnp.where(same_head, bmm(jnp.concatenate([u0.astype(BF16), v], axis=1),
                                         jnp.concatenate([bw, stack(kw_ref[0, rs, :])], axis=1), tn), 0.0)
        m_s[rs, :] = unstack(m)
        u0_s[rs, :] = unstack(u0)
        o0_s[rs, :] = unstack(kv[:, C:])
        arb_s[rs, :] = unstack(gm[:, C:, :GW]).astype(BF16)
        for q in range(n_groups):
            p_s[i * span:(i + 1) * span, q] = low_rank[q * span:(q + 1) * span].astype(BF16)
            c_s[i * span:(i + 1) * span, q] = const[q * span:(q + 1) * span]

    def groups(x):
        return jnp.stack([x[:, q * GW:(q + 1) * GW] for q in range(n_groups)], axis=0)

    def chunk_state(i, c):
        s0 = state[...]
        sh_s[i] = s0.astype(BF16)
        w_end = groups(wend_ref[0, pl.ds(j * n_sub + i, 1), :])
        state[...] = s0 * w_end + bmm(s0, p_s[i], nn) + c_s[i]
        return c

    lax.fori_loop(0, n_sub, chunk_state, 0, unroll=True)

    ones_p = jnp.broadcast_to(ones_bd, (n_groups * span, GW, GW))
    for i in range(n_sub // span):
        rs = slice(i * rows, (i + 1) * rows)
        s0 = jnp.concatenate([sh_s[i * span:(i + 1) * span, q] for q in range(n_groups)], axis=0)
        mr_s = bmm(jnp.concatenate([stack(m_s[rs, :]), stack(rt_ref[0, rs, :])], axis=1), s0, nt)
        u = mr_s[:, :C] + stack(u0_s[rs, :])
        o = mr_s[:, C:] + stack(o0_s[rs, :]) + bmm(stack(arb_s[rs, :]), bd(u), nn)
        hi, lo2 = _split2(o)
        s1 = bmm(jnp.concatenate([hi, lo2], axis=1), ones_p, nn)
        dlt = o - (s1[:, :C] + s1[:, C:]) * (1.0 / N)
        hi, lo2 = _split2(dlt * dlt)
        s2 = bmm(jnp.concatenate([hi, lo2], axis=1), ones_p, nn)
        on = unstack(dlt * lax.rsqrt((s2[:, :C] + s2[:, C:]) * (1.0 / N) + GN_EPS))
        on = on * lnw_ref[...] + lnb_ref[...]
        o_ref[0, rs, :] = (on + bonus_ref[0, rs, :]) * g_ref[0, rs, :]


def _wkv(at, rt, bt, kt, bw, kw, v, bonus, g, wend, ln_w, ln_b):
    B, S, C = at.shape
    tc = 512
    tm_spec = pl.BlockSpec((1, tc, C), lambda bb, j: (bb, j, 0))
    par_spec = pl.BlockSpec((1, C), lambda bb, j: (0, 0))
    blocks = (C // WKV_GROUP_WIDTH, WKV_GROUP_WIDTH, WKV_GROUP_WIDTH)
    return pl.pallas_call(
        _wkv_kernel,
        out_shape=jax.ShapeDtypeStruct((B, S, C), F32),
        grid=(B, S // tc),
        in_specs=([tm_spec] * 9 + [pl.BlockSpec((1, S // WKV_CHUNK, C), lambda bb, j: (bb, 0, 0))]
                  + [par_spec] * 2),
        out_specs=tm_spec,
        scratch_shapes=[pltpu.VMEM(blocks, F32),
                        pltpu.VMEM((tc, C), BF16), pltpu.VMEM((tc, C), F32), pltpu.VMEM((tc, C), F32),
                        pltpu.VMEM((tc, C), BF16),
                        pltpu.VMEM((tc // WKV_CHUNK,) + blocks, BF16), pltpu.VMEM((tc // WKV_CHUNK,) + blocks, F32),
                        pltpu.VMEM((tc // WKV_CHUNK,) + blocks, BF16)],
        compiler_params=pltpu.CompilerParams(
            dimension_semantics=("arbitrary", "arbitrary"), vmem_limit_bytes=VMEM_LIMIT),
        name="wkv7",
    )(at, rt, bt, kt, bw, kw, v, bonus, g, wend, ln_w, ln_b)


def _merge_kernel(att_ref, o_ref, gate_ref, x_ref, wa_ref, wr_ref, wo_ref, g2_ref, wrt_ref, brt_ref,
                  x1_ref, h2_ref, lg_ref, cnt_ref):
    y_att = _dot(att_ref[0], wa_ref[...])
    y_rwkv = _dot(o_ref[0], wr_ref[...])
    gates = _sigmoid(gate_ref[0])
    mixed = gates[:, :D_MODEL] * y_att + gates[:, D_MODEL:] * y_rwkv
    x1 = x_ref[0] + _dot(mixed, wo_ref[...])
    x1_ref[0] = x1
    ms = jnp.mean(x1 * x1, axis=-1, keepdims=True)
    h2 = x1 * lax.rsqrt(ms + NORM_EPS) * g2_ref[...]
    hh, hl = _split2(h2)
    half = D_MODEL // 2
    rounded = hh.astype(F32)
    w_hi = pltpu.bitcast(rounded[:, :half], jnp.uint32)
    w_lo = pltpu.bitcast(rounded[:, half:], jnp.uint32)
    h2_ref[0] = w_hi | jnp.right_shift(w_lo, jnp.uint32(16))
    wh, wl = wrt_ref[0], wrt_ref[1]
    lg = (jnp.dot(hh, wh, preferred_element_type=F32)
          + (jnp.dot(hh, wl, preferred_element_type=F32) + jnp.dot(hl, wh, preferred_element_type=F32)))
    lg = lg + brt_ref[...]

    lane = lax.broadcasted_iota(jnp.int32, (1, ROUTER_PAD), 1)
    gmask = lane < N_EXPERT_GROUPS
    gl = jnp.where(gmask, lg, -jnp.inf)
    gmax = jnp.max(gl, axis=-1, keepdims=True)
    grp_idx = jnp.min(jnp.where(gl == gmax, lane, ROUTER_PAD), axis=-1, keepdims=True)
    grp_p = 1.0 / jnp.sum(jnp.where(gmask, jnp.exp(lg - gmax), 0.0), axis=-1, keepdims=True)
    e_lane = lane - N_EXPERT_GROUPS
    emask = ((e_lane >= 0) & (e_lane < N_EXPERTS)
             & (jnp.right_shift(e_lane, int(math.log2(EXPERTS_PER_GROUP))) == grp_idx))
    el = jnp.where(emask, lg, -jnp.inf)
    m1 = jnp.max(el, axis=-1, keepdims=True)
    i1 = jnp.min(jnp.where(el == m1, lane, ROUTER_PAD), axis=-1, keepdims=True)
    el2 = jnp.where(lane == i1, -jnp.inf, el)
    m2 = jnp.max(el2, axis=-1, keepdims=True)
    i2 = jnp.min(jnp.where(el2 == m2, lane, ROUTER_PAD), axis=-1, keepdims=True)
    e2 = jnp.exp(m2 - m1)
    gate1 = grp_p / (1.0 + e2)
    gate2 = gate1 * e2

    @pl.when((pl.program_id(0) == 0) & (pl.program_id(1) == 0))
    def _():
        cnt_ref[...] = jnp.zeros_like(cnt_ref)

    tm = lg.shape[0]
    oh = jnp.where((lane == i1) | (lane == i2), 1.0, 0.0)
    ti = lax.broadcasted_iota(jnp.int32, (tm, tm), 0)
    tj = lax.broadcasted_iota(jnp.int32, (tm, tm), 1)
    before = jnp.where(tj < ti, 1.0, 0.0).astype(BF16)
    pref = jnp.dot(before, oh.astype(BF16), preferred_element_type=F32) + cnt_ref[...]
    rank1 = jnp.sum(jnp.where(lane == i1, pref, 0.0), axis=-1, keepdims=True)
    rank2 = jnp.sum(jnp.where(lane == i2, pref, 0.0), axis=-1, keepdims=True)
    cnt_ref[...] = cnt_ref[...] + jnp.sum(oh, axis=0, keepdims=True)
    info = jnp.where(lane == 0, (i1 - N_EXPERT_GROUPS).astype(F32), 0.0)
    info = jnp.where(lane == 1, (i2 - N_EXPERT_GROUPS).astype(F32), info)
    info = jnp.where(lane == 2, rank1, info)
    info = jnp.where(lane == 3, rank2, info)
    info = jnp.where(lane == 4, gate1, info)
    info = jnp.where(lane == 5, gate2, info)
    lg_ref[0] = info


def _merge(att, o_g, p3, x, w_att_out, w_rwkv_out, w_out, gain2, w_router, b_router):
    B, S, D = x.shape
    tm = 512
    full = lambda shape: pl.BlockSpec(shape, lambda b, j: (0,) * len(shape))
    return pl.pallas_call(
        _merge_kernel,
        out_shape=[jax.ShapeDtypeStruct((B, S, D), F32), jax.ShapeDtypeStruct((B, S, D // 2), jnp.uint32),
                   jax.ShapeDtypeStruct((B, S, ROUTER_PAD), F32),
                   jax.ShapeDtypeStruct((1, ROUTER_PAD), F32)],
        grid=(B, S // tm),
        in_specs=[pl.BlockSpec((1, tm, ATT_OUT_WIDTH), lambda b, j: (b, j, 0)),
                  pl.BlockSpec((1, tm, RWKV_WIDTH), lambda b, j: (b, j, 0)),
                  pl.BlockSpec((1, tm, 2 * D_MODEL), lambda b, j: (b, j, 1)),
                  pl.BlockSpec((1, tm, D), lambda b, j: (b, j, 0)),
                  full((ATT_OUT_WIDTH, D)), full((RWKV_WIDTH, D)), full((D, D)),
                  full((1, D)), full((2, D, ROUTER_PAD)), full((1, ROUTER_PAD))],
        out_specs=[pl.BlockSpec((1, tm, D), lambda b, j: (b, j, 0)),
                   pl.BlockSpec((1, tm, D // 2), lambda b, j: (b, j, 0)),
                   pl.BlockSpec((1, tm, ROUTER_PAD), lambda b, j: (b, j, 0)),
                   pl.BlockSpec((1, ROUTER_PAD), lambda b, j: (0, 0))],
        compiler_params=pltpu.CompilerParams(
            dimension_semantics=("arbitrary", "arbitrary"), vmem_limit_bytes=VMEM_LIMIT),
        name="merge_outproj_router",
    )(att, o_g, p3, x, w_att_out, w_rwkv_out, w_out, gain2, w_router, b_router)


def _expert_kernel(be_ref, nb_ref, x_ref, wg_ref, wu_ref, wd_ref, y_ref, wgu_s, wd_s):
    i = pl.program_id(0)

    @pl.when((i == 0) | (be_ref[i] != be_ref[jnp.maximum(i - 1, 0)]))
    def _():
        wgu_s[:, :D_EXPERT] = wg_ref[0].astype(BF16)
        wgu_s[:, D_EXPERT:] = wu_ref[0].astype(BF16)
        wd_s[...] = wd_ref[0].astype(BF16)

    @pl.when(i < nb_ref[0])
    def _():
        words = x_ref[...]
        x_a = pltpu.bitcast(words & jnp.uint32(0xFFFF0000), F32)
        x_b = pltpu.bitcast(jnp.left_shift(words, jnp.uint32(16)), F32)
        xb = jnp.concatenate([x_a, x_b], axis=-1).astype(BF16)
        h = jnp.dot(xb, wgu_s[...], preferred_element_type=F32)
        hg, hu = h[:, :D_EXPERT], h[:, D_EXPERT:]
        hid = hg * _sigmoid(hg) * hu
        y_ref[...] = jnp.dot(hid.astype(BF16), wd_s[...], preferred_element_type=F32)

    @pl.when(i >= nb_ref[0])
    def _():
        y_ref[...] = jnp.zeros_like(y_ref)


def _experts(blk_expert, n_used, xin, w_gate, w_up, w_down):
    n_rows = xin.shape[0]
    D = D_MODEL
    n_blocks = n_rows // MOE_ROWS
    return pl.pallas_call(
        _expert_kernel,
        out_shape=jax.ShapeDtypeStruct((n_rows, D), F32),
        grid_spec=pltpu.PrefetchScalarGridSpec(
            num_scalar_prefetch=2,
            grid=(n_blocks,),
            scratch_shapes=[pltpu.VMEM((D, 2 * D_EXPERT), BF16), pltpu.VMEM((D_EXPERT, D), BF16)],
            in_specs=[pl.BlockSpec((MOE_ROWS, D // 2), lambda i, be, nb: (i, 0)),
                      pl.BlockSpec((1, D, D_EXPERT), lambda i, be, nb: (be[i], 0, 0)),
                      pl.BlockSpec((1, D, D_EXPERT), lambda i, be, nb: (be[i], 0, 0)),
                      pl.BlockSpec((1, D_EXPERT, D), lambda i, be, nb: (be[i], 0, 0))],
            out_specs=pl.BlockSpec((MOE_ROWS, D), lambda i, be, nb: (i, 0))),
        compiler_params=pltpu.CompilerParams(
            dimension_semantics=("arbitrary",), vmem_limit_bytes=VMEM_LIMIT),
        name="moe_experts",
    )(blk_expert, n_used, xin, w_gate, w_up, w_down)


def _row_layout(info, counts, T):
    A = T * TOP_K
    eid = info[:, 0:2].astype(jnp.int32)
    rank = info[:, 2:4].astype(jnp.int32)
    gates = info[:, 4:6]
    counts = counts.astype(jnp.int32)
    pcounts = (counts + MOE_ROWS - 1) // MOE_ROWS * MOE_ROWS
    pends = jnp.cumsum(pcounts)
    pstarts = pends - pcounts
    dest = (pstarts[eid] + rank).astype(jnp.int32)
    n_blocks = -(-A // MOE_ROWS) + N_EXPERTS
    n_rows = n_blocks * MOE_ROWS
    tok = jnp.broadcast_to(jnp.arange(T, dtype=jnp.int32)[:, None], (T, TOP_K))
    row_tok = (jnp.arange(n_rows, dtype=jnp.int32) % T).at[dest.reshape(A)].set(tok.reshape(A))
    blk_start = jnp.arange(n_blocks, dtype=jnp.int32) * MOE_ROWS
    blk_expert = jnp.minimum(jnp.sum(blk_start[:, None] >= pends[None, :], axis=1), N_EXPERTS - 1)
    n_used = (pends[N_EXPERTS - 1] // MOE_ROWS).astype(jnp.int32)[None]
    return row_tok, blk_expert.astype(jnp.int32), n_used, dest, gates


def _pad_rows(w, rows):
    return jnp.pad(w, ((0, rows - w.shape[0]), (0, 0)))


def kernel(x, norm1_gain, w_in, q_norm_gain, k_norm_gain, rel_bias_table, w_att_out, rwkv_shift_mu,
           rwkv_w0, rwkv_w_up, rwkv_a0, rwkv_a_up, rwkv_g_up, rwkv_k_k, rwkv_k_a, rwkv_r_k,
           rwkv_ln_w, rwkv_ln_b, w_rwkv_out, w_out, norm2_gain, w_group_router, b_group_router,
           w_expert_router, b_expert_router, w_expert_gate, w_expert_up, w_expert_down):
    B, S, D = x.shape
    T = B * S
    C = RWKV_WIDTH
    bias = _attention_bias(rel_bias_table)
    for l in range(norm1_gain.shape[0]):
        wi = w_in[l]
        w_qkv = wi[:, :3 * ATT_WIDTH]
        w_rw = wi[:, 3 * ATT_WIDTH:3 * ATT_WIDTH + RWKV_SHIFT_WIDTH]
        w_gt = wi[:, 3 * ATT_WIDTH + RWKV_SHIFT_WIDTH:]
        pad_cols = lambda w, n: jnp.pad(w, ((0, 0), (0, n - w.shape[1])))
        o_w, o_a, o_g = 3 * C, 3 * C + DECAY_LORA, 3 * C + DECAY_LORA + ICLR_LORA

        def rw_layout(w):
            return jnp.concatenate([w[:, :o_w], pad_cols(w[:, o_w:o_a], LANES),
                                    pad_cols(w[:, o_a:o_g], LANES), pad_cols(w[:, o_g:], 2 * LANES)], axis=1)

        w_all = jnp.concatenate([rw_layout(w_rw), w_gt, w_qkv], axis=1).astype(BF16)
        mu = rw_layout(rwkv_shift_mu[l][None, :])

        p = _inproj(x.reshape(T, D), norm1_gain[l][None, :], w_all)
        p3 = p.reshape(B, S, P_WIDTH)

        tile2 = lambda gmat: jnp.tile(gmat, (1, 2))[:, None, :]
        att = _attention(p3, tile2(q_norm_gain[l]), tile2(k_norm_gain[l]), bias)

        row = lambda v: v[None, :]
        prep = _rwkv_prep(
            p3, mu, row(rwkv_w0[l]), _pad_rows(rwkv_w_up[l], LANES).astype(BF16), row(rwkv_a0[l]),
            _pad_rows(rwkv_a_up[l], LANES).astype(BF16), _pad_rows(rwkv_g_up[l], 2 * LANES).astype(BF16),
            row(rwkv_k_k[l]), row(rwkv_k_a[l]), rwkv_r_k[l].reshape(1, C))
        o_g = _wkv(*prep, row(rwkv_ln_w[l]), row(rwkv_ln_b[l]))

        w_router = jnp.concatenate([w_group_router[l], w_expert_router[l]], axis=1)
        w_router = jnp.pad(w_router, ((0, 0), (0, ROUTER_PAD - w_router.shape[1])))
        wr_hi = w_router.astype(BF16)
        wr_lo = (w_router - wr_hi.astype(F32)).astype(BF16)
        b_router = jnp.concatenate([b_group_router[l], b_expert_router[l]])
        b_router = jnp.pad(b_router, (0, ROUTER_PAD - b_router.shape[0]))[None, :]
        x1, h2, info, counts = _merge(
            att, o_g, p3, x, w_att_out[l].astype(BF16),
            w_rwkv_out[l].astype(BF16), w_out[l].astype(BF16),
            norm2_gain[l][None, :], jnp.stack([wr_hi, wr_lo]), b_router)

        row_tok, blk_expert, n_used, dest, gates = _row_layout(
            info.reshape(T, ROUTER_PAD), counts[0, N_EXPERT_GROUPS:N_EXPERT_GROUPS + N_EXPERTS], T)
        xin = h2.reshape(T, D // 2)[row_tok]
        yb = _experts(blk_expert, n_used, xin, w_expert_gate[l], w_expert_up[l], w_expert_down[l])
        y = gates[:, 0:1] * yb[dest[:, 0]] + gates[:, 1:2] * yb[dest[:, 1]]
        x = x1 + y.reshape(B, S, D)
    return x
```

```python
import functools
import math

import jax
import jax.numpy as jnp
from jax import lax
from jax.experimental import pallas as pl
from jax.experimental.pallas import tpu as pltpu

F32 = jnp.float32
BF16 = jnp.bfloat16

D_MODEL = 1024
ATT_PATTERNS = ((128, 1), (512, 4), (2048, 16))
N_ATT_GROUPS = 3
HEADS_PER_GROUP = 4
ATT_HEADS = 12
HEAD_DIM = 64
ATT_WIDTH = 768
ATT_OUT_WIDTH = 256
REL_BUCKETS = 32
REL_MAX_DIST = 2048
RWKV_HEADS = 8
RWKV_HEAD_DIM = 64
RWKV_WIDTH = 512
DECAY_LORA = 64
ICLR_LORA = 64
GATE_LORA = 160
RWKV_SHIFT_WIDTH = 1824
GN_EPS = 64e-5
N_EXPERT_GROUPS = 4
EXPERTS_PER_GROUP = 8
N_EXPERTS = 32
TOP_K = 2
D_EXPERT = 512
NORM_EPS = 1e-6
NEG_INF = -1e30

LANES = 128
ATT_BLOCK = 128
ATT_BATCH = 4
WKV_CHUNK = 64
RWKV_PAD_WIDTH = 2048
P_WIDTH = 6400
QKV_COL_BLOCK = (RWKV_PAD_WIDTH + 2 * D_MODEL) // LANES
ROUTER_PAD = 128
MOE_ROWS = 512
VMEM_LIMIT = 56 * 1024 * 1024


def _sigmoid(x):
    return 1.0 / (1.0 + jnp.exp(-x))


def _dot(a, b):
    return jnp.dot(a.astype(BF16), b.astype(BF16), preferred_element_type=F32)


def _dot_t(a, b):
    return lax.dot_general(a.astype(BF16), b.astype(BF16), (((1,), (1,)), ((), ())),
                           preferred_element_type=F32)


def _split3(a):
    hi = a.astype(BF16)
    r1 = a - hi.astype(F32)
    mid = r1.astype(BF16)
    lo = (r1 - mid.astype(F32)).astype(BF16)
    return hi, mid, lo


def _split2(a):
    hi = a.astype(BF16)
    lo = (a - hi.astype(F32)).astype(BF16)
    return hi, lo


def _inproj_kernel(x_ref, g_ref, w_ref, o_ref, h_ref):
    @pl.when(pl.program_id(1) == 0)
    def _():
        x = x_ref[...]
        ms = jnp.mean(x * x, axis=-1, keepdims=True)
        h_ref[...] = (x * lax.rsqrt(ms + NORM_EPS) * g_ref[...]).astype(BF16)

    o_ref[...] = jnp.dot(h_ref[...], w_ref[...], preferred_element_type=F32)


def _inproj(x2, gain, w_all):
    T = x2.shape[0]
    tm, tn = 1024, 1280
    return pl.pallas_call(
        _inproj_kernel,
        out_shape=jax.ShapeDtypeStruct((T, P_WIDTH), F32),
        grid=(T // tm, P_WIDTH // tn),
        in_specs=[pl.BlockSpec((tm, D_MODEL), lambda i, j: (i, 0)),
                  pl.BlockSpec((1, D_MODEL), lambda i, j: (0, 0)),
                  pl.BlockSpec((D_MODEL, tn), lambda i, j: (0, j))],
        out_specs=pl.BlockSpec((tm, tn), lambda i, j: (i, j)),
        scratch_shapes=[pltpu.VMEM((tm, D_MODEL), BF16)],
        compiler_params=pltpu.CompilerParams(
            dimension_semantics=("arbitrary", "arbitrary"), vmem_limit_bytes=VMEM_LIMIT),
        name="inproj",
    )(x2, gain, w_all)


def _attn_kernel(q0, q1, q2, k0, k1, k2, v0, v1, v2, qg_ref, kg_ref, bias_ref, out_ref,
                 qn_s, kn_s, acc_s, m_s, l_s):
    S = out_ref.shape[1]
    q_refs, k_refs, v_refs = (q0, q1, q2), (k0, k1, k2), (v0, v1, v2)
    lane = lax.broadcasted_iota(jnp.int32, (1, LANES), 1)
    lo = lane < HEAD_DIM
    col = lax.broadcasted_iota(jnp.int32, (1, 2 * ATT_BLOCK), 1)

    same_head = (lax.broadcasted_iota(jnp.int32, (LANES, LANES), 0) < HEAD_DIM) == (
        lax.broadcasted_iota(jnp.int32, (LANES, LANES), 1) < HEAD_DIM)
    ones_bd = jnp.where(same_head, 1.0, 0.0).astype(BF16)

    def pair_norm(x, gain):
        hi, lo2 = _split2(x * x)
        ss = (jnp.dot(hi, ones_bd, preferred_element_type=F32)
              + jnp.dot(lo2, ones_bd, preferred_element_type=F32))
        return x * lax.rsqrt(ss * (1.0 / HEAD_DIM) + NORM_EPS) * gain

    rows_per = 256

    def norm_body(i, c):
        rs = pl.ds(pl.multiple_of(i * rows_per, rows_per), rows_per)
        for g in range(N_ATT_GROUPS):
            qn_s[g, rs, :] = pair_norm(q_refs[g][0, rs, :], qg_ref[g]) * (HEAD_DIM ** -0.5)
            kn_s[g, rs, :] = pair_norm(k_refs[g][0, rs, :], kg_ref[g])
        return c

    lax.fori_loop(0, S // rows_per, norm_body, 0)

    for g, (window, d) in enumerate(ATT_PATTERNS):
        n_blk = (S // d) // ATT_BLOCK
        has_prev = n_blk > 1

        def rows_at(start, d=d):
            if d == 1:
                return pl.ds(pl.multiple_of(start, ATT_BLOCK), ATT_BLOCK)
            return pl.ds(start, ATT_BLOCK, stride=d)

        def blk_body(it, c, g=g, d=d, has_prev=has_prev, rows_at=rows_at):
            curs, q2s, kcats, vcats, pens = [], [], [], [], []
            for b in range(ATT_BATCH):
                idx = it * ATT_BATCH + b
                r = idx % d
                n = idx // d
                cur = rows_at(n * (ATT_BLOCK * d) + r)
                q = qn_s[g, cur, :]
                q2s.append(jnp.concatenate([jnp.where(lo, q, 0.0), jnp.where(lo, 0.0, q)], axis=0).astype(BF16))
                kc = kn_s[g, cur, :].astype(BF16)
                vc = v_refs[g][0, cur, :].astype(BF16)
                if has_prev:
                    prv = rows_at(jnp.maximum(n - 1, 0) * (ATT_BLOCK * d) + r)
                    kc = jnp.concatenate([kn_s[g, prv, :].astype(BF16), kc], axis=0)
                    vc = jnp.concatenate([v_refs[g][0, prv, :].astype(BF16), vc], axis=0)
                    pens.append(jnp.where(col < ATT_BLOCK, jnp.where(n == 0, NEG_INF, 0.0), 0.0)[None])
                curs.append(cur)
                kcats.append(kc)
                vcats.append(jnp.concatenate([vc, jnp.ones_like(vc)], axis=1))
            q2, kcat, vcat = jnp.stack(q2s), jnp.stack(kcats), jnp.stack(vcats)
            s = lax.dot_general(q2, kcat, (((2,), (2,)), ((0,), (0,))), preferred_element_type=F32)
            if has_prev:
                s = s + bias_ref[g, 0] + jnp.concatenate(pens, axis=0)
            else:
                s = s + bias_ref[g, 0, :, ATT_BLOCK:]
            m = jnp.max(s, axis=-1, keepdims=True)
            p = jnp.exp(s - m).astype(BF16)
            ol = lax.dot_general(p, vcat, (((2,), (1,)), ((0,), (0,))), preferred_element_type=F32)
            for b in range(ATT_BATCH):
                cur = curs[b]
                acc_s[g, cur, :] = jnp.where(lo, ol[b, :ATT_BLOCK, :LANES], ol[b, ATT_BLOCK:, :LANES])
                l_s[g, cur, :] = jnp.where(lo, ol[b, :ATT_BLOCK, LANES:], ol[b, ATT_BLOCK:, LANES:])
                m_s[g, cur, :] = jnp.where(lo, m[b, :ATT_BLOCK], m[b, ATT_BLOCK:])
            return c

        lax.fori_loop(0, d * n_blk // ATT_BATCH, blk_body, 0)

    def comb_body(i, c):
        rs = pl.ds(pl.multiple_of(i * rows_per, rows_per), rows_per)
        m = jnp.maximum(jnp.maximum(m_s[0, rs, :], m_s[1, rs, :]), m_s[2, rs, :])
        num = jnp.zeros((rows_per, LANES), F32)
        den = jnp.zeros((rows_per, LANES), F32)
        for g in range(N_ATT_GROUPS):
            e = jnp.exp(m_s[g, rs, :] - m)
            num = num + e * acc_s[g, rs, :]
            den = den + e * l_s[g, rs, :]
        out_ref[0, rs, :] = num / den
        return c

    lax.fori_loop(0, S // rows_per, comb_body, 0)


def _attention(p3, q_gain, k_gain, bias):
    B, S, _ = p3.shape
    n_pairs = HEADS_PER_GROUP // 2

    def col_spec(base):
        return [pl.BlockSpec((1, S, LANES),
                             functools.partial(lambda b, sp, off: (b, 0, off + sp), off=base + g * n_pairs))
                for g in range(N_ATT_GROUPS)]

    n_qkv_blocks = ATT_WIDTH // LANES
    in_specs = (col_spec(QKV_COL_BLOCK) + col_spec(QKV_COL_BLOCK + n_qkv_blocks)
                + col_spec(QKV_COL_BLOCK + 2 * n_qkv_blocks)
                + [pl.BlockSpec((N_ATT_GROUPS, 1, LANES), lambda b, sp: (0, 0, 0)),
                   pl.BlockSpec((N_ATT_GROUPS, 1, LANES), lambda b, sp: (0, 0, 0)),
                   pl.BlockSpec((N_ATT_GROUPS, 1, 2 * ATT_BLOCK, 2 * ATT_BLOCK),
                                lambda b, sp: (0, sp, 0, 0))])
    return pl.pallas_call(
        _attn_kernel,
        out_shape=jax.ShapeDtypeStruct((B, S, ATT_OUT_WIDTH), F32),
        grid=(B, n_pairs),
        in_specs=in_specs,
        out_specs=pl.BlockSpec((1, S, LANES), lambda b, sp: (b, 0, sp)),
        scratch_shapes=[pltpu.VMEM((N_ATT_GROUPS, S, LANES), F32) for _ in range(5)],
        compiler_params=pltpu.CompilerParams(
            dimension_semantics=("arbitrary", "arbitrary"), vmem_limit_bytes=VMEM_LIMIT),
        name="dilated_attention",
    )(*([p3] * 9), q_gain, k_gain, bias)


def _t5_causal_bucket(dist):
    max_exact = REL_BUCKETS // 2
    d = jnp.maximum(dist.astype(F32), 1.0)
    large = max_exact + (jnp.log(d / max_exact) / math.log(REL_MAX_DIST / max_exact)
                         * (REL_BUCKETS - max_exact)).astype(jnp.int32)
    large = jnp.minimum(large, REL_BUCKETS - 1)
    return jnp.where(dist < max_exact, dist, large)


def _attention_bias(rel_bias_table):
    W = ATT_BLOCK
    qi = jnp.arange(W)[:, None]
    kj = jnp.arange(2 * W)[None, :]
    rel = qi + W - kj
    valid = (rel >= 0) & (rel <= W)
    per_group = []
    for g, (_, d) in enumerate(ATT_PATTERNS):
        tab = rel_bias_table[:, g * HEADS_PER_GROUP:(g + 1) * HEADS_PER_GROUP]
        bucket = _t5_causal_bucket(jnp.clip(rel, 0, W) * d)
        onehot = (bucket[:, :, None] == jnp.arange(REL_BUCKETS)[None, None, :]).astype(F32)
        bias = jnp.einsum('qkb,bh->qkh', onehot, tab.astype(F32),
                          precision=lax.Precision.HIGHEST)
        bias = jnp.where(valid[:, :, None], bias, NEG_INF)
        per_group.append(jnp.transpose(bias, (2, 0, 1)).reshape(2, 2 * W, 2 * W))
    return jnp.stack(per_group, axis=0)


def _rwkv_prep_kernel(p_ref, mu_ref, w0_ref, wup_ref, a0_ref, aup_ref, gup_ref, kk_ref, ka_ref, rk_ref,
                      at_o, rt_o, bt_o, kt_o, bw_o, kw_o, v_o, bonus_o, g_o, wend_o, carry):
    tq = p_ref.shape[1]
    C = RWKV_WIDTH
    L = WKV_CHUNK

    @pl.when(pl.program_id(1) == 0)
    def _():
        carry[...] = jnp.zeros_like(carry)

    p = p_ref[0]
    row = lax.broadcasted_iota(jnp.int32, (tq, 1), 0)
    prev = jnp.where(row == 0, carry[...], pltpu.roll(p, 1, axis=0))
    carry[...] = p[tq - 1:tq, :]
    pm = p + (prev - p) * mu_ref[...]
    r, k, v = pm[:, 0:C], pm[:, C:2 * C], pm[:, 2 * C:3 * C]
    xw = pm[:, 3 * C:3 * C + LANES]
    xa = pm[:, 3 * C + LANES:3 * C + 2 * LANES]
    xg = pm[:, 3 * C + 2 * LANES:]

    z = w0_ref[...] + _dot(jnp.tanh(xw), wup_ref[...])
    softplus_neg = jnp.maximum(-z, 0.0) + jnp.log(1.0 + jnp.exp(-jnp.abs(z)))
    lw = -jnp.exp(-softplus_neg - 0.5)
    a = _sigmoid(a0_ref[...] + _dot(xa, aup_ref[...]))
    g = _dot(_sigmoid(xg), gup_ref[...])

    hr = jnp.right_shift(lax.broadcasted_iota(jnp.int32, (C, C), 0), 6)
    hc = jnp.right_shift(lax.broadcasted_iota(jnp.int32, (C, C), 1), 6)
    ones_bd = jnp.where(hr == hc, 1.0, 0.0).astype(BF16)

    def head_sums(x):
        hi, lo2 = _split2(x)
        return (jnp.dot(hi, ones_bd, preferred_element_type=F32)
                + jnp.dot(lo2, ones_bd, preferred_element_type=F32))

    k2 = k * (1.0 + (a - 1.0) * ka_ref[...])
    kk = k * kk_ref[...]
    kk = kk / jnp.maximum(jnp.sqrt(head_sums(kk * kk)), 1e-12)
    bonus = head_sums(r * k2 * rk_ref[...]) * v
    b = kk * a

    half = 256
    ri = lax.broadcasted_iota(jnp.int32, (half, half), 0)
    ci = lax.broadcasted_iota(jnp.int32, (half, half), 1)
    tri = jnp.where((jnp.right_shift(ri, 6) == jnp.right_shift(ci, 6)) & (ri >= ci), 1.0, 0.0).astype(BF16)
    cums = []
    for s in range(tq // half):
        hi, mid, lo3 = _split3(lw[s * half:(s + 1) * half, :])
        cums.append(jnp.dot(tri, hi, preferred_element_type=F32)
                    + (jnp.dot(tri, mid, preferred_element_type=F32)
                       + jnp.dot(tri, lo3, preferred_element_type=F32)))
    cum = jnp.concatenate(cums, axis=0)
    n_chunks = tq // L
    ends = [cum[c * L + L - 1:c * L + L, :] for c in range(n_chunks)]
    tot = jnp.concatenate([jnp.broadcast_to(e, (L, C)) for e in ends], axis=0)
    wend = jnp.exp(jnp.concatenate(ends, axis=0))

    e_pos = jnp.exp(cum)
    e_neg = jnp.exp(-cum)
    e_rem = jnp.exp(tot - cum)
    at = -kk * jnp.exp(cum - lw)
    rt = r * e_pos
    bt = b * e_neg
    kt = k2 * e_neg
    bw = b * e_rem
    kw = k2 * e_rem
    at_o[0] = at.astype(BF16)
    rt_o[0] = rt.astype(BF16)
    bt_o[0] = bt.astype(BF16)
    kt_o[0] = kt.astype(BF16)
    bw_o[0] = bw.astype(BF16)
    kw_o[0] = kw.astype(BF16)
    v_o[0] = v.astype(BF16)
    bonus_o[0] = bonus
    g_o[0] = g
    wend_o[0] = wend


def _rwkv_prep(p3, mu, w0, wup, a0, aup, gup, k_k, k_a, r_k):
    B, S, _ = p3.shape
    tq = 512
    C = RWKV_WIDTH
    tm_shape = lambda dt: jax.ShapeDtypeStruct((B, S, C), dt)
    tm_spec = pl.BlockSpec((1, tq, C), lambda b, j: (b, j, 0))
    full = lambda shape: pl.BlockSpec(shape, lambda b, j: (0,) * len(shape))
    return pl.pallas_call(
        _rwkv_prep_kernel,
        out_shape=[tm_shape(BF16)] * 7 + [tm_shape(F32)] * 2 + [jax.ShapeDtypeStruct((B, S // WKV_CHUNK, C), F32)],
        grid=(B, S // tq),
        in_specs=[pl.BlockSpec((1, tq, RWKV_PAD_WIDTH), lambda b, j: (b, j, 0)),
                  full((1, RWKV_PAD_WIDTH)), full((1, C)), full((LANES, C)), full((1, C)),
                  full((LANES, C)), full((2 * LANES, C)), full((1, C)), full((1, C)), full((1, C))],
        out_specs=[tm_spec] * 9 + [pl.BlockSpec((1, tq // WKV_CHUNK, C), lambda b, j: (b, j, 0))],
        scratch_shapes=[pltpu.VMEM((1, RWKV_PAD_WIDTH), F32)],
        compiler_params=pltpu.CompilerParams(
            dimension_semantics=("arbitrary", "arbitrary"), vmem_limit_bytes=VMEM_LIMIT),
        name="rwkv_prep",
    )(p3, mu, w0, wup, a0, aup, gup, k_k, k_a, r_k)


def _dot_tn(a, b):
    return lax.dot_general(a.astype(BF16), b.astype(BF16), (((0,), (0,)), ((), ())),
                           preferred_element_type=F32)


WKV_GROUP = 4
WKV_GROUP_WIDTH = WKV_GROUP * RWKV_HEAD_DIM


def _wkv_kernel(at_ref, rt_ref, bt_ref, kt_ref, bw_ref, kw_ref, v_ref, bonus_ref, g_ref, wend_ref,
                lnw_ref, lnb_ref, o_ref, state, m_s, u0_s, o0_s, arb_s, p_s, c_s, sh_s):
    N, C, GW = RWKV_HEAD_DIM, WKV_CHUNK, WKV_GROUP_WIDTH
    n_groups = RWKV_WIDTH // GW
    n_sub = at_ref.shape[1] // C
    j = pl.program_id(1)

    @pl.when(j == 0)
    def _():
        state[...] = jnp.zeros_like(state)

    def iota(shape, dim):
        return lax.broadcasted_iota(jnp.int32, shape, dim)

    same_head = jnp.right_shift(iota((1, GW, GW), 1), 6) == jnp.right_shift(iota((1, GW, GW), 2), 6)
    ones_bd = jnp.where(same_head, 1.0, 0.0).astype(BF16)
    g_row = iota((1, 2 * C, 2 * GW), 1)
    g_col = jnp.bitwise_and(iota((1, 2 * C, 2 * GW), 2), N - 1)
    band = ((g_row < C) & (g_row > g_col)) | ((g_row >= C) & ((g_row - C) >= g_col))
    eye = jnp.where(iota((1, C, GW), 1) == jnp.bitwise_and(iota((1, C, GW), 2), N - 1), 1.0, 0.0)

    def bmm(a, b, dims):
        return lax.dot_general(a.astype(BF16), b.astype(BF16), (dims, ((0,), (0,))),
                               preferred_element_type=F32)

    nn, nt, tn = ((2,), (1,)), ((2,), (2,)), ((1,), (1,))

    def bd(x):
        xb = x.astype(BF16)
        return jnp.where(same_head, jnp.concatenate([xb] * WKV_GROUP, axis=1), jnp.zeros((), BF16))

    span = 4
    rows = span * C

    def stack(x):
        x3 = x.reshape(span, C, n_groups * GW)
        return jnp.concatenate([x3[:, :, q * GW:(q + 1) * GW] for q in range(n_groups)], axis=0)

    def unstack(y):
        return jnp.concatenate([y[q * span:(q + 1) * span].reshape(rows, GW) for q in range(n_groups)],
                               axis=1)

    for i in range(n_sub // span):
        rs = slice(i * rows, (i + 1) * rows)
        at = stack(at_ref[0, rs, :])
        ar = jnp.concatenate([at, stack(rt_ref[0, rs, :])], axis=1)
        bk_bd = jnp.concatenate([bd(stack(bt_ref[0, rs, :])), bd(stack(kt_ref[0, rs, :]))], axis=1)
        gm = jnp.where(band, bmm(ar, bk_bd, nt), 0.0)
        a_ab = gm[:, :C, :GW]
        inv = eye + a_ab
        pw = bmm(a_ab, bd(a_ab), nn)
        for _ in range(int(math.log2(C)) - 2):
            prod = bmm(jnp.concatenate([inv, pw], axis=1), bd(pw), nn)
            inv = inv + prod[:, :C]
            pw = prod[:, C:]
        inv = inv + bmm(inv, bd(pw), nn)
        v = stack(v_ref[0, rs, :])
        kv = bmm(gm[:, :, GW:], bd(v), nn)
        mu0 = bmm(inv, jnp.concatenate([bd(at), bd(kv[:, :C])], axis=2), nn)
        m = mu0[:, :, :GW].astype(BF16)
        u0 = mu0[:, :, GW:]
        bw = stack(bw_ref[0, rs, :])
        low_rank = jnp.where(same_head, bmm(m, bw, tn), 0.0)
        const = jnp.where(same_head, bmm(jnp.concatenate([u0.astype(BF16), v], axis=1),
                                         jnp.concatenate([bw, stack(kw_ref[0, rs, :])], axis=1), tn), 0.0)
        m_s[rs, :] = unstack(m)
        u0_s[rs, :] = unstack(u0)
        o0_s[rs, :] = unstack(kv[:, C:])
        arb_s[rs, :] = unstack(gm[:, C:, :GW]).astype(BF16)
        for q in range(n_groups):
            p_s[i * span:(i + 1) * span, q] = low_rank[q * span:(q + 1) * span].astype(BF16)
            c_s[i * span:(i + 1) * span, q] = const[q * span:(q + 1) * span]

    def groups(x):
        return jnp.stack([x[:, q * GW:(q + 1) * GW] for q in range(n_groups)], axis=0)

    def chunk_state(i, c):
        s0 = state[...]
        sh_s[i] = s0.astype(BF16)
        w_end = groups(wend_ref[0, pl.ds(j * n_sub + i, 1), :])
        state[...] = s0 * w_end + bmm(s0, p_s[i], nn) + c_s[i]
        return c

    lax.fori_loop(0, n_sub, chunk_state, 0, unroll=True)

    ones_p = jnp.broadcast_to(ones_bd, (n_groups * span, GW, GW))
    for i in range(n_sub // span):
        rs = slice(i * rows, (i + 1) * rows)
        s0 = jnp.concatenate([sh_s[i * span:(i + 1) * span, q] for q in range(n_groups)], axis=0)
        mr_s = bmm(jnp.concatenate([stack(m_s[rs, :]), stack(rt_ref[0, rs, :])], axis=1), s0, nt)
        u = mr_s[:, :C] + stack(u0_s[rs, :])
        o = mr_s[:, C:] + stack(o0_s[rs, :]) + bmm(stack(arb_s[rs, :]), bd(u), nn)
        hi, lo2 = _split2(o)
        s1 = bmm(jnp.concatenate([hi, lo2], axis=1), ones_p, nn)
        dlt = o - (s1[:, :C] + s1[:, C:]) * (1.0 / N)
        hi, lo2 = _split2(dlt * dlt)
        s2 = bmm(jnp.concatenate([hi, lo2], axis=1), ones_p, nn)
        on = unstack(dlt * lax.rsqrt((s2[:, :C] + s2[:, C:]) * (1.0 / N) + GN_EPS))
        on = on * lnw_ref[...] + lnb_ref[...]
        o_ref[0, rs, :] = (on + bonus_ref[0, rs, :]) * g_ref[0, rs, :]


def _wkv(at, rt, bt, kt, bw, kw, v, bonus, g, wend, ln_w, ln_b):
    B, S, C = at.shape
    tc = 512
    tm_spec = pl.BlockSpec((1, tc, C), lambda bb, j: (bb, j, 0))
    par_spec = pl.BlockSpec((1, C), lambda bb, j: (0, 0))
    blocks = (C // WKV_GROUP_WIDTH, WKV_GROUP_WIDTH, WKV_GROUP_WIDTH)
    return pl.pallas_call(
        _wkv_kernel,
        out_shape=jax.ShapeDtypeStruct((B, S, C), F32),
        grid=(B, S // tc),
        in_specs=([tm_spec] * 9 + [pl.BlockSpec((1, S // WKV_CHUNK, C), lambda bb, j: (bb, 0, 0))]
                  + [par_spec] * 2),
        out_specs=tm_spec,
        scratch_shapes=[pltpu.VMEM(blocks, F32),
                        pltpu.VMEM((tc, C), BF16), pltpu.VMEM((tc, C), F32), pltpu.VMEM((tc, C), F32),
                        pltpu.VMEM((tc, C), BF16),
                        pltpu.VMEM((tc // WKV_CHUNK,) + blocks, BF16), pltpu.VMEM((tc // WKV_CHUNK,) + blocks, F32),
                        pltpu.VMEM((tc // WKV_CHUNK,) + blocks, BF16)],
        compiler_params=pltpu.CompilerParams(
            dimension_semantics=("arbitrary", "arbitrary"), vmem_limit_bytes=VMEM_LIMIT),
        name="wkv7",
    )(at, rt, bt, kt, bw, kw, v, bonus, g, wend, ln_w, ln_b)


def _merge_kernel(att_ref, o_ref, gate_ref, x_ref, wa_ref, wr_ref, wo_ref, g2_ref, wrt_ref, brt_ref,
                  x1_ref, h2_ref, lg_ref, cnt_ref):
    y_att = _dot(att_ref[0], wa_ref[...])
    y_rwkv = _dot(o_ref[0], wr_ref[...])
    gates = _sigmoid(gate_ref[0])
    mixed = gates[:, :D_MODEL] * y_att + gates[:, D_MODEL:] * y_rwkv
    x1 = x_ref[0] + _dot(mixed, wo_ref[...])
    x1_ref[0] = x1
    ms = jnp.mean(x1 * x1, axis=-1, keepdims=True)
    h2 = x1 * lax.rsqrt(ms + NORM_EPS) * g2_ref[...]
    hh, hl = _split2(h2)
    half = D_MODEL // 2
    rounded = hh.astype(F32)
    w_hi = pltpu.bitcast(rounded[:, :half], jnp.uint32)
    w_lo = pltpu.bitcast(rounded[:, half:], jnp.uint32)
    h2_ref[0] = w_hi | jnp.right_shift(w_lo, jnp.uint32(16))
    wh, wl = wrt_ref[0], wrt_ref[1]
    lg = (jnp.dot(hh, wh, preferred_element_type=F32)
          + (jnp.dot(hh, wl, preferred_element_type=F32) + jnp.dot(hl, wh, preferred_element_type=F32)))
    lg = lg + brt_ref[...]

    lane = lax.broadcasted_iota(jnp.int32, (1, ROUTER_PAD), 1)
    gmask = lane < N_EXPERT_GROUPS
    gl = jnp.where(gmask, lg, -jnp.inf)
    gmax = jnp.max(gl, axis=-1, keepdims=True)
    grp_idx = jnp.min(jnp.where(gl == gmax, lane, ROUTER_PAD), axis=-1, keepdims=True)
    grp_p = 1.0 / jnp.sum(jnp.where(gmask, jnp.exp(lg - gmax), 0.0), axis=-1, keepdims=True)
    e_lane = lane - N_EXPERT_GROUPS
    emask = ((e_lane >= 0) & (e_lane < N_EXPERTS)
             & (jnp.right_shift(e_lane, int(math.log2(EXPERTS_PER_GROUP))) == grp_idx))
    el = jnp.where(emask, lg, -jnp.inf)
    m1 = jnp.max(el, axis=-1, keepdims=True)
    i1 = jnp.min(jnp.where(el == m1, lane, ROUTER_PAD), axis=-1, keepdims=True)
    el2 = jnp.where(lane == i1, -jnp.inf, el)
    m2 = jnp.max(el2, axis=-1, keepdims=True)
    i2 = jnp.min(jnp.where(el2 == m2, lane, ROUTER_PAD), axis=-1, keepdims=True)
    e2 = jnp.exp(m2 - m1)
    gate1 = grp_p / (1.0 + e2)
    gate2 = gate1 * e2

    @pl.when((pl.program_id(0) == 0) & (pl.program_id(1) == 0))
    def _():
        cnt_ref[...] = jnp.zeros_like(cnt_ref)

    tm = lg.shape[0]
    oh = jnp.where((lane == i1) | (lane == i2), 1.0, 0.0)
    ti = lax.broadcasted_iota(jnp.int32, (tm, tm), 0)
    tj = lax.broadcasted_iota(jnp.int32, (tm, tm), 1)
    before = jnp.where(tj < ti, 1.0, 0.0).astype(BF16)
    pref = jnp.dot(before, oh.astype(BF16), preferred_element_type=F32) + cnt_ref[...]
    rank1 = jnp.sum(jnp.where(lane == i1, pref, 0.0), axis=-1, keepdims=True)
    rank2 = jnp.sum(jnp.where(lane == i2, pref, 0.0), axis=-1, keepdims=True)
    cnt_ref[...] = cnt_ref[...] + jnp.sum(oh, axis=0, keepdims=True)
    info = jnp.where(lane == 0, (i1 - N_EXPERT_GROUPS).astype(F32), 0.0)
    info = jnp.where(lane == 1, (i2 - N_EXPERT_GROUPS).astype(F32), info)
    info = jnp.where(lane == 2, rank1, info)
    info = jnp.where(lane == 3, rank2, info)
    info = jnp.where(lane == 4, gate1, info)
    info = jnp.where(lane == 5, gate2, info)
    lg_ref[0] = info


def _merge(att, o_g, p3, x, w_att_out, w_rwkv_out, w_out, gain2, w_router, b_router):
    B, S, D = x.shape
    tm = 512
    full = lambda shape: pl.BlockSpec(shape, lambda b, j: (0,) * len(shape))
    return pl.pallas_call(
        _merge_kernel,
        out_shape=[jax.ShapeDtypeStruct((B, S, D), F32), jax.ShapeDtypeStruct((B, S, D // 2), jnp.uint32),
                   jax.ShapeDtypeStruct((B, S, ROUTER_PAD), F32),
                   jax.ShapeDtypeStruct((1, ROUTER_PAD), F32)],
        grid=(B, S // tm),
        in_specs=[pl.BlockSpec((1, tm, ATT_OUT_WIDTH), lambda b, j: (b, j, 0)),
                  pl.BlockSpec((1, tm, RWKV_WIDTH), lambda b, j: (b, j, 0)),
                  pl.BlockSpec((1, tm, 2 * D_MODEL), lambda b, j: (b, j, 1)),
                  pl.BlockSpec((1, tm, D), lambda b, j: (b, j, 0)),
                  full((ATT_OUT_WIDTH, D)), full((RWKV_WIDTH, D)), full((D, D)),
                  full((1, D)), full((2, D, ROUTER_PAD)), full((1, ROUTER_PAD))],
        out_specs=[pl.BlockSpec((1, tm, D), lambda b, j: (b, j, 0)),
                   pl.BlockSpec((1, tm, D // 2), lambda b, j: (b, j, 0)),
                   pl.BlockSpec((1, tm, ROUTER_PAD), lambda b, j: (b, j, 0)),
                   pl.BlockSpec((1, ROUTER_PAD), lambda b, j: (0, 0))],
        compiler_params=pltpu.CompilerParams(
            dimension_semantics=("arbitrary", "arbitrary"), vmem_limit_bytes=VMEM_LIMIT),
        name="merge_outproj_router",
    )(att, o_g, p3, x, w_att_out, w_rwkv_out, w_out, gain2, w_router, b_router)


def _expert_kernel(be_ref, nb_ref, x_ref, wg_ref, wu_ref, wd_ref, y_ref, wgu_s, wd_s):
    i = pl.program_id(0)

    @pl.when((i == 0) | (be_ref[i] != be_ref[jnp.maximum(i - 1, 0)]))
    def _():
        wgu_s[:, :D_EXPERT] = wg_ref[0].astype(BF16)
        wgu_s[:, D_EXPERT:] = wu_ref[0].astype(BF16)
        wd_s[...] = wd_ref[0].astype(BF16)

    @pl.when(i < nb_ref[0])
    def _():
        words = x_ref[...]
        x_a = pltpu.bitcast(words & jnp.uint32(0xFFFF0000), F32)
        x_b = pltpu.bitcast(jnp.left_shift(words, jnp.uint32(16)), F32)
        xb = jnp.concatenate([x_a, x_b], axis=-1).astype(BF16)
        h = jnp.dot(xb, wgu_s[...], preferred_element_type=F32)
        hg, hu = h[:, :D_EXPERT], h[:, D_EXPERT:]
        hid = hg * _sigmoid(hg) * hu
        y_ref[...] = jnp.dot(hid.astype(BF16), wd_s[...], preferred_element_type=F32)

    @pl.when(i >= nb_ref[0])
    def _():
        y_ref[...] = jnp.zeros_like(y_ref)


def _experts(blk_expert, n_used, xin, w_gate, w_up, w_down):
    n_rows = xin.shape[0]
    D = D_MODEL
    n_blocks = n_rows // MOE_ROWS
    return pl.pallas_call(
        _expert_kernel,
        out_shape=jax.ShapeDtypeStruct((n_rows, D), F32),
        grid_spec=pltpu.PrefetchScalarGridSpec(
            num_scalar_prefetch=2,
            grid=(n_blocks,),
            scratch_shapes=[pltpu.VMEM((D, 2 * D_EXPERT), BF16), pltpu.VMEM((D_EXPERT, D), BF16)],
            in_specs=[pl.BlockSpec((MOE_ROWS, D // 2), lambda i, be, nb: (i, 0)),
                      pl.BlockSpec((1, D, D_EXPERT), lambda i, be, nb: (be[i], 0, 0)),
                      pl.BlockSpec((1, D, D_EXPERT), lambda i, be, nb: (be[i], 0, 0)),
                      pl.BlockSpec((1, D_EXPERT, D), lambda i, be, nb: (be[i], 0, 0))],
            out_specs=pl.BlockSpec((MOE_ROWS, D), lambda i, be, nb: (i, 0))),
        compiler_params=pltpu.CompilerParams(
            dimension_semantics=("arbitrary",), vmem_limit_bytes=VMEM_LIMIT),
        name="moe_experts",
    )(blk_expert, n_used, xin, w_gate, w_up, w_down)


def _row_token_kernel(dest_ref, out_ref, *, n_tokens):
    n = dest_ref.shape[0]
    n_rows = out_ref.shape[0]
    step = pl.program_id(0)

    @pl.when(step == 0)
    def _():
        for base in range(0, n_rows, n_tokens):
            def fill(t, c, base=base):
                out_ref[base + t] = t
                return c
            lax.fori_loop(0, min(n_tokens, n_rows - base), fill, 0, unroll=8)

    def put(a, c):
        out_ref[dest_ref[a]] = (step * n + a) // TOP_K
        return c

    lax.fori_loop(0, n, put, 0, unroll=8)


def _row_tokens(dest_flat, n_rows):
    A = dest_flat.shape[0]
    chunk = 8192
    return pl.pallas_call(
        functools.partial(_row_token_kernel, n_tokens=A // TOP_K),
        out_shape=jax.ShapeDtypeStruct((n_rows,), jnp.int32),
        grid=(A // chunk,),
        in_specs=[pl.BlockSpec((chunk,), lambda i: (i,), memory_space=pltpu.SMEM)],
        out_specs=pl.BlockSpec((n_rows,), lambda i: (0,), memory_space=pltpu.SMEM),
        compiler_params=pltpu.CompilerParams(dimension_semantics=("arbitrary",)),
        name="row_tokens",
    )(dest_flat)


def _row_layout(info, counts, T):
    A = T * TOP_K
    eid = info[:, 0:2].astype(jnp.int32)
    rank = info[:, 2:4].astype(jnp.int32)
    gates = info[:, 4:6]
    counts = counts.astype(jnp.int32)
    pcounts = (counts + MOE_ROWS - 1) // MOE_ROWS * MOE_ROWS
    pends = jnp.cumsum(pcounts)
    pstarts = pends - pcounts
    dest = (pstarts[eid] + rank).astype(jnp.int32)
    n_blocks = -(-A // MOE_ROWS) + N_EXPERTS
    n_rows = n_blocks * MOE_ROWS
    row_tok = _row_tokens(dest.reshape(A), n_rows)
    blk_start = jnp.arange(n_blocks, dtype=jnp.int32) * MOE_ROWS
    blk_expert = jnp.minimum(jnp.sum(blk_start[:, None] >= pends[None, :], axis=1), N_EXPERTS - 1)
    n_used = (pends[N_EXPERTS - 1] // MOE_ROWS).astype(jnp.int32)[None]
    return row_tok, blk_expert.astype(jnp.int32), n_used, dest, gates


def _pad_rows(w, rows):
    return jnp.pad(w, ((0, rows - w.shape[0]), (0, 0)))


def kernel(x, norm1_gain, w_in, q_norm_gain, k_norm_gain, rel_bias_table, w_att_out, rwkv_shift_mu,
           rwkv_w0, rwkv_w_up, rwkv_a0, rwkv_a_up, rwkv_g_up, rwkv_k_k, rwkv_k_a, rwkv_r_k,
           rwkv_ln_w, rwkv_ln_b, w_rwkv_out, w_out, norm2_gain, w_group_router, b_group_router,
           w_expert_router, b_expert_router, w_expert_gate, w_expert_up, w_expert_down):
    B, S, D = x.shape
    T = B * S
    C = RWKV_WIDTH
    bias = _attention_bias(rel_bias_table)
    for l in range(norm1_gain.shape[0]):
        wi = w_in[l]
        w_qkv = wi[:, :3 * ATT_WIDTH]
        w_rw = wi[:, 3 * ATT_WIDTH:3 * ATT_WIDTH + RWKV_SHIFT_WIDTH]
        w_gt = wi[:, 3 * ATT_WIDTH + RWKV_SHIFT_WIDTH:]
        pad_cols = lambda w, n: jnp.pad(w, ((0, 0), (0, n - w.shape[1])))
        o_w, o_a, o_g = 3 * C, 3 * C + DECAY_LORA, 3 * C + DECAY_LORA + ICLR_LORA

        def rw_layout(w):
            return jnp.concatenate([w[:, :o_w], pad_cols(w[:, o_w:o_a], LANES),
                                    pad_cols(w[:, o_a:o_g], LANES), pad_cols(w[:, o_g:], 2 * LANES)], axis=1)

        w_all = jnp.concatenate([rw_layout(w_rw), w_gt, w_qkv], axis=1).astype(BF16)
        mu = rw_layout(rwkv_shift_mu[l][None, :])

        p = _inproj(x.reshape(T, D), norm1_gain[l][None, :], w_all)
        p3 = p.reshape(B, S, P_WIDTH)

        tile2 = lambda gmat: jnp.tile(gmat, (1, 2))[:, None, :]
        att = _attention(p3, tile2(q_norm_gain[l]), tile2(k_norm_gain[l]), bias)

        row = lambda v: v[None, :]
        prep = _rwkv_prep(
            p3, mu, row(rwkv_w0[l]), _pad_rows(rwkv_w_up[l], LANES).astype(BF16), row(rwkv_a0[l]),
            _pad_rows(rwkv_a_up[l], LANES).astype(BF16), _pad_rows(rwkv_g_up[l], 2 * LANES).astype(BF16),
            row(rwkv_k_k[l]), row(rwkv_k_a[l]), rwkv_r_k[l].reshape(1, C))
        o_g = _wkv(*prep, row(rwkv_ln_w[l]), row(rwkv_ln_b[l]))

        w_router = jnp.concatenate([w_group_router[l], w_expert_router[l]], axis=1)
        w_router = jnp.pad(w_router, ((0, 0), (0, ROUTER_PAD - w_router.shape[1])))
        wr_hi = w_router.astype(BF16)
        wr_lo = (w_router - wr_hi.astype(F32)).astype(BF16)
        b_router = jnp.concatenate([b_group_router[l], b_expert_router[l]])
        b_router = jnp.pad(b_router, (0, ROUTER_PAD - b_router.shape[0]))[None, :]
        x1, h2, info, counts = _merge(
            att, o_g, p3, x, w_att_out[l].astype(BF16),
            w_rwkv_out[l].astype(BF16), w_out[l].astype(BF16),
            norm2_gain[l][None, :], jnp.stack([wr_hi, wr_lo]), b_router)

        row_tok, blk_expert, n_used, dest, gates = _row_layout(
            info.reshape(T, ROUTER_PAD), counts[0, N_EXPERT_GROUPS:N_EXPERT_GROUPS + N_EXPERTS], T)
        xin = h2.reshape(T, D // 2)[row_tok]
        yb = _experts(blk_expert, n_used, xin, w_expert_gate[l], w_expert_up[l], w_expert_down[l])
        y = gates[:, 0:1] * yb[dest[:, 0]] + gates[:, 1:2] * yb[dest[:, 1]]
        x = x1 + y.reshape(B, S, D)
    return x
```

```python
import functools
import math

import jax
import jax.numpy as jnp
from jax import lax
from jax.experimental import pallas as pl
from jax.experimental.pallas import tpu as pltpu

F32 = jnp.float32
BF16 = jnp.bfloat16

D_MODEL = 1024
ATT_PATTERNS = ((128, 1), (512, 4), (2048, 16))
N_ATT_GROUPS = 3
HEADS_PER_GROUP = 4
ATT_HEADS = 12
HEAD_DIM = 64
ATT_WIDTH = 768
ATT_OUT_WIDTH = 256
REL_BUCKETS = 32
REL_MAX_DIST = 2048
RWKV_HEADS = 8
RWKV_HEAD_DIM = 64
RWKV_WIDTH = 512
DECAY_LORA = 64
ICLR_LORA = 64
GATE_LORA = 160
RWKV_SHIFT_WIDTH = 1824
GN_EPS = 64e-5
N_EXPERT_GROUPS = 4
EXPERTS_PER_GROUP = 8
N_EXPERTS = 32
TOP_K = 2
D_EXPERT = 512
NORM_EPS = 1e-6
NEG_INF = -1e30

LANES = 128
ATT_BLOCK = 128
ATT_BATCH = 4
WKV_CHUNK = 64
RWKV_PAD_WIDTH = 2048
QKV_COL_BLOCK = 0
ROUTER_PAD = 128
MOE_ROWS = 512
VMEM_LIMIT = 56 * 1024 * 1024


def _sigmoid(x):
    return 1.0 / (1.0 + jnp.exp(-x))


def _dot(a, b):
    return jnp.dot(a.astype(BF16), b.astype(BF16), preferred_element_type=F32)


def _dot_t(a, b):
    return lax.dot_general(a.astype(BF16), b.astype(BF16), (((1,), (1,)), ((), ())),
                           preferred_element_type=F32)


def _split3(a):
    hi = a.astype(BF16)
    r1 = a - hi.astype(F32)
    mid = r1.astype(BF16)
    lo = (r1 - mid.astype(F32)).astype(BF16)
    return hi, mid, lo


def _split2(a):
    hi = a.astype(BF16)
    lo = (a - hi.astype(F32)).astype(BF16)
    return hi, lo


INPROJ_COLS = 512


def _inproj_kernel(x_ref, g_ref, wrg_ref, wqkv_ref, rg_ref, qkv_ref):
    x = x_ref[...]
    ms = jnp.mean(x * x, axis=-1, keepdims=True)
    h = (x * lax.rsqrt(ms + NORM_EPS) * g_ref[...]).astype(BF16)
    for w_ref, o_ref in ((wrg_ref, rg_ref), (wqkv_ref, qkv_ref)):
        width = w_ref.shape[1]
        for c in range(0, width, INPROJ_COLS):
            cs = slice(c, min(c + INPROJ_COLS, width))
            o_ref[:, cs] = jnp.dot(h, w_ref[:, cs], preferred_element_type=F32).astype(o_ref.dtype)


def _inproj(x2, gain, w_rg, w_qkv):
    T = x2.shape[0]
    tm = 512
    n_rg, n_qkv = w_rg.shape[1], w_qkv.shape[1]
    return pl.pallas_call(
        _inproj_kernel,
        out_shape=[jax.ShapeDtypeStruct((T, n_rg), BF16), jax.ShapeDtypeStruct((T, n_qkv), F32)],
        grid=(T // tm,),
        in_specs=[pl.BlockSpec((tm, D_MODEL), lambda i: (i, 0)),
                  pl.BlockSpec((1, D_MODEL), lambda i: (0, 0)),
                  pl.BlockSpec((D_MODEL, n_rg), lambda i: (0, 0)),
                  pl.BlockSpec((D_MODEL, n_qkv), lambda i: (0, 0))],
        out_specs=[pl.BlockSpec((tm, n_rg), lambda i: (i, 0)),
                   pl.BlockSpec((tm, n_qkv), lambda i: (i, 0))],
        compiler_params=pltpu.CompilerParams(
            dimension_semantics=("arbitrary",), vmem_limit_bytes=VMEM_LIMIT),
        name="inproj",
    )(x2, gain, w_rg, w_qkv)


def _attn_kernel(q0, q1, q2, k0, k1, k2, v0, v1, v2, qg_ref, kg_ref, bias_ref, out_ref,
                 qn_s, kn_s, acc_s, m_s, l_s):
    S = out_ref.shape[1]
    q_refs, k_refs, v_refs = (q0, q1, q2), (k0, k1, k2), (v0, v1, v2)
    lane = lax.broadcasted_iota(jnp.int32, (1, LANES), 1)
    lo = lane < HEAD_DIM
    col = lax.broadcasted_iota(jnp.int32, (1, 2 * ATT_BLOCK), 1)

    same_head = (lax.broadcasted_iota(jnp.int32, (LANES, LANES), 0) < HEAD_DIM) == (
        lax.broadcasted_iota(jnp.int32, (LANES, LANES), 1) < HEAD_DIM)
    ones_bd = jnp.where(same_head, 1.0, 0.0).astype(BF16)

    def pair_norm(x, gain):
        hi, lo2 = _split2(x * x)
        ss = (jnp.dot(hi, ones_bd, preferred_element_type=F32)
              + jnp.dot(lo2, ones_bd, preferred_element_type=F32))
        return x * lax.rsqrt(ss * (1.0 / HEAD_DIM) + NORM_EPS) * gain

    rows_per = 256

    def norm_body(i, c):
        rs = pl.ds(pl.multiple_of(i * rows_per, rows_per), rows_per)
        for g in range(N_ATT_GROUPS):
            qn_s[g, rs, :] = pair_norm(q_refs[g][0, rs, :], qg_ref[g]) * (HEAD_DIM ** -0.5)
            kn_s[g, rs, :] = pair_norm(k_refs[g][0, rs, :], kg_ref[g])
        return c

    lax.fori_loop(0, S // rows_per, norm_body, 0)

    for g, (window, d) in enumerate(ATT_PATTERNS):
        n_blk = (S // d) // ATT_BLOCK
        has_prev = n_blk > 1

        def rows_at(start, d=d):
            if d == 1:
                return pl.ds(pl.multiple_of(start, ATT_BLOCK), ATT_BLOCK)
            return pl.ds(start, ATT_BLOCK, stride=d)

        def blk_body(it, c, g=g, d=d, has_prev=has_prev, rows_at=rows_at):
            curs, q2s, kcats, vcats, pens = [], [], [], [], []
            for b in range(ATT_BATCH):
                idx = it * ATT_BATCH + b
                r = idx % d
                n = idx // d
                cur = rows_at(n * (ATT_BLOCK * d) + r)
                q = qn_s[g, cur, :]
                q2s.append(jnp.concatenate([jnp.where(lo, q, 0.0), jnp.where(lo, 0.0, q)], axis=0).astype(BF16))
                kc = kn_s[g, cur, :].astype(BF16)
                vc = v_refs[g][0, cur, :].astype(BF16)
                if has_prev:
                    prv = rows_at(jnp.maximum(n - 1, 0) * (ATT_BLOCK * d) + r)
                    kc = jnp.concatenate([kn_s[g, prv, :].astype(BF16), kc], axis=0)
                    vc = jnp.concatenate([v_refs[g][0, prv, :].astype(BF16), vc], axis=0)
                    pens.append(jnp.where(col < ATT_BLOCK, jnp.where(n == 0, NEG_INF, 0.0), 0.0)[None])
                curs.append(cur)
                kcats.append(kc)
                vcats.append(jnp.concatenate([vc, jnp.ones_like(vc)], axis=1))
            q2, kcat, vcat = jnp.stack(q2s), jnp.stack(kcats), jnp.stack(vcats)
            s = lax.dot_general(q2, kcat, (((2,), (2,)), ((0,), (0,))), preferred_element_type=F32)
            if has_prev:
                s = s + bias_ref[g, 0] + jnp.concatenate(pens, axis=0)
            else:
                s = s + bias_ref[g, 0, :, ATT_BLOCK:]
            m = jnp.max(s, axis=-1, keepdims=True)
            p = jnp.exp(s - m).astype(BF16)
            ol = lax.dot_general(p, vcat, (((2,), (1,)), ((0,), (0,))), preferred_element_type=F32)
            for b in range(ATT_BATCH):
                cur = curs[b]
                acc_s[g, cur, :] = jnp.where(lo, ol[b, :ATT_BLOCK, :LANES], ol[b, ATT_BLOCK:, :LANES])
                l_s[g, cur, :] = jnp.where(lo, ol[b, :ATT_BLOCK, LANES:], ol[b, ATT_BLOCK:, LANES:])
                m_s[g, cur, :] = jnp.where(lo, m[b, :ATT_BLOCK], m[b, ATT_BLOCK:])
            return c

        lax.fori_loop(0, d * n_blk // ATT_BATCH, blk_body, 0)

    def comb_body(i, c):
        rs = pl.ds(pl.multiple_of(i * rows_per, rows_per), rows_per)
        m = jnp.maximum(jnp.maximum(m_s[0, rs, :], m_s[1, rs, :]), m_s[2, rs, :])
        num = jnp.zeros((rows_per, LANES), F32)
        den = jnp.zeros((rows_per, LANES), F32)
        for g in range(N_ATT_GROUPS):
            e = jnp.exp(m_s[g, rs, :] - m)
            num = num + e * acc_s[g, rs, :]
            den = den + e * l_s[g, rs, :]
        out_ref[0, rs, :] = num / den
        return c

    lax.fori_loop(0, S // rows_per, comb_body, 0)


def _attention(p3, q_gain, k_gain, bias):
    B, S, _ = p3.shape
    n_pairs = HEADS_PER_GROUP // 2

    def col_spec(base):
        return [pl.BlockSpec((1, S, LANES),
                             functools.partial(lambda b, sp, off: (b, 0, off + sp), off=base + g * n_pairs))
                for g in range(N_ATT_GROUPS)]

    n_qkv_blocks = ATT_WIDTH // LANES
    in_specs = (col_spec(QKV_COL_BLOCK) + col_spec(QKV_COL_BLOCK + n_qkv_blocks)
                + col_spec(QKV_COL_BLOCK + 2 * n_qkv_blocks)
                + [pl.BlockSpec((N_ATT_GROUPS, 1, LANES), lambda b, sp: (0, 0, 0)),
                   pl.BlockSpec((N_ATT_GROUPS, 1, LANES), lambda b, sp: (0, 0, 0)),
                   pl.BlockSpec((N_ATT_GROUPS, 1, 2 * ATT_BLOCK, 2 * ATT_BLOCK),
                                lambda b, sp: (0, sp, 0, 0))])
    return pl.pallas_call(
        _attn_kernel,
        out_shape=jax.ShapeDtypeStruct((B, S, ATT_OUT_WIDTH), F32),
        grid=(B, n_pairs),
        in_specs=in_specs,
        out_specs=pl.BlockSpec((1, S, LANES), lambda b, sp: (b, 0, sp)),
        scratch_shapes=[pltpu.VMEM((N_ATT_GROUPS, S, LANES), F32) for _ in range(5)],
        compiler_params=pltpu.CompilerParams(
            dimension_semantics=("arbitrary", "arbitrary"), vmem_limit_bytes=VMEM_LIMIT),
        name="dilated_attention",
    )(*([p3] * 9), q_gain, k_gain, bias)


def _t5_causal_bucket(dist):
    max_exact = REL_BUCKETS // 2
    d = jnp.maximum(dist.astype(F32), 1.0)
    large = max_exact + (jnp.log(d / max_exact) / math.log(REL_MAX_DIST / max_exact)
                         * (REL_BUCKETS - max_exact)).astype(jnp.int32)
    large = jnp.minimum(large, REL_BUCKETS - 1)
    return jnp.where(dist < max_exact, dist, large)


def _attention_bias(rel_bias_table):
    W = ATT_BLOCK
    qi = jnp.arange(W)[:, None]
    kj = jnp.arange(2 * W)[None, :]
    rel = qi + W - kj
    valid = (rel >= 0) & (rel <= W)
    per_group = []
    for g, (_, d) in enumerate(ATT_PATTERNS):
        tab = rel_bias_table[:, g * HEADS_PER_GROUP:(g + 1) * HEADS_PER_GROUP]
        bucket = _t5_causal_bucket(jnp.clip(rel, 0, W) * d)
        onehot = (bucket[:, :, None] == jnp.arange(REL_BUCKETS)[None, None, :]).astype(F32)
        bias = jnp.einsum('qkb,bh->qkh', onehot, tab.astype(F32),
                          precision=lax.Precision.HIGHEST)
        bias = jnp.where(valid[:, :, None], bias, NEG_INF)
        per_group.append(jnp.transpose(bias, (2, 0, 1)).reshape(2, 2 * W, 2 * W))
    return jnp.stack(per_group, axis=0)


def _rwkv_prep_kernel(p_ref, mu_ref, w0_ref, wup_ref, a0_ref, aup_ref, gup_ref, kk_ref, ka_ref, rk_ref,
                      at_o, rt_o, bt_o, kt_o, bw_o, kw_o, v_o, bonus_o, g_o, wend_o, carry):
    tq = p_ref.shape[1]
    C = RWKV_WIDTH
    L = WKV_CHUNK

    @pl.when(pl.program_id(1) == 0)
    def _():
        carry[...] = jnp.zeros_like(carry)

    p = p_ref[0].astype(F32)
    row = lax.broadcasted_iota(jnp.int32, (tq, 1), 0)
    prev = jnp.where(row == 0, carry[...], pltpu.roll(p, 1, axis=0))
    carry[...] = p[tq - 1:tq, :]
    pm = p + (prev - p) * mu_ref[...]
    r, k, v = pm[:, 0:C], pm[:, C:2 * C], pm[:, 2 * C:3 * C]
    xw = pm[:, 3 * C:3 * C + LANES]
    xa = pm[:, 3 * C + LANES:3 * C + 2 * LANES]
    xg = pm[:, 3 * C + 2 * LANES:]

    z = w0_ref[...] + _dot(jnp.tanh(xw), wup_ref[...])
    softplus_neg = jnp.maximum(-z, 0.0) + jnp.log(1.0 + jnp.exp(-jnp.abs(z)))
    lw = -jnp.exp(-softplus_neg - 0.5)
    a = _sigmoid(a0_ref[...] + _dot(xa, aup_ref[...]))
    g = _dot(_sigmoid(xg), gup_ref[...])

    hr = jnp.right_shift(lax.broadcasted_iota(jnp.int32, (C, C), 0), 6)
    hc = jnp.right_shift(lax.broadcasted_iota(jnp.int32, (C, C), 1), 6)
    ones_bd = jnp.where(hr == hc, 1.0, 0.0).astype(BF16)

    def head_sums(x):
        hi, lo2 = _split2(x)
        return (jnp.dot(hi, ones_bd, preferred_element_type=F32)
                + jnp.dot(lo2, ones_bd, preferred_element_type=F32))

    k2 = k * (1.0 + (a - 1.0) * ka_ref[...])
    kk = k * kk_ref[...]
    kk = kk / jnp.maximum(jnp.sqrt(head_sums(kk * kk)), 1e-12)
    bonus = head_sums(r * k2 * rk_ref[...]) * v
    b = kk * a

    half = 256
    ri = lax.broadcasted_iota(jnp.int32, (half, half), 0)
    ci = lax.broadcasted_iota(jnp.int32, (half, half), 1)
    tri = jnp.where((jnp.right_shift(ri, 6) == jnp.right_shift(ci, 6)) & (ri >= ci), 1.0, 0.0).astype(BF16)
    cums = []
    for s in range(tq // half):
        hi, mid, lo3 = _split3(lw[s * half:(s + 1) * half, :])
        cums.append(jnp.dot(tri, hi, preferred_element_type=F32)
                    + (jnp.dot(tri, mid, preferred_element_type=F32)
                       + jnp.dot(tri, lo3, preferred_element_type=F32)))
    cum = jnp.concatenate(cums, axis=0)
    n_chunks = tq // L
    ends = [cum[c * L + L - 1:c * L + L, :] for c in range(n_chunks)]
    tot = jnp.concatenate([jnp.broadcast_to(e, (L, C)) for e in ends], axis=0)
    wend = jnp.exp(jnp.concatenate(ends, axis=0))

    e_pos = jnp.exp(cum)
    e_neg = jnp.exp(-cum)
    e_rem = jnp.exp(tot - cum)
    at = -kk * jnp.exp(cum - lw)
    rt = r * e_pos
    bt = b * e_neg
    kt = k2 * e_neg
    bw = b * e_rem
    kw = k2 * e_rem
    at_o[0] = at.astype(BF16)
    rt_o[0] = rt.astype(BF16)
    bt_o[0] = bt.astype(BF16)
    kt_o[0] = kt.astype(BF16)
    bw_o[0] = bw.astype(BF16)
    kw_o[0] = kw.astype(BF16)
    v_o[0] = v.astype(BF16)
    bonus_o[0] = bonus
    g_o[0] = g
    wend_o[0] = wend


def _rwkv_prep(p3, mu, w0, wup, a0, aup, gup, k_k, k_a, r_k):
    B, S, _ = p3.shape
    tq = 512
    C = RWKV_WIDTH
    tm_shape = lambda dt: jax.ShapeDtypeStruct((B, S, C), dt)
    tm_spec = pl.BlockSpec((1, tq, C), lambda b, j: (b, j, 0))
    full = lambda shape: pl.BlockSpec(shape, lambda b, j: (0,) * len(shape))
    return pl.pallas_call(
        _rwkv_prep_kernel,
        out_shape=[tm_shape(BF16)] * 7 + [tm_shape(F32)] * 2 + [jax.ShapeDtypeStruct((B, S // WKV_CHUNK, C), F32)],
        grid=(B, S // tq),
        in_specs=[pl.BlockSpec((1, tq, RWKV_PAD_WIDTH), lambda b, j: (b, j, 0)),
                  full((1, RWKV_PAD_WIDTH)), full((1, C)), full((LANES, C)), full((1, C)),
                  full((LANES, C)), full((2 * LANES, C)), full((1, C)), full((1, C)), full((1, C))],
        out_specs=[tm_spec] * 9 + [pl.BlockSpec((1, tq // WKV_CHUNK, C), lambda b, j: (b, j, 0))],
        scratch_shapes=[pltpu.VMEM((1, RWKV_PAD_WIDTH), F32)],
        compiler_params=pltpu.CompilerParams(
            dimension_semantics=("arbitrary", "arbitrary"), vmem_limit_bytes=VMEM_LIMIT),
        name="rwkv_prep",
    )(p3, mu, w0, wup, a0, aup, gup, k_k, k_a, r_k)


def _dot_tn(a, b):
    return lax.dot_general(a.astype(BF16), b.astype(BF16), (((0,), (0,)), ((), ())),
                           preferred_element_type=F32)


WKV_GROUP = 4
WKV_GROUP_WIDTH = WKV_GROUP * RWKV_HEAD_DIM


def _wkv_kernel(at_ref, rt_ref, bt_ref, kt_ref, bw_ref, kw_ref, v_ref, bonus_ref, g_ref, wend_ref,
                lnw_ref, lnb_ref, o_ref, state, m_s, u0_s, o0_s, arb_s, p_s, c_s, sh_s):
    N, C, GW = RWKV_HEAD_DIM, WKV_CHUNK, WKV_GROUP_WIDTH
    n_groups = RWKV_WIDTH // GW
    n_sub = at_ref.shape[1] // C
    j = pl.program_id(1)

    @pl.when(j == 0)
    def _():
        state[...] = jnp.zeros_like(state)

    def iota(shape, dim):
        return lax.broadcasted_iota(jnp.int32, shape, dim)

    same_head = jnp.right_shift(iota((1, GW, GW), 1), 6) == jnp.right_shift(iota((1, GW, GW), 2), 6)
    ones_bd = jnp.where(same_head, 1.0, 0.0).astype(BF16)
    g_row = iota((1, 2 * C, 2 * GW), 1)
    g_col = jnp.bitwise_and(iota((1, 2 * C, 2 * GW), 2), N - 1)
    band = ((g_row < C) & (g_row > g_col)) | ((g_row >= C) & ((g_row - C) >= g_col))
    eye = jnp.where(iota((1, C, GW), 1) == jnp.bitwise_and(iota((1, C, GW), 2), N - 1), 1.0, 0.0)

    def bmm(a, b, dims):
        return lax.dot_general(a.astype(BF16), b.astype(BF16), (dims, ((0,), (0,))),
                               preferred_element_type=F32)

    nn, nt, tn = ((2,), (1,)), ((2,), (2,)), ((1,), (1,))

    def bd(x):
        xb = x.astype(BF16)
        return jnp.where(same_head, jnp.concatenate([xb] * WKV_GROUP, axis=1), jnp.zeros((), BF16))

    span = 4
    rows = span * C

    def stack(x):
        x3 = x.reshape(span, C, n_groups * GW)
        return jnp.concatenate([x3[:, :, q * GW:(q + 1) * GW] for q in range(n_groups)], axis=0)

    def unstack(y):
        return jnp.concatenate([y[q * span:(q + 1) * span].reshape(rows, GW) for q in range(n_groups)],
                               axis=1)

    for i in range(n_sub // span):
        rs = slice(i * rows, (i + 1) * rows)
        at = stack(at_ref[0, rs, :])
        ar = jnp.concatenate([at, stack(rt_ref[0, rs, :])], axis=1)
        bk_bd = jnp.concatenate([bd(stack(bt_ref[0, rs, :])), bd(stack(kt_ref[0, rs, :]))], axis=1)
        gm = jnp.where(band, bmm(ar, bk_bd, nt), 0.0)
        a_ab = gm[:, :C, :GW]
        inv = eye + a_ab
        pw = bmm(a_ab, bd(a_ab), nn)
        for _ in range(int(math.log2(C)) - 2):
            prod = bmm(jnp.concatenate([inv, pw], axis=1), bd(pw), nn)
            inv = inv + prod[:, :C]
            pw = prod[:, C:]
        inv = inv + bmm(inv, bd(pw), nn)
        v = stack(v_ref[0, rs, :])
        kv = bmm(gm[:, :, GW:], bd(v), nn)
        mu0 = bmm(inv, jnp.concatenate([bd(at), bd(kv[:, :C])], axis=2), nn)
        m = mu0[:, :, :GW].astype(BF16)
        u0 = mu0[:, :, GW:]
        bw = stack(bw_ref[0, rs, :])
        low_rank = jnp.where(same_head, bmm(m, bw, tn), 0.0)
        const = jnp.where(same_head, bmm(jnp.concatenate([u0.astype(BF16), v], axis=1),
                                         jnp.concatenate([bw, stack(kw_ref[0, rs, :])], axis=1), tn), 0.0)
        m_s[rs, :] = unstack(m)
        u0_s[rs, :] = unstack(u0)
        o0_s[rs, :] = unstack(kv[:, C:])
        arb_s[rs, :] = unstack(gm[:, C:, :GW]).astype(BF16)
        for q in range(n_groups):
            p_s[i * span:(i + 1) * span, q] = low_rank[q * span:(q + 1) * span].astype(BF16)
            c_s[i * span:(i + 1) * span, q] = const[q * span:(q + 1) * span]

    def groups(x):
        return jnp.stack([x[:, q * GW:(q + 1) * GW] for q in range(n_groups)], axis=0)

    def chunk_state(i, c):
        s0 = state[...]
        sh_s[i] = s0.astype(BF16)
        w_end = groups(wend_ref[0, pl.ds(j * n_sub + i, 1), :])
        state[...] = s0 * w_end + bmm(s0, p_s[i], nn) + c_s[i]
        return c

    lax.fori_loop(0, n_sub, chunk_state, 0, unroll=True)

    ones_p = jnp.broadcast_to(ones_bd, (n_groups * span, GW, GW))
    for i in range(n_sub // span):
        rs = slice(i * rows, (i + 1) * rows)
        s0 = jnp.concatenate([sh_s[i * span:(i + 1) * span, q] for q in range(n_groups)], axis=0)
        mr_s = bmm(jnp.concatenate([stack(m_s[rs, :]), stack(rt_ref[0, rs, :])], axis=1), s0, nt)
        u = mr_s[:, :C] + stack(u0_s[rs, :])
        o = mr_s[:, C:] + stack(o0_s[rs, :]) + bmm(stack(arb_s[rs, :]), bd(u), nn)
        hi, lo2 = _split2(o)
        s1 = bmm(jnp.concatenate([hi, lo2], axis=1), ones_p, nn)
        dlt = o - (s1[:, :C] + s1[:, C:]) * (1.0 / N)
        hi, lo2 = _split2(dlt * dlt)
        s2 = bmm(jnp.concatenate([hi, lo2], axis=1), ones_p, nn)
        on = unstack(dlt * lax.rsqrt((s2[:, :C] + s2[:, C:]) * (1.0 / N) + GN_EPS))
        on = on * lnw_ref[...] + lnb_ref[...]
        o_ref[0, rs, :] = (on + bonus_ref[0, rs, :]) * g_ref[0, rs, :]


def _wkv(at, rt, bt, kt, bw, kw, v, bonus, g, wend, ln_w, ln_b):
    B, S, C = at.shape
    tc = 512
    tm_spec = pl.BlockSpec((1, tc, C), lambda bb, j: (bb, j, 0))
    par_spec = pl.BlockSpec((1, C), lambda bb, j: (0, 0))
    blocks = (C // WKV_GROUP_WIDTH, WKV_GROUP_WIDTH, WKV_GROUP_WIDTH)
    return pl.pallas_call(
        _wkv_kernel,
        out_shape=jax.ShapeDtypeStruct((B, S, C), F32),
        grid=(B, S // tc),
        in_specs=([tm_spec] * 9 + [pl.BlockSpec((1, S // WKV_CHUNK, C), lambda bb, j: (bb, 0, 0))]
                  + [par_spec] * 2),
        out_specs=tm_spec,
        scratch_shapes=[pltpu.VMEM(blocks, F32),
                        pltpu.VMEM((tc, C), BF16), pltpu.VMEM((tc, C), F32), pltpu.VMEM((tc, C), F32),
                        pltpu.VMEM((tc, C), BF16),
                        pltpu.VMEM((tc // WKV_CHUNK,) + blocks, BF16), pltpu.VMEM((tc // WKV_CHUNK,) + blocks, F32),
                        pltpu.VMEM((tc // WKV_CHUNK,) + blocks, BF16)],
        compiler_params=pltpu.CompilerParams(
            dimension_semantics=("arbitrary", "arbitrary"), vmem_limit_bytes=VMEM_LIMIT),
        name="wkv7",
    )(at, rt, bt, kt, bw, kw, v, bonus, g, wend, ln_w, ln_b)


def _merge_kernel(att_ref, o_ref, gate_ref, x_ref, wa_ref, wr_ref, wo_ref, g2_ref, wrt_ref, brt_ref,
                  x1_ref, h2_ref, lg_ref, cnt_ref):
    y_att = _dot(att_ref[0], wa_ref[...])
    y_rwkv = _dot(o_ref[0], wr_ref[...])
    gates = _sigmoid(gate_ref[0].astype(F32))
    mixed = gates[:, :D_MODEL] * y_att + gates[:, D_MODEL:] * y_rwkv
    x1 = x_ref[0] + _dot(mixed, wo_ref[...])
    x1_ref[0] = x1
    ms = jnp.mean(x1 * x1, axis=-1, keepdims=True)
    h2 = x1 * lax.rsqrt(ms + NORM_EPS) * g2_ref[...]
    hh, hl = _split2(h2)
    half = D_MODEL // 2
    rounded = hh.astype(F32)
    w_hi = pltpu.bitcast(rounded[:, :half], jnp.uint32)
    w_lo = pltpu.bitcast(rounded[:, half:], jnp.uint32)
    h2_ref[0] = w_hi | jnp.right_shift(w_lo, jnp.uint32(16))
    wh, wl = wrt_ref[0], wrt_ref[1]
    lg = (jnp.dot(hh, wh, preferred_element_type=F32)
          + (jnp.dot(hh, wl, preferred_element_type=F32) + jnp.dot(hl, wh, preferred_element_type=F32)))
    lg = lg + brt_ref[...]

    lane = lax.broadcasted_iota(jnp.int32, (1, ROUTER_PAD), 1)
    gmask = lane < N_EXPERT_GROUPS
    gl = jnp.where(gmask, lg, -jnp.inf)
    gmax = jnp.max(gl, axis=-1, keepdims=True)
    grp_idx = jnp.min(jnp.where(gl == gmax, lane, ROUTER_PAD), axis=-1, keepdims=True)
    grp_p = 1.0 / jnp.sum(jnp.where(gmask, jnp.exp(lg - gmax), 0.0), axis=-1, keepdims=True)
    e_lane = lane - N_EXPERT_GROUPS
    emask = ((e_lane >= 0) & (e_lane < N_EXPERTS)
             & (jnp.right_shift(e_lane, int(math.log2(EXPERTS_PER_GROUP))) == grp_idx))
    el = jnp.where(emask, lg, -jnp.inf)
    m1 = jnp.max(el, axis=-1, keepdims=True)
    i1 = jnp.min(jnp.where(el == m1, lane, ROUTER_PAD), axis=-1, keepdims=True)
    el2 = jnp.where(lane == i1, -jnp.inf, el)
    m2 = jnp.max(el2, axis=-1, keepdims=True)
    i2 = jnp.min(jnp.where(el2 == m2, lane, ROUTER_PAD), axis=-1, keepdims=True)
    e2 = jnp.exp(m2 - m1)
    gate1 = grp_p / (1.0 + e2)
    gate2 = gate1 * e2

    @pl.when((pl.program_id(0) == 0) & (pl.program_id(1) == 0))
    def _():
        cnt_ref[...] = jnp.zeros_like(cnt_ref)

    tm = lg.shape[0]
    oh = jnp.where((lane == i1) | (lane == i2), 1.0, 0.0)
    ti = lax.broadcasted_iota(jnp.int32, (tm, tm), 0)
    tj = lax.broadcasted_iota(jnp.int32, (tm, tm), 1)
    before = jnp.where(tj < ti, 1.0, 0.0).astype(BF16)
    pref = jnp.dot(before, oh.astype(BF16), preferred_element_type=F32) + cnt_ref[...]
    rank1 = jnp.sum(jnp.where(lane == i1, pref, 0.0), axis=-1, keepdims=True)
    rank2 = jnp.sum(jnp.where(lane == i2, pref, 0.0), axis=-1, keepdims=True)
    cnt_ref[...] = cnt_ref[...] + jnp.sum(oh, axis=0, keepdims=True)
    info = jnp.where(lane == 0, (i1 - N_EXPERT_GROUPS).astype(F32), 0.0)
    info = jnp.where(lane == 1, (i2 - N_EXPERT_GROUPS).astype(F32), info)
    info = jnp.where(lane == 2, rank1, info)
    info = jnp.where(lane == 3, rank2, info)
    info = jnp.where(lane == 4, gate1, info)
    info = jnp.where(lane == 5, gate2, info)
    lg_ref[0] = info


def _merge(att, o_g, p3, x, w_att_out, w_rwkv_out, w_out, gain2, w_router, b_router):
    B, S, D = x.shape
    tm = 512
    full = lambda shape: pl.BlockSpec(shape, lambda b, j: (0,) * len(shape))
    return pl.pallas_call(
        _merge_kernel,
        out_shape=[jax.ShapeDtypeStruct((B, S, D), F32), jax.ShapeDtypeStruct((B, S, D // 2), jnp.uint32),
                   jax.ShapeDtypeStruct((B, S, ROUTER_PAD), F32),
                   jax.ShapeDtypeStruct((1, ROUTER_PAD), F32)],
        grid=(B, S // tm),
        in_specs=[pl.BlockSpec((1, tm, ATT_OUT_WIDTH), lambda b, j: (b, j, 0)),
                  pl.BlockSpec((1, tm, RWKV_WIDTH), lambda b, j: (b, j, 0)),
                  pl.BlockSpec((1, tm, 2 * D_MODEL), lambda b, j: (b, j, 1)),
                  pl.BlockSpec((1, tm, D), lambda b, j: (b, j, 0)),
                  full((ATT_OUT_WIDTH, D)), full((RWKV_WIDTH, D)), full((D, D)),
                  full((1, D)), full((2, D, ROUTER_PAD)), full((1, ROUTER_PAD))],
        out_specs=[pl.BlockSpec((1, tm, D), lambda b, j: (b, j, 0)),
                   pl.BlockSpec((1, tm, D // 2), lambda b, j: (b, j, 0)),
                   pl.BlockSpec((1, tm, ROUTER_PAD), lambda b, j: (b, j, 0)),
                   pl.BlockSpec((1, ROUTER_PAD), lambda b, j: (0, 0))],
        compiler_params=pltpu.CompilerParams(
            dimension_semantics=("arbitrary", "arbitrary"), vmem_limit_bytes=VMEM_LIMIT),
        name="merge_outproj_router",
    )(att, o_g, p3, x, w_att_out, w_rwkv_out, w_out, gain2, w_router, b_router)


def _expert_kernel(be_ref, nb_ref, x_ref, wg_ref, wu_ref, wd_ref, y_ref, wgu_s, wd_s):
    i = pl.program_id(0)

    @pl.when((i == 0) | (be_ref[i] != be_ref[jnp.maximum(i - 1, 0)]))
    def _():
        wgu_s[:, :D_EXPERT] = wg_ref[0].astype(BF16)
        wgu_s[:, D_EXPERT:] = wu_ref[0].astype(BF16)
        wd_s[...] = wd_ref[0].astype(BF16)

    @pl.when(i < nb_ref[0])
    def _():
        words = x_ref[...]
        x_a = pltpu.bitcast(words & jnp.uint32(0xFFFF0000), F32)
        x_b = pltpu.bitcast(jnp.left_shift(words, jnp.uint32(16)), F32)
        xb = jnp.concatenate([x_a, x_b], axis=-1).astype(BF16)
        h = jnp.dot(xb, wgu_s[...], preferred_element_type=F32)
        hg, hu = h[:, :D_EXPERT], h[:, D_EXPERT:]
        hid = hg * _sigmoid(hg) * hu
        y_ref[...] = jnp.dot(hid.astype(BF16), wd_s[...], preferred_element_type=F32).astype(y_ref.dtype)

    @pl.when(i >= nb_ref[0])
    def _():
        y_ref[...] = jnp.zeros_like(y_ref)


def _experts(blk_expert, n_used, xin, w_gate, w_up, w_down):
    n_rows = xin.shape[0]
    D = D_MODEL
    n_blocks = n_rows // MOE_ROWS
    return pl.pallas_call(
        _expert_kernel,
        out_shape=jax.ShapeDtypeStruct((n_rows, D), BF16),
        grid_spec=pltpu.PrefetchScalarGridSpec(
            num_scalar_prefetch=2,
            grid=(n_blocks,),
            scratch_shapes=[pltpu.VMEM((D, 2 * D_EXPERT), BF16), pltpu.VMEM((D_EXPERT, D), BF16)],
            in_specs=[pl.BlockSpec((MOE_ROWS, D // 2), lambda i, be, nb: (i, 0)),
                      pl.BlockSpec((1, D, D_EXPERT), lambda i, be, nb: (be[i], 0, 0)),
                      pl.BlockSpec((1, D, D_EXPERT), lambda i, be, nb: (be[i], 0, 0)),
                      pl.BlockSpec((1, D_EXPERT, D), lambda i, be, nb: (be[i], 0, 0))],
            out_specs=pl.BlockSpec((MOE_ROWS, D), lambda i, be, nb: (i, 0))),
        compiler_params=pltpu.CompilerParams(
            dimension_semantics=("arbitrary",), vmem_limit_bytes=VMEM_LIMIT),
        name="moe_experts",
    )(blk_expert, n_used, xin, w_gate, w_up, w_down)


def _row_tokens(dest_flat, counts, pcounts, pstarts, pends, n_rows, T):
    A = dest_flat.shape[0]
    n_pad = n_rows - A
    seg_count = jnp.concatenate([pcounts - counts, n_rows - pends[-1:]])
    seg_first = jnp.concatenate([pstarts + counts, pends[-1:]])
    seg_end = jnp.cumsum(seg_count)
    i = jnp.arange(n_pad, dtype=jnp.int32)
    seg = jnp.sum(i[:, None] >= seg_end[None, :], axis=1)
    pad_rows = (seg_first[seg] + i - (seg_end - seg_count)[seg]).astype(jnp.int32)
    rows = jnp.concatenate([dest_flat, pad_rows])
    toks = jnp.concatenate([jnp.arange(A, dtype=jnp.int32) // TOP_K, i % T])
    return lax.sort_key_val(rows, toks)[1]


def _row_layout(info, counts, T):
    A = T * TOP_K
    eid = info[:, 0:2].astype(jnp.int32)
    rank = info[:, 2:4].astype(jnp.int32)
    gates = info[:, 4:6]
    counts = counts.astype(jnp.int32)
    pcounts = (counts + MOE_ROWS - 1) // MOE_ROWS * MOE_ROWS
    pends = jnp.cumsum(pcounts)
    pstarts = pends - pcounts
    dest = (pstarts[eid] + rank).astype(jnp.int32)
    n_blocks = -(-A // MOE_ROWS) + N_EXPERTS
    n_rows = n_blocks * MOE_ROWS
    row_tok = _row_tokens(dest.reshape(A), counts, pcounts, pstarts, pends, n_rows, T)
    blk_start = jnp.arange(n_blocks, dtype=jnp.int32) * MOE_ROWS
    blk_expert = jnp.minimum(jnp.sum(blk_start[:, None] >= pends[None, :], axis=1), N_EXPERTS - 1)
    n_used = (pends[N_EXPERTS - 1] // MOE_ROWS).astype(jnp.int32)[None]
    return row_tok, blk_expert.astype(jnp.int32), n_used, dest, gates


def _pad_rows(w, rows):
    return jnp.pad(w, ((0, rows - w.shape[0]), (0, 0)))


def kernel(x, norm1_gain, w_in, q_norm_gain, k_norm_gain, rel_bias_table, w_att_out, rwkv_shift_mu,
           rwkv_w0, rwkv_w_up, rwkv_a0, rwkv_a_up, rwkv_g_up, rwkv_k_k, rwkv_k_a, rwkv_r_k,
           rwkv_ln_w, rwkv_ln_b, w_rwkv_out, w_out, norm2_gain, w_group_router, b_group_router,
           w_expert_router, b_expert_router, w_expert_gate, w_expert_up, w_expert_down):
    B, S, D = x.shape
    T = B * S
    C = RWKV_WIDTH
    bias = _attention_bias(rel_bias_table)
    for l in range(norm1_gain.shape[0]):
        wi = w_in[l]
        w_qkv = wi[:, :3 * ATT_WIDTH]
        w_rw = wi[:, 3 * ATT_WIDTH:3 * ATT_WIDTH + RWKV_SHIFT_WIDTH]
        w_gt = wi[:, 3 * ATT_WIDTH + RWKV_SHIFT_WIDTH:]
        pad_cols = lambda w, n: jnp.pad(w, ((0, 0), (0, n - w.shape[1])))
        o_w, o_a, o_g = 3 * C, 3 * C + DECAY_LORA, 3 * C + DECAY_LORA + ICLR_LORA

        def rw_layout(w):
            return jnp.concatenate([w[:, :o_w], pad_cols(w[:, o_w:o_a], LANES),
                                    pad_cols(w[:, o_a:o_g], LANES), pad_cols(w[:, o_g:], 2 * LANES)], axis=1)

        w_rg = jnp.concatenate([rw_layout(w_rw), w_gt], axis=1).astype(BF16)
        mu = rw_layout(rwkv_shift_mu[l][None, :])

        p_rg, p_qkv = _inproj(x.reshape(T, D), norm1_gain[l][None, :], w_rg, w_qkv.astype(BF16))
        p3 = p_rg.reshape(B, S, RWKV_PAD_WIDTH + 2 * D)

        tile2 = lambda gmat: jnp.tile(gmat, (1, 2))[:, None, :]
        att = _attention(p_qkv.reshape(B, S, 3 * ATT_WIDTH), tile2(q_norm_gain[l]), tile2(k_norm_gain[l]),
                         bias)

        row = lambda v: v[None, :]
        prep = _rwkv_prep(
            p3, mu, row(rwkv_w0[l]), _pad_rows(rwkv_w_up[l], LANES).astype(BF16), row(rwkv_a0[l]),
            _pad_rows(rwkv_a_up[l], LANES).astype(BF16), _pad_rows(rwkv_g_up[l], 2 * LANES).astype(BF16),
            row(rwkv_k_k[l]), row(rwkv_k_a[l]), rwkv_r_k[l].reshape(1, C))
        o_g = _wkv(*prep, row(rwkv_ln_w[l]), row(rwkv_ln_b[l]))

        w_router = jnp.concatenate([w_group_router[l], w_expert_router[l]], axis=1)
        w_router = jnp.pad(w_router, ((0, 0), (0, ROUTER_PAD - w_router.shape[1])))
        wr_hi = w_router.astype(BF16)
        wr_lo = (w_router - wr_hi.astype(F32)).astype(BF16)
        b_router = jnp.concatenate([b_group_router[l], b_expert_router[l]])
        b_router = jnp.pad(b_router, (0, ROUTER_PAD - b_router.shape[0]))[None, :]
        x1, h2, info, counts = _merge(
            att, o_g, p3, x, w_att_out[l].astype(BF16),
            w_rwkv_out[l].astype(BF16), w_out[l].astype(BF16),
            norm2_gain[l][None, :], jnp.stack([wr_hi, wr_lo]), b_router)

        row_tok, blk_expert, n_used, dest, gates = _row_layout(
            info.reshape(T, ROUTER_PAD), counts[0, N_EXPERT_GROUPS:N_EXPERT_GROUPS + N_EXPERTS], T)
        xin = h2.reshape(T, D // 2)[row_tok]
        yb = _experts(blk_expert, n_used, xin, w_expert_gate[l], w_expert_up[l], w_expert_down[l])
        y = (gates[:, 0:1] * yb[dest[:, 0]].astype(F32) + gates[:, 1:2] * yb[dest[:, 1]].astype(F32))
        x = x1 + y.reshape(B, S, D)
    return x
```

```python
import functools
import math

import jax
import jax.numpy as jnp
from jax import lax
from jax.experimental import pallas as pl
from jax.experimental.pallas import tpu as pltpu

F32 = jnp.float32
BF16 = jnp.bfloat16

D_MODEL = 1024
ATT_PATTERNS = ((128, 1), (512, 4), (2048, 16))
N_ATT_GROUPS = 3
HEADS_PER_GROUP = 4
ATT_HEADS = 12
HEAD_DIM = 64
ATT_WIDTH = 768
ATT_OUT_WIDTH = 256
REL_BUCKETS = 32
REL_MAX_DIST = 2048
RWKV_HEADS = 8
RWKV_HEAD_DIM = 64
RWKV_WIDTH = 512
DECAY_LORA = 64
ICLR_LORA = 64
GATE_LORA = 160
RWKV_SHIFT_WIDTH = 1824
GN_EPS = 64e-5
N_EXPERT_GROUPS = 4
EXPERTS_PER_GROUP = 8
N_EXPERTS = 32
TOP_K = 2
D_EXPERT = 512
NORM_EPS = 1e-6
NEG_INF = -1e30

LANES = 128
ATT_BLOCK = 128
ATT_BATCH = 4
WKV_CHUNK = 64
RWKV_PAD_WIDTH = 2048
QKV_COL_BLOCK = 0
ROUTER_PAD = 128
MOE_ROWS = 512
VMEM_LIMIT = 56 * 1024 * 1024


def _sigmoid(x):
    return 1.0 / (1.0 + jnp.exp(-x))


def _dot(a, b):
    return jnp.dot(a.astype(BF16), b.astype(BF16), preferred_element_type=F32)


def _dot_t(a, b):
    return lax.dot_general(a.astype(BF16), b.astype(BF16), (((1,), (1,)), ((), ())),
                           preferred_element_type=F32)


def _split3(a):
    hi = a.astype(BF16)
    r1 = a - hi.astype(F32)
    mid = r1.astype(BF16)
    lo = (r1 - mid.astype(F32)).astype(BF16)
    return hi, mid, lo


def _split2(a):
    hi = a.astype(BF16)
    lo = (a - hi.astype(F32)).astype(BF16)
    return hi, lo


INPROJ_COLS = 512


def _inproj_kernel(x_ref, g_ref, wrg_ref, wqkv_ref, rg_ref, qkv_ref):
    x = x_ref[...]
    ms = jnp.mean(x * x, axis=-1, keepdims=True)
    h = (x * lax.rsqrt(ms + NORM_EPS) * g_ref[...]).astype(BF16)
    for w_ref, o_ref in ((wrg_ref, rg_ref), (wqkv_ref, qkv_ref)):
        width = w_ref.shape[1]
        for c in range(0, width, INPROJ_COLS):
            cs = slice(c, min(c + INPROJ_COLS, width))
            o_ref[:, cs] = jnp.dot(h, w_ref[:, cs], preferred_element_type=F32).astype(o_ref.dtype)


def _inproj(x2, gain, w_rg, w_qkv):
    T = x2.shape[0]
    tm = 512
    n_rg, n_qkv = w_rg.shape[1], w_qkv.shape[1]
    return pl.pallas_call(
        _inproj_kernel,
        out_shape=[jax.ShapeDtypeStruct((T, n_rg), BF16), jax.ShapeDtypeStruct((T, n_qkv), F32)],
        grid=(T // tm,),
        in_specs=[pl.BlockSpec((tm, D_MODEL), lambda i: (i, 0)),
                  pl.BlockSpec((1, D_MODEL), lambda i: (0, 0)),
                  pl.BlockSpec((D_MODEL, n_rg), lambda i: (0, 0)),
                  pl.BlockSpec((D_MODEL, n_qkv), lambda i: (0, 0))],
        out_specs=[pl.BlockSpec((tm, n_rg), lambda i: (i, 0)),
                   pl.BlockSpec((tm, n_qkv), lambda i: (i, 0))],
        compiler_params=pltpu.CompilerParams(
            dimension_semantics=("arbitrary",), vmem_limit_bytes=VMEM_LIMIT),
        name="inproj",
    )(x2, gain, w_rg, w_qkv)


def _attn_kernel(q0, q1, q2, k0, k1, k2, v0, v1, v2, qg_ref, kg_ref, bias_ref, out_ref,
                 qn_s, kn_s, acc_s, m_s, l_s):
    S = out_ref.shape[1]
    q_refs, k_refs, v_refs = (q0, q1, q2), (k0, k1, k2), (v0, v1, v2)
    lane = lax.broadcasted_iota(jnp.int32, (1, LANES), 1)
    lo = lane < HEAD_DIM
    col = lax.broadcasted_iota(jnp.int32, (1, 2 * ATT_BLOCK), 1)

    same_head = (lax.broadcasted_iota(jnp.int32, (LANES, LANES), 0) < HEAD_DIM) == (
        lax.broadcasted_iota(jnp.int32, (LANES, LANES), 1) < HEAD_DIM)
    ones_bd = jnp.where(same_head, 1.0, 0.0).astype(BF16)

    def pair_norm(x, gain):
        hi, lo2 = _split2(x * x)
        ss = (jnp.dot(hi, ones_bd, preferred_element_type=F32)
              + jnp.dot(lo2, ones_bd, preferred_element_type=F32))
        return x * lax.rsqrt(ss * (1.0 / HEAD_DIM) + NORM_EPS) * gain

    rows_per = 256

    def norm_body(i, c):
        rs = pl.ds(pl.multiple_of(i * rows_per, rows_per), rows_per)
        for g in range(N_ATT_GROUPS):
            qn_s[g, rs, :] = pair_norm(q_refs[g][0, rs, :], qg_ref[g]) * (HEAD_DIM ** -0.5)
            kn_s[g, rs, :] = pair_norm(k_refs[g][0, rs, :], kg_ref[g])
        return c

    lax.fori_loop(0, S // rows_per, norm_body, 0)

    for g, (window, d) in enumerate(ATT_PATTERNS):
        n_blk = (S // d) // ATT_BLOCK
        has_prev = n_blk > 1

        def rows_at(start, d=d):
            if d == 1:
                return pl.ds(pl.multiple_of(start, ATT_BLOCK), ATT_BLOCK)
            return pl.ds(start, ATT_BLOCK, stride=d)

        def blk_body(it, c, g=g, d=d, has_prev=has_prev, rows_at=rows_at):
            curs, q2s, kcats, vcats, pens = [], [], [], [], []
            for b in range(ATT_BATCH):
                idx = it * ATT_BATCH + b
                r = idx % d
                n = idx // d
                cur = rows_at(n * (ATT_BLOCK * d) + r)
                q = qn_s[g, cur, :]
                q2s.append(jnp.concatenate([jnp.where(lo, q, 0.0), jnp.where(lo, 0.0, q)], axis=0).astype(BF16))
                kc = kn_s[g, cur, :].astype(BF16)
                vc = v_refs[g][0, cur, :].astype(BF16)
                if has_prev:
                    prv = rows_at(jnp.maximum(n - 1, 0) * (ATT_BLOCK * d) + r)
                    kc = jnp.concatenate([kn_s[g, prv, :].astype(BF16), kc], axis=0)
                    vc = jnp.concatenate([v_refs[g][0, prv, :].astype(BF16), vc], axis=0)
                    pens.append(jnp.where(col < ATT_BLOCK, jnp.where(n == 0, NEG_INF, 0.0), 0.0)[None])
                curs.append(cur)
                kcats.append(kc)
                vcats.append(jnp.concatenate([vc, jnp.ones_like(vc)], axis=1))
            q2, kcat, vcat = jnp.stack(q2s), jnp.stack(kcats), jnp.stack(vcats)
            s = lax.dot_general(q2, kcat, (((2,), (2,)), ((0,), (0,))), preferred_element_type=F32)
            if has_prev:
                s = s + bias_ref[g, 0] + jnp.concatenate(pens, axis=0)
            else:
                s = s + bias_ref[g, 0, :, ATT_BLOCK:]
            m = jnp.max(s, axis=-1, keepdims=True)
            p = jnp.exp(s - m).astype(BF16)
            ol = lax.dot_general(p, vcat, (((2,), (1,)), ((0,), (0,))), preferred_element_type=F32)
            for b in range(ATT_BATCH):
                cur = curs[b]
                acc_s[g, cur, :] = jnp.where(lo, ol[b, :ATT_BLOCK, :LANES], ol[b, ATT_BLOCK:, :LANES])
                l_s[g, cur, :] = jnp.where(lo, ol[b, :ATT_BLOCK, LANES:], ol[b, ATT_BLOCK:, LANES:])
                m_s[g, cur, :] = jnp.where(lo, m[b, :ATT_BLOCK], m[b, ATT_BLOCK:])
            return c

        lax.fori_loop(0, d * n_blk // ATT_BATCH, blk_body, 0)

    def comb_body(i, c):
        rs = pl.ds(pl.multiple_of(i * rows_per, rows_per), rows_per)
        m = jnp.maximum(jnp.maximum(m_s[0, rs, :], m_s[1, rs, :]), m_s[2, rs, :])
        num = jnp.zeros((rows_per, LANES), F32)
        den = jnp.zeros((rows_per, LANES), F32)
        for g in range(N_ATT_GROUPS):
            e = jnp.exp(m_s[g, rs, :] - m)
            num = num + e * acc_s[g, rs, :]
            den = den + e * l_s[g, rs, :]
        out_ref[0, rs, :] = num / den
        return c

    lax.fori_loop(0, S // rows_per, comb_body, 0)


def _attention(p3, q_gain, k_gain, bias):
    B, S, _ = p3.shape
    n_pairs = HEADS_PER_GROUP // 2

    def col_spec(base):
        return [pl.BlockSpec((1, S, LANES),
                             functools.partial(lambda b, sp, off: (b, 0, off + sp), off=base + g * n_pairs))
                for g in range(N_ATT_GROUPS)]

    n_qkv_blocks = ATT_WIDTH // LANES
    in_specs = (col_spec(QKV_COL_BLOCK) + col_spec(QKV_COL_BLOCK + n_qkv_blocks)
                + col_spec(QKV_COL_BLOCK + 2 * n_qkv_blocks)
                + [pl.BlockSpec((N_ATT_GROUPS, 1, LANES), lambda b, sp: (0, 0, 0)),
                   pl.BlockSpec((N_ATT_GROUPS, 1, LANES), lambda b, sp: (0, 0, 0)),
                   pl.BlockSpec((N_ATT_GROUPS, 1, 2 * ATT_BLOCK, 2 * ATT_BLOCK),
                                lambda b, sp: (0, sp, 0, 0))])
    return pl.pallas_call(
        _attn_kernel,
        out_shape=jax.ShapeDtypeStruct((B, S, ATT_OUT_WIDTH), F32),
        grid=(B, n_pairs),
        in_specs=in_specs,
        out_specs=pl.BlockSpec((1, S, LANES), lambda b, sp: (b, 0, sp)),
        scratch_shapes=[pltpu.VMEM((N_ATT_GROUPS, S, LANES), F32) for _ in range(5)],
        compiler_params=pltpu.CompilerParams(
            dimension_semantics=("arbitrary", "arbitrary"), vmem_limit_bytes=VMEM_LIMIT),
        name="dilated_attention",
    )(*([p3] * 9), q_gain, k_gain, bias)


def _t5_causal_bucket(dist):
    max_exact = REL_BUCKETS // 2
    d = jnp.maximum(dist.astype(F32), 1.0)
    large = max_exact + (jnp.log(d / max_exact) / math.log(REL_MAX_DIST / max_exact)
                         * (REL_BUCKETS - max_exact)).astype(jnp.int32)
    large = jnp.minimum(large, REL_BUCKETS - 1)
    return jnp.where(dist < max_exact, dist, large)


def _attention_bias(rel_bias_table):
    W = ATT_BLOCK
    qi = jnp.arange(W)[:, None]
    kj = jnp.arange(2 * W)[None, :]
    rel = qi + W - kj
    valid = (rel >= 0) & (rel <= W)
    per_group = []
    for g, (_, d) in enumerate(ATT_PATTERNS):
        tab = rel_bias_table[:, g * HEADS_PER_GROUP:(g + 1) * HEADS_PER_GROUP]
        bucket = _t5_causal_bucket(jnp.clip(rel, 0, W) * d)
        onehot = (bucket[:, :, None] == jnp.arange(REL_BUCKETS)[None, None, :]).astype(F32)
        bias = jnp.einsum('qkb,bh->qkh', onehot, tab.astype(F32),
                          precision=lax.Precision.HIGHEST)
        bias = jnp.where(valid[:, :, None], bias, NEG_INF)
        per_group.append(jnp.transpose(bias, (2, 0, 1)).reshape(2, 2 * W, 2 * W))
    return jnp.stack(per_group, axis=0)


def _rwkv_prep_kernel(p_ref, mu_ref, w0_ref, wup_ref, a0_ref, aup_ref, gup_ref, kk_ref, ka_ref, rk_ref,
                      at_o, rt_o, bt_o, kt_o, bw_o, kw_o, v_o, bonus_o, g_o, wend_o, carry):
    tq = p_ref.shape[1]
    C = RWKV_WIDTH
    L = WKV_CHUNK

    @pl.when(pl.program_id(1) == 0)
    def _():
        carry[...] = jnp.zeros_like(carry)

    p = p_ref[0].astype(F32)
    row = lax.broadcasted_iota(jnp.int32, (tq, 1), 0)
    prev = jnp.where(row == 0, carry[...], pltpu.roll(p, 1, axis=0))
    carry[...] = p[tq - 1:tq, :]
    pm = p + (prev - p) * mu_ref[...]
    r, k, v = pm[:, 0:C], pm[:, C:2 * C], pm[:, 2 * C:3 * C]
    xw = pm[:, 3 * C:3 * C + LANES]
    xa = pm[:, 3 * C + LANES:3 * C + 2 * LANES]
    xg = pm[:, 3 * C + 2 * LANES:]

    z = w0_ref[...] + _dot(jnp.tanh(xw), wup_ref[...])
    softplus_neg = jnp.maximum(-z, 0.0) + jnp.log(1.0 + jnp.exp(-jnp.abs(z)))
    lw = -jnp.exp(-softplus_neg - 0.5)
    a = _sigmoid(a0_ref[...] + _dot(xa, aup_ref[...]))
    g = _dot(_sigmoid(xg), gup_ref[...])

    hr = jnp.right_shift(lax.broadcasted_iota(jnp.int32, (C, C), 0), 6)
    hc = jnp.right_shift(lax.broadcasted_iota(jnp.int32, (C, C), 1), 6)
    ones_bd = jnp.where(hr == hc, 1.0, 0.0).astype(BF16)

    def head_sums(x):
        hi, lo2 = _split2(x)
        return (jnp.dot(hi, ones_bd, preferred_element_type=F32)
                + jnp.dot(lo2, ones_bd, preferred_element_type=F32))

    k2 = k * (1.0 + (a - 1.0) * ka_ref[...])
    kk = k * kk_ref[...]
    kk = kk / jnp.maximum(jnp.sqrt(head_sums(kk * kk)), 1e-12)
    bonus = head_sums(r * k2 * rk_ref[...]) * v
    b = kk * a

    half = 256
    ri = lax.broadcasted_iota(jnp.int32, (half, half), 0)
    ci = lax.broadcasted_iota(jnp.int32, (half, half), 1)
    tri = jnp.where((jnp.right_shift(ri, 6) == jnp.right_shift(ci, 6)) & (ri >= ci), 1.0, 0.0).astype(BF16)
    cums = []
    for s in range(tq // half):
        hi, mid, lo3 = _split3(lw[s * half:(s + 1) * half, :])
        cums.append(jnp.dot(tri, hi, preferred_element_type=F32)
                    + (jnp.dot(tri, mid, preferred_element_type=F32)
                       + jnp.dot(tri, lo3, preferred_element_type=F32)))
    cum = jnp.concatenate(cums, axis=0)
    n_chunks = tq // L
    ends = [cum[c * L + L - 1:c * L + L, :] for c in range(n_chunks)]
    tot = jnp.concatenate([jnp.broadcast_to(e, (L, C)) for e in ends], axis=0)
    wend = jnp.exp(jnp.concatenate(ends, axis=0))

    e_pos = jnp.exp(cum)
    e_neg = jnp.exp(-cum)
    e_rem = jnp.exp(tot - cum)
    at = -kk * jnp.exp(cum - lw)
    rt = r * e_pos
    bt = b * e_neg
    kt = k2 * e_neg
    bw = b * e_rem
    kw = k2 * e_rem
    at_o[0] = at.astype(BF16)
    rt_o[0] = rt.astype(BF16)
    bt_o[0] = bt.astype(BF16)
    kt_o[0] = kt.astype(BF16)
    bw_o[0] = bw.astype(BF16)
    kw_o[0] = kw.astype(BF16)
    v_o[0] = v.astype(BF16)
    bonus_o[0] = bonus
    g_o[0] = g
    wend_o[0] = wend


def _rwkv_prep(p3, mu, w0, wup, a0, aup, gup, k_k, k_a, r_k):
    B, S, _ = p3.shape
    tq = 512
    C = RWKV_WIDTH
    tm_shape = lambda dt: jax.ShapeDtypeStruct((B, S, C), dt)
    tm_spec = pl.BlockSpec((1, tq, C), lambda b, j: (b, j, 0))
    full = lambda shape: pl.BlockSpec(shape, lambda b, j: (0,) * len(shape))
    return pl.pallas_call(
        _rwkv_prep_kernel,
        out_shape=[tm_shape(BF16)] * 7 + [tm_shape(F32)] * 2 + [jax.ShapeDtypeStruct((B, S // WKV_CHUNK, C), F32)],
        grid=(B, S // tq),
        in_specs=[pl.BlockSpec((1, tq, RWKV_PAD_WIDTH), lambda b, j: (b, j, 0)),
                  full((1, RWKV_PAD_WIDTH)), full((1, C)), full((LANES, C)), full((1, C)),
                  full((LANES, C)), full((2 * LANES, C)), full((1, C)), full((1, C)), full((1, C))],
        out_specs=[tm_spec] * 9 + [pl.BlockSpec((1, tq // WKV_CHUNK, C), lambda b, j: (b, j, 0))],
        scratch_shapes=[pltpu.VMEM((1, RWKV_PAD_WIDTH), F32)],
        compiler_params=pltpu.CompilerParams(
            dimension_semantics=("arbitrary", "arbitrary"), vmem_limit_bytes=VMEM_LIMIT),
        name="rwkv_prep",
    )(p3, mu, w0, wup, a0, aup, gup, k_k, k_a, r_k)


def _dot_tn(a, b):
    return lax.dot_general(a.astype(BF16), b.astype(BF16), (((0,), (0,)), ((), ())),
                           preferred_element_type=F32)


WKV_GROUP = 4
WKV_GROUP_WIDTH = WKV_GROUP * RWKV_HEAD_DIM


def _wkv_kernel(at_ref, rt_ref, bt_ref, kt_ref, bw_ref, kw_ref, v_ref, bonus_ref, g_ref, wend_ref,
                lnw_ref, lnb_ref, o_ref, state, m_s, u0_s, o0_s, arb_s, p_s, c_s, sh_s):
    N, C, GW = RWKV_HEAD_DIM, WKV_CHUNK, WKV_GROUP_WIDTH
    n_groups = RWKV_WIDTH // GW
    n_sub = at_ref.shape[1] // C
    j = pl.program_id(1)

    @pl.when(j == 0)
    def _():
        state[...] = jnp.zeros_like(state)

    def iota(shape, dim):
        return lax.broadcasted_iota(jnp.int32, shape, dim)

    same_head = jnp.right_shift(iota((1, GW, GW), 1), 6) == jnp.right_shift(iota((1, GW, GW), 2), 6)
    ones_bd = jnp.where(same_head, 1.0, 0.0).astype(BF16)
    g_row = iota((1, 2 * C, 2 * GW), 1)
    g_col = jnp.bitwise_and(iota((1, 2 * C, 2 * GW), 2), N - 1)
    band = ((g_row < C) & (g_row > g_col)) | ((g_row >= C) & ((g_row - C) >= g_col))
    eye = jnp.where(iota((1, C, GW), 1) == jnp.bitwise_and(iota((1, C, GW), 2), N - 1), 1.0, 0.0)

    def bmm(a, b, dims):
        return lax.dot_general(a.astype(BF16), b.astype(BF16), (dims, ((0,), (0,))),
                               preferred_element_type=F32)

    nn, nt, tn = ((2,), (1,)), ((2,), (2,)), ((1,), (1,))

    def bd(x):
        xb = x.astype(BF16)
        return jnp.where(same_head, jnp.concatenate([xb] * WKV_GROUP, axis=1), jnp.zeros((), BF16))

    span = 4
    rows = span * C

    def stack(x):
        x3 = x.reshape(span, C, n_groups * GW)
        return jnp.concatenate([x3[:, :, q * GW:(q + 1) * GW] for q in range(n_groups)], axis=0)

    def unstack(y):
        return jnp.concatenate([y[q * span:(q + 1) * span].reshape(rows, GW) for q in range(n_groups)],
                               axis=1)

    for i in range(n_sub // span):
        rs = slice(i * rows, (i + 1) * rows)
        at = stack(at_ref[0, rs, :])
        ar = jnp.concatenate([at, stack(rt_ref[0, rs, :])], axis=1)
        bk_bd = jnp.concatenate([bd(stack(bt_ref[0, rs, :])), bd(stack(kt_ref[0, rs, :]))], axis=1)
        gm = jnp.where(band, bmm(ar, bk_bd, nt), 0.0)
        a_ab = gm[:, :C, :GW]
        inv = eye + a_ab
        pw = bmm(a_ab, bd(a_ab), nn)
        for _ in range(int(math.log2(C)) - 2):
            prod = bmm(jnp.concatenate([inv, pw], axis=1), bd(pw), nn)
            inv = inv + prod[:, :C]
            pw = prod[:, C:]
        inv = inv + bmm(inv, bd(pw), nn)
        v = stack(v_ref[0, rs, :])
        kv = bmm(gm[:, :, GW:], bd(v), nn)
        mu0 = bmm(inv, jnp.concatenate([bd(at), bd(kv[:, :C])], axis=2), nn)
        m = mu0[:, :, :GW].astype(BF16)
        u0 = mu0[:, :, GW:]
        bw = stack(bw_ref[0, rs, :])
        low_rank = jnp.where(same_head, bmm(m, bw, tn), 0.0)
        const = jnp.where(same_head, bmm(jnp.concatenate([u0.astype(BF16), v], axis=1),
                                         jnp.concatenate([bw, stack(kw_ref[0, rs, :])], axis=1), tn), 0.0)
        m_s[rs, :] = unstack(m)
        u0_s[rs, :] = unstack(u0)
        o0_s[rs, :] = unstack(kv[:, C:])
        arb_s[rs, :] = unstack(gm[:, C:, :GW]).astype(BF16)
        for q in range(n_groups):
            p_s[i * span:(i + 1) * span, q] = low_rank[q * span:(q + 1) * span].astype(BF16)
            c_s[i * span:(i + 1) * span, q] = const[q * span:(q + 1) * span]

    def groups(x):
        return jnp.stack([x[:, q * GW:(q + 1) * GW] for q in range(n_groups)], axis=0)

    def chunk_state(i, c):
        s0 = state[...]
        sh_s[i] = s0.astype(BF16)
        w_end = groups(wend_ref[0, pl.ds(j * n_sub + i, 1), :])
        state[...] = s0 * w_end + bmm(s0, p_s[i], nn) + c_s[i]
        return c

    lax.fori_loop(0, n_sub, chunk_state, 0, unroll=True)

    ones_p = jnp.broadcast_to(ones_bd, (n_groups * span, GW, GW))
    for i in range(n_sub // span):
        rs = slice(i * rows, (i + 1) * rows)
        s0 = jnp.concatenate([sh_s[i * span:(i + 1) * span, q] for q in range(n_groups)], axis=0)
        mr_s = bmm(jnp.concatenate([stack(m_s[rs, :]), stack(rt_ref[0, rs, :])], axis=1), s0, nt)
        u = mr_s[:, :C] + stack(u0_s[rs, :])
        o = mr_s[:, C:] + stack(o0_s[rs, :]) + bmm(stack(arb_s[rs, :]), bd(u), nn)
        hi, lo2 = _split2(o)
        s1 = bmm(jnp.concatenate([hi, lo2], axis=1), ones_p, nn)
        dlt = o - (s1[:, :C] + s1[:, C:]) * (1.0 / N)
        hi, lo2 = _split2(dlt * dlt)
        s2 = bmm(jnp.concatenate([hi, lo2], axis=1), ones_p, nn)
        on = unstack(dlt * lax.rsqrt((s2[:, :C] + s2[:, C:]) * (1.0 / N) + GN_EPS))
        on = on * lnw_ref[...] + lnb_ref[...]
        o_ref[0, rs, :] = (on + bonus_ref[0, rs, :]) * g_ref[0, rs, :]


def _wkv(at, rt, bt, kt, bw, kw, v, bonus, g, wend, ln_w, ln_b):
    B, S, C = at.shape
    tc = 512
    tm_spec = pl.BlockSpec((1, tc, C), lambda bb, j: (bb, j, 0))
    par_spec = pl.BlockSpec((1, C), lambda bb, j: (0, 0))
    blocks = (C // WKV_GROUP_WIDTH, WKV_GROUP_WIDTH, WKV_GROUP_WIDTH)
    return pl.pallas_call(
        _wkv_kernel,
        out_shape=jax.ShapeDtypeStruct((B, S, C), F32),
        grid=(B, S // tc),
        in_specs=([tm_spec] * 9 + [pl.BlockSpec((1, S // WKV_CHUNK, C), lambda bb, j: (bb, 0, 0))]
                  + [par_spec] * 2),
        out_specs=tm_spec,
        scratch_shapes=[pltpu.VMEM(blocks, F32),
                        pltpu.VMEM((tc, C), BF16), pltpu.VMEM((tc, C), F32), pltpu.VMEM((tc, C), F32),
                        pltpu.VMEM((tc, C), BF16),
                        pltpu.VMEM((tc // WKV_CHUNK,) + blocks, BF16), pltpu.VMEM((tc // WKV_CHUNK,) + blocks, F32),
                        pltpu.VMEM((tc // WKV_CHUNK,) + blocks, BF16)],
        compiler_params=pltpu.CompilerParams(
            dimension_semantics=("arbitrary", "arbitrary"), vmem_limit_bytes=VMEM_LIMIT),
        name="wkv7",
    )(at, rt, bt, kt, bw, kw, v, bonus, g, wend, ln_w, ln_b)


MERGE_SPLIT = 2


def _merge_kernel(att_ref, o_ref, gate_ref, x_ref, wa_ref, wr_ref, wo_ref, g2_ref, wrt_ref, brt_ref,
                  x1_ref, h2_ref, lg_ref, cnt_ref):
    tm = x_ref.shape[1]
    ts = tm // MERGE_SPLIT
    parts = [slice(h * ts, (h + 1) * ts) for h in range(MERGE_SPLIT)]
    lane = lax.broadcasted_iota(jnp.int32, (1, ROUTER_PAD), 1)

    @pl.when((pl.program_id(0) == 0) & (pl.program_id(1) == 0))
    def _():
        cnt_ref[...] = jnp.zeros_like(cnt_ref)

    ys = [(_dot(att_ref[0, rs, :], wa_ref[...]), _dot(o_ref[0, rs, :], wr_ref[...])) for rs in parts]

    mixed = []
    for rs, (y_att, y_rwkv) in zip(parts, ys):
        gates = _sigmoid(gate_ref[0, rs, :].astype(F32))
        mixed.append(gates[:, :D_MODEL] * y_att + gates[:, D_MODEL:] * y_rwkv)

    x1s = []
    for rs, mx in zip(parts, mixed):
        x1 = x_ref[0, rs, :] + _dot(mx, wo_ref[...])
        x1_ref[0, rs, :] = x1
        x1s.append(x1)

    splits = []
    for rs, x1 in zip(parts, x1s):
        ms = jnp.mean(x1 * x1, axis=-1, keepdims=True)
        h2 = x1 * lax.rsqrt(ms + NORM_EPS) * g2_ref[...]
        hh, hl = _split2(h2)
        half = D_MODEL // 2
        rounded = hh.astype(F32)
        w_hi = pltpu.bitcast(rounded[:, :half], jnp.uint32)
        w_lo = pltpu.bitcast(rounded[:, half:], jnp.uint32)
        h2_ref[0, rs, :] = w_hi | jnp.right_shift(w_lo, jnp.uint32(16))
        splits.append((hh, hl))

    wh, wl = wrt_ref[0], wrt_ref[1]
    logits = [jnp.dot(hh, wh, preferred_element_type=F32)
              + (jnp.dot(hh, wl, preferred_element_type=F32) + jnp.dot(hl, wh, preferred_element_type=F32))
              + brt_ref[...] for hh, hl in splits]

    ti = lax.broadcasted_iota(jnp.int32, (ts, ts), 0)
    tj = lax.broadcasted_iota(jnp.int32, (ts, ts), 1)
    before = jnp.where(tj < ti, 1.0, 0.0).astype(BF16)
    for rs, lg in zip(parts, logits):
        gmask = lane < N_EXPERT_GROUPS
        gl = jnp.where(gmask, lg, -jnp.inf)
        gmax = jnp.max(gl, axis=-1, keepdims=True)
        grp_idx = jnp.min(jnp.where(gl == gmax, lane, ROUTER_PAD), axis=-1, keepdims=True)
        grp_p = 1.0 / jnp.sum(jnp.where(gmask, jnp.exp(lg - gmax), 0.0), axis=-1, keepdims=True)
        e_lane = lane - N_EXPERT_GROUPS
        emask = ((e_lane >= 0) & (e_lane < N_EXPERTS)
                 & (jnp.right_shift(e_lane, int(math.log2(EXPERTS_PER_GROUP))) == grp_idx))
        el = jnp.where(emask, lg, -jnp.inf)
        m1 = jnp.max(el, axis=-1, keepdims=True)
        i1 = jnp.min(jnp.where(el == m1, lane, ROUTER_PAD), axis=-1, keepdims=True)
        el2 = jnp.where(lane == i1, -jnp.inf, el)
        m2 = jnp.max(el2, axis=-1, keepdims=True)
        i2 = jnp.min(jnp.where(el2 == m2, lane, ROUTER_PAD), axis=-1, keepdims=True)
        e2 = jnp.exp(m2 - m1)
        gate1 = grp_p / (1.0 + e2)
        gate2 = gate1 * e2

        oh = jnp.where((lane == i1) | (lane == i2), 1.0, 0.0)
        pref = jnp.dot(before, oh.astype(BF16), preferred_element_type=F32) + cnt_ref[...]
        rank1 = jnp.sum(jnp.where(lane == i1, pref, 0.0), axis=-1, keepdims=True)
        rank2 = jnp.sum(jnp.where(lane == i2, pref, 0.0), axis=-1, keepdims=True)
        cnt_ref[...] = cnt_ref[...] + jnp.sum(oh, axis=0, keepdims=True)
        info = jnp.where(lane == 0, (i1 - N_EXPERT_GROUPS).astype(F32), 0.0)
        info = jnp.where(lane == 1, (i2 - N_EXPERT_GROUPS).astype(F32), info)
        info = jnp.where(lane == 2, rank1, info)
        info = jnp.where(lane == 3, rank2, info)
        info = jnp.where(lane == 4, gate1, info)
        info = jnp.where(lane == 5, gate2, info)
        lg_ref[0, rs, :] = info


def _merge(att, o_g, p3, x, w_att_out, w_rwkv_out, w_out, gain2, w_router, b_router):
    B, S, D = x.shape
    tm = 512
    full = lambda shape: pl.BlockSpec(shape, lambda b, j: (0,) * len(shape))
    return pl.pallas_call(
        _merge_kernel,
        out_shape=[jax.ShapeDtypeStruct((B, S, D), F32), jax.ShapeDtypeStruct((B, S, D // 2), jnp.uint32),
                   jax.ShapeDtypeStruct((B, S, ROUTER_PAD), F32),
                   jax.ShapeDtypeStruct((1, ROUTER_PAD), F32)],
        grid=(B, S // tm),
        in_specs=[pl.BlockSpec((1, tm, ATT_OUT_WIDTH), lambda b, j: (b, j, 0)),
                  pl.BlockSpec((1, tm, RWKV_WIDTH), lambda b, j: (b, j, 0)),
                  pl.BlockSpec((1, tm, 2 * D_MODEL), lambda b, j: (b, j, 1)),
                  pl.BlockSpec((1, tm, D), lambda b, j: (b, j, 0)),
                  full((ATT_OUT_WIDTH, D)), full((RWKV_WIDTH, D)), full((D, D)),
                  full((1, D)), full((2, D, ROUTER_PAD)), full((1, ROUTER_PAD))],
        out_specs=[pl.BlockSpec((1, tm, D), lambda b, j: (b, j, 0)),
                   pl.BlockSpec((1, tm, D // 2), lambda b, j: (b, j, 0)),
                   pl.BlockSpec((1, tm, ROUTER_PAD), lambda b, j: (b, j, 0)),
                   pl.BlockSpec((1, ROUTER_PAD), lambda b, j: (0, 0))],
        compiler_params=pltpu.CompilerParams(
            dimension_semantics=("arbitrary", "arbitrary"), vmem_limit_bytes=VMEM_LIMIT),
        name="merge_outproj_router",
    )(att, o_g, p3, x, w_att_out, w_rwkv_out, w_out, gain2, w_router, b_router)


def _expert_kernel(be_ref, nb_ref, x_ref, wg_ref, wu_ref, wd_ref, y_ref, wgu_s, wd_s):
    i = pl.program_id(0)

    @pl.when((i == 0) | (be_ref[i] != be_ref[jnp.maximum(i - 1, 0)]))
    def _():
        wgu_s[:, :D_EXPERT] = wg_ref[0].astype(BF16)
        wgu_s[:, D_EXPERT:] = wu_ref[0].astype(BF16)
        wd_s[...] = wd_ref[0].astype(BF16)

    @pl.when(i < nb_ref[0])
    def _():
        words = x_ref[...]
        x_a = pltpu.bitcast(words & jnp.uint32(0xFFFF0000), F32)
        x_b = pltpu.bitcast(jnp.left_shift(words, jnp.uint32(16)), F32)
        xb = jnp.concatenate([x_a, x_b], axis=-1).astype(BF16)
        h = jnp.dot(xb, wgu_s[...], preferred_element_type=F32)
        hg, hu = h[:, :D_EXPERT], h[:, D_EXPERT:]
        hid = hg * _sigmoid(hg) * hu
        y_ref[...] = jnp.dot(hid.astype(BF16), wd_s[...], preferred_element_type=F32).astype(y_ref.dtype)

    @pl.when(i >= nb_ref[0])
    def _():
        y_ref[...] = jnp.zeros_like(y_ref)


def _experts(blk_expert, n_used, xin, w_gate, w_up, w_down):
    n_rows = xin.shape[0]
    D = D_MODEL
    n_blocks = n_rows // MOE_ROWS
    return pl.pallas_call(
        _expert_kernel,
        out_shape=jax.ShapeDtypeStruct((n_rows, D), BF16),
        grid_spec=pltpu.PrefetchScalarGridSpec(
            num_scalar_prefetch=2,
            grid=(n_blocks,),
            scratch_shapes=[pltpu.VMEM((D, 2 * D_EXPERT), BF16), pltpu.VMEM((D_EXPERT, D), BF16)],
            in_specs=[pl.BlockSpec((MOE_ROWS, D // 2), lambda i, be, nb: (i, 0)),
                      pl.BlockSpec((1, D, D_EXPERT), lambda i, be, nb: (be[i], 0, 0)),
                      pl.BlockSpec((1, D, D_EXPERT), lambda i, be, nb: (be[i], 0, 0)),
                      pl.BlockSpec((1, D_EXPERT, D), lambda i, be, nb: (be[i], 0, 0))],
            out_specs=pl.BlockSpec((MOE_ROWS, D), lambda i, be, nb: (i, 0))),
        compiler_params=pltpu.CompilerParams(
            dimension_semantics=("arbitrary",), vmem_limit_bytes=VMEM_LIMIT),
        name="moe_experts",
    )(blk_expert, n_used, xin, w_gate, w_up, w_down)


def _row_tokens(dest_flat, counts, pcounts, pstarts, pends, n_rows, T):
    A = dest_flat.shape[0]
    n_pad = n_rows - A
    seg_count = jnp.concatenate([pcounts - counts, n_rows - pends[-1:]])
    seg_first = jnp.concatenate([pstarts + counts, pends[-1:]])
    seg_end = jnp.cumsum(seg_count)
    i = jnp.arange(n_pad, dtype=jnp.int32)
    seg = jnp.sum(i[:, None] >= seg_end[None, :], axis=1)
    pad_rows = (seg_first[seg] + i - (seg_end - seg_count)[seg]).astype(jnp.int32)
    rows = jnp.concatenate([dest_flat, pad_rows])
    toks = jnp.concatenate([jnp.arange(A, dtype=jnp.int32) // TOP_K, i % T])
    tok_bits = max(T - 1, 1).bit_length()
    assert n_rows << tok_bits <= 1 << 32
    packed = jnp.sort((rows.astype(jnp.uint32) << tok_bits) | toks.astype(jnp.uint32))
    return (packed & jnp.uint32((1 << tok_bits) - 1)).astype(jnp.int32)


def _row_layout(info, counts, T):
    A = T * TOP_K
    eid = info[:, 0:2].astype(jnp.int32)
    rank = info[:, 2:4].astype(jnp.int32)
    gates = info[:, 4:6]
    counts = counts.astype(jnp.int32)
    pcounts = (counts + MOE_ROWS - 1) // MOE_ROWS * MOE_ROWS
    pends = jnp.cumsum(pcounts)
    pstarts = pends - pcounts
    dest = (pstarts[eid] + rank).astype(jnp.int32)
    n_blocks = -(-A // MOE_ROWS) + N_EXPERTS
    n_rows = n_blocks * MOE_ROWS
    row_tok = _row_tokens(dest.reshape(A), counts, pcounts, pstarts, pends, n_rows, T)
    blk_start = jnp.arange(n_blocks, dtype=jnp.int32) * MOE_ROWS
    blk_expert = jnp.minimum(jnp.sum(blk_start[:, None] >= pends[None, :], axis=1), N_EXPERTS - 1)
    n_used = (pends[N_EXPERTS - 1] // MOE_ROWS).astype(jnp.int32)[None]
    return row_tok, blk_expert.astype(jnp.int32), n_used, dest, gates


def _combine_kernel(x1_ref, y0_ref, y1_ref, g_ref, o_ref):
    g = g_ref[...]
    o_ref[...] = x1_ref[...] + (g[:, 0:1] * y0_ref[...].astype(F32) + g[:, 1:2] * y1_ref[...].astype(F32))


def _combine(x1, y0, y1, gates):
    T, D = x1.shape
    tm = 1024
    row_spec = pl.BlockSpec((tm, D), lambda i: (i, 0))
    return pl.pallas_call(
        _combine_kernel,
        out_shape=jax.ShapeDtypeStruct((T, D), F32),
        grid=(T // tm,),
        in_specs=[row_spec, row_spec, row_spec, pl.BlockSpec((tm, TOP_K), lambda i: (i, 0))],
        out_specs=row_spec,
        compiler_params=pltpu.CompilerParams(
            dimension_semantics=("arbitrary",), vmem_limit_bytes=VMEM_LIMIT),
        name="moe_combine",
    )(x1, y0, y1, gates)


def _pad_rows(w, rows):
    return jnp.pad(w, ((0, rows - w.shape[0]), (0, 0)))


def kernel(x, norm1_gain, w_in, q_norm_gain, k_norm_gain, rel_bias_table, w_att_out, rwkv_shift_mu,
           rwkv_w0, rwkv_w_up, rwkv_a0, rwkv_a_up, rwkv_g_up, rwkv_k_k, rwkv_k_a, rwkv_r_k,
           rwkv_ln_w, rwkv_ln_b, w_rwkv_out, w_out, norm2_gain, w_group_router, b_group_router,
           w_expert_router, b_expert_router, w_expert_gate, w_expert_up, w_expert_down):
    B, S, D = x.shape
    T = B * S
    C = RWKV_WIDTH
    bias = _attention_bias(rel_bias_table)
    for l in range(norm1_gain.shape[0]):
        wi = w_in[l]
        w_qkv = wi[:, :3 * ATT_WIDTH]
        w_rw = wi[:, 3 * ATT_WIDTH:3 * ATT_WIDTH + RWKV_SHIFT_WIDTH]
        w_gt = wi[:, 3 * ATT_WIDTH + RWKV_SHIFT_WIDTH:]
        pad_cols = lambda w, n: jnp.pad(w, ((0, 0), (0, n - w.shape[1])))
        o_w, o_a, o_g = 3 * C, 3 * C + DECAY_LORA, 3 * C + DECAY_LORA + ICLR_LORA

        def rw_layout(w):
            return jnp.concatenate([w[:, :o_w], pad_cols(w[:, o_w:o_a], LANES),
                                    pad_cols(w[:, o_a:o_g], LANES), pad_cols(w[:, o_g:], 2 * LANES)], axis=1)

        w_rg = jnp.concatenate([rw_layout(w_rw), w_gt], axis=1).astype(BF16)
        mu = rw_layout(rwkv_shift_mu[l][None, :])

        p_rg, p_qkv = _inproj(x.reshape(T, D), norm1_gain[l][None, :], w_rg, w_qkv.astype(BF16))
        p3 = p_rg.reshape(B, S, RWKV_PAD_WIDTH + 2 * D)

        tile2 = lambda gmat: jnp.tile(gmat, (1, 2))[:, None, :]
        att = _attention(p_qkv.reshape(B, S, 3 * ATT_WIDTH), tile2(q_norm_gain[l]), tile2(k_norm_gain[l]),
                         bias)

        row = lambda v: v[None, :]
        prep = _rwkv_prep(
            p3, mu, row(rwkv_w0[l]), _pad_rows(rwkv_w_up[l], LANES).astype(BF16), row(rwkv_a0[l]),
            _pad_rows(rwkv_a_up[l], LANES).astype(BF16), _pad_rows(rwkv_g_up[l], 2 * LANES).astype(BF16),
            row(rwkv_k_k[l]), row(rwkv_k_a[l]), rwkv_r_k[l].reshape(1, C))
        o_g = _wkv(*prep, row(rwkv_ln_w[l]), row(rwkv_ln_b[l]))

        w_router = jnp.concatenate([w_group_router[l], w_expert_router[l]], axis=1)
        w_router = jnp.pad(w_router, ((0, 0), (0, ROUTER_PAD - w_router.shape[1])))
        wr_hi = w_router.astype(BF16)
        wr_lo = (w_router - wr_hi.astype(F32)).astype(BF16)
        b_router = jnp.concatenate([b_group_router[l], b_expert_router[l]])
        b_router = jnp.pad(b_router, (0, ROUTER_PAD - b_router.shape[0]))[None, :]
        x1, h2, info, counts = _merge(
            att, o_g, p3, x, w_att_out[l].astype(BF16),
            w_rwkv_out[l].astype(BF16), w_out[l].astype(BF16),
            norm2_gain[l][None, :], jnp.stack([wr_hi, wr_lo]), b_router)

        row_tok, blk_expert, n_used, dest, gates = _row_layout(
            info.reshape(T, ROUTER_PAD), counts[0, N_EXPERT_GROUPS:N_EXPERT_GROUPS + N_EXPERTS], T)
        xin = h2.reshape(T, D // 2)[row_tok]
        yb = _experts(blk_expert, n_used, xin, w_expert_gate[l], w_expert_up[l], w_expert_down[l])
        x = _combine(x1.reshape(T, D), yb[dest[:, 0]], yb[dest[:, 1]], gates).reshape(B, S, D)
    return x
```

```python
import functools
import math

import jax
import jax.numpy as jnp
from jax import lax
from jax.experimental import pallas as pl
from jax.experimental.pallas import tpu as pltpu

F32 = jnp.float32
BF16 = jnp.bfloat16

D_MODEL = 1024
ATT_PATTERNS = ((128, 1), (512, 4), (2048, 16))
N_ATT_GROUPS = 3
HEADS_PER_GROUP = 4
ATT_HEADS = 12
HEAD_DIM = 64
ATT_WIDTH = 768
ATT_OUT_WIDTH = 256
REL_BUCKETS = 32
REL_MAX_DIST = 2048
RWKV_HEADS = 8
RWKV_HEAD_DIM = 64
RWKV_WIDTH = 512
DECAY_LORA = 64
ICLR_LORA = 64
GATE_LORA = 160
RWKV_SHIFT_WIDTH = 1824
GN_EPS = 64e-5
N_EXPERT_GROUPS = 4
EXPERTS_PER_GROUP = 8
N_EXPERTS = 32
TOP_K = 2
D_EXPERT = 512
NORM_EPS = 1e-6
NEG_INF = -1e30

LANES = 128
ATT_BLOCK = 128
ATT_BATCH = 4
WKV_CHUNK = 64
RWKV_PAD_WIDTH = 2048
QKV_COL_BLOCK = 0
ROUTER_PAD = 128
MOE_ROWS = 512
VMEM_LIMIT = 56 * 1024 * 1024


def _sigmoid(x):
    return 1.0 / (1.0 + jnp.exp(-x))


def _dot(a, b):
    return jnp.dot(a.astype(BF16), b.astype(BF16), preferred_element_type=F32)


def _dot_t(a, b):
    return lax.dot_general(a.astype(BF16), b.astype(BF16), (((1,), (1,)), ((), ())),
                           preferred_element_type=F32)


def _split3(a):
    hi = a.astype(BF16)
    r1 = a - hi.astype(F32)
    mid = r1.astype(BF16)
    lo = (r1 - mid.astype(F32)).astype(BF16)
    return hi, mid, lo


def _split2(a):
    hi = a.astype(BF16)
    lo = (a - hi.astype(F32)).astype(BF16)
    return hi, lo


INPROJ_COLS = 512


def _inproj_kernel(x_ref, g_ref, wrg_ref, wqkv_ref, rg_ref, qkv_ref):
    x = x_ref[...]
    ms = jnp.mean(x * x, axis=-1, keepdims=True)
    h = (x * lax.rsqrt(ms + NORM_EPS) * g_ref[...]).astype(BF16)
    for w_ref, o_ref in ((wrg_ref, rg_ref), (wqkv_ref, qkv_ref)):
        width = w_ref.shape[1]
        for c in range(0, width, INPROJ_COLS):
            cs = slice(c, min(c + INPROJ_COLS, width))
            o_ref[:, cs] = jnp.dot(h, w_ref[:, cs], preferred_element_type=F32).astype(o_ref.dtype)


def _inproj(x2, gain, w_rg, w_qkv):
    T = x2.shape[0]
    tm = 512
    n_rg, n_qkv = w_rg.shape[1], w_qkv.shape[1]
    return pl.pallas_call(
        _inproj_kernel,
        out_shape=[jax.ShapeDtypeStruct((T, n_rg), BF16), jax.ShapeDtypeStruct((T, n_qkv), F32)],
        grid=(T // tm,),
        in_specs=[pl.BlockSpec((tm, D_MODEL), lambda i: (i, 0)),
                  pl.BlockSpec((1, D_MODEL), lambda i: (0, 0)),
                  pl.BlockSpec((D_MODEL, n_rg), lambda i: (0, 0)),
                  pl.BlockSpec((D_MODEL, n_qkv), lambda i: (0, 0))],
        out_specs=[pl.BlockSpec((tm, n_rg), lambda i: (i, 0)),
                   pl.BlockSpec((tm, n_qkv), lambda i: (i, 0))],
        compiler_params=pltpu.CompilerParams(
            dimension_semantics=("arbitrary",), vmem_limit_bytes=VMEM_LIMIT),
        name="inproj",
    )(x2, gain, w_rg, w_qkv)


def _attn_kernel(q0, q1, q2, k0, k1, k2, v0, v1, v2, qg_ref, kg_ref, bias_ref, out_ref,
                 qn_s, kn_s, acc_s, m_s, l_s):
    S = out_ref.shape[1]
    q_refs, k_refs, v_refs = (q0, q1, q2), (k0, k1, k2), (v0, v1, v2)
    lane = lax.broadcasted_iota(jnp.int32, (1, LANES), 1)
    lo = lane < HEAD_DIM
    col = lax.broadcasted_iota(jnp.int32, (1, 2 * ATT_BLOCK), 1)

    same_head = (lax.broadcasted_iota(jnp.int32, (LANES, LANES), 0) < HEAD_DIM) == (
        lax.broadcasted_iota(jnp.int32, (LANES, LANES), 1) < HEAD_DIM)
    ones_bd = jnp.where(same_head, 1.0, 0.0).astype(BF16)

    def pair_norm(x, gain):
        ss = jnp.dot((x * x).astype(BF16), ones_bd, preferred_element_type=F32)
        return x * lax.rsqrt(ss * (1.0 / HEAD_DIM) + NORM_EPS) * gain

    rows_per = 256

    def norm_body(i, c):
        rs = pl.ds(pl.multiple_of(i * rows_per, rows_per), rows_per)
        for g in range(N_ATT_GROUPS):
            qn_s[g, rs, :] = pair_norm(q_refs[g][0, rs, :], qg_ref[g]) * (HEAD_DIM ** -0.5)
            kn_s[g, rs, :] = pair_norm(k_refs[g][0, rs, :], kg_ref[g])
        return c

    lax.fori_loop(0, S // rows_per, norm_body, 0)

    for g, (window, d) in enumerate(ATT_PATTERNS):
        n_blk = (S // d) // ATT_BLOCK
        has_prev = n_blk > 1

        def rows_at(start, d=d):
            if d == 1:
                return pl.ds(pl.multiple_of(start, ATT_BLOCK), ATT_BLOCK)
            return pl.ds(start, ATT_BLOCK, stride=d)

        def blk_body(it, c, g=g, d=d, has_prev=has_prev, rows_at=rows_at):
            curs, q2s, kcats, vcats, pens = [], [], [], [], []
            for b in range(ATT_BATCH):
                idx = it * ATT_BATCH + b
                r = idx % d
                n = idx // d
                cur = rows_at(n * (ATT_BLOCK * d) + r)
                q = qn_s[g, cur, :]
                q2s.append(jnp.concatenate([jnp.where(lo, q, 0.0), jnp.where(lo, 0.0, q)], axis=0).astype(BF16))
                kc = kn_s[g, cur, :].astype(BF16)
                vc = v_refs[g][0, cur, :].astype(BF16)
                if has_prev:
                    prv = rows_at(jnp.maximum(n - 1, 0) * (ATT_BLOCK * d) + r)
                    kc = jnp.concatenate([kn_s[g, prv, :].astype(BF16), kc], axis=0)
                    vc = jnp.concatenate([v_refs[g][0, prv, :].astype(BF16), vc], axis=0)
                    pens.append(jnp.where(col < ATT_BLOCK, jnp.where(n == 0, NEG_INF, 0.0), 0.0)[None])
                curs.append(cur)
                kcats.append(kc)
                vcats.append(jnp.concatenate([vc, jnp.ones_like(vc)], axis=1))
            q2, kcat, vcat = jnp.stack(q2s), jnp.stack(kcats), jnp.stack(vcats)
            s = lax.dot_general(q2, kcat, (((2,), (2,)), ((0,), (0,))), preferred_element_type=F32)
            if has_prev:
                s = s + bias_ref[g, 0] + jnp.concatenate(pens, axis=0)
            else:
                s = s + bias_ref[g, 0, :, ATT_BLOCK:]
            m = jnp.max(s, axis=-1, keepdims=True)
            p = jnp.exp(s - m).astype(BF16)
            ol = lax.dot_general(p, vcat, (((2,), (1,)), ((0,), (0,))), preferred_element_type=F32)
            for b in range(ATT_BATCH):
                cur = curs[b]
                acc_s[g, cur, :] = jnp.where(lo, ol[b, :ATT_BLOCK, :LANES], ol[b, ATT_BLOCK:, :LANES])
                l_s[g, cur, :] = jnp.where(lo, ol[b, :ATT_BLOCK, LANES:], ol[b, ATT_BLOCK:, LANES:])
                m_s[g, cur, :] = jnp.where(lo, m[b, :ATT_BLOCK], m[b, ATT_BLOCK:])
            return c

        lax.fori_loop(0, d * n_blk // ATT_BATCH, blk_body, 0)

    def comb_body(i, c):
        rs = pl.ds(pl.multiple_of(i * rows_per, rows_per), rows_per)
        m = jnp.maximum(jnp.maximum(m_s[0, rs, :], m_s[1, rs, :]), m_s[2, rs, :])
        num = jnp.zeros((rows_per, LANES), F32)
        den = jnp.zeros((rows_per, LANES), F32)
        for g in range(N_ATT_GROUPS):
            e = jnp.exp(m_s[g, rs, :] - m)
            num = num + e * acc_s[g, rs, :]
            den = den + e * l_s[g, rs, :]
        out_ref[0, rs, :] = num / den
        return c

    lax.fori_loop(0, S // rows_per, comb_body, 0)


def _attention(p3, q_gain, k_gain, bias):
    B, S, _ = p3.shape
    n_pairs = HEADS_PER_GROUP // 2

    def col_spec(base):
        return [pl.BlockSpec((1, S, LANES),
                             functools.partial(lambda b, sp, off: (b, 0, off + sp), off=base + g * n_pairs))
                for g in range(N_ATT_GROUPS)]

    n_qkv_blocks = ATT_WIDTH // LANES
    in_specs = (col_spec(QKV_COL_BLOCK) + col_spec(QKV_COL_BLOCK + n_qkv_blocks)
                + col_spec(QKV_COL_BLOCK + 2 * n_qkv_blocks)
                + [pl.BlockSpec((N_ATT_GROUPS, 1, LANES), lambda b, sp: (0, 0, 0)),
                   pl.BlockSpec((N_ATT_GROUPS, 1, LANES), lambda b, sp: (0, 0, 0)),
                   pl.BlockSpec((N_ATT_GROUPS, 1, 2 * ATT_BLOCK, 2 * ATT_BLOCK),
                                lambda b, sp: (0, sp, 0, 0))])
    return pl.pallas_call(
        _attn_kernel,
        out_shape=jax.ShapeDtypeStruct((B, S, ATT_OUT_WIDTH), F32),
        grid=(B, n_pairs),
        in_specs=in_specs,
        out_specs=pl.BlockSpec((1, S, LANES), lambda b, sp: (b, 0, sp)),
        scratch_shapes=[pltpu.VMEM((N_ATT_GROUPS, S, LANES), F32) for _ in range(5)],
        compiler_params=pltpu.CompilerParams(
            dimension_semantics=("arbitrary", "arbitrary"), vmem_limit_bytes=VMEM_LIMIT),
        name="dilated_attention",
    )(*([p3] * 9), q_gain, k_gain, bias)


def _t5_causal_bucket(dist):
    max_exact = REL_BUCKETS // 2
    d = jnp.maximum(dist.astype(F32), 1.0)
    large = max_exact + (jnp.log(d / max_exact) / math.log(REL_MAX_DIST / max_exact)
                         * (REL_BUCKETS - max_exact)).astype(jnp.int32)
    large = jnp.minimum(large, REL_BUCKETS - 1)
    return jnp.where(dist < max_exact, dist, large)


def _attention_bias(rel_bias_table):
    W = ATT_BLOCK
    qi = jnp.arange(W)[:, None]
    kj = jnp.arange(2 * W)[None, :]
    rel = qi + W - kj
    valid = (rel >= 0) & (rel <= W)
    per_group = []
    for g, (_, d) in enumerate(ATT_PATTERNS):
        tab = rel_bias_table[:, g * HEADS_PER_GROUP:(g + 1) * HEADS_PER_GROUP]
        bucket = _t5_causal_bucket(jnp.clip(rel, 0, W) * d)
        onehot = (bucket[:, :, None] == jnp.arange(REL_BUCKETS)[None, None, :]).astype(F32)
        bias = jnp.einsum('qkb,bh->qkh', onehot, tab.astype(F32),
                          precision=lax.Precision.HIGHEST)
        bias = jnp.where(valid[:, :, None], bias, NEG_INF)
        per_group.append(jnp.transpose(bias, (2, 0, 1)).reshape(2, 2 * W, 2 * W))
    return jnp.stack(per_group, axis=0)


def _rwkv_prep_kernel(p_ref, mu_ref, w0_ref, wup_ref, a0_ref, aup_ref, gup_ref, kk_ref, ka_ref, rk_ref,
                      at_o, rt_o, bt_o, kt_o, bw_o, kw_o, v_o, bonus_o, g_o, wend_o, carry):
    tq = p_ref.shape[1]
    C = RWKV_WIDTH
    L = WKV_CHUNK

    @pl.when(pl.program_id(1) == 0)
    def _():
        carry[...] = jnp.zeros_like(carry)

    p = p_ref[0].astype(F32)
    row = lax.broadcasted_iota(jnp.int32, (tq, 1), 0)
    prev = jnp.where(row == 0, carry[...], pltpu.roll(p, 1, axis=0))
    carry[...] = p[tq - 1:tq, :]
    pm = p + (prev - p) * mu_ref[...]
    r, k, v = pm[:, 0:C], pm[:, C:2 * C], pm[:, 2 * C:3 * C]
    xw = pm[:, 3 * C:3 * C + LANES]
    xa = pm[:, 3 * C + LANES:3 * C + 2 * LANES]
    xg = pm[:, 3 * C + 2 * LANES:]

    z = w0_ref[...] + _dot(jnp.tanh(xw), wup_ref[...])
    softplus_neg = jnp.maximum(-z, 0.0) + jnp.log(1.0 + jnp.exp(-jnp.abs(z)))
    lw = -jnp.exp(-softplus_neg - 0.5)
    a = _sigmoid(a0_ref[...] + _dot(xa, aup_ref[...]))
    g = _dot(_sigmoid(xg), gup_ref[...])

    hr = jnp.right_shift(lax.broadcasted_iota(jnp.int32, (C, C), 0), 6)
    hc = jnp.right_shift(lax.broadcasted_iota(jnp.int32, (C, C), 1), 6)
    ones_bd = jnp.where(hr == hc, 1.0, 0.0).astype(BF16)

    def head_sums(x):
        return jnp.dot(x.astype(BF16), ones_bd, preferred_element_type=F32)

    k2 = k * (1.0 + (a - 1.0) * ka_ref[...])
    kk = k * kk_ref[...]
    kk = kk / jnp.maximum(jnp.sqrt(head_sums(kk * kk)), 1e-12)
    bonus = head_sums(r * k2 * rk_ref[...]) * v
    b = kk * a

    half = 256
    ri = lax.broadcasted_iota(jnp.int32, (half, half), 0)
    ci = lax.broadcasted_iota(jnp.int32, (half, half), 1)
    tri = jnp.where((jnp.right_shift(ri, 6) == jnp.right_shift(ci, 6)) & (ri >= ci), 1.0, 0.0).astype(BF16)
    cums = []
    for s in range(tq // half):
        hi, mid, lo3 = _split3(lw[s * half:(s + 1) * half, :])
        cums.append(jnp.dot(tri, hi, preferred_element_type=F32)
                    + (jnp.dot(tri, mid, preferred_element_type=F32)
                       + jnp.dot(tri, lo3, preferred_element_type=F32)))
    cum = jnp.concatenate(cums, axis=0)
    n_chunks = tq // L
    ends = [cum[c * L + L - 1:c * L + L, :] for c in range(n_chunks)]
    tot = jnp.concatenate([jnp.broadcast_to(e, (L, C)) for e in ends], axis=0)
    wend = jnp.exp(jnp.concatenate(ends, axis=0))

    e_pos = jnp.exp(cum)
    e_neg = jnp.exp(-cum)
    e_rem = jnp.exp(tot - cum)
    at = -kk * jnp.exp(cum - lw)
    rt = r * e_pos
    bt = b * e_neg
    kt = k2 * e_neg
    bw = b * e_rem
    kw = k2 * e_rem
    at_o[0] = at.astype(BF16)
    rt_o[0] = rt.astype(BF16)
    bt_o[0] = bt.astype(BF16)
    kt_o[0] = kt.astype(BF16)
    bw_o[0] = bw.astype(BF16)
    kw_o[0] = kw.astype(BF16)
    v_o[0] = v.astype(BF16)
    bonus_o[0] = bonus
    g_o[0] = g
    wend_o[0] = wend


def _rwkv_prep(p3, mu, w0, wup, a0, aup, gup, k_k, k_a, r_k):
    B, S, _ = p3.shape
    tq = 512
    C = RWKV_WIDTH
    tm_shape = lambda dt: jax.ShapeDtypeStruct((B, S, C), dt)
    tm_spec = pl.BlockSpec((1, tq, C), lambda b, j: (b, j, 0))
    full = lambda shape: pl.BlockSpec(shape, lambda b, j: (0,) * len(shape))
    return pl.pallas_call(
        _rwkv_prep_kernel,
        out_shape=[tm_shape(BF16)] * 7 + [tm_shape(F32)] * 2 + [jax.ShapeDtypeStruct((B, S // WKV_CHUNK, C), F32)],
        grid=(B, S // tq),
        in_specs=[pl.BlockSpec((1, tq, RWKV_PAD_WIDTH), lambda b, j: (b, j, 0)),
                  full((1, RWKV_PAD_WIDTH)), full((1, C)), full((LANES, C)), full((1, C)),
                  full((LANES, C)), full((2 * LANES, C)), full((1, C)), full((1, C)), full((1, C))],
        out_specs=[tm_spec] * 9 + [pl.BlockSpec((1, tq // WKV_CHUNK, C), lambda b, j: (b, j, 0))],
        scratch_shapes=[pltpu.VMEM((1, RWKV_PAD_WIDTH), F32)],
        compiler_params=pltpu.CompilerParams(
            dimension_semantics=("arbitrary", "arbitrary"), vmem_limit_bytes=VMEM_LIMIT),
        name="rwkv_prep",
    )(p3, mu, w0, wup, a0, aup, gup, k_k, k_a, r_k)


def _dot_tn(a, b):
    return lax.dot_general(a.astype(BF16), b.astype(BF16), (((0,), (0,)), ((), ())),
                           preferred_element_type=F32)


WKV_GROUP = 4
WKV_GROUP_WIDTH = WKV_GROUP * RWKV_HEAD_DIM


def _wkv_kernel(at_ref, rt_ref, bt_ref, kt_ref, bw_ref, kw_ref, v_ref, bonus_ref, g_ref, wend_ref,
                lnw_ref, lnb_ref, o_ref, state, m_s, u0_s, o0_s, arb_s, p_s, c_s, sh_s):
    N, C, GW = RWKV_HEAD_DIM, WKV_CHUNK, WKV_GROUP_WIDTH
    n_groups = RWKV_WIDTH // GW
    n_sub = at_ref.shape[1] // C
    j = pl.program_id(1)

    @pl.when(j == 0)
    def _():
        state[...] = jnp.zeros_like(state)

    def iota(shape, dim):
        return lax.broadcasted_iota(jnp.int32, shape, dim)

    same_head = jnp.right_shift(iota((1, GW, GW), 1), 6) == jnp.right_shift(iota((1, GW, GW), 2), 6)
    ones_bd = jnp.where(same_head, 1.0, 0.0).astype(BF16)
    g_row = iota((1, 2 * C, 2 * GW), 1)
    g_col = jnp.bitwise_and(iota((1, 2 * C, 2 * GW), 2), N - 1)
    band = ((g_row < C) & (g_row > g_col)) | ((g_row >= C) & ((g_row - C) >= g_col))
    eye = jnp.where(iota((1, C, GW), 1) == jnp.bitwise_and(iota((1, C, GW), 2), N - 1), 1.0, 0.0)

    def bmm(a, b, dims):
        return lax.dot_general(a.astype(BF16), b.astype(BF16), (dims, ((0,), (0,))),
                               preferred_element_type=F32)

    nn, nt, tn = ((2,), (1,)), ((2,), (2,)), ((1,), (1,))

    def bd(x):
        xb = x.astype(BF16)
        return jnp.where(same_head, jnp.concatenate([xb] * WKV_GROUP, axis=1), jnp.zeros((), BF16))

    span = 4
    rows = span * C

    def stack(x):
        x3 = x.reshape(span, C, n_groups * GW)
        return jnp.concatenate([x3[:, :, q * GW:(q + 1) * GW] for q in range(n_groups)], axis=0)

    def unstack(y):
        return jnp.concatenate([y[q * span:(q + 1) * span].reshape(rows, GW) for q in range(n_groups)],
                               axis=1)

    for i in range(n_sub // span):
        rs = slice(i * rows, (i + 1) * rows)
        at = stack(at_ref[0, rs, :])
        ar = jnp.concatenate([at, stack(rt_ref[0, rs, :])], axis=1)
        bk_bd = jnp.concatenate([bd(stack(bt_ref[0, rs, :])), bd(stack(kt_ref[0, rs, :]))], axis=1)
        gm = jnp.where(band, bmm(ar, bk_bd, nt), 0.0)
        a_ab = gm[:, :C, :GW]
        inv = eye + a_ab
        pw = bmm(a_ab, bd(a_ab), nn)
        for _ in range(int(math.log2(C)) - 2):
            prod = bmm(jnp.concatenate([inv, pw], axis=1), bd(pw), nn)
            inv = inv + prod[:, :C]
            pw = prod[:, C:]
        inv = inv + bmm(inv, bd(pw), nn)
        v = stack(v_ref[0, rs, :])
        kv = bmm(gm[:, :, GW:], bd(v), nn)
        mu0 = bmm(inv, jnp.concatenate([bd(at), bd(kv[:, :C])], axis=2), nn)
        m = mu0[:, :, :GW].astype(BF16)
        u0 = mu0[:, :, GW:]
        bw = stack(bw_ref[0, rs, :])
        low_rank = jnp.where(same_head, bmm(m, bw, tn), 0.0)
        const = jnp.where(same_head, bmm(jnp.concatenate([u0.astype(BF16), v], axis=1),
                                         jnp.concatenate([bw, stack(kw_ref[0, rs, :])], axis=1), tn), 0.0)
        m_s[rs, :] = unstack(m)
        u0_s[rs, :] = unstack(u0)
        o0_s[rs, :] = unstack(kv[:, C:])
        arb_s[rs, :] = unstack(gm[:, C:, :GW]).astype(BF16)
        for q in range(n_groups):
            p_s[i * span:(i + 1) * span, q] = low_rank[q * span:(q + 1) * span].astype(BF16)
            c_s[i * span:(i + 1) * span, q] = const[q * span:(q + 1) * span]

    def groups(x):
        return jnp.stack([x[:, q * GW:(q + 1) * GW] for q in range(n_groups)], axis=0)

    def chunk_state(i, c):
        s0 = state[...]
        sh_s[i] = s0.astype(BF16)
        w_end = groups(wend_ref[0, pl.ds(j * n_sub + i, 1), :])
        state[...] = s0 * w_end + bmm(s0, p_s[i], nn) + c_s[i]
        return c

    lax.fori_loop(0, n_sub, chunk_state, 0, unroll=True)

    n_prob = n_groups * span

    ones_p = jnp.broadcast_to(ones_bd, (n_prob, GW, GW))

    def head_sums(x):
        hi, lo2 = _split2(x)
        s = bmm(jnp.concatenate([hi, lo2], axis=1), ones_p, nn)
        return s[:, :C] + s[:, C:]

    for i in range(n_sub // span):
        rs = slice(i * rows, (i + 1) * rows)
        s0 = jnp.concatenate([sh_s[i * span:(i + 1) * span, q] for q in range(n_groups)], axis=0)
        mr_s = bmm(jnp.concatenate([stack(m_s[rs, :]), stack(rt_ref[0, rs, :])], axis=1), s0, nt)
        u = mr_s[:, :C] + stack(u0_s[rs, :])
        o = mr_s[:, C:] + stack(o0_s[rs, :]) + bmm(stack(arb_s[rs, :]), bd(u), nn)
        dlt = o - head_sums(o) * (1.0 / N)
        on = unstack(dlt * lax.rsqrt(head_sums(dlt * dlt) * (1.0 / N) + GN_EPS))
        on = on * lnw_ref[...] + lnb_ref[...]
        o_ref[0, rs, :] = (on + bonus_ref[0, rs, :]) * g_ref[0, rs, :]


def _wkv(at, rt, bt, kt, bw, kw, v, bonus, g, wend, ln_w, ln_b):
    B, S, C = at.shape
    tc = 512
    tm_spec = pl.BlockSpec((1, tc, C), lambda bb, j: (bb, j, 0))
    par_spec = pl.BlockSpec((1, C), lambda bb, j: (0, 0))
    blocks = (C // WKV_GROUP_WIDTH, WKV_GROUP_WIDTH, WKV_GROUP_WIDTH)
    return pl.pallas_call(
        _wkv_kernel,
        out_shape=jax.ShapeDtypeStruct((B, S, C), F32),
        grid=(B, S // tc),
        in_specs=([tm_spec] * 9 + [pl.BlockSpec((1, S // WKV_CHUNK, C), lambda bb, j: (bb, 0, 0))]
                  + [par_spec] * 2),
        out_specs=tm_spec,
        scratch_shapes=[pltpu.VMEM(blocks, F32),
                        pltpu.VMEM((tc, C), BF16), pltpu.VMEM((tc, C), F32), pltpu.VMEM((tc, C), F32),
                        pltpu.VMEM((tc, C), BF16),
                        pltpu.VMEM((tc // WKV_CHUNK,) + blocks, BF16), pltpu.VMEM((tc // WKV_CHUNK,) + blocks, F32),
                        pltpu.VMEM((tc // WKV_CHUNK,) + blocks, BF16)],
        compiler_params=pltpu.CompilerParams(
            dimension_semantics=("arbitrary", "arbitrary"), vmem_limit_bytes=VMEM_LIMIT),
        name="wkv7",
    )(at, rt, bt, kt, bw, kw, v, bonus, g, wend, ln_w, ln_b)


MERGE_SPLIT = 2
INFO_FIELDS = 8


def _merge_kernel(att_ref, o_ref, gate_ref, x_ref, wa_ref, wr_ref, wo_ref, g2_ref, wrt_ref, brt_ref,
                  x1_ref, h2_ref, lg_ref, cnt_ref, lt_ref):
    tm = x_ref.shape[1]
    ts = tm // MERGE_SPLIT
    parts = [slice(h * ts, (h + 1) * ts) for h in range(MERGE_SPLIT)]
    lane = lax.broadcasted_iota(jnp.int32, (1, ROUTER_PAD), 1)

    @pl.when((pl.program_id(0) == 0) & (pl.program_id(1) == 0))
    def _():
        cnt_ref[...] = jnp.zeros_like(cnt_ref)

    ys = [(_dot(att_ref[0, rs, :], wa_ref[...]), _dot(o_ref[0, rs, :], wr_ref[...])) for rs in parts]

    mixed = []
    for rs, (y_att, y_rwkv) in zip(parts, ys):
        gates = _sigmoid(gate_ref[0, rs, :].astype(F32))
        mixed.append(gates[:, :D_MODEL] * y_att + gates[:, D_MODEL:] * y_rwkv)

    x1s = []
    for rs, mx in zip(parts, mixed):
        x1 = x_ref[0, rs, :] + _dot(mx, wo_ref[...])
        x1_ref[0, rs, :] = x1
        x1s.append(x1)

    splits = []
    for rs, x1 in zip(parts, x1s):
        ms = jnp.mean(x1 * x1, axis=-1, keepdims=True)
        h2 = x1 * lax.rsqrt(ms + NORM_EPS) * g2_ref[...]
        hh, hl = _split2(h2)
        half = D_MODEL // 2
        rounded = hh.astype(F32)
        w_hi = pltpu.bitcast(rounded[:, :half], jnp.uint32)
        w_lo = pltpu.bitcast(rounded[:, half:], jnp.uint32)
        h2_ref[0, rs, :] = w_hi | jnp.right_shift(w_lo, jnp.uint32(16))
        splits.append((hh, hl))

    wh, wl = wrt_ref[0], wrt_ref[1]
    logits = [jnp.dot(hh, wh, preferred_element_type=F32)
              + (jnp.dot(hh, wl, preferred_element_type=F32) + jnp.dot(hl, wh, preferred_element_type=F32))
              + brt_ref[...] for hh, hl in splits]

    ti = lax.broadcasted_iota(jnp.int32, (ts, ts), 0)
    tj = lax.broadcasted_iota(jnp.int32, (ts, ts), 1)
    before = jnp.where(tj < ti, 1.0, 0.0).astype(BF16)
    for rs, lg in zip(parts, logits):
        gmask = lane < N_EXPERT_GROUPS
        gl = jnp.where(gmask, lg, -jnp.inf)
        gmax = jnp.max(gl, axis=-1, keepdims=True)
        grp_idx = jnp.min(jnp.where(gl == gmax, lane, ROUTER_PAD), axis=-1, keepdims=True)
        grp_p = 1.0 / jnp.sum(jnp.where(gmask, jnp.exp(lg - gmax), 0.0), axis=-1, keepdims=True)
        e_lane = lane - N_EXPERT_GROUPS
        emask = ((e_lane >= 0) & (e_lane < N_EXPERTS)
                 & (jnp.right_shift(e_lane, int(math.log2(EXPERTS_PER_GROUP))) == grp_idx))
        el = jnp.where(emask, lg, -jnp.inf)
        m1 = jnp.max(el, axis=-1, keepdims=True)
        i1 = jnp.min(jnp.where(el == m1, lane, ROUTER_PAD), axis=-1, keepdims=True)
        el2 = jnp.where(lane == i1, -jnp.inf, el)
        m2 = jnp.max(el2, axis=-1, keepdims=True)
        i2 = jnp.min(jnp.where(el2 == m2, lane, ROUTER_PAD), axis=-1, keepdims=True)
        e2 = jnp.exp(m2 - m1)
        gate1 = grp_p / (1.0 + e2)
        gate2 = gate1 * e2

        oh = jnp.where((lane == i1) | (lane == i2), 1.0, 0.0)
        pref = jnp.dot(before, oh.astype(BF16), preferred_element_type=F32) + cnt_ref[...]
        rank1 = jnp.sum(jnp.where(lane == i1, pref, 0.0), axis=-1, keepdims=True)
        rank2 = jnp.sum(jnp.where(lane == i2, pref, 0.0), axis=-1, keepdims=True)
        cnt_ref[...] = cnt_ref[...] + jnp.sum(oh, axis=0, keepdims=True)
        info = jnp.where(lane == 0, (i1 - N_EXPERT_GROUPS).astype(F32), 0.0)
        info = jnp.where(lane == 1, (i2 - N_EXPERT_GROUPS).astype(F32), info)
        info = jnp.where(lane == 2, rank1, info)
        info = jnp.where(lane == 3, rank2, info)
        info = jnp.where(lane == 4, gate1, info)
        info = jnp.where(lane == 5, gate2, info)
        lg_ref[0, rs, :] = info
        lt_ref[:, rs] = info.T[:INFO_FIELDS, :]


def _merge(att, o_g, p3, x, w_att_out, w_rwkv_out, w_out, gain2, w_router, b_router):
    B, S, D = x.shape
    tm = 512
    full = lambda shape: pl.BlockSpec(shape, lambda b, j: (0,) * len(shape))
    return pl.pallas_call(
        _merge_kernel,
        out_shape=[jax.ShapeDtypeStruct((B, S, D), F32), jax.ShapeDtypeStruct((B, S, D // 2), jnp.uint32),
                   jax.ShapeDtypeStruct((B, S, ROUTER_PAD), F32),
                   jax.ShapeDtypeStruct((1, ROUTER_PAD), F32),
                   jax.ShapeDtypeStruct((INFO_FIELDS, B * S), F32)],
        grid=(B, S // tm),
        in_specs=[pl.BlockSpec((1, tm, ATT_OUT_WIDTH), lambda b, j: (b, j, 0)),
                  pl.BlockSpec((1, tm, RWKV_WIDTH), lambda b, j: (b, j, 0)),
                  pl.BlockSpec((1, tm, 2 * D_MODEL), lambda b, j: (b, j, 1)),
                  pl.BlockSpec((1, tm, D), lambda b, j: (b, j, 0)),
                  full((ATT_OUT_WIDTH, D)), full((RWKV_WIDTH, D)), full((D, D)),
                  full((1, D)), full((2, D, ROUTER_PAD)), full((1, ROUTER_PAD))],
        out_specs=[pl.BlockSpec((1, tm, D), lambda b, j: (b, j, 0)),
                   pl.BlockSpec((1, tm, D // 2), lambda b, j: (b, j, 0)),
                   pl.BlockSpec((1, tm, ROUTER_PAD), lambda b, j: (b, j, 0)),
                   pl.BlockSpec((1, ROUTER_PAD), lambda b, j: (0, 0)),
                   pl.BlockSpec((INFO_FIELDS, tm), lambda b, j: (0, b * (S // tm) + j))],
        compiler_params=pltpu.CompilerParams(
            dimension_semantics=("arbitrary", "arbitrary"), vmem_limit_bytes=VMEM_LIMIT),
        name="merge_outproj_router",
    )(att, o_g, p3, x, w_att_out, w_rwkv_out, w_out, gain2, w_router, b_router)


def _expert_kernel(be_ref, nb_ref, x_ref, wg_ref, wu_ref, wd_ref, y_ref, wgu_s, wd_s):
    i = pl.program_id(0)

    @pl.when((i == 0) | (be_ref[i] != be_ref[jnp.maximum(i - 1, 0)]))
    def _():
        wgu_s[:, :D_EXPERT] = wg_ref[0].astype(BF16)
        wgu_s[:, D_EXPERT:] = wu_ref[0].astype(BF16)
        wd_s[...] = wd_ref[0].astype(BF16)

    @pl.when(i < nb_ref[0])
    def _():
        words = x_ref[...]
        x_a = pltpu.bitcast(words & jnp.uint32(0xFFFF0000), F32)
        x_b = pltpu.bitcast(jnp.left_shift(words, jnp.uint32(16)), F32)
        xb = jnp.concatenate([x_a, x_b], axis=-1).astype(BF16)
        h = jnp.dot(xb, wgu_s[...], preferred_element_type=F32)
        hg, hu = h[:, :D_EXPERT], h[:, D_EXPERT:]
        hid = hg * _sigmoid(hg) * hu
        y_ref[...] = jnp.dot(hid.astype(BF16), wd_s[...], preferred_element_type=F32).astype(y_ref.dtype)

    @pl.when(i >= nb_ref[0])
    def _():
        y_ref[...] = jnp.zeros_like(y_ref)


def _experts(blk_expert, n_used, xin, w_gate, w_up, w_down):
    n_rows = xin.shape[0]
    D = D_MODEL
    n_blocks = n_rows // MOE_ROWS
    return pl.pallas_call(
        _expert_kernel,
        out_shape=jax.ShapeDtypeStruct((n_rows, D), BF16),
        grid_spec=pltpu.PrefetchScalarGridSpec(
            num_scalar_prefetch=2,
            grid=(n_blocks,),
            scratch_shapes=[pltpu.VMEM((D, 2 * D_EXPERT), BF16), pltpu.VMEM((D_EXPERT, D), BF16)],
            in_specs=[pl.BlockSpec((MOE_ROWS, D // 2), lambda i, be, nb: (i, 0)),
                      pl.BlockSpec((1, D, D_EXPERT), lambda i, be, nb: (be[i], 0, 0)),
                      pl.BlockSpec((1, D, D_EXPERT), lambda i, be, nb: (be[i], 0, 0)),
                      pl.BlockSpec((1, D_EXPERT, D), lambda i, be, nb: (be[i], 0, 0))],
            out_specs=pl.BlockSpec((MOE_ROWS, D), lambda i, be, nb: (i, 0))),
        compiler_params=pltpu.CompilerParams(
            dimension_semantics=("arbitrary",), vmem_limit_bytes=VMEM_LIMIT),
        name="moe_experts",
    )(blk_expert, n_used, xin, w_gate, w_up, w_down)


def _row_tokens(dests, counts, pcounts, pstarts, pends, n_rows, T):
    dest_flat = jnp.concatenate(dests)
    A = dest_flat.shape[0]
    n_pad = n_rows - A
    seg_count = jnp.concatenate([pcounts - counts, n_rows - pends[-1:]])
    seg_first = jnp.concatenate([pstarts + counts, pends[-1:]])
    seg_end = jnp.cumsum(seg_count)
    i = jnp.arange(n_pad, dtype=jnp.int32)
    seg = jnp.sum(i[:, None] >= seg_end[None, :], axis=1)
    pad_rows = (seg_first[seg] + i - (seg_end - seg_count)[seg]).astype(jnp.int32)
    rows = jnp.concatenate([dest_flat, pad_rows])
    toks = jnp.concatenate([jnp.arange(T, dtype=jnp.int32)] * len(dests) + [i % T])
    tok_bits = max(T - 1, 1).bit_length()
    assert n_rows << tok_bits <= 1 << 32
    packed = jnp.sort((rows.astype(jnp.uint32) << tok_bits) | toks.astype(jnp.uint32))
    return (packed & jnp.uint32((1 << tok_bits) - 1)).astype(jnp.int32)


def _row_layout(info, counts, T):
    A = T * TOP_K
    counts = counts.astype(jnp.int32)
    pcounts = (counts + MOE_ROWS - 1) // MOE_ROWS * MOE_ROWS
    pends = jnp.cumsum(pcounts)
    pstarts = pends - pcounts
    dest = [(pstarts[info[k].astype(jnp.int32)] + info[TOP_K + k].astype(jnp.int32)).astype(jnp.int32)
            for k in range(TOP_K)]
    n_blocks = -(-A // MOE_ROWS) + N_EXPERTS
    n_rows = n_blocks * MOE_ROWS
    row_tok = _row_tokens(dest, counts, pcounts, pstarts, pends, n_rows, T)
    blk_start = jnp.arange(n_blocks, dtype=jnp.int32) * MOE_ROWS
    blk_expert = jnp.minimum(jnp.sum(blk_start[:, None] >= pends[None, :], axis=1), N_EXPERTS - 1)
    n_used = (pends[N_EXPERTS - 1] // MOE_ROWS).astype(jnp.int32)[None]
    return row_tok, blk_expert.astype(jnp.int32), n_used, dest


def _combine_kernel(x1_ref, y0_ref, y1_ref, info_ref, o_ref):
    gate = 2 * TOP_K
    g0 = info_ref[:, gate:gate + 1]
    g1 = info_ref[:, gate + 1:gate + 2]
    o_ref[...] = x1_ref[...] + (g0 * y0_ref[...].astype(F32) + g1 * y1_ref[...].astype(F32))


def _combine(x1, y0, y1, info):
    T, D = x1.shape
    tm = 1024
    row_spec = pl.BlockSpec((tm, D), lambda i: (i, 0))
    return pl.pallas_call(
        _combine_kernel,
        out_shape=jax.ShapeDtypeStruct((T, D), F32),
        grid=(T // tm,),
        in_specs=[row_spec, row_spec, row_spec, pl.BlockSpec((tm, ROUTER_PAD), lambda i: (i, 0))],
        out_specs=row_spec,
        compiler_params=pltpu.CompilerParams(
            dimension_semantics=("arbitrary",), vmem_limit_bytes=VMEM_LIMIT),
        name="moe_combine",
    )(x1, y0, y1, info)


def _pad_rows(w, rows):
    return jnp.pad(w, ((0, rows - w.shape[0]), (0, 0)))


def kernel(x, norm1_gain, w_in, q_norm_gain, k_norm_gain, rel_bias_table, w_att_out, rwkv_shift_mu,
           rwkv_w0, rwkv_w_up, rwkv_a0, rwkv_a_up, rwkv_g_up, rwkv_k_k, rwkv_k_a, rwkv_r_k,
           rwkv_ln_w, rwkv_ln_b, w_rwkv_out, w_out, norm2_gain, w_group_router, b_group_router,
           w_expert_router, b_expert_router, w_expert_gate, w_expert_up, w_expert_down):
    B, S, D = x.shape
    T = B * S
    C = RWKV_WIDTH
    bias = _attention_bias(rel_bias_table)
    for l in range(norm1_gain.shape[0]):
        wi = w_in[l]
        w_qkv = wi[:, :3 * ATT_WIDTH]
        w_rw = wi[:, 3 * ATT_WIDTH:3 * ATT_WIDTH + RWKV_SHIFT_WIDTH]
        w_gt = wi[:, 3 * ATT_WIDTH + RWKV_SHIFT_WIDTH:]
        pad_cols = lambda w, n: jnp.pad(w, ((0, 0), (0, n - w.shape[1])))
        o_w, o_a, o_g = 3 * C, 3 * C + DECAY_LORA, 3 * C + DECAY_LORA + ICLR_LORA

        def rw_layout(w):
            return jnp.concatenate([w[:, :o_w], pad_cols(w[:, o_w:o_a], LANES),
                                    pad_cols(w[:, o_a:o_g], LANES), pad_cols(w[:, o_g:], 2 * LANES)], axis=1)

        w_rg = jnp.concatenate([rw_layout(w_rw), w_gt], axis=1).astype(BF16)
        mu = rw_layout(rwkv_shift_mu[l][None, :])

        p_rg, p_qkv = _inproj(x.reshape(T, D), norm1_gain[l][None, :], w_rg, w_qkv.astype(BF16))
        p3 = p_rg.reshape(B, S, RWKV_PAD_WIDTH + 2 * D)

        tile2 = lambda gmat: jnp.tile(gmat, (1, 2))[:, None, :]
        att = _attention(p_qkv.reshape(B, S, 3 * ATT_WIDTH), tile2(q_norm_gain[l]), tile2(k_norm_gain[l]),
                         bias)

        row = lambda v: v[None, :]
        prep = _rwkv_prep(
            p3, mu, row(rwkv_w0[l]), _pad_rows(rwkv_w_up[l], LANES).astype(BF16), row(rwkv_a0[l]),
            _pad_rows(rwkv_a_up[l], LANES).astype(BF16), _pad_rows(rwkv_g_up[l], 2 * LANES).astype(BF16),
            row(rwkv_k_k[l]), row(rwkv_k_a[l]), rwkv_r_k[l].reshape(1, C))
        o_g = _wkv(*prep, row(rwkv_ln_w[l]), row(rwkv_ln_b[l]))

        w_router = jnp.concatenate([w_group_router[l], w_expert_router[l]], axis=1)
        w_router = jnp.pad(w_router, ((0, 0), (0, ROUTER_PAD - w_router.shape[1])))
        wr_hi = w_router.astype(BF16)
        wr_lo = (w_router - wr_hi.astype(F32)).astype(BF16)
        b_router = jnp.concatenate([b_group_router[l], b_expert_router[l]])
        b_router = jnp.pad(b_router, (0, ROUTER_PAD - b_router.shape[0]))[None, :]
        x1, h2, info, counts, info_t = _merge(
            att, o_g, p3, x, w_att_out[l].astype(BF16),
            w_rwkv_out[l].astype(BF16), w_out[l].astype(BF16),
            norm2_gain[l][None, :], jnp.stack([wr_hi, wr_lo]), b_router)

        row_tok, blk_expert, n_used, dest = _row_layout(
            info_t, counts[0, N_EXPERT_GROUPS:N_EXPERT_GROUPS + N_EXPERTS], T)
        xin = h2.reshape(T, D // 2)[row_tok]
        yb = _experts(blk_expert, n_used, xin, w_expert_gate[l], w_expert_up[l], w_expert_down[l])
        x = _combine(x1.reshape(T, D), yb[dest[0]], yb[dest[1]],
                     info.reshape(T, ROUTER_PAD)).reshape(B, S, D)
    return x
```

```python
import functools
import math

import jax
import jax.numpy as jnp
from jax import lax
from jax.experimental import pallas as pl
from jax.experimental.pallas import tpu as pltpu

F32 = jnp.float32
BF16 = jnp.bfloat16

D_MODEL = 1024
ATT_PATTERNS = ((128, 1), (512, 4), (2048, 16))
N_ATT_GROUPS = 3
HEADS_PER_GROUP = 4
ATT_HEADS = 12
HEAD_DIM = 64
ATT_WIDTH = 768
ATT_OUT_WIDTH = 256
REL_BUCKETS = 32
REL_MAX_DIST = 2048
RWKV_HEADS = 8
RWKV_HEAD_DIM = 64
RWKV_WIDTH = 512
DECAY_LORA = 64
ICLR_LORA = 64
GATE_LORA = 160
RWKV_SHIFT_WIDTH = 1824
GN_EPS = 64e-5
N_EXPERT_GROUPS = 4
EXPERTS_PER_GROUP = 8
N_EXPERTS = 32
TOP_K = 2
D_EXPERT = 512
NORM_EPS = 1e-6
NEG_INF = -1e30

LANES = 128
ATT_BLOCK = 128
ATT_BATCH = 4
WKV_CHUNK = 64
RWKV_PAD_WIDTH = 2048
QKV_COL_BLOCK = 0
ROUTER_PAD = 128
MOE_ROWS = 512
VMEM_LIMIT = 56 * 1024 * 1024


def _sigmoid(x):
    return 1.0 / (1.0 + jnp.exp(-x))


def _dot(a, b):
    return jnp.dot(a.astype(BF16), b.astype(BF16), preferred_element_type=F32)


def _dot_t(a, b):
    return lax.dot_general(a.astype(BF16), b.astype(BF16), (((1,), (1,)), ((), ())),
                           preferred_element_type=F32)


def _split3(a):
    hi = a.astype(BF16)
    r1 = a - hi.astype(F32)
    mid = r1.astype(BF16)
    lo = (r1 - mid.astype(F32)).astype(BF16)
    return hi, mid, lo


def _split2(a):
    hi = a.astype(BF16)
    lo = (a - hi.astype(F32)).astype(BF16)
    return hi, lo


INPROJ_COLS = 512


def _inproj_kernel(x_ref, g_ref, wrg_ref, wqkv_ref, rg_ref, qkv_ref):
    x = x_ref[...]
    ms = jnp.mean(x * x, axis=-1, keepdims=True)
    h = (x * lax.rsqrt(ms + NORM_EPS) * g_ref[...]).astype(BF16)
    for w_ref, o_ref in ((wrg_ref, rg_ref), (wqkv_ref, qkv_ref)):
        width = w_ref.shape[1]
        for c in range(0, width, INPROJ_COLS):
            cs = slice(c, min(c + INPROJ_COLS, width))
            o_ref[:, cs] = jnp.dot(h, w_ref[:, cs], preferred_element_type=F32).astype(o_ref.dtype)


def _inproj(x2, gain, w_rg, w_qkv):
    T = x2.shape[0]
    tm = 512
    n_rg, n_qkv = w_rg.shape[1], w_qkv.shape[1]
    return pl.pallas_call(
        _inproj_kernel,
        out_shape=[jax.ShapeDtypeStruct((T, n_rg), BF16), jax.ShapeDtypeStruct((T, n_qkv), F32)],
        grid=(T // tm,),
        in_specs=[pl.BlockSpec((tm, D_MODEL), lambda i: (i, 0)),
                  pl.BlockSpec((1, D_MODEL), lambda i: (0, 0)),
                  pl.BlockSpec((D_MODEL, n_rg), lambda i: (0, 0)),
                  pl.BlockSpec((D_MODEL, n_qkv), lambda i: (0, 0))],
        out_specs=[pl.BlockSpec((tm, n_rg), lambda i: (i, 0)),
                   pl.BlockSpec((tm, n_qkv), lambda i: (i, 0))],
        compiler_params=pltpu.CompilerParams(
            dimension_semantics=("arbitrary",), vmem_limit_bytes=VMEM_LIMIT),
        name="inproj",
    )(x2, gain, w_rg, w_qkv)


def _attn_kernel(q0, q1, q2, k0, k1, k2, v0, v1, v2, qg_ref, kg_ref, bias_ref, out_ref,
                 qn_s, kn_s, acc_s, m_s, l_s):
    S = out_ref.shape[1]
    q_refs, k_refs, v_refs = (q0, q1, q2), (k0, k1, k2), (v0, v1, v2)
    lane = lax.broadcasted_iota(jnp.int32, (1, LANES), 1)
    lo = lane < HEAD_DIM
    col = lax.broadcasted_iota(jnp.int32, (1, 2 * ATT_BLOCK), 1)

    same_head = (lax.broadcasted_iota(jnp.int32, (LANES, LANES), 0) < HEAD_DIM) == (
        lax.broadcasted_iota(jnp.int32, (LANES, LANES), 1) < HEAD_DIM)
    ones_bd = jnp.where(same_head, 1.0, 0.0).astype(BF16)

    def pair_norm(x, gain):
        ss = jnp.dot((x * x).astype(BF16), ones_bd, preferred_element_type=F32)
        return x * lax.rsqrt(ss * (1.0 / HEAD_DIM) + NORM_EPS) * gain

    rows_per = 256

    def norm_body(i, c):
        rs = pl.ds(pl.multiple_of(i * rows_per, rows_per), rows_per)
        for g in range(N_ATT_GROUPS):
            qn_s[g, rs, :] = pair_norm(q_refs[g][0, rs, :], qg_ref[g]) * (HEAD_DIM ** -0.5)
            kn_s[g, rs, :] = pair_norm(k_refs[g][0, rs, :], kg_ref[g])
        return c

    lax.fori_loop(0, S // rows_per, norm_body, 0)

    for g, (window, d) in enumerate(ATT_PATTERNS):
        n_blk = (S // d) // ATT_BLOCK
        has_prev = n_blk > 1

        def rows_at(start, d=d):
            return pl.ds(start, ATT_BLOCK) if d == 1 else pl.ds(start, ATT_BLOCK, stride=d)

        def blk_stages(it, g=g, d=d, has_prev=has_prev, rows_at=rows_at):
            curs, q2s, kcats, vcats, pens = [], [], [], [], []
            for b in range(ATT_BATCH):
                idx = it * ATT_BATCH + b
                r = idx % d
                n = idx // d
                cur = rows_at(n * (ATT_BLOCK * d) + r)
                q = qn_s[g, cur, :]
                q2s.append(jnp.concatenate([jnp.where(lo, q, 0.0), jnp.where(lo, 0.0, q)], axis=0).astype(BF16))
                kc = kn_s[g, cur, :].astype(BF16)
                vc = v_refs[g][0, cur, :].astype(BF16)
                if has_prev:
                    prv = rows_at(max(n - 1, 0) * (ATT_BLOCK * d) + r)
                    kc = jnp.concatenate([kn_s[g, prv, :].astype(BF16), kc], axis=0)
                    vc = jnp.concatenate([v_refs[g][0, prv, :].astype(BF16), vc], axis=0)
                    pens.append(jnp.where(col < ATT_BLOCK, NEG_INF if n == 0 else 0.0, 0.0)[None])
                curs.append(cur)
                kcats.append(kc)
                vcats.append(jnp.concatenate([vc, jnp.ones_like(vc)], axis=1))
            q2, kcat, vcat = jnp.stack(q2s), jnp.stack(kcats), jnp.stack(vcats)
            s = lax.dot_general(q2, kcat, (((2,), (2,)), ((0,), (0,))), preferred_element_type=F32)
            yield
            if has_prev:
                s = s + bias_ref[g, 0] + jnp.concatenate(pens, axis=0)
            else:
                s = s + bias_ref[g, 0, :, ATT_BLOCK:]
            m = jnp.max(s, axis=-1, keepdims=True)
            p = jnp.exp(s - m).astype(BF16)
            ol = lax.dot_general(p, vcat, (((2,), (1,)), ((0,), (0,))), preferred_element_type=F32)
            for b in range(ATT_BATCH):
                cur = curs[b]
                acc_s[g, cur, :] = jnp.where(lo, ol[b, :ATT_BLOCK, :LANES], ol[b, ATT_BLOCK:, :LANES])
                l_s[g, cur, :] = jnp.where(lo, ol[b, :ATT_BLOCK, LANES:], ol[b, ATT_BLOCK:, LANES:])
                m_s[g, cur, :] = jnp.where(lo, m[b, :ATT_BLOCK], m[b, ATT_BLOCK:])

        steps = [blk_stages(it) for it in range(d * n_blk // ATT_BATCH)]
        next(steps[0])
        for prev_step, step in zip(steps, steps[1:]):
            next(step)
            for _ in prev_step:
                pass
        for _ in steps[-1]:
            pass

    def comb_body(i, c):
        rs = pl.ds(pl.multiple_of(i * rows_per, rows_per), rows_per)
        m = jnp.maximum(jnp.maximum(m_s[0, rs, :], m_s[1, rs, :]), m_s[2, rs, :])
        num = jnp.zeros((rows_per, LANES), F32)
        den = jnp.zeros((rows_per, LANES), F32)
        for g in range(N_ATT_GROUPS):
            e = jnp.exp(m_s[g, rs, :] - m)
            num = num + e * acc_s[g, rs, :]
            den = den + e * l_s[g, rs, :]
        out_ref[0, rs, :] = num / den
        return c

    lax.fori_loop(0, S // rows_per, comb_body, 0)


def _attention(p3, q_gain, k_gain, bias):
    B, S, _ = p3.shape
    n_pairs = HEADS_PER_GROUP // 2

    def col_spec(base):
        return [pl.BlockSpec((1, S, LANES),
                             functools.partial(lambda b, sp, off: (b, 0, off + sp), off=base + g * n_pairs))
                for g in range(N_ATT_GROUPS)]

    n_qkv_blocks = ATT_WIDTH // LANES
    in_specs = (col_spec(QKV_COL_BLOCK) + col_spec(QKV_COL_BLOCK + n_qkv_blocks)
                + col_spec(QKV_COL_BLOCK + 2 * n_qkv_blocks)
                + [pl.BlockSpec((N_ATT_GROUPS, 1, LANES), lambda b, sp: (0, 0, 0)),
                   pl.BlockSpec((N_ATT_GROUPS, 1, LANES), lambda b, sp: (0, 0, 0)),
                   pl.BlockSpec((N_ATT_GROUPS, 1, 2 * ATT_BLOCK, 2 * ATT_BLOCK),
                                lambda b, sp: (0, sp, 0, 0))])
    return pl.pallas_call(
        _attn_kernel,
        out_shape=jax.ShapeDtypeStruct((B, S, ATT_OUT_WIDTH), F32),
        grid=(B, n_pairs),
        in_specs=in_specs,
        out_specs=pl.BlockSpec((1, S, LANES), lambda b, sp: (b, 0, sp)),
        scratch_shapes=[pltpu.VMEM((N_ATT_GROUPS, S, LANES), F32) for _ in range(5)],
        compiler_params=pltpu.CompilerParams(
            dimension_semantics=("arbitrary", "arbitrary"), vmem_limit_bytes=VMEM_LIMIT),
        name="dilated_attention",
    )(*([p3] * 9), q_gain, k_gain, bias)


def _t5_causal_bucket(dist):
    max_exact = REL_BUCKETS // 2
    d = jnp.maximum(dist.astype(F32), 1.0)
    large = max_exact + (jnp.log(d / max_exact) / math.log(REL_MAX_DIST / max_exact)
                         * (REL_BUCKETS - max_exact)).astype(jnp.int32)
    large = jnp.minimum(large, REL_BUCKETS - 1)
    return jnp.where(dist < max_exact, dist, large)


def _attention_bias(rel_bias_table):
    W = ATT_BLOCK
    qi = jnp.arange(W)[:, None]
    kj = jnp.arange(2 * W)[None, :]
    rel = qi + W - kj
    valid = (rel >= 0) & (rel <= W)
    per_group = []
    for g, (_, d) in enumerate(ATT_PATTERNS):
        tab = rel_bias_table[:, g * HEADS_PER_GROUP:(g + 1) * HEADS_PER_GROUP]
        bucket = _t5_causal_bucket(jnp.clip(rel, 0, W) * d)
        onehot = (bucket[:, :, None] == jnp.arange(REL_BUCKETS)[None, None, :]).astype(F32)
        bias = jnp.einsum('qkb,bh->qkh', onehot, tab.astype(F32),
                          precision=lax.Precision.HIGHEST)
        bias = jnp.where(valid[:, :, None], bias, NEG_INF)
        per_group.append(jnp.transpose(bias, (2, 0, 1)).reshape(2, 2 * W, 2 * W))
    return jnp.stack(per_group, axis=0)


def _rwkv_prep_kernel(p_ref, mu_ref, w0_ref, wup_ref, a0_ref, aup_ref, gup_ref, kk_ref, ka_ref, rk_ref,
                      at_o, rt_o, bt_o, kt_o, bw_o, kw_o, v_o, bonus_o, g_o, wend_o, carry):
    tq = p_ref.shape[1]
    C = RWKV_WIDTH
    L = WKV_CHUNK

    @pl.when(pl.program_id(1) == 0)
    def _():
        carry[...] = jnp.zeros_like(carry)

    p = p_ref[0].astype(F32)
    row = lax.broadcasted_iota(jnp.int32, (tq, 1), 0)
    prev = jnp.where(row == 0, carry[...], pltpu.roll(p, 1, axis=0))
    carry[...] = p[tq - 1:tq, :]
    pm = p + (prev - p) * mu_ref[...]
    r, k, v = pm[:, 0:C], pm[:, C:2 * C], pm[:, 2 * C:3 * C]
    xw = pm[:, 3 * C:3 * C + LANES]
    xa = pm[:, 3 * C + LANES:3 * C + 2 * LANES]
    xg = pm[:, 3 * C + 2 * LANES:]

    z = w0_ref[...] + _dot(jnp.tanh(xw), wup_ref[...])
    softplus_neg = jnp.maximum(-z, 0.0) + jnp.log(1.0 + jnp.exp(-jnp.abs(z)))
    lw = -jnp.exp(-softplus_neg - 0.5)
    a = _sigmoid(a0_ref[...] + _dot(xa, aup_ref[...]))
    g = _dot(_sigmoid(xg), gup_ref[...])

    hr = jnp.right_shift(lax.broadcasted_iota(jnp.int32, (C, C), 0), 6)
    hc = jnp.right_shift(lax.broadcasted_iota(jnp.int32, (C, C), 1), 6)
    ones_bd = jnp.where(hr == hc, 1.0, 0.0).astype(BF16)

    def head_sums(x):
        return jnp.dot(x.astype(BF16), ones_bd, preferred_element_type=F32)

    k2 = k * (1.0 + (a - 1.0) * ka_ref[...])
    kk = k * kk_ref[...]
    kk = kk / jnp.maximum(jnp.sqrt(head_sums(kk * kk)), 1e-12)
    bonus = head_sums(r * k2 * rk_ref[...]) * v
    b = kk * a

    half = 256
    ri = lax.broadcasted_iota(jnp.int32, (half, half), 0)
    ci = lax.broadcasted_iota(jnp.int32, (half, half), 1)
    tri = jnp.where((jnp.right_shift(ri, 6) == jnp.right_shift(ci, 6)) & (ri >= ci), 1.0, 0.0).astype(BF16)
    cums = []
    for s in range(tq // half):
        hi, mid, lo3 = _split3(lw[s * half:(s + 1) * half, :])
        cums.append(jnp.dot(tri, hi, preferred_element_type=F32)
                    + (jnp.dot(tri, mid, preferred_element_type=F32)
                       + jnp.dot(tri, lo3, preferred_element_type=F32)))
    cum = jnp.concatenate(cums, axis=0)
    n_chunks = tq // L
    ends = [cum[c * L + L - 1:c * L + L, :] for c in range(n_chunks)]
    tot = jnp.concatenate([jnp.broadcast_to(e, (L, C)) for e in ends], axis=0)
    wend = jnp.exp(jnp.concatenate(ends, axis=0))

    e_pos = jnp.exp(cum)
    e_neg = jnp.exp(-cum)
    e_rem = jnp.exp(tot - cum)
    at = -kk * jnp.exp(cum - lw)
    rt = r * e_pos
    bt = b * e_neg
    kt = k2 * e_neg
    bw = b * e_rem
    kw = k2 * e_rem
    at_o[0] = at.astype(BF16)
    rt_o[0] = rt.astype(BF16)
    bt_o[0] = bt.astype(BF16)
    kt_o[0] = kt.astype(BF16)
    bw_o[0] = bw.astype(BF16)
    kw_o[0] = kw.astype(BF16)
    v_o[0] = v.astype(BF16)
    bonus_o[0] = bonus
    g_o[0] = g
    wend_o[0] = wend


def _rwkv_prep(p3, mu, w0, wup, a0, aup, gup, k_k, k_a, r_k):
    B, S, _ = p3.shape
    tq = 512
    C = RWKV_WIDTH
    tm_shape = lambda dt: jax.ShapeDtypeStruct((B, S, C), dt)
    tm_spec = pl.BlockSpec((1, tq, C), lambda b, j: (b, j, 0))
    full = lambda shape: pl.BlockSpec(shape, lambda b, j: (0,) * len(shape))
    return pl.pallas_call(
        _rwkv_prep_kernel,
        out_shape=[tm_shape(BF16)] * 7 + [tm_shape(F32)] * 2 + [jax.ShapeDtypeStruct((B, S // WKV_CHUNK, C), F32)],
        grid=(B, S // tq),
        in_specs=[pl.BlockSpec((1, tq, RWKV_PAD_WIDTH), lambda b, j: (b, j, 0)),
                  full((1, RWKV_PAD_WIDTH)), full((1, C)), full((LANES, C)), full((1, C)),
                  full((LANES, C)), full((2 * LANES, C)), full((1, C)), full((1, C)), full((1, C))],
        out_specs=[tm_spec] * 9 + [pl.BlockSpec((1, tq // WKV_CHUNK, C), lambda b, j: (b, j, 0))],
        scratch_shapes=[pltpu.VMEM((1, RWKV_PAD_WIDTH), F32)],
        compiler_params=pltpu.CompilerParams(
            dimension_semantics=("arbitrary", "arbitrary"), vmem_limit_bytes=VMEM_LIMIT),
        name="rwkv_prep",
    )(p3, mu, w0, wup, a0, aup, gup, k_k, k_a, r_k)


def _dot_tn(a, b):
    return lax.dot_general(a.astype(BF16), b.astype(BF16), (((0,), (0,)), ((), ())),
                           preferred_element_type=F32)


WKV_GROUP = 4
WKV_GROUP_WIDTH = WKV_GROUP * RWKV_HEAD_DIM


def _wkv_kernel(at_ref, rt_ref, bt_ref, kt_ref, bw_ref, kw_ref, v_ref, bonus_ref, g_ref, wend_ref,
                lnw_ref, lnb_ref, o_ref, state, m_s, u0_s, o0_s, arb_s, p_s, c_s, sh_s):
    N, C, GW = RWKV_HEAD_DIM, WKV_CHUNK, WKV_GROUP_WIDTH
    n_groups = RWKV_WIDTH // GW
    n_sub = at_ref.shape[1] // C
    j = pl.program_id(1)

    @pl.when(j == 0)
    def _():
        state[...] = jnp.zeros_like(state)

    def iota(shape, dim):
        return lax.broadcasted_iota(jnp.int32, shape, dim)

    same_head = jnp.right_shift(iota((1, GW, GW), 1), 6) == jnp.right_shift(iota((1, GW, GW), 2), 6)
    ones_bd = jnp.where(same_head, 1.0, 0.0).astype(BF16)
    g_row = iota((1, 2 * C, 2 * GW), 1)
    g_col = jnp.bitwise_and(iota((1, 2 * C, 2 * GW), 2), N - 1)
    band = ((g_row < C) & (g_row > g_col)) | ((g_row >= C) & ((g_row - C) >= g_col))
    eye = jnp.where(iota((1, C, GW), 1) == jnp.bitwise_and(iota((1, C, GW), 2), N - 1), 1.0, 0.0)

    def bmm(a, b, dims):
        return lax.dot_general(a.astype(BF16), b.astype(BF16), (dims, ((0,), (0,))),
                               preferred_element_type=F32)

    nn, nt, tn = ((2,), (1,)), ((2,), (2,)), ((1,), (1,))

    def bd(x):
        xb = x.astype(BF16)
        return jnp.where(same_head, jnp.concatenate([xb] * WKV_GROUP, axis=1), jnp.zeros((), BF16))

    span = 4
    rows = span * C

    def stack(x):
        x3 = x.reshape(span, C, n_groups * GW)
        return jnp.concatenate([x3[:, :, q * GW:(q + 1) * GW] for q in range(n_groups)], axis=0)

    def unstack(y):
        return jnp.concatenate([y[q * span:(q + 1) * span].reshape(rows, GW) for q in range(n_groups)],
                               axis=1)

    def chunk_terms(i):
        rs = slice(i * rows, (i + 1) * rows)
        at = stack(at_ref[0, rs, :])
        ar = jnp.concatenate([at, stack(rt_ref[0, rs, :])], axis=1)
        bk_bd = jnp.concatenate([bd(stack(bt_ref[0, rs, :])), bd(stack(kt_ref[0, rs, :]))], axis=1)
        gm = jnp.where(band, bmm(ar, bk_bd, nt), 0.0)
        yield
        a_ab = gm[:, :C, :GW]
        inv = eye + a_ab
        pw = bmm(a_ab, bd(a_ab), nn)
        yield
        for _ in range(int(math.log2(C)) - 2):
            prod = bmm(jnp.concatenate([inv, pw], axis=1), bd(pw), nn)
            yield
            inv = inv + prod[:, :C]
            pw = prod[:, C:]
        inv = inv + bmm(inv, bd(pw), nn)
        yield
        v = stack(v_ref[0, rs, :])
        kv = bmm(gm[:, :, GW:], bd(v), nn)
        yield
        mu0 = bmm(inv, jnp.concatenate([bd(at), bd(kv[:, :C])], axis=2), nn)
        yield
        m = mu0[:, :, :GW].astype(BF16)
        u0 = mu0[:, :, GW:]
        bw = stack(bw_ref[0, rs, :])
        low_rank = jnp.where(same_head, bmm(m, bw, tn), 0.0)
        const = jnp.where(same_head, bmm(jnp.concatenate([u0.astype(BF16), v], axis=1),
                                         jnp.concatenate([bw, stack(kw_ref[0, rs, :])], axis=1), tn), 0.0)
        m_s[rs, :] = unstack(m)
        u0_s[rs, :] = unstack(u0)
        o0_s[rs, :] = unstack(kv[:, C:])
        arb_s[rs, :] = unstack(gm[:, C:, :GW]).astype(BF16)
        for q in range(n_groups):
            p_s[i * span:(i + 1) * span, q] = low_rank[q * span:(q + 1) * span].astype(BF16)
            c_s[i * span:(i + 1) * span, q] = const[q * span:(q + 1) * span]

    def groups(x):
        return jnp.stack([x[:, q * GW:(q + 1) * GW] for q in range(n_groups)], axis=0)

    def chunk_state(i):
        s0 = state[...]
        sh_s[i] = s0.astype(BF16)
        w_end = groups(wend_ref[0, pl.ds(j * n_sub + i, 1), :])
        state[...] = s0 * w_end + bmm(s0, p_s[i], nn) + c_s[i]

    n_prob = n_groups * span

    ones_p = jnp.broadcast_to(ones_bd, (n_prob, GW, GW))

    def head_sums(x):
        hi, lo2 = _split2(x)
        s = bmm(jnp.concatenate([hi, lo2], axis=1), ones_p, nn)
        return s[:, :C] + s[:, C:]

    def chunk_outputs(i):
        rs = slice(i * rows, (i + 1) * rows)
        s0 = jnp.concatenate([sh_s[i * span:(i + 1) * span, q] for q in range(n_groups)], axis=0)
        mr_s = bmm(jnp.concatenate([stack(m_s[rs, :]), stack(rt_ref[0, rs, :])], axis=1), s0, nt)
        yield
        u = mr_s[:, :C] + stack(u0_s[rs, :])
        o = mr_s[:, C:] + stack(o0_s[rs, :]) + bmm(stack(arb_s[rs, :]), bd(u), nn)
        yield
        mean = head_sums(o) * (1.0 / N)
        yield
        dlt = o - mean
        on = unstack(dlt * lax.rsqrt(head_sums(dlt * dlt) * (1.0 / N) + GN_EPS))
        on = on * lnw_ref[...] + lnb_ref[...]
        o_ref[0, rs, :] = (on + bonus_ref[0, rs, :]) * g_ref[0, rs, :]

    def run(stages, between=()):
        pending = iter(between)
        for _ in stages:
            step = next(pending, None)
            if step is not None:
                step()
        for step in pending:
            step()

    n_spans = n_sub // span
    assert n_spans >= 2
    states = lambda i: [functools.partial(chunk_state, k) for k in range(i * span, (i + 1) * span)]
    run(chunk_terms(0))
    for i in range(1, n_spans):
        run(chunk_terms(i), states(i - 1))
    run(chunk_outputs(0), states(n_spans - 1))
    for i in range(1, n_spans):
        run(chunk_outputs(i))


def _wkv(at, rt, bt, kt, bw, kw, v, bonus, g, wend, ln_w, ln_b):
    B, S, C = at.shape
    tc = 512
    tm_spec = pl.BlockSpec((1, tc, C), lambda bb, j: (bb, j, 0))
    par_spec = pl.BlockSpec((1, C), lambda bb, j: (0, 0))
    blocks = (C // WKV_GROUP_WIDTH, WKV_GROUP_WIDTH, WKV_GROUP_WIDTH)
    return pl.pallas_call(
        _wkv_kernel,
        out_shape=jax.ShapeDtypeStruct((B, S, C), F32),
        grid=(B, S // tc),
        in_specs=([tm_spec] * 9 + [pl.BlockSpec((1, S // WKV_CHUNK, C), lambda bb, j: (bb, 0, 0))]
                  + [par_spec] * 2),
        out_specs=tm_spec,
        scratch_shapes=[pltpu.VMEM(blocks, F32),
                        pltpu.VMEM((tc, C), BF16), pltpu.VMEM((tc, C), F32), pltpu.VMEM((tc, C), F32),
                        pltpu.VMEM((tc, C), BF16),
                        pltpu.VMEM((tc // WKV_CHUNK,) + blocks, BF16), pltpu.VMEM((tc // WKV_CHUNK,) + blocks, F32),
                        pltpu.VMEM((tc // WKV_CHUNK,) + blocks, BF16)],
        compiler_params=pltpu.CompilerParams(
            dimension_semantics=("arbitrary", "arbitrary"), vmem_limit_bytes=VMEM_LIMIT),
        name="wkv7",
    )(at, rt, bt, kt, bw, kw, v, bonus, g, wend, ln_w, ln_b)


MERGE_SPLIT = 2
INFO_FIELDS = 8


def _merge_kernel(att_ref, o_ref, gate_ref, x_ref, wa_ref, wr_ref, wo_ref, g2_ref, wrt_ref, brt_ref,
                  x1_ref, h2_ref, lg_ref, cnt_ref, lt_ref):
    tm = x_ref.shape[1]
    ts = tm // MERGE_SPLIT
    parts = [slice(h * ts, (h + 1) * ts) for h in range(MERGE_SPLIT)]
    lane = lax.broadcasted_iota(jnp.int32, (1, ROUTER_PAD), 1)

    @pl.when((pl.program_id(0) == 0) & (pl.program_id(1) == 0))
    def _():
        cnt_ref[...] = jnp.zeros_like(cnt_ref)

    ys = [(_dot(att_ref[0, rs, :], wa_ref[...]), _dot(o_ref[0, rs, :], wr_ref[...])) for rs in parts]

    mixed = []
    for rs, (y_att, y_rwkv) in zip(parts, ys):
        gates = _sigmoid(gate_ref[0, rs, :].astype(F32))
        mixed.append(gates[:, :D_MODEL] * y_att + gates[:, D_MODEL:] * y_rwkv)

    x1s = []
    for rs, mx in zip(parts, mixed):
        x1 = x_ref[0, rs, :] + _dot(mx, wo_ref[...])
        x1_ref[0, rs, :] = x1
        x1s.append(x1)

    splits = []
    for rs, x1 in zip(parts, x1s):
        ms = jnp.mean(x1 * x1, axis=-1, keepdims=True)
        h2 = x1 * lax.rsqrt(ms + NORM_EPS) * g2_ref[...]
        hh, hl = _split2(h2)
        half = D_MODEL // 2
        rounded = hh.astype(F32)
        w_hi = pltpu.bitcast(rounded[:, :half], jnp.uint32)
        w_lo = pltpu.bitcast(rounded[:, half:], jnp.uint32)
        h2_ref[0, rs, :] = w_hi | jnp.right_shift(w_lo, jnp.uint32(16))
        splits.append((hh, hl))

    wh, wl = wrt_ref[0], wrt_ref[1]
    logits = [jnp.dot(hh, wh, preferred_element_type=F32)
              + (jnp.dot(hh, wl, preferred_element_type=F32) + jnp.dot(hl, wh, preferred_element_type=F32))
              + brt_ref[...] for hh, hl in splits]

    ti = lax.broadcasted_iota(jnp.int32, (ts, ts), 0)
    tj = lax.broadcasted_iota(jnp.int32, (ts, ts), 1)
    before = jnp.where(tj < ti, 1.0, 0.0).astype(BF16)
    for rs, lg in zip(parts, logits):
        gmask = lane < N_EXPERT_GROUPS
        gl = jnp.where(gmask, lg, -jnp.inf)
        gmax = jnp.max(gl, axis=-1, keepdims=True)
        grp_idx = jnp.min(jnp.where(gl == gmax, lane, ROUTER_PAD), axis=-1, keepdims=True)
        grp_p = 1.0 / jnp.sum(jnp.where(gmask, jnp.exp(lg - gmax), 0.0), axis=-1, keepdims=True)
        e_lane = lane - N_EXPERT_GROUPS
        emask = ((e_lane >= 0) & (e_lane < N_EXPERTS)
                 & (jnp.right_shift(e_lane, int(math.log2(EXPERTS_PER_GROUP))) == grp_idx))
        el = jnp.where(emask, lg, -jnp.inf)
        m1 = jnp.max(el, axis=-1, keepdims=True)
        i1 = jnp.min(jnp.where(el == m1, lane, ROUTER_PAD), axis=-1, keepdims=True)
        el2 = jnp.where(lane == i1, -jnp.inf, el)
        m2 = jnp.max(el2, axis=-1, keepdims=True)
        i2 = jnp.min(jnp.where(el2 == m2, lane, ROUTER_PAD), axis=-1, keepdims=True)
        e2 = jnp.exp(m2 - m1)
        gate1 = grp_p / (1.0 + e2)
        gate2 = gate1 * e2

        oh = jnp.where((lane == i1) | (lane == i2), 1.0, 0.0)
        pref = jnp.dot(before, oh.astype(BF16), preferred_element_type=F32) + cnt_ref[...]
        rank1 = jnp.sum(jnp.where(lane == i1, pref, 0.0), axis=-1, keepdims=True)
        rank2 = jnp.sum(jnp.where(lane == i2, pref, 0.0), axis=-1, keepdims=True)
        cnt_ref[...] = cnt_ref[...] + jnp.sum(oh, axis=0, keepdims=True)
        info = jnp.where(lane == 0, (i1 - N_EXPERT_GROUPS).astype(F32), 0.0)
        info = jnp.where(lane == 1, (i2 - N_EXPERT_GROUPS).astype(F32), info)
        info = jnp.where(lane == 2, rank1, info)
        info = jnp.where(lane == 3, rank2, info)
        info = jnp.where(lane == 4, gate1, info)
        info = jnp.where(lane == 5, gate2, info)
        lg_ref[0, rs, :] = info
        lt_ref[:, rs] = info.T[:INFO_FIELDS, :]


def _merge(att, o_g, p3, x, w_att_out, w_rwkv_out, w_out, gain2, w_router, b_router):
    B, S, D = x.shape
    tm = 512
    full = lambda shape: pl.BlockSpec(shape, lambda b, j: (0,) * len(shape))
    return pl.pallas_call(
        _merge_kernel,
        out_shape=[jax.ShapeDtypeStruct((B, S, D), F32), jax.ShapeDtypeStruct((B, S, D // 2), jnp.uint32),
                   jax.ShapeDtypeStruct((B, S, ROUTER_PAD), F32),
                   jax.ShapeDtypeStruct((1, ROUTER_PAD), F32),
                   jax.ShapeDtypeStruct((INFO_FIELDS, B * S), F32)],
        grid=(B, S // tm),
        in_specs=[pl.BlockSpec((1, tm, ATT_OUT_WIDTH), lambda b, j: (b, j, 0)),
                  pl.BlockSpec((1, tm, RWKV_WIDTH), lambda b, j: (b, j, 0)),
                  pl.BlockSpec((1, tm, 2 * D_MODEL), lambda b, j: (b, j, 1)),
                  pl.BlockSpec((1, tm, D), lambda b, j: (b, j, 0)),
                  full((ATT_OUT_WIDTH, D)), full((RWKV_WIDTH, D)), full((D, D)),
                  full((1, D)), full((2, D, ROUTER_PAD)), full((1, ROUTER_PAD))],
        out_specs=[pl.BlockSpec((1, tm, D), lambda b, j: (b, j, 0)),
                   pl.BlockSpec((1, tm, D // 2), lambda b, j: (b, j, 0)),
                   pl.BlockSpec((1, tm, ROUTER_PAD), lambda b, j: (b, j, 0)),
                   pl.BlockSpec((1, ROUTER_PAD), lambda b, j: (0, 0)),
                   pl.BlockSpec((INFO_FIELDS, tm), lambda b, j: (0, b * (S // tm) + j))],
        compiler_params=pltpu.CompilerParams(
            dimension_semantics=("arbitrary", "arbitrary"), vmem_limit_bytes=VMEM_LIMIT),
        name="merge_outproj_router",
    )(att, o_g, p3, x, w_att_out, w_rwkv_out, w_out, gain2, w_router, b_router)


def _expert_kernel(be_ref, nb_ref, x_ref, wg_ref, wu_ref, wd_ref, y_ref, wgu_s, wd_s):
    i = pl.program_id(0)

    @pl.when((i == 0) | (be_ref[i] != be_ref[jnp.maximum(i - 1, 0)]))
    def _():
        wgu_s[:, :D_EXPERT] = wg_ref[0].astype(BF16)
        wgu_s[:, D_EXPERT:] = wu_ref[0].astype(BF16)
        wd_s[...] = wd_ref[0].astype(BF16)

    @pl.when(i < nb_ref[0])
    def _():
        words = x_ref[...]
        x_a = pltpu.bitcast(words & jnp.uint32(0xFFFF0000), F32)
        x_b = pltpu.bitcast(jnp.left_shift(words, jnp.uint32(16)), F32)
        xb = jnp.concatenate([x_a, x_b], axis=-1).astype(BF16)
        hw = D_EXPERT // 2
        hgu = [(jnp.dot(xb, wgu_s[:, c * hw:(c + 1) * hw], preferred_element_type=F32),
                jnp.dot(xb, wgu_s[:, D_EXPERT + c * hw:D_EXPERT + (c + 1) * hw], preferred_element_type=F32))
               for c in range(2)]
        y = None
        for c, (hg, hu) in enumerate(hgu):
            hid = (hg * _sigmoid(hg) * hu).astype(BF16)
            part = jnp.dot(hid, wd_s[c * hw:(c + 1) * hw, :], preferred_element_type=F32)
            y = part if y is None else y + part
        y_ref[...] = y.astype(y_ref.dtype)

    @pl.when(i >= nb_ref[0])
    def _():
        y_ref[...] = jnp.zeros_like(y_ref)


def _experts(blk_expert, n_used, xin, w_gate, w_up, w_down):
    n_rows = xin.shape[0]
    D = D_MODEL
    n_blocks = n_rows // MOE_ROWS
    return pl.pallas_call(
        _expert_kernel,
        out_shape=jax.ShapeDtypeStruct((n_rows, D), BF16),
        grid_spec=pltpu.PrefetchScalarGridSpec(
            num_scalar_prefetch=2,
            grid=(n_blocks,),
            scratch_shapes=[pltpu.VMEM((D, 2 * D_EXPERT), BF16), pltpu.VMEM((D_EXPERT, D), BF16)],
            in_specs=[pl.BlockSpec((MOE_ROWS, D // 2), lambda i, be, nb: (i, 0)),
                      pl.BlockSpec((1, D, D_EXPERT), lambda i, be, nb: (be[i], 0, 0)),
                      pl.BlockSpec((1, D, D_EXPERT), lambda i, be, nb: (be[i], 0, 0)),
                      pl.BlockSpec((1, D_EXPERT, D), lambda i, be, nb: (be[i], 0, 0))],
            out_specs=pl.BlockSpec((MOE_ROWS, D), lambda i, be, nb: (i, 0))),
        compiler_params=pltpu.CompilerParams(
            dimension_semantics=("arbitrary",), vmem_limit_bytes=VMEM_LIMIT),
        name="moe_experts",
    )(blk_expert, n_used, xin, w_gate, w_up, w_down)


def _row_tokens(dests, counts, pcounts, pstarts, pends, n_rows, T):
    dest_flat = jnp.concatenate(dests)
    A = dest_flat.shape[0]
    n_pad = n_rows - A
    seg_count = jnp.concatenate([pcounts - counts, n_rows - pends[-1:]])
    seg_first = jnp.concatenate([pstarts + counts, pends[-1:]])
    seg_end = jnp.cumsum(seg_count)
    i = jnp.arange(n_pad, dtype=jnp.int32)
    seg = jnp.sum(i[:, None] >= seg_end[None, :], axis=1)
    pad_rows = (seg_first[seg] + i - (seg_end - seg_count)[seg]).astype(jnp.int32)
    rows = jnp.concatenate([dest_flat, pad_rows])
    toks = jnp.concatenate([jnp.arange(T, dtype=jnp.int32)] * len(dests) + [i % T])
    tok_bits = max(T - 1, 1).bit_length()
    assert n_rows << tok_bits <= 1 << 32
    packed = jnp.sort((rows.astype(jnp.uint32) << tok_bits) | toks.astype(jnp.uint32))
    return (packed & jnp.uint32((1 << tok_bits) - 1)).astype(jnp.int32)


def _row_layout(info, counts, T):
    A = T * TOP_K
    counts = counts.astype(jnp.int32)
    pcounts = (counts + MOE_ROWS - 1) // MOE_ROWS * MOE_ROWS
    pends = jnp.cumsum(pcounts)
    pstarts = pends - pcounts
    dest = [(pstarts[info[k].astype(jnp.int32)] + info[TOP_K + k].astype(jnp.int32)).astype(jnp.int32)
            for k in range(TOP_K)]
    n_blocks = -(-A // MOE_ROWS) + N_EXPERTS
    n_rows = n_blocks * MOE_ROWS
    row_tok = _row_tokens(dest, counts, pcounts, pstarts, pends, n_rows, T)
    blk_start = jnp.arange(n_blocks, dtype=jnp.int32) * MOE_ROWS
    blk_expert = jnp.minimum(jnp.sum(blk_start[:, None] >= pends[None, :], axis=1), N_EXPERTS - 1)
    n_used = (pends[N_EXPERTS - 1] // MOE_ROWS).astype(jnp.int32)[None]
    return row_tok, blk_expert.astype(jnp.int32), n_used, dest


def _combine_kernel(x1_ref, y0_ref, y1_ref, info_ref, o_ref):
    gate = 2 * TOP_K
    g0 = info_ref[:, gate:gate + 1]
    g1 = info_ref[:, gate + 1:gate + 2]
    o_ref[...] = x1_ref[...] + (g0 * y0_ref[...].astype(F32) + g1 * y1_ref[...].astype(F32))


def _combine(x1, y0, y1, info):
    T, D = x1.shape
    tm = 1024
    row_spec = pl.BlockSpec((tm, D), lambda i: (i, 0))
    return pl.pallas_call(
        _combine_kernel,
        out_shape=jax.ShapeDtypeStruct((T, D), F32),
        grid=(T // tm,),
        in_specs=[row_spec, row_spec, row_spec, pl.BlockSpec((tm, ROUTER_PAD), lambda i: (i, 0))],
        out_specs=row_spec,
        compiler_params=pltpu.CompilerParams(
            dimension_semantics=("arbitrary",), vmem_limit_bytes=VMEM_LIMIT),
        name="moe_combine",
    )(x1, y0, y1, info)


def _pad_rows(w, rows):
    return jnp.pad(w, ((0, rows - w.shape[0]), (0, 0)))


def kernel(x, norm1_gain, w_in, q_norm_gain, k_norm_gain, rel_bias_table, w_att_out, rwkv_shift_mu,
           rwkv_w0, rwkv_w_up, rwkv_a0, rwkv_a_up, rwkv_g_up, rwkv_k_k, rwkv_k_a, rwkv_r_k,
           rwkv_ln_w, rwkv_ln_b, w_rwkv_out, w_out, norm2_gain, w_group_router, b_group_router,
           w_expert_router, b_expert_router, w_expert_gate, w_expert_up, w_expert_down):
    B, S, D = x.shape
    T = B * S
    C = RWKV_WIDTH
    bias = _attention_bias(rel_bias_table)
    for l in range(norm1_gain.shape[0]):
        wi = w_in[l]
        w_qkv = wi[:, :3 * ATT_WIDTH]
        w_rw = wi[:, 3 * ATT_WIDTH:3 * ATT_WIDTH + RWKV_SHIFT_WIDTH]
        w_gt = wi[:, 3 * ATT_WIDTH + RWKV_SHIFT_WIDTH:]
        pad_cols = lambda w, n: jnp.pad(w, ((0, 0), (0, n - w.shape[1])))
        o_w, o_a, o_g = 3 * C, 3 * C + DECAY_LORA, 3 * C + DECAY_LORA + ICLR_LORA

        def rw_layout(w):
            return jnp.concatenate([w[:, :o_w], pad_cols(w[:, o_w:o_a], LANES),
                                    pad_cols(w[:, o_a:o_g], LANES), pad_cols(w[:, o_g:], 2 * LANES)], axis=1)

        w_rg = jnp.concatenate([rw_layout(w_rw), w_gt], axis=1).astype(BF16)
        mu = rw_layout(rwkv_shift_mu[l][None, :])

        p_rg, p_qkv = _inproj(x.reshape(T, D), norm1_gain[l][None, :], w_rg, w_qkv.astype(BF16))
        p3 = p_rg.reshape(B, S, RWKV_PAD_WIDTH + 2 * D)

        tile2 = lambda gmat: jnp.tile(gmat, (1, 2))[:, None, :]
        att = _attention(p_qkv.reshape(B, S, 3 * ATT_WIDTH), tile2(q_norm_gain[l]), tile2(k_norm_gain[l]),
                         bias)

        row = lambda v: v[None, :]
        prep = _rwkv_prep(
            p3, mu, row(rwkv_w0[l]), _pad_rows(rwkv_w_up[l], LANES).astype(BF16), row(rwkv_a0[l]),
            _pad_rows(rwkv_a_up[l], LANES).astype(BF16), _pad_rows(rwkv_g_up[l], 2 * LANES).astype(BF16),
            row(rwkv_k_k[l]), row(rwkv_k_a[l]), rwkv_r_k[l].reshape(1, C))
        o_g = _wkv(*prep, row(rwkv_ln_w[l]), row(rwkv_ln_b[l]))

        w_router = jnp.concatenate([w_group_router[l], w_expert_router[l]], axis=1)
        w_router = jnp.pad(w_router, ((0, 0), (0, ROUTER_PAD - w_router.shape[1])))
        wr_hi = w_router.astype(BF16)
        wr_lo = (w_router - wr_hi.astype(F32)).astype(BF16)
        b_router = jnp.concatenate([b_group_router[l], b_expert_router[l]])
        b_router = jnp.pad(b_router, (0, ROUTER_PAD - b_router.shape[0]))[None, :]
        x1, h2, info, counts, info_t = _merge(
            att, o_g, p3, x, w_att_out[l].astype(BF16),
            w_rwkv_out[l].astype(BF16), w_out[l].astype(BF16),
            norm2_gain[l][None, :], jnp.stack([wr_hi, wr_lo]), b_router)

        row_tok, blk_expert, n_used, dest = _row_layout(
            info_t, counts[0, N_EXPERT_GROUPS:N_EXPERT_GROUPS + N_EXPERTS], T)
        xin = h2.reshape(T, D // 2)[row_tok]
        yb = _experts(blk_expert, n_used, xin, w_expert_gate[l], w_expert_up[l], w_expert_down[l])
        x = _combine(x1.reshape(T, D), yb[dest[0]], yb[dest[1]],
                     info.reshape(T, ROUTER_PAD)).reshape(B, S, D)
    return x
```

```python
import functools
import math

import jax
import jax.numpy as jnp
from jax import lax
from jax.experimental import pallas as pl
from jax.experimental.pallas import tpu as pltpu

F32 = jnp.float32
BF16 = jnp.bfloat16

D_MODEL = 1024
ATT_PATTERNS = ((128, 1), (512, 4), (2048, 16))
N_ATT_GROUPS = 3
HEADS_PER_GROUP = 4
ATT_HEADS = 12
HEAD_DIM = 64
ATT_WIDTH = 768
ATT_OUT_WIDTH = 256
REL_BUCKETS = 32
REL_MAX_DIST = 2048
RWKV_HEADS = 8
RWKV_HEAD_DIM = 64
RWKV_WIDTH = 512
DECAY_LORA = 64
ICLR_LORA = 64
GATE_LORA = 160
RWKV_SHIFT_WIDTH = 1824
GN_EPS = 64e-5
N_EXPERT_GROUPS = 4
EXPERTS_PER_GROUP = 8
N_EXPERTS = 32
TOP_K = 2
D_EXPERT = 512
NORM_EPS = 1e-6
NEG_INF = -1e30

LANES = 128
ATT_BLOCK = 128
ATT_BATCH = 4
WKV_CHUNK = 64
RWKV_PAD_WIDTH = 2048
QKV_COL_BLOCK = 0
ROUTER_PAD = 128
MOE_ROWS = 512
VMEM_LIMIT = 56 * 1024 * 1024


def _sigmoid(x):
    return 0.5 * jnp.tanh(0.5 * x) + 0.5


def _dot(a, b):
    return jnp.dot(a.astype(BF16), b.astype(BF16), preferred_element_type=F32)


def _dot_t(a, b):
    return lax.dot_general(a.astype(BF16), b.astype(BF16), (((1,), (1,)), ((), ())),
                           preferred_element_type=F32)


def _split3(a):
    hi = a.astype(BF16)
    r1 = a - hi.astype(F32)
    mid = r1.astype(BF16)
    lo = (r1 - mid.astype(F32)).astype(BF16)
    return hi, mid, lo


def _split2(a):
    hi = a.astype(BF16)
    lo = (a - hi.astype(F32)).astype(BF16)
    return hi, lo


INPROJ_COLS = 512


def _inproj_kernel(x_ref, g_ref, wrg_ref, wqkv_ref, rg_ref, qkv_ref):
    x = x_ref[...]
    ms = jnp.mean(x * x, axis=-1, keepdims=True)
    h = (x * lax.rsqrt(ms + NORM_EPS) * g_ref[...]).astype(BF16)
    for w_ref, o_ref in ((wrg_ref, rg_ref), (wqkv_ref, qkv_ref)):
        width = w_ref.shape[1]
        for c in range(0, width, INPROJ_COLS):
            cs = slice(c, min(c + INPROJ_COLS, width))
            o_ref[:, cs] = jnp.dot(h, w_ref[:, cs], preferred_element_type=F32).astype(o_ref.dtype)


def _inproj(x2, gain, w_rg, w_qkv):
    T = x2.shape[0]
    tm = 512
    n_rg, n_qkv = w_rg.shape[1], w_qkv.shape[1]
    return pl.pallas_call(
        _inproj_kernel,
        out_shape=[jax.ShapeDtypeStruct((T, n_rg), BF16), jax.ShapeDtypeStruct((T, n_qkv), F32)],
        grid=(T // tm,),
        in_specs=[pl.BlockSpec((tm, D_MODEL), lambda i: (i, 0)),
                  pl.BlockSpec((1, D_MODEL), lambda i: (0, 0)),
                  pl.BlockSpec((D_MODEL, n_rg), lambda i: (0, 0)),
                  pl.BlockSpec((D_MODEL, n_qkv), lambda i: (0, 0))],
        out_specs=[pl.BlockSpec((tm, n_rg), lambda i: (i, 0)),
                   pl.BlockSpec((tm, n_qkv), lambda i: (i, 0))],
        compiler_params=pltpu.CompilerParams(
            dimension_semantics=("arbitrary",), vmem_limit_bytes=VMEM_LIMIT),
        name="inproj",
    )(x2, gain, w_rg, w_qkv)


def _attn_kernel(q0, q1, q2, k0, k1, k2, v0, v1, v2, qg_ref, kg_ref, bias_ref, out_ref,
                 qn_s, kn_s, acc_s, m_s, l_s):
    S = out_ref.shape[1]
    q_refs, k_refs, v_refs = (q0, q1, q2), (k0, k1, k2), (v0, v1, v2)
    lane = lax.broadcasted_iota(jnp.int32, (1, LANES), 1)
    lo = lane < HEAD_DIM
    col = lax.broadcasted_iota(jnp.int32, (1, 2 * ATT_BLOCK), 1)

    same_head = (lax.broadcasted_iota(jnp.int32, (LANES, LANES), 0) < HEAD_DIM) == (
        lax.broadcasted_iota(jnp.int32, (LANES, LANES), 1) < HEAD_DIM)
    ones_bd = jnp.where(same_head, 1.0, 0.0).astype(BF16)

    def pair_norm(x, gain):
        ss = jnp.dot((x * x).astype(BF16), ones_bd, preferred_element_type=F32)
        return x * lax.rsqrt(ss * (1.0 / HEAD_DIM) + NORM_EPS) * gain

    rows_per = 256

    def norm_body(i, c):
        rs = pl.ds(pl.multiple_of(i * rows_per, rows_per), rows_per)
        for g in range(N_ATT_GROUPS):
            qn_s[g, rs, :] = pair_norm(q_refs[g][0, rs, :], qg_ref[g])
            kn_s[g, rs, :] = pair_norm(k_refs[g][0, rs, :], kg_ref[g])
        return c

    lax.fori_loop(0, S // rows_per, norm_body, 0)

    for g, (window, d) in enumerate(ATT_PATTERNS):
        n_blk = (S // d) // ATT_BLOCK
        has_prev = n_blk > 1

        def rows_at(start, d=d):
            return pl.ds(start, ATT_BLOCK) if d == 1 else pl.ds(start, ATT_BLOCK, stride=d)

        def blk_stages(it, g=g, d=d, has_prev=has_prev, rows_at=rows_at):
            curs, q2s, kcats, vcats, pens = [], [], [], [], []
            for b in range(ATT_BATCH):
                idx = it * ATT_BATCH + b
                r = idx % d
                n = idx // d
                cur = rows_at(n * (ATT_BLOCK * d) + r)
                q = qn_s[g, cur, :]
                q2s.append(jnp.concatenate([jnp.where(lo, q, 0.0), jnp.where(lo, 0.0, q)], axis=0).astype(BF16))
                kc = kn_s[g, cur, :].astype(BF16)
                vc = v_refs[g][0, cur, :].astype(BF16)
                if has_prev:
                    prv = rows_at(max(n - 1, 0) * (ATT_BLOCK * d) + r)
                    kc = jnp.concatenate([kn_s[g, prv, :].astype(BF16), kc], axis=0)
                    vc = jnp.concatenate([v_refs[g][0, prv, :].astype(BF16), vc], axis=0)
                    pens.append(jnp.where(col < ATT_BLOCK, NEG_INF if n == 0 else 0.0, 0.0)[None])
                curs.append(cur)
                kcats.append(kc)
                vcats.append(jnp.concatenate([vc, jnp.ones_like(vc)], axis=1))
            q2, kcat, vcat = jnp.stack(q2s), jnp.stack(kcats), jnp.stack(vcats)
            s = lax.dot_general(q2, kcat, (((2,), (2,)), ((0,), (0,))), preferred_element_type=F32)
            yield
            if has_prev:
                s = s + bias_ref[g, 0] + jnp.concatenate(pens, axis=0)
            else:
                s = s + bias_ref[g, 0, :, ATT_BLOCK:]
            m = jnp.max(s, axis=-1, keepdims=True)
            p = jnp.exp(s - m).astype(BF16)
            ol = lax.dot_general(p, vcat, (((2,), (1,)), ((0,), (0,))), preferred_element_type=F32)
            for b in range(ATT_BATCH):
                cur = curs[b]
                acc_s[g, cur, :] = jnp.where(lo, ol[b, :ATT_BLOCK, :LANES], ol[b, ATT_BLOCK:, :LANES])
                l_s[g, cur, :] = jnp.where(lo, ol[b, :ATT_BLOCK, LANES:], ol[b, ATT_BLOCK:, LANES:])
                m_s[g, cur, :] = jnp.where(lo, m[b, :ATT_BLOCK], m[b, ATT_BLOCK:])

        steps = [blk_stages(it) for it in range(d * n_blk // ATT_BATCH)]
        next(steps[0])
        for prev_step, step in zip(steps, steps[1:]):
            next(step)
            for _ in prev_step:
                pass
        for _ in steps[-1]:
            pass

    def comb_body(i, c):
        rs = pl.ds(pl.multiple_of(i * rows_per, rows_per), rows_per)
        m = jnp.maximum(jnp.maximum(m_s[0, rs, :], m_s[1, rs, :]), m_s[2, rs, :])
        num = jnp.zeros((rows_per, LANES), F32)
        den = jnp.zeros((rows_per, LANES), F32)
        for g in range(N_ATT_GROUPS):
            e = jnp.exp(m_s[g, rs, :] - m)
            num = num + e * acc_s[g, rs, :]
            den = den + e * l_s[g, rs, :]
        out_ref[0, rs, :] = num / den
        return c

    lax.fori_loop(0, S // rows_per, comb_body, 0)


def _attention(p3, q_gain, k_gain, bias):
    B, S, _ = p3.shape
    n_pairs = HEADS_PER_GROUP // 2

    def col_spec(base):
        return [pl.BlockSpec((1, S, LANES),
                             functools.partial(lambda b, sp, off: (b, 0, off + sp), off=base + g * n_pairs))
                for g in range(N_ATT_GROUPS)]

    n_qkv_blocks = ATT_WIDTH // LANES
    in_specs = (col_spec(QKV_COL_BLOCK) + col_spec(QKV_COL_BLOCK + n_qkv_blocks)
                + col_spec(QKV_COL_BLOCK + 2 * n_qkv_blocks)
                + [pl.BlockSpec((N_ATT_GROUPS, 1, LANES), lambda b, sp: (0, 0, 0)),
                   pl.BlockSpec((N_ATT_GROUPS, 1, LANES), lambda b, sp: (0, 0, 0)),
                   pl.BlockSpec((N_ATT_GROUPS, 1, 2 * ATT_BLOCK, 2 * ATT_BLOCK),
                                lambda b, sp: (0, sp, 0, 0))])
    return pl.pallas_call(
        _attn_kernel,
        out_shape=jax.ShapeDtypeStruct((B, S, ATT_OUT_WIDTH), F32),
        grid=(B, n_pairs),
        in_specs=in_specs,
        out_specs=pl.BlockSpec((1, S, LANES), lambda b, sp: (b, 0, sp)),
        scratch_shapes=[pltpu.VMEM((N_ATT_GROUPS, S, LANES), F32) for _ in range(5)],
        compiler_params=pltpu.CompilerParams(
            dimension_semantics=("arbitrary", "arbitrary"), vmem_limit_bytes=VMEM_LIMIT),
        name="dilated_attention",
    )(*([p3] * 9), q_gain, k_gain, bias)


def _t5_causal_bucket(dist):
    max_exact = REL_BUCKETS // 2
    d = jnp.maximum(dist.astype(F32), 1.0)
    large = max_exact + (jnp.log(d / max_exact) / math.log(REL_MAX_DIST / max_exact)
                         * (REL_BUCKETS - max_exact)).astype(jnp.int32)
    large = jnp.minimum(large, REL_BUCKETS - 1)
    return jnp.where(dist < max_exact, dist, large)


def _attention_bias(rel_bias_table):
    W = ATT_BLOCK
    qi = jnp.arange(W)[:, None]
    kj = jnp.arange(2 * W)[None, :]
    rel = qi + W - kj
    valid = (rel >= 0) & (rel <= W)
    per_group = []
    for g, (_, d) in enumerate(ATT_PATTERNS):
        tab = rel_bias_table[:, g * HEADS_PER_GROUP:(g + 1) * HEADS_PER_GROUP]
        bucket = _t5_causal_bucket(jnp.clip(rel, 0, W) * d)
        onehot = (bucket[:, :, None] == jnp.arange(REL_BUCKETS)[None, None, :]).astype(F32)
        bias = jnp.einsum('qkb,bh->qkh', onehot, tab.astype(F32),
                          precision=lax.Precision.HIGHEST)
        bias = jnp.where(valid[:, :, None], bias, NEG_INF)
        per_group.append(jnp.transpose(bias, (2, 0, 1)).reshape(2, 2 * W, 2 * W))
    return jnp.stack(per_group, axis=0)


def _rwkv_prep_kernel(p_ref, mu_ref, w0_ref, wup_ref, a0_ref, aup_ref, gup_ref, kk_ref, ka_ref, rk_ref,
                      at_o, rt_o, bt_o, kt_o, bw_o, kw_o, v_o, bonus_o, g_o, wend_o, carry):
    tq = p_ref.shape[1]
    C = RWKV_WIDTH
    L = WKV_CHUNK

    @pl.when(pl.program_id(1) == 0)
    def _():
        carry[...] = jnp.zeros_like(carry)

    p = p_ref[0].astype(F32)
    row = lax.broadcasted_iota(jnp.int32, (tq, 1), 0)
    prev = jnp.where(row == 0, carry[...], pltpu.roll(p, 1, axis=0))
    carry[...] = p[tq - 1:tq, :]
    pm = p + (prev - p) * mu_ref[...]
    r, k, v = pm[:, 0:C], pm[:, C:2 * C], pm[:, 2 * C:3 * C]
    xw = pm[:, 3 * C:3 * C + LANES]
    xa = pm[:, 3 * C + LANES:3 * C + 2 * LANES]
    xg = pm[:, 3 * C + 2 * LANES:]

    z = w0_ref[...] + _dot(jnp.tanh(xw), wup_ref[...])
    softplus_neg = jnp.maximum(-z, 0.0) + jnp.log(1.0 + jnp.exp(-jnp.abs(z)))
    lw = -jnp.exp(-softplus_neg - 0.5)
    a = _sigmoid(a0_ref[...] + _dot(xa, aup_ref[...]))
    g = _dot(_sigmoid(xg), gup_ref[...])

    hr = jnp.right_shift(lax.broadcasted_iota(jnp.int32, (C, C), 0), 6)
    hc = jnp.right_shift(lax.broadcasted_iota(jnp.int32, (C, C), 1), 6)
    ones_bd = jnp.where(hr == hc, 1.0, 0.0).astype(BF16)

    def head_sums(x):
        return jnp.dot(x.astype(BF16), ones_bd, preferred_element_type=F32)

    k2 = k * (1.0 + (a - 1.0) * ka_ref[...])
    kk = k * kk_ref[...]
    kk = kk / jnp.maximum(jnp.sqrt(head_sums(kk * kk)), 1e-12)
    bonus = head_sums(r * k2 * rk_ref[...]) * v
    b = kk * a

    half = 256
    ri = lax.broadcasted_iota(jnp.int32, (half, half), 0)
    ci = lax.broadcasted_iota(jnp.int32, (half, half), 1)
    tri = jnp.where((jnp.right_shift(ri, 6) == jnp.right_shift(ci, 6)) & (ri >= ci), 1.0, 0.0).astype(BF16)
    cums = []
    for s in range(tq // half):
        hi, mid, lo3 = _split3(lw[s * half:(s + 1) * half, :])
        cums.append(jnp.dot(tri, hi, preferred_element_type=F32)
                    + (jnp.dot(tri, mid, preferred_element_type=F32)
                       + jnp.dot(tri, lo3, preferred_element_type=F32)))
    cum = jnp.concatenate(cums, axis=0)
    n_chunks = tq // L
    ends = [cum[c * L + L - 1:c * L + L, :] for c in range(n_chunks)]
    tot = jnp.concatenate([jnp.broadcast_to(e, (L, C)) for e in ends], axis=0)
    wend = jnp.exp(jnp.concatenate(ends, axis=0))

    e_pos = jnp.exp(cum)
    e_neg = jnp.exp(-cum)
    e_rem = jnp.exp(tot - cum)
    at = -kk * jnp.exp(cum - lw)
    rt = r * e_pos
    bt = b * e_neg
    kt = k2 * e_neg
    bw = b * e_rem
    kw = k2 * e_rem
    at_o[0] = at.astype(BF16)
    rt_o[0] = rt.astype(BF16)
    bt_o[0] = bt.astype(BF16)
    kt_o[0] = kt.astype(BF16)
    bw_o[0] = bw.astype(BF16)
    kw_o[0] = kw.astype(BF16)
    v_o[0] = v.astype(BF16)
    bonus_o[0] = bonus
    g_o[0] = g
    wend_o[0] = wend


def _rwkv_prep(p3, mu, w0, wup, a0, aup, gup, k_k, k_a, r_k):
    B, S, _ = p3.shape
    tq = 512
    C = RWKV_WIDTH
    tm_shape = lambda dt: jax.ShapeDtypeStruct((B, S, C), dt)
    tm_spec = pl.BlockSpec((1, tq, C), lambda b, j: (b, j, 0))
    full = lambda shape: pl.BlockSpec(shape, lambda b, j: (0,) * len(shape))
    return pl.pallas_call(
        _rwkv_prep_kernel,
        out_shape=[tm_shape(BF16)] * 7 + [tm_shape(F32)] * 2 + [jax.ShapeDtypeStruct((B, S // WKV_CHUNK, C), F32)],
        grid=(B, S // tq),
        in_specs=[pl.BlockSpec((1, tq, RWKV_PAD_WIDTH), lambda b, j: (b, j, 0)),
                  full((1, RWKV_PAD_WIDTH)), full((1, C)), full((LANES, C)), full((1, C)),
                  full((LANES, C)), full((2 * LANES, C)), full((1, C)), full((1, C)), full((1, C))],
        out_specs=[tm_spec] * 9 + [pl.BlockSpec((1, tq // WKV_CHUNK, C), lambda b, j: (b, j, 0))],
        scratch_shapes=[pltpu.VMEM((1, RWKV_PAD_WIDTH), F32)],
        compiler_params=pltpu.CompilerParams(
            dimension_semantics=("arbitrary", "arbitrary"), vmem_limit_bytes=VMEM_LIMIT),
        name="rwkv_prep",
    )(p3, mu, w0, wup, a0, aup, gup, k_k, k_a, r_k)


def _dot_tn(a, b):
    return lax.dot_general(a.astype(BF16), b.astype(BF16), (((0,), (0,)), ((), ())),
                           preferred_element_type=F32)


WKV_GROUP = 4
WKV_GROUP_WIDTH = WKV_GROUP * RWKV_HEAD_DIM


def _wkv_kernel(at_ref, rt_ref, bt_ref, kt_ref, bw_ref, kw_ref, v_ref, bonus_ref, g_ref, wend_ref,
                lnw_ref, lnb_ref, o_ref, state, m_s, u0_s, o0_s, arb_s, p_s, c_s, sh_s):
    N, C, GW = RWKV_HEAD_DIM, WKV_CHUNK, WKV_GROUP_WIDTH
    n_groups = RWKV_WIDTH // GW
    n_sub = at_ref.shape[1] // C
    j = pl.program_id(1)

    @pl.when(j == 0)
    def _():
        state[...] = jnp.zeros_like(state)

    def iota(shape, dim):
        return lax.broadcasted_iota(jnp.int32, shape, dim)

    same_head = jnp.right_shift(iota((1, GW, GW), 1), 6) == jnp.right_shift(iota((1, GW, GW), 2), 6)
    ones_bd = jnp.where(same_head, 1.0, 0.0).astype(BF16)
    g_row = iota((1, 2 * C, 2 * GW), 1)
    g_col = jnp.bitwise_and(iota((1, 2 * C, 2 * GW), 2), N - 1)
    band = ((g_row < C) & (g_row > g_col)) | ((g_row >= C) & ((g_row - C) >= g_col))
    eye = jnp.where(iota((1, C, GW), 1) == jnp.bitwise_and(iota((1, C, GW), 2), N - 1), 1.0, 0.0)

    def bmm(a, b, dims):
        return lax.dot_general(a.astype(BF16), b.astype(BF16), (dims, ((0,), (0,))),
                               preferred_element_type=F32)

    nn, nt, tn = ((2,), (1,)), ((2,), (2,)), ((1,), (1,))

    def bd(x):
        xb = x.astype(BF16)
        return jnp.where(same_head, jnp.concatenate([xb] * WKV_GROUP, axis=1), jnp.zeros((), BF16))

    def narrow(x):
        return sum(x[:, h * N:(h + 1) * N] for h in range(WKV_GROUP))

    span = 4
    rows = span * C

    def stack(x):
        x3 = x.reshape(span, C, n_groups * GW)
        return jnp.concatenate([x3[:, :, q * GW:(q + 1) * GW] for q in range(n_groups)], axis=0)

    def unstack(y):
        return jnp.concatenate([y[q * span:(q + 1) * span].reshape(rows, GW) for q in range(n_groups)],
                               axis=1)

    def chunk_terms(i):
        rs = slice(i * rows, (i + 1) * rows)
        at = stack(at_ref[0, rs, :])
        ar = jnp.concatenate([at, stack(rt_ref[0, rs, :])], axis=1)
        bk_bd = jnp.concatenate([bd(stack(bt_ref[0, rs, :])), bd(stack(kt_ref[0, rs, :]))], axis=1)
        gm = jnp.where(band, bmm(ar, bk_bd, nt), 0.0)
        yield
        a_ab = gm[:, :C, :GW]
        inv = eye + a_ab
        pw = bmm(a_ab, bd(a_ab), nn)
        yield
        for _ in range(int(math.log2(C)) - 2):
            prod = bmm(jnp.concatenate([inv, pw], axis=1), bd(pw), nn)
            yield
            inv = inv + prod[:, :C]
            pw = prod[:, C:]
        inv = inv + bmm(inv, bd(pw), nn)
        yield
        v = stack(v_ref[0, rs, :])
        kv = bmm(gm[:, :, GW:], bd(v), nn)
        yield
        mu0 = bmm(inv, jnp.concatenate([bd(at), bd(kv[:, :C])], axis=2), nn)
        yield
        m = mu0[:, :, :GW].astype(BF16)
        u0 = mu0[:, :, GW:]
        bw = stack(bw_ref[0, rs, :])
        low_rank = jnp.where(same_head, bmm(m, bw, tn), 0.0)
        const = jnp.where(same_head, bmm(jnp.concatenate([u0.astype(BF16), v], axis=1),
                                         jnp.concatenate([bw, stack(kw_ref[0, rs, :])], axis=1), tn), 0.0)
        m_s[rs, :] = unstack(m)
        u0_s[rs, :] = unstack(u0)
        o0_s[rs, :] = unstack(kv[:, C:])
        arb_s[rs, :] = unstack(gm[:, C:, :GW]).astype(BF16)
        for q in range(n_groups):
            p_s[i * span:(i + 1) * span, q] = low_rank[q * span:(q + 1) * span].astype(BF16)
            c_s[i * span:(i + 1) * span, q] = narrow(const[q * span:(q + 1) * span])

    def groups(x):
        return jnp.stack([x[:, q * GW:(q + 1) * GW] for q in range(n_groups)], axis=0)

    def chunk_state(i):
        s0 = state[...]
        sh_s[i] = s0.astype(BF16)
        w_end = groups(wend_ref[0, pl.ds(j * n_sub + i, 1), :])
        state[...] = s0 * w_end + bmm(s0, p_s[i], nn) + c_s[i]

    n_prob = n_groups * span

    ones_p = jnp.broadcast_to(ones_bd, (n_prob, GW, GW))

    def head_sums(x):
        return bmm(x, ones_p, nn)

    def chunk_outputs(i):
        rs = slice(i * rows, (i + 1) * rows)
        s0 = bd(jnp.concatenate([sh_s[i * span:(i + 1) * span, q] for q in range(n_groups)], axis=0))
        mr_s = bmm(jnp.concatenate([stack(m_s[rs, :]), stack(rt_ref[0, rs, :])], axis=1), s0, nt)
        yield
        u = mr_s[:, :C] + stack(u0_s[rs, :])
        o = mr_s[:, C:] + stack(o0_s[rs, :]) + bmm(stack(arb_s[rs, :]), bd(u), nn)
        yield
        mean = head_sums(o) * (1.0 / N)
        yield
        dlt = o - mean
        on = unstack(dlt * lax.rsqrt(head_sums(dlt * dlt) * (1.0 / N) + GN_EPS))
        on = on * lnw_ref[...] + lnb_ref[...]
        o_ref[0, rs, :] = (on + bonus_ref[0, rs, :]) * g_ref[0, rs, :]

    def run(stages, between=()):
        pending = iter(between)
        for _ in stages:
            step = next(pending, None)
            if step is not None:
                step()
        for step in pending:
            step()

    n_spans = n_sub // span
    assert n_spans >= 2
    states = lambda i: [functools.partial(chunk_state, k) for k in range(i * span, (i + 1) * span)]
    run(chunk_terms(0))
    for i in range(1, n_spans):
        run(chunk_terms(i), states(i - 1))
    run(chunk_outputs(0), states(n_spans - 1))
    for i in range(1, n_spans):
        run(chunk_outputs(i))


def _wkv(at, rt, bt, kt, bw, kw, v, bonus, g, wend, ln_w, ln_b):
    B, S, C = at.shape
    tc = 512
    tm_spec = pl.BlockSpec((1, tc, C), lambda bb, j: (bb, j, 0))
    par_spec = pl.BlockSpec((1, C), lambda bb, j: (0, 0))
    blocks = (C // WKV_GROUP_WIDTH, WKV_GROUP_WIDTH, WKV_GROUP_WIDTH)
    wide = (C // WKV_GROUP_WIDTH, RWKV_HEAD_DIM, WKV_GROUP_WIDTH)
    return pl.pallas_call(
        _wkv_kernel,
        out_shape=jax.ShapeDtypeStruct((B, S, C), F32),
        grid=(B, S // tc),
        in_specs=([tm_spec] * 9 + [pl.BlockSpec((1, S // WKV_CHUNK, C), lambda bb, j: (bb, 0, 0))]
                  + [par_spec] * 2),
        out_specs=tm_spec,
        scratch_shapes=[pltpu.VMEM(wide, F32),
                        pltpu.VMEM((tc, C), BF16), pltpu.VMEM((tc, C), F32), pltpu.VMEM((tc, C), F32),
                        pltpu.VMEM((tc, C), BF16),
                        pltpu.VMEM((tc // WKV_CHUNK,) + blocks, BF16), pltpu.VMEM((tc // WKV_CHUNK,) + wide, F32),
                        pltpu.VMEM((tc // WKV_CHUNK,) + wide, BF16)],
        compiler_params=pltpu.CompilerParams(
            dimension_semantics=("arbitrary", "arbitrary"), vmem_limit_bytes=VMEM_LIMIT),
        name="wkv7",
    )(at, rt, bt, kt, bw, kw, v, bonus, g, wend, ln_w, ln_b)


MERGE_SPLIT = 2
INFO_FIELDS = 8


def _merge_kernel(att_ref, o_ref, gate_ref, x_ref, wa_ref, wr_ref, wo_ref, g2_ref, wrt_ref, brt_ref,
                  x1_ref, h2_ref, lg_ref, cnt_ref, lt_ref):
    tm = x_ref.shape[1]
    ts = tm // MERGE_SPLIT
    parts = [slice(h * ts, (h + 1) * ts) for h in range(MERGE_SPLIT)]
    lane = lax.broadcasted_iota(jnp.int32, (1, ROUTER_PAD), 1)

    @pl.when((pl.program_id(0) == 0) & (pl.program_id(1) == 0))
    def _():
        cnt_ref[...] = jnp.zeros_like(cnt_ref)

    ys = [(_dot(att_ref[0, rs, :], wa_ref[...]), _dot(o_ref[0, rs, :], wr_ref[...])) for rs in parts]

    mixed = []
    for rs, (y_att, y_rwkv) in zip(parts, ys):
        gates = _sigmoid(gate_ref[0, rs, :].astype(F32))
        mixed.append(gates[:, :D_MODEL] * y_att + gates[:, D_MODEL:] * y_rwkv)

    x1s = []
    for rs, mx in zip(parts, mixed):
        x1 = x_ref[0, rs, :] + _dot(mx, wo_ref[...])
        x1_ref[0, rs, :] = x1
        x1s.append(x1)

    splits = []
    for rs, x1 in zip(parts, x1s):
        ms = jnp.mean(x1 * x1, axis=-1, keepdims=True)
        h2 = x1 * lax.rsqrt(ms + NORM_EPS) * g2_ref[...]
        hh, hl = _split2(h2)
        half = D_MODEL // 2
        rounded = hh.astype(F32)
        w_hi = pltpu.bitcast(rounded[:, :half], jnp.uint32)
        w_lo = pltpu.bitcast(rounded[:, half:], jnp.uint32)
        h2_ref[0, rs, :] = w_hi | jnp.right_shift(w_lo, jnp.uint32(16))
        splits.append((hh, hl))

    wh, wl = wrt_ref[0], wrt_ref[1]
    logits = [jnp.dot(hh, wh, preferred_element_type=F32)
              + (jnp.dot(hh, wl, preferred_element_type=F32) + jnp.dot(hl, wh, preferred_element_type=F32))
              + brt_ref[...] for hh, hl in splits]

    ti = lax.broadcasted_iota(jnp.int32, (ts, ts), 0)
    tj = lax.broadcasted_iota(jnp.int32, (ts, ts), 1)
    before = jnp.where(tj < ti, 1.0, 0.0).astype(BF16)
    for rs, lg in zip(parts, logits):
        gmask = lane < N_EXPERT_GROUPS
        gl = jnp.where(gmask, lg, -jnp.inf)
        gmax = jnp.max(gl, axis=-1, keepdims=True)
        grp_idx = jnp.min(jnp.where(gl == gmax, lane, ROUTER_PAD), axis=-1, keepdims=True)
        grp_p = 1.0 / jnp.sum(jnp.where(gmask, jnp.exp(lg - gmax), 0.0), axis=-1, keepdims=True)
        e_lane = lane - N_EXPERT_GROUPS
        emask = ((e_lane >= 0) & (e_lane < N_EXPERTS)
                 & (jnp.right_shift(e_lane, int(math.log2(EXPERTS_PER_GROUP))) == grp_idx))
        el = jnp.where(emask, lg, -jnp.inf)
        m1 = jnp.max(el, axis=-1, keepdims=True)
        i1 = jnp.min(jnp.where(el == m1, lane, ROUTER_PAD), axis=-1, keepdims=True)
        el2 = jnp.where(lane == i1, -jnp.inf, el)
        m2 = jnp.max(el2, axis=-1, keepdims=True)
        i2 = jnp.min(jnp.where(el2 == m2, lane, ROUTER_PAD), axis=-1, keepdims=True)
        e2 = jnp.exp(m2 - m1)
        gate1 = grp_p / (1.0 + e2)
        gate2 = gate1 * e2

        oh = jnp.where((lane == i1) | (lane == i2), 1.0, 0.0)
        pref = jnp.dot(before, oh.astype(BF16), preferred_element_type=F32) + cnt_ref[...]
        rank1 = jnp.sum(jnp.where(lane == i1, pref, 0.0), axis=-1, keepdims=True)
        rank2 = jnp.sum(jnp.where(lane == i2, pref, 0.0), axis=-1, keepdims=True)
        cnt_ref[...] = cnt_ref[...] + jnp.sum(oh, axis=0, keepdims=True)
        info = jnp.where(lane == 0, (i1 - N_EXPERT_GROUPS).astype(F32), 0.0)
        info = jnp.where(lane == 1, (i2 - N_EXPERT_GROUPS).astype(F32), info)
        info = jnp.where(lane == 2, rank1, info)
        info = jnp.where(lane == 3, rank2, info)
        info = jnp.where(lane == 4, gate1, info)
        info = jnp.where(lane == 5, gate2, info)
        lg_ref[0, rs, :] = info
        lt_ref[:, rs] = info.T[:INFO_FIELDS, :]


def _merge(att, o_g, p3, x, w_att_out, w_rwkv_out, w_out, gain2, w_router, b_router):
    B, S, D = x.shape
    tm = 512
    full = lambda shape: pl.BlockSpec(shape, lambda b, j: (0,) * len(shape))
    return pl.pallas_call(
        _merge_kernel,
        out_shape=[jax.ShapeDtypeStruct((B, S, D), F32), jax.ShapeDtypeStruct((B, S, D // 2), jnp.uint32),
                   jax.ShapeDtypeStruct((B, S, ROUTER_PAD), F32),
                   jax.ShapeDtypeStruct((1, ROUTER_PAD), F32),
                   jax.ShapeDtypeStruct((INFO_FIELDS, B * S), F32)],
        grid=(B, S // tm),
        in_specs=[pl.BlockSpec((1, tm, ATT_OUT_WIDTH), lambda b, j: (b, j, 0)),
                  pl.BlockSpec((1, tm, RWKV_WIDTH), lambda b, j: (b, j, 0)),
                  pl.BlockSpec((1, tm, 2 * D_MODEL), lambda b, j: (b, j, 1)),
                  pl.BlockSpec((1, tm, D), lambda b, j: (b, j, 0)),
                  full((ATT_OUT_WIDTH, D)), full((RWKV_WIDTH, D)), full((D, D)),
                  full((1, D)), full((2, D, ROUTER_PAD)), full((1, ROUTER_PAD))],
        out_specs=[pl.BlockSpec((1, tm, D), lambda b, j: (b, j, 0)),
                   pl.BlockSpec((1, tm, D // 2), lambda b, j: (b, j, 0)),
                   pl.BlockSpec((1, tm, ROUTER_PAD), lambda b, j: (b, j, 0)),
                   pl.BlockSpec((1, ROUTER_PAD), lambda b, j: (0, 0)),
                   pl.BlockSpec((INFO_FIELDS, tm), lambda b, j: (0, b * (S // tm) + j))],
        compiler_params=pltpu.CompilerParams(
            dimension_semantics=("arbitrary", "arbitrary"), vmem_limit_bytes=VMEM_LIMIT),
        name="merge_outproj_router",
    )(att, o_g, p3, x, w_att_out, w_rwkv_out, w_out, gain2, w_router, b_router)


def _expert_kernel(be_ref, nb_ref, x_ref, wg_ref, wu_ref, wd_ref, y_ref, wgu_s, wd_s):
    i = pl.program_id(0)

    @pl.when((i == 0) | (be_ref[i] != be_ref[jnp.maximum(i - 1, 0)]))
    def _():
        wgu_s[:, :D_EXPERT] = wg_ref[0].astype(BF16)
        wgu_s[:, D_EXPERT:] = wu_ref[0].astype(BF16)
        wd_s[...] = wd_ref[0].astype(BF16)

    @pl.when(i < nb_ref[0])
    def _():
        words = x_ref[...]
        x_a = pltpu.bitcast(words & jnp.uint32(0xFFFF0000), F32)
        x_b = pltpu.bitcast(jnp.left_shift(words, jnp.uint32(16)), F32)
        xb = jnp.concatenate([x_a, x_b], axis=-1).astype(BF16)
        hw = D_EXPERT // 2
        hgu = [(jnp.dot(xb, wgu_s[:, c * hw:(c + 1) * hw], preferred_element_type=F32),
                jnp.dot(xb, wgu_s[:, D_EXPERT + c * hw:D_EXPERT + (c + 1) * hw], preferred_element_type=F32))
               for c in range(2)]
        y = None
        for c, (hg, hu) in enumerate(hgu):
            hid = (hg * _sigmoid(hg) * hu).astype(BF16)
            part = jnp.dot(hid, wd_s[c * hw:(c + 1) * hw, :], preferred_element_type=F32)
            y = part if y is None else y + part
        y_ref[...] = y.astype(y_ref.dtype)

    @pl.when(i >= nb_ref[0])
    def _():
        y_ref[...] = jnp.zeros_like(y_ref)


def _experts(blk_expert, n_used, xin, w_gate, w_up, w_down):
    n_rows = xin.shape[0]
    D = D_MODEL
    n_blocks = n_rows // MOE_ROWS
    return pl.pallas_call(
        _expert_kernel,
        out_shape=jax.ShapeDtypeStruct((n_rows, D), BF16),
        grid_spec=pltpu.PrefetchScalarGridSpec(
            num_scalar_prefetch=2,
            grid=(n_blocks,),
            scratch_shapes=[pltpu.VMEM((D, 2 * D_EXPERT), BF16), pltpu.VMEM((D_EXPERT, D), BF16)],
            in_specs=[pl.BlockSpec((MOE_ROWS, D // 2), lambda i, be, nb: (i, 0)),
                      pl.BlockSpec((1, D, D_EXPERT), lambda i, be, nb: (be[i], 0, 0)),
                      pl.BlockSpec((1, D, D_EXPERT), lambda i, be, nb: (be[i], 0, 0)),
                      pl.BlockSpec((1, D_EXPERT, D), lambda i, be, nb: (be[i], 0, 0))],
            out_specs=pl.BlockSpec((MOE_ROWS, D), lambda i, be, nb: (i, 0))),
        compiler_params=pltpu.CompilerParams(
            dimension_semantics=("arbitrary",), vmem_limit_bytes=VMEM_LIMIT),
        name="moe_experts",
    )(blk_expert, n_used, xin, w_gate, w_up, w_down)


def _row_tokens(dests, counts, pcounts, pstarts, pends, n_rows, T):
    dest_flat = jnp.concatenate(dests)
    A = dest_flat.shape[0]
    n_pad = n_rows - A
    seg_count = jnp.concatenate([pcounts - counts, n_rows - pends[-1:]])
    seg_first = jnp.concatenate([pstarts + counts, pends[-1:]])
    seg_end = jnp.cumsum(seg_count)
    i = jnp.arange(n_pad, dtype=jnp.int32)
    seg = jnp.sum(i[None, :] >= seg_end[:, None], axis=0)
    in_seg = seg[None, :] == jnp.arange(seg_end.shape[0], dtype=jnp.int32)[:, None]
    shift = (seg_first - (seg_end - seg_count)).astype(jnp.int32)
    pad_rows = i + jnp.sum(jnp.where(in_seg, shift[:, None], 0), axis=0)
    rows = jnp.concatenate([dest_flat, pad_rows])
    toks = jnp.concatenate([jnp.arange(T, dtype=jnp.int32)] * len(dests) + [i % T])
    tok_bits = max(T - 1, 1).bit_length()
    assert n_rows << tok_bits <= 1 << 32
    packed = jnp.sort((rows.astype(jnp.uint32) << tok_bits) | toks.astype(jnp.uint32))
    return (packed & jnp.uint32((1 << tok_bits) - 1)).astype(jnp.int32)


def _row_layout(info, counts, T):
    A = T * TOP_K
    counts = counts.astype(jnp.int32)
    pcounts = (counts + MOE_ROWS - 1) // MOE_ROWS * MOE_ROWS
    pends = jnp.cumsum(pcounts)
    pstarts = pends - pcounts
    dest = [(pstarts[info[k].astype(jnp.int32)] + info[TOP_K + k].astype(jnp.int32)).astype(jnp.int32)
            for k in range(TOP_K)]
    n_blocks = -(-A // MOE_ROWS) + N_EXPERTS
    n_rows = n_blocks * MOE_ROWS
    row_tok = _row_tokens(dest, counts, pcounts, pstarts, pends, n_rows, T)
    blk_start = jnp.arange(n_blocks, dtype=jnp.int32) * MOE_ROWS
    blk_expert = jnp.minimum(jnp.sum(blk_start[:, None] >= pends[None, :], axis=1), N_EXPERTS - 1)
    n_used = (pends[N_EXPERTS - 1] // MOE_ROWS).astype(jnp.int32)[None]
    return row_tok, blk_expert.astype(jnp.int32), n_used, dest


def _combine_kernel(x1_ref, y0_ref, y1_ref, info_ref, o_ref):
    gate = 2 * TOP_K
    g0 = info_ref[:, gate:gate + 1]
    g1 = info_ref[:, gate + 1:gate + 2]
    o_ref[...] = x1_ref[...] + (g0 * y0_ref[...].astype(F32) + g1 * y1_ref[...].astype(F32))


def _combine(x1, y0, y1, info):
    T, D = x1.shape
    tm = 1024
    row_spec = pl.BlockSpec((tm, D), lambda i: (i, 0))
    return pl.pallas_call(
        _combine_kernel,
        out_shape=jax.ShapeDtypeStruct((T, D), F32),
        grid=(T // tm,),
        in_specs=[row_spec, row_spec, row_spec, pl.BlockSpec((tm, ROUTER_PAD), lambda i: (i, 0))],
        out_specs=row_spec,
        compiler_params=pltpu.CompilerParams(
            dimension_semantics=("arbitrary",), vmem_limit_bytes=VMEM_LIMIT),
        name="moe_combine",
    )(x1, y0, y1, info)


def _pad_rows(w, rows):
    return jnp.pad(w, ((0, rows - w.shape[0]), (0, 0)))


def kernel(x, norm1_gain, w_in, q_norm_gain, k_norm_gain, rel_bias_table, w_att_out, rwkv_shift_mu,
           rwkv_w0, rwkv_w_up, rwkv_a0, rwkv_a_up, rwkv_g_up, rwkv_k_k, rwkv_k_a, rwkv_r_k,
           rwkv_ln_w, rwkv_ln_b, w_rwkv_out, w_out, norm2_gain, w_group_router, b_group_router,
           w_expert_router, b_expert_router, w_expert_gate, w_expert_up, w_expert_down):
    B, S, D = x.shape
    T = B * S
    C = RWKV_WIDTH
    bias = _attention_bias(rel_bias_table)
    for l in range(norm1_gain.shape[0]):
        wi = w_in[l]
        w_qkv = wi[:, :3 * ATT_WIDTH]
        w_rw = wi[:, 3 * ATT_WIDTH:3 * ATT_WIDTH + RWKV_SHIFT_WIDTH]
        w_gt = wi[:, 3 * ATT_WIDTH + RWKV_SHIFT_WIDTH:]
        pad_cols = lambda w, n: jnp.pad(w, ((0, 0), (0, n - w.shape[1])))
        o_w, o_a, o_g = 3 * C, 3 * C + DECAY_LORA, 3 * C + DECAY_LORA + ICLR_LORA

        def rw_layout(w):
            return jnp.concatenate([w[:, :o_w], pad_cols(w[:, o_w:o_a], LANES),
                                    pad_cols(w[:, o_a:o_g], LANES), pad_cols(w[:, o_g:], 2 * LANES)], axis=1)

        w_rg = jnp.concatenate([rw_layout(w_rw), w_gt], axis=1).astype(BF16)
        mu = rw_layout(rwkv_shift_mu[l][None, :])

        p_rg, p_qkv = _inproj(x.reshape(T, D), norm1_gain[l][None, :], w_rg, w_qkv.astype(BF16))
        p3 = p_rg.reshape(B, S, RWKV_PAD_WIDTH + 2 * D)

        tile2 = lambda gmat: jnp.tile(gmat, (1, 2))[:, None, :]
        att = _attention(p_qkv.reshape(B, S, 3 * ATT_WIDTH), tile2(q_norm_gain[l]) * (HEAD_DIM ** -0.5),
                         tile2(k_norm_gain[l]),
                         bias)

        row = lambda v: v[None, :]
        prep = _rwkv_prep(
            p3, mu, row(rwkv_w0[l]), _pad_rows(rwkv_w_up[l], LANES).astype(BF16), row(rwkv_a0[l]),
            _pad_rows(rwkv_a_up[l], LANES).astype(BF16), _pad_rows(rwkv_g_up[l], 2 * LANES).astype(BF16),
            row(rwkv_k_k[l]), row(rwkv_k_a[l]), rwkv_r_k[l].reshape(1, C))
        o_g = _wkv(*prep, row(rwkv_ln_w[l]), row(rwkv_ln_b[l]))

        w_router = jnp.concatenate([w_group_router[l], w_expert_router[l]], axis=1)
        w_router = jnp.pad(w_router, ((0, 0), (0, ROUTER_PAD - w_router.shape[1])))
        wr_hi = w_router.astype(BF16)
        wr_lo = (w_router - wr_hi.astype(F32)).astype(BF16)
        b_router = jnp.concatenate([b_group_router[l], b_expert_router[l]])
        b_router = jnp.pad(b_router, (0, ROUTER_PAD - b_router.shape[0]))[None, :]
        x1, h2, info, counts, info_t = _merge(
            att, o_g, p3, x, w_att_out[l].astype(BF16),
            w_rwkv_out[l].astype(BF16), w_out[l].astype(BF16),
            norm2_gain[l][None, :], jnp.stack([wr_hi, wr_lo]), b_router)

        row_tok, blk_expert, n_used, dest = _row_layout(
            info_t, counts[0, N_EXPERT_GROUPS:N_EXPERT_GROUPS + N_EXPERTS], T)
        xin = h2.reshape(T, D // 2)[row_tok]
        yb = _experts(blk_expert, n_used, xin, w_expert_gate[l], w_expert_up[l], w_expert_down[l])
        x = _combine(x1.reshape(T, D), yb[dest[0]], yb[dest[1]],
                     info.reshape(T, ROUTER_PAD)).reshape(B, S, D)
    return x
```

```python
import functools
import math

import jax
import jax.numpy as jnp
from jax import lax
from jax.experimental import pallas as pl
from jax.experimental.pallas import tpu as pltpu

F32 = jnp.float32
BF16 = jnp.bfloat16

D_MODEL = 1024
ATT_PATTERNS = ((128, 1), (512, 4), (2048, 16))
N_ATT_GROUPS = 3
HEADS_PER_GROUP = 4
HEAD_DIM = 64
ATT_WIDTH = 768
ATT_OUT_WIDTH = 256
REL_BUCKETS = 32
REL_MAX_DIST = 2048
RWKV_HEAD_DIM = 64
RWKV_WIDTH = 512
DECAY_LORA = 64
ICLR_LORA = 64
RWKV_SHIFT_WIDTH = 1824
GN_EPS = 64e-5
N_EXPERT_GROUPS = 4
EXPERTS_PER_GROUP = 8
N_EXPERTS = 32
TOP_K = 2
D_EXPERT = 512
NORM_EPS = 1e-6
NEG_INF = -1e30

LANES = 128
ATT_BLOCK = 128
ATT_BATCH = 4
ATT_AHEAD = 1
WKV_CHUNK = 64
RWKV_PAD_WIDTH = 2048
QKV_COL_BLOCK = 0
ROUTER_PAD = 128
MOE_ROWS = 512
VMEM_LIMIT = 56 * 1024 * 1024


def _sigmoid(x):
    return 0.5 * jnp.tanh(0.5 * x) + 0.5


def _dot(a, b):
    return jnp.dot(a.astype(BF16), b.astype(BF16), preferred_element_type=F32)


def _split3(a):
    hi = a.astype(BF16)
    r1 = a - hi.astype(F32)
    mid = r1.astype(BF16)
    lo = (r1 - mid.astype(F32)).astype(BF16)
    return hi, mid, lo


def _split2(a):
    hi = a.astype(BF16)
    lo = (a - hi.astype(F32)).astype(BF16)
    return hi, lo


INPROJ_COLS = 512


def _inproj_kernel(x_ref, g_ref, wrg_ref, wqkv_ref, rg_ref, qkv_ref):
    x = x_ref[...]
    ms = jnp.mean(x * x, axis=-1, keepdims=True)
    h = (x * lax.rsqrt(ms + NORM_EPS) * g_ref[...]).astype(BF16)
    for w_ref, o_ref in ((wrg_ref, rg_ref), (wqkv_ref, qkv_ref)):
        width = w_ref.shape[1]
        for c in range(0, width, INPROJ_COLS):
            cs = slice(c, min(c + INPROJ_COLS, width))
            o_ref[:, cs] = jnp.dot(h, w_ref[:, cs], preferred_element_type=F32).astype(o_ref.dtype)


def _inproj(x2, gain, w_rg, w_qkv):
    T = x2.shape[0]
    tm = 512
    n_rg, n_qkv = w_rg.shape[1], w_qkv.shape[1]
    return pl.pallas_call(
        _inproj_kernel,
        out_shape=[jax.ShapeDtypeStruct((T, n_rg), BF16), jax.ShapeDtypeStruct((T, n_qkv), F32)],
        grid=(T // tm,),
        in_specs=[pl.BlockSpec((tm, D_MODEL), lambda i: (i, 0)),
                  pl.BlockSpec((1, D_MODEL), lambda i: (0, 0)),
                  pl.BlockSpec((D_MODEL, n_rg), lambda i: (0, 0)),
                  pl.BlockSpec((D_MODEL, n_qkv), lambda i: (0, 0))],
        out_specs=[pl.BlockSpec((tm, n_rg), lambda i: (i, 0)),
                   pl.BlockSpec((tm, n_qkv), lambda i: (i, 0))],
        compiler_params=pltpu.CompilerParams(
            dimension_semantics=("arbitrary",), vmem_limit_bytes=VMEM_LIMIT),
        name="inproj",
    )(x2, gain, w_rg, w_qkv)


def _attn_kernel(q0, q1, q2, k0, k1, k2, v0, v1, v2, qg_ref, kg_ref, bias_ref, out_ref,
                 qn_s, kn_s, acc_s, m_s, l_s):
    S = out_ref.shape[1]
    q_refs, k_refs, v_refs = (q0, q1, q2), (k0, k1, k2), (v0, v1, v2)
    lane = lax.broadcasted_iota(jnp.int32, (1, LANES), 1)
    lo = lane < HEAD_DIM
    col = lax.broadcasted_iota(jnp.int32, (1, 2 * ATT_BLOCK), 1)

    same_head = (lax.broadcasted_iota(jnp.int32, (LANES, LANES), 0) < HEAD_DIM) == (
        lax.broadcasted_iota(jnp.int32, (LANES, LANES), 1) < HEAD_DIM)
    ones_bd = jnp.where(same_head, 1.0, 0.0).astype(BF16)

    def pair_norm(x, gain):
        ss = jnp.dot((x * x).astype(BF16), ones_bd, preferred_element_type=F32)
        return x * lax.rsqrt(ss * (1.0 / HEAD_DIM) + NORM_EPS) * gain

    rows_per = 256

    def norm_body(i, c):
        rs = pl.ds(pl.multiple_of(i * rows_per, rows_per), rows_per)
        for g in range(N_ATT_GROUPS):
            qn_s[g, rs, :] = pair_norm(q_refs[g][0, rs, :], qg_ref[g])
            kn_s[g, rs, :] = pair_norm(k_refs[g][0, rs, :], kg_ref[g])
        return c

    lax.fori_loop(0, S // rows_per, norm_body, 0)

    for g, (window, d) in enumerate(ATT_PATTERNS):
        n_blk = (S // d) // ATT_BLOCK
        has_prev = n_blk > 1

        def rows_at(start, d=d):
            return pl.ds(start, ATT_BLOCK) if d == 1 else pl.ds(start, ATT_BLOCK, stride=d)

        def blk_stages(it, g=g, d=d, has_prev=has_prev, rows_at=rows_at):
            curs, q2s, kcats, vcats, pens = [], [], [], [], []
            for b in range(ATT_BATCH):
                idx = it * ATT_BATCH + b
                r = idx % d
                n = idx // d
                cur = rows_at(n * (ATT_BLOCK * d) + r)
                q = qn_s[g, cur, :]
                q2s.append(jnp.concatenate([jnp.where(lo, q, 0.0), jnp.where(lo, 0.0, q)], axis=0).astype(BF16))
                kc = kn_s[g, cur, :].astype(BF16)
                vc = v_refs[g][0, cur, :].astype(BF16)
                if has_prev:
                    prv = rows_at(max(n - 1, 0) * (ATT_BLOCK * d) + r)
                    kc = jnp.concatenate([kn_s[g, prv, :].astype(BF16), kc], axis=0)
                    vc = jnp.concatenate([v_refs[g][0, prv, :].astype(BF16), vc], axis=0)
                    pens.append(jnp.where(col < ATT_BLOCK, NEG_INF if n == 0 else 0.0, 0.0)[None])
                curs.append(cur)
                kcats.append(kc)
                vcats.append(jnp.concatenate([vc, jnp.ones_like(vc)], axis=1))
            q2, kcat, vcat = jnp.stack(q2s), jnp.stack(kcats), jnp.stack(vcats)
            s = lax.dot_general(q2, kcat, (((2,), (2,)), ((0,), (0,))), preferred_element_type=F32)
            yield
            if has_prev:
                s = s + bias_ref[g, 0] + jnp.concatenate(pens, axis=0)
            else:
                s = s + bias_ref[g, 0, :, ATT_BLOCK:]
            m = jnp.max(s, axis=-1, keepdims=True)
            p = jnp.exp(s - m).astype(BF16)
            ol = lax.dot_general(p, vcat, (((2,), (1,)), ((0,), (0,))), preferred_element_type=F32)
            for b in range(ATT_BATCH):
                cur = curs[b]
                acc_s[g, cur, :] = jnp.where(lo, ol[b, :ATT_BLOCK, :LANES], ol[b, ATT_BLOCK:, :LANES])
                l_s[g, cur, :] = jnp.where(lo, ol[b, :ATT_BLOCK, LANES:], ol[b, ATT_BLOCK:, LANES:])
                m_s[g, cur, :] = jnp.where(lo, m[b, :ATT_BLOCK], m[b, ATT_BLOCK:])

        steps = [blk_stages(it) for it in range(d * n_blk // ATT_BATCH)]
        for step in steps[:ATT_AHEAD]:
            next(step)
        for k, step in enumerate(steps):
            if k + ATT_AHEAD < len(steps):
                next(steps[k + ATT_AHEAD])
            for _ in step:
                pass

    def comb_body(i, c):
        rs = pl.ds(pl.multiple_of(i * rows_per, rows_per), rows_per)
        m = jnp.maximum(jnp.maximum(m_s[0, rs, :], m_s[1, rs, :]), m_s[2, rs, :])
        num = jnp.zeros((rows_per, LANES), F32)
        den = jnp.zeros((rows_per, LANES), F32)
        for g in range(N_ATT_GROUPS):
            e = jnp.exp(m_s[g, rs, :] - m)
            num = num + e * acc_s[g, rs, :]
            den = den + e * l_s[g, rs, :]
        out_ref[0, rs, :] = num / den
        return c

    lax.fori_loop(0, S // rows_per, comb_body, 0)


def _attention(p3, q_gain, k_gain, bias):
    B, S, _ = p3.shape
    n_pairs = HEADS_PER_GROUP // 2

    def col_spec(base):
        return [pl.BlockSpec((1, S, LANES),
                             functools.partial(lambda b, sp, off: (b, 0, off + sp), off=base + g * n_pairs))
                for g in range(N_ATT_GROUPS)]

    n_qkv_blocks = ATT_WIDTH // LANES
    in_specs = (col_spec(QKV_COL_BLOCK) + col_spec(QKV_COL_BLOCK + n_qkv_blocks)
                + col_spec(QKV_COL_BLOCK + 2 * n_qkv_blocks)
                + [pl.BlockSpec((N_ATT_GROUPS, 1, LANES), lambda b, sp: (0, 0, 0)),
                   pl.BlockSpec((N_ATT_GROUPS, 1, LANES), lambda b, sp: (0, 0, 0)),
                   pl.BlockSpec((N_ATT_GROUPS, 1, 2 * ATT_BLOCK, 2 * ATT_BLOCK),
                                lambda b, sp: (0, sp, 0, 0))])
    return pl.pallas_call(
        _attn_kernel,
        out_shape=jax.ShapeDtypeStruct((B, S, ATT_OUT_WIDTH), F32),
        grid=(B, n_pairs),
        in_specs=in_specs,
        out_specs=pl.BlockSpec((1, S, LANES), lambda b, sp: (b, 0, sp)),
        scratch_shapes=[pltpu.VMEM((N_ATT_GROUPS, S, LANES), F32) for _ in range(5)],
        compiler_params=pltpu.CompilerParams(
            dimension_semantics=("arbitrary", "arbitrary"), vmem_limit_bytes=VMEM_LIMIT),
        name="dilated_attention",
    )(*([p3] * 9), q_gain, k_gain, bias)


def _t5_causal_bucket(dist):
    max_exact = REL_BUCKETS // 2
    d = jnp.maximum(dist.astype(F32), 1.0)
    large = max_exact + (jnp.log(d / max_exact) / math.log(REL_MAX_DIST / max_exact)
                         * (REL_BUCKETS - max_exact)).astype(jnp.int32)
    large = jnp.minimum(large, REL_BUCKETS - 1)
    return jnp.where(dist < max_exact, dist, large)


def _attention_bias(rel_bias_table):
    W = ATT_BLOCK
    qi = jnp.arange(W)[:, None]
    kj = jnp.arange(2 * W)[None, :]
    rel = qi + W - kj
    valid = (rel >= 0) & (rel <= W)
    per_group = []
    for g, (_, d) in enumerate(ATT_PATTERNS):
        tab = rel_bias_table[:, g * HEADS_PER_GROUP:(g + 1) * HEADS_PER_GROUP]
        bucket = _t5_causal_bucket(jnp.clip(rel, 0, W) * d)
        onehot = (bucket[:, :, None] == jnp.arange(REL_BUCKETS)[None, None, :]).astype(F32)
        bias = jnp.einsum('qkb,bh->qkh', onehot, tab.astype(F32),
                          precision=lax.Precision.HIGHEST)
        bias = jnp.where(valid[:, :, None], bias, NEG_INF)
        per_group.append(jnp.transpose(bias, (2, 0, 1)).reshape(2, 2 * W, 2 * W))
    return jnp.stack(per_group, axis=0)


def _rwkv_prep_kernel(p_ref, mu_ref, w0_ref, wup_ref, a0_ref, aup_ref, gup_ref, kk_ref, ka_ref, rk_ref,
                      at_o, rt_o, bt_o, kt_o, bw_o, kw_o, v_o, bonus_o, g_o, wend_o, carry):
    tq = p_ref.shape[1]
    C = RWKV_WIDTH
    L = WKV_CHUNK

    @pl.when(pl.program_id(1) == 0)
    def _():
        carry[...] = jnp.zeros_like(carry)

    p = p_ref[0].astype(F32)
    row = lax.broadcasted_iota(jnp.int32, (tq, 1), 0)
    prev = jnp.where(row == 0, carry[...], pltpu.roll(p, 1, axis=0))
    carry[...] = p[tq - 1:tq, :]
    pm = p + (prev - p) * mu_ref[...]
    r, k, v = pm[:, 0:C], pm[:, C:2 * C], pm[:, 2 * C:3 * C]
    xw = pm[:, 3 * C:3 * C + LANES]
    xa = pm[:, 3 * C + LANES:3 * C + 2 * LANES]
    xg = pm[:, 3 * C + 2 * LANES:]

    z = w0_ref[...] + _dot(jnp.tanh(xw), wup_ref[...])
    softplus_neg = jnp.maximum(-z, 0.0) + jnp.log(1.0 + jnp.exp(-jnp.abs(z)))
    lw = -jnp.exp(-softplus_neg - 0.5)
    a = _sigmoid(a0_ref[...] + _dot(xa, aup_ref[...]))
    g = _dot(_sigmoid(xg), gup_ref[...])

    hr = jnp.right_shift(lax.broadcasted_iota(jnp.int32, (C, C), 0), 6)
    hc = jnp.right_shift(lax.broadcasted_iota(jnp.int32, (C, C), 1), 6)
    ones_bd = jnp.where(hr == hc, 1.0, 0.0).astype(BF16)

    def head_sums(x):
        return jnp.dot(x.astype(BF16), ones_bd, preferred_element_type=F32)

    k2 = k * (1.0 + (a - 1.0) * ka_ref[...])
    kk = k * kk_ref[...]
    kk = kk / jnp.maximum(jnp.sqrt(head_sums(kk * kk)), 1e-12)
    bonus = head_sums(r * k2 * rk_ref[...]) * v
    b = kk * a

    half = 256
    ri = lax.broadcasted_iota(jnp.int32, (half, half), 0)
    ci = lax.broadcasted_iota(jnp.int32, (half, half), 1)
    tri = jnp.where((jnp.right_shift(ri, 6) == jnp.right_shift(ci, 6)) & (ri >= ci), 1.0, 0.0).astype(BF16)
    cums = []
    for s in range(tq // half):
        hi, mid, lo3 = _split3(lw[s * half:(s + 1) * half, :])
        cums.append(jnp.dot(tri, hi, preferred_element_type=F32)
                    + (jnp.dot(tri, mid, preferred_element_type=F32)
                       + jnp.dot(tri, lo3, preferred_element_type=F32)))
    cum = jnp.concatenate(cums, axis=0)
    n_chunks = tq // L
    ends = [cum[c * L + L - 1:c * L + L, :] for c in range(n_chunks)]
    tot = jnp.concatenate([jnp.broadcast_to(e, (L, C)) for e in ends], axis=0)
    wend = jnp.exp(jnp.concatenate(ends, axis=0))

    e_pos = jnp.exp(cum)
    e_neg = jnp.exp(-cum)
    e_rem = jnp.exp(tot - cum)
    at = -kk * jnp.exp(cum - lw)
    rt = r * e_pos
    bt = b * e_neg
    kt = k2 * e_neg
    bw = b * e_rem
    kw = k2 * e_rem
    at_o[0] = at.astype(BF16)
    rt_o[0] = rt.astype(BF16)
    bt_o[0] = bt.astype(BF16)
    kt_o[0] = kt.astype(BF16)
    bw_o[0] = bw.astype(BF16)
    kw_o[0] = kw.astype(BF16)
    v_o[0] = v.astype(BF16)
    bonus_o[0] = bonus
    g_o[0] = g
    wend_o[0] = wend


def _rwkv_prep(p3, mu, w0, wup, a0, aup, gup, k_k, k_a, r_k):
    B, S, _ = p3.shape
    tq = 512
    C = RWKV_WIDTH
    tm_shape = lambda dt: jax.ShapeDtypeStruct((B, S, C), dt)
    tm_spec = pl.BlockSpec((1, tq, C), lambda b, j: (b, j, 0))
    full = lambda shape: pl.BlockSpec(shape, lambda b, j: (0,) * len(shape))
    return pl.pallas_call(
        _rwkv_prep_kernel,
        out_shape=[tm_shape(BF16)] * 7 + [tm_shape(F32)] * 2 + [jax.ShapeDtypeStruct((B, S // WKV_CHUNK, C), F32)],
        grid=(B, S // tq),
        in_specs=[pl.BlockSpec((1, tq, RWKV_PAD_WIDTH), lambda b, j: (b, j, 0)),
                  full((1, RWKV_PAD_WIDTH)), full((1, C)), full((LANES, C)), full((1, C)),
                  full((LANES, C)), full((2 * LANES, C)), full((1, C)), full((1, C)), full((1, C))],
        out_specs=[tm_spec] * 9 + [pl.BlockSpec((1, tq // WKV_CHUNK, C), lambda b, j: (b, j, 0))],
        scratch_shapes=[pltpu.VMEM((1, RWKV_PAD_WIDTH), F32)],
        compiler_params=pltpu.CompilerParams(
            dimension_semantics=("arbitrary", "arbitrary"), vmem_limit_bytes=VMEM_LIMIT),
        name="rwkv_prep",
    )(p3, mu, w0, wup, a0, aup, gup, k_k, k_a, r_k)


WKV_GROUP = 4
WKV_GROUP_WIDTH = WKV_GROUP * RWKV_HEAD_DIM


def _wkv_kernel(at_ref, rt_ref, bt_ref, kt_ref, bw_ref, kw_ref, v_ref, bonus_ref, g_ref, wend_ref,
                lnw_ref, lnb_ref, o_ref, state, m_s, u0_s, o0_s, arb_s, p_s, c_s, sh_s):
    N, C, GW = RWKV_HEAD_DIM, WKV_CHUNK, WKV_GROUP_WIDTH
    n_groups = RWKV_WIDTH // GW
    n_sub = at_ref.shape[1] // C
    j = pl.program_id(1)

    @pl.when(j == 0)
    def _():
        state[...] = jnp.zeros_like(state)

    def iota(shape, dim):
        return lax.broadcasted_iota(jnp.int32, shape, dim)

    same_head = jnp.right_shift(iota((1, GW, GW), 1), 6) == jnp.right_shift(iota((1, GW, GW), 2), 6)
    ones_bd = jnp.where(same_head, 1.0, 0.0).astype(BF16)
    g_row = iota((1, 2 * C, 2 * GW), 1)
    g_col = jnp.bitwise_and(iota((1, 2 * C, 2 * GW), 2), N - 1)
    band = ((g_row < C) & (g_row > g_col)) | ((g_row >= C) & ((g_row - C) >= g_col))
    eye = jnp.where(iota((1, C, GW), 1) == jnp.bitwise_and(iota((1, C, GW), 2), N - 1), 1.0, 0.0)

    def bmm(a, b, dims):
        return lax.dot_general(a.astype(BF16), b.astype(BF16), (dims, ((0,), (0,))),
                               preferred_element_type=F32)

    nn, nt, tn = ((2,), (1,)), ((2,), (2,)), ((1,), (1,))

    def bd(x):
        xb = x.astype(BF16)
        return jnp.where(same_head, jnp.concatenate([xb] * WKV_GROUP, axis=1), jnp.zeros((), BF16))

    def narrow(x):
        return sum(x[:, h * N:(h + 1) * N] for h in range(WKV_GROUP))

    span = 4
    rows = span * C

    def stack(x):
        x3 = x.reshape(span, C, n_groups * GW)
        return jnp.concatenate([x3[:, :, q * GW:(q + 1) * GW] for q in range(n_groups)], axis=0)

    def unstack(y):
        return jnp.concatenate([y[q * span:(q + 1) * span].reshape(rows, GW) for q in range(n_groups)],
                               axis=1)

    def chunk_terms(i):
        rs = slice(i * rows, (i + 1) * rows)
        at = stack(at_ref[0, rs, :])
        ar = jnp.concatenate([at, stack(rt_ref[0, rs, :])], axis=1)
        bk_bd = jnp.concatenate([bd(stack(bt_ref[0, rs, :])), bd(stack(kt_ref[0, rs, :]))], axis=1)
        gm = jnp.where(band, bmm(ar, bk_bd, nt), 0.0)
        yield
        a_ab = gm[:, :C, :GW]
        inv = eye + a_ab
        pw = bmm(a_ab, bd(a_ab), nn)
        yield
        for _ in range(int(math.log2(C)) - 2):
            prod = bmm(jnp.concatenate([inv, pw], axis=1), bd(pw), nn)
            yield
            inv = inv + prod[:, :C]
            pw = prod[:, C:]
        inv = inv + bmm(inv, bd(pw), nn)
        yield
        v = stack(v_ref[0, rs, :])
        kv = bmm(gm[:, :, GW:], bd(v), nn)
        yield
        mu0 = bmm(inv, jnp.concatenate([bd(at), bd(kv[:, :C])], axis=2), nn)
        yield
        m = mu0[:, :, :GW].astype(BF16)
        u0 = mu0[:, :, GW:]
        bw = stack(bw_ref[0, rs, :])
        low_rank = jnp.where(same_head, bmm(m, bw, tn), 0.0)
        const = jnp.where(same_head, bmm(jnp.concatenate([u0.astype(BF16), v], axis=1),
                                         jnp.concatenate([bw, stack(kw_ref[0, rs, :])], axis=1), tn), 0.0)
        m_s[rs, :] = unstack(m)
        u0_s[rs, :] = unstack(u0)
        o0_s[rs, :] = unstack(kv[:, C:])
        arb_s[rs, :] = unstack(gm[:, C:, :GW]).astype(BF16)
        for q in range(n_groups):
            p_s[i * span:(i + 1) * span, q] = low_rank[q * span:(q + 1) * span].astype(BF16)
            c_s[i * span:(i + 1) * span, q] = narrow(const[q * span:(q + 1) * span])

    def groups(x):
        return jnp.stack([x[:, q * GW:(q + 1) * GW] for q in range(n_groups)], axis=0)

    def chunk_state(i):
        s0 = state[...]
        sh_s[i] = s0.astype(BF16)
        w_end = groups(wend_ref[0, pl.ds(j * n_sub + i, 1), :])
        state[...] = s0 * w_end + bmm(s0, p_s[i], nn) + c_s[i]

    n_prob = n_groups * span

    ones_p = jnp.broadcast_to(ones_bd, (n_prob, GW, GW))

    def head_sums(x):
        return bmm(x, ones_p, nn)

    def chunk_outputs(i):
        rs = slice(i * rows, (i + 1) * rows)
        s0 = bd(jnp.concatenate([sh_s[i * span:(i + 1) * span, q] for q in range(n_groups)], axis=0))
        mr_s = bmm(jnp.concatenate([stack(m_s[rs, :]), stack(rt_ref[0, rs, :])], axis=1), s0, nt)
        yield
        u = mr_s[:, :C] + stack(u0_s[rs, :])
        o = mr_s[:, C:] + stack(o0_s[rs, :]) + bmm(stack(arb_s[rs, :]), bd(u), nn)
        yield
        mean = head_sums(o) * (1.0 / N)
        yield
        dlt = o - mean
        on = unstack(dlt * lax.rsqrt(head_sums(dlt * dlt) * (1.0 / N) + GN_EPS))
        on = on * lnw_ref[...] + lnb_ref[...]
        o_ref[0, rs, :] = (on + bonus_ref[0, rs, :]) * g_ref[0, rs, :]

    def run(stages, between=()):
        pending = iter(between)
        for _ in stages:
            step = next(pending, None)
            if step is not None:
                step()
        for step in pending:
            step()

    n_spans = n_sub // span
    assert n_spans >= 2
    states = lambda i: [functools.partial(chunk_state, k) for k in range(i * span, (i + 1) * span)]
    run(chunk_terms(0))
    for i in range(1, n_spans):
        run(chunk_terms(i), states(i - 1))
    run(chunk_outputs(0), states(n_spans - 1))
    for i in range(1, n_spans):
        run(chunk_outputs(i))


def _wkv(at, rt, bt, kt, bw, kw, v, bonus, g, wend, ln_w, ln_b):
    B, S, C = at.shape
    tc = 512
    tm_spec = pl.BlockSpec((1, tc, C), lambda bb, j: (bb, j, 0))
    par_spec = pl.BlockSpec((1, C), lambda bb, j: (0, 0))
    blocks = (C // WKV_GROUP_WIDTH, WKV_GROUP_WIDTH, WKV_GROUP_WIDTH)
    wide = (C // WKV_GROUP_WIDTH, RWKV_HEAD_DIM, WKV_GROUP_WIDTH)
    return pl.pallas_call(
        _wkv_kernel,
        out_shape=jax.ShapeDtypeStruct((B, S, C), F32),
        grid=(B, S // tc),
        in_specs=([tm_spec] * 9 + [pl.BlockSpec((1, S // WKV_CHUNK, C), lambda bb, j: (bb, 0, 0))]
                  + [par_spec] * 2),
        out_specs=tm_spec,
        scratch_shapes=[pltpu.VMEM(wide, F32),
                        pltpu.VMEM((tc, C), BF16), pltpu.VMEM((tc, C), F32), pltpu.VMEM((tc, C), F32),
                        pltpu.VMEM((tc, C), BF16),
                        pltpu.VMEM((tc // WKV_CHUNK,) + blocks, BF16), pltpu.VMEM((tc // WKV_CHUNK,) + wide, F32),
                        pltpu.VMEM((tc // WKV_CHUNK,) + wide, BF16)],
        compiler_params=pltpu.CompilerParams(
            dimension_semantics=("arbitrary", "arbitrary"), vmem_limit_bytes=VMEM_LIMIT),
        name="wkv7",
    )(at, rt, bt, kt, bw, kw, v, bonus, g, wend, ln_w, ln_b)


MERGE_SPLIT = 2
INFO_FIELDS = 8


def _merge_kernel(att_ref, o_ref, gate_ref, x_ref, wa_ref, wr_ref, wo_ref, g2_ref, wrt_ref, brt_ref,
                  x1_ref, h2_ref, lg_ref, cnt_ref, lt_ref):
    tm = x_ref.shape[1]
    ts = tm // MERGE_SPLIT
    parts = [slice(h * ts, (h + 1) * ts) for h in range(MERGE_SPLIT)]
    lane = lax.broadcasted_iota(jnp.int32, (1, ROUTER_PAD), 1)

    @pl.when((pl.program_id(0) == 0) & (pl.program_id(1) == 0))
    def _():
        cnt_ref[...] = jnp.zeros_like(cnt_ref)

    ys = [(_dot(att_ref[0, rs, :], wa_ref[...]), _dot(o_ref[0, rs, :], wr_ref[...])) for rs in parts]

    mixed = []
    for rs, (y_att, y_rwkv) in zip(parts, ys):
        gates = _sigmoid(gate_ref[0, rs, :].astype(F32))
        mixed.append(gates[:, :D_MODEL] * y_att + gates[:, D_MODEL:] * y_rwkv)

    x1s = []
    for rs, mx in zip(parts, mixed):
        x1 = x_ref[0, rs, :] + _dot(mx, wo_ref[...])
        x1_ref[0, rs, :] = x1
        x1s.append(x1)

    splits = []
    for rs, x1 in zip(parts, x1s):
        ms = jnp.mean(x1 * x1, axis=-1, keepdims=True)
        h2 = x1 * lax.rsqrt(ms + NORM_EPS) * g2_ref[...]
        hh, hl = _split2(h2)
        half = D_MODEL // 2
        rounded = hh.astype(F32)
        w_hi = pltpu.bitcast(rounded[:, :half], jnp.uint32)
        w_lo = pltpu.bitcast(rounded[:, half:], jnp.uint32)
        h2_ref[0, rs, :] = w_hi | jnp.right_shift(w_lo, jnp.uint32(16))
        splits.append((hh, hl))

    wh, wl = wrt_ref[0], wrt_ref[1]
    logits = [jnp.dot(hh, wh, preferred_element_type=F32)
              + (jnp.dot(hh, wl, preferred_element_type=F32) + jnp.dot(hl, wh, preferred_element_type=F32))
              + brt_ref[...] for hh, hl in splits]

    ti = lax.broadcasted_iota(jnp.int32, (ts, ts), 0)
    tj = lax.broadcasted_iota(jnp.int32, (ts, ts), 1)
    before = jnp.where(tj < ti, 1.0, 0.0).astype(BF16)
    for rs, lg in zip(parts, logits):
        gmask = lane < N_EXPERT_GROUPS
        gl = jnp.where(gmask, lg, -jnp.inf)
        gmax = jnp.max(gl, axis=-1, keepdims=True)
        grp_idx = jnp.min(jnp.where(gl == gmax, lane, ROUTER_PAD), axis=-1, keepdims=True)
        grp_p = 1.0 / jnp.sum(jnp.where(gmask, jnp.exp(lg - gmax), 0.0), axis=-1, keepdims=True)
        e_lane = lane - N_EXPERT_GROUPS
        emask = ((e_lane >= 0) & (e_lane < N_EXPERTS)
                 & (jnp.right_shift(e_lane, int(math.log2(EXPERTS_PER_GROUP))) == grp_idx))
        el = jnp.where(emask, lg, -jnp.inf)
        m1 = jnp.max(el, axis=-1, keepdims=True)
        i1 = jnp.min(jnp.where(el == m1, lane, ROUTER_PAD), axis=-1, keepdims=True)
        el2 = jnp.where(lane == i1, -jnp.inf, el)
        m2 = jnp.max(el2, axis=-1, keepdims=True)
        i2 = jnp.min(jnp.where(el2 == m2, lane, ROUTER_PAD), axis=-1, keepdims=True)
        e2 = jnp.exp(m2 - m1)
        gate1 = grp_p / (1.0 + e2)
        gate2 = gate1 * e2

        oh = jnp.where((lane == i1) | (lane == i2), 1.0, 0.0)
        pref = jnp.dot(before, oh.astype(BF16), preferred_element_type=F32) + cnt_ref[...]
        rank1 = jnp.sum(jnp.where(lane == i1, pref, 0.0), axis=-1, keepdims=True)
        rank2 = jnp.sum(jnp.where(lane == i2, pref, 0.0), axis=-1, keepdims=True)
        cnt_ref[...] = cnt_ref[...] + jnp.sum(oh, axis=0, keepdims=True)
        info = jnp.where(lane == 0, (i1 - N_EXPERT_GROUPS).astype(F32), 0.0)
        info = jnp.where(lane == 1, (i2 - N_EXPERT_GROUPS).astype(F32), info)
        info = jnp.where(lane == 2, rank1, info)
        info = jnp.where(lane == 3, rank2, info)
        info = jnp.where(lane == 4, gate1, info)
        info = jnp.where(lane == 5, gate2, info)
        lg_ref[0, rs, :] = info
        lt_ref[:, rs] = info.T[:INFO_FIELDS, :]


def _merge(att, o_g, p3, x, w_att_out, w_rwkv_out, w_out, gain2, w_router, b_router):
    B, S, D = x.shape
    tm = 512
    full = lambda shape: pl.BlockSpec(shape, lambda b, j: (0,) * len(shape))
    return pl.pallas_call(
        _merge_kernel,
        out_shape=[jax.ShapeDtypeStruct((B, S, D), F32), jax.ShapeDtypeStruct((B, S, D // 2), jnp.uint32),
                   jax.ShapeDtypeStruct((B, S, ROUTER_PAD), F32),
                   jax.ShapeDtypeStruct((1, ROUTER_PAD), F32),
                   jax.ShapeDtypeStruct((INFO_FIELDS, B * S), F32)],
        grid=(B, S // tm),
        in_specs=[pl.BlockSpec((1, tm, ATT_OUT_WIDTH), lambda b, j: (b, j, 0)),
                  pl.BlockSpec((1, tm, RWKV_WIDTH), lambda b, j: (b, j, 0)),
                  pl.BlockSpec((1, tm, 2 * D_MODEL), lambda b, j: (b, j, 1)),
                  pl.BlockSpec((1, tm, D), lambda b, j: (b, j, 0)),
                  full((ATT_OUT_WIDTH, D)), full((RWKV_WIDTH, D)), full((D, D)),
                  full((1, D)), full((2, D, ROUTER_PAD)), full((1, ROUTER_PAD))],
        out_specs=[pl.BlockSpec((1, tm, D), lambda b, j: (b, j, 0)),
                   pl.BlockSpec((1, tm, D // 2), lambda b, j: (b, j, 0)),
                   pl.BlockSpec((1, tm, ROUTER_PAD), lambda b, j: (b, j, 0)),
                   pl.BlockSpec((1, ROUTER_PAD), lambda b, j: (0, 0)),
                   pl.BlockSpec((INFO_FIELDS, tm), lambda b, j: (0, b * (S // tm) + j))],
        compiler_params=pltpu.CompilerParams(
            dimension_semantics=("arbitrary", "arbitrary"), vmem_limit_bytes=VMEM_LIMIT),
        name="merge_outproj_router",
    )(att, o_g, p3, x, w_att_out, w_rwkv_out, w_out, gain2, w_router, b_router)


def _expert_kernel(be_ref, nb_ref, x_ref, wg_ref, wu_ref, wd_ref, y_ref, wgu_s, wd_s):
    i = pl.program_id(0)

    @pl.when((i == 0) | (be_ref[i] != be_ref[jnp.maximum(i - 1, 0)]))
    def _():
        wgu_s[:, :D_EXPERT] = wg_ref[0].astype(BF16)
        wgu_s[:, D_EXPERT:] = wu_ref[0].astype(BF16)
        wd_s[...] = wd_ref[0].astype(BF16)

    @pl.when(i < nb_ref[0])
    def _():
        words = x_ref[...]
        x_a = pltpu.bitcast(words & jnp.uint32(0xFFFF0000), F32)
        x_b = pltpu.bitcast(jnp.left_shift(words, jnp.uint32(16)), F32)
        xb = jnp.concatenate([x_a, x_b], axis=-1).astype(BF16)
        hw = D_EXPERT // 2
        hgu = [(jnp.dot(xb, wgu_s[:, c * hw:(c + 1) * hw], preferred_element_type=F32),
                jnp.dot(xb, wgu_s[:, D_EXPERT + c * hw:D_EXPERT + (c + 1) * hw], preferred_element_type=F32))
               for c in range(2)]
        y = None
        for c, (hg, hu) in enumerate(hgu):
            hid = (hg * _sigmoid(hg) * hu).astype(BF16)
            part = jnp.dot(hid, wd_s[c * hw:(c + 1) * hw, :], preferred_element_type=F32)
            y = part if y is None else y + part
        y_ref[...] = y.astype(y_ref.dtype)

    @pl.when(i >= nb_ref[0])
    def _():
        y_ref[...] = jnp.zeros_like(y_ref)


def _experts(blk_expert, n_used, xin, w_gate, w_up, w_down):
    n_rows = xin.shape[0]
    D = D_MODEL
    n_blocks = n_rows // MOE_ROWS
    return pl.pallas_call(
        _expert_kernel,
        out_shape=jax.ShapeDtypeStruct((n_rows, D), BF16),
        grid_spec=pltpu.PrefetchScalarGridSpec(
            num_scalar_prefetch=2,
            grid=(n_blocks,),
            scratch_shapes=[pltpu.VMEM((D, 2 * D_EXPERT), BF16), pltpu.VMEM((D_EXPERT, D), BF16)],
            in_specs=[pl.BlockSpec((MOE_ROWS, D // 2), lambda i, be, nb: (i, 0)),
                      pl.BlockSpec((1, D, D_EXPERT), lambda i, be, nb: (be[i], 0, 0)),
                      pl.BlockSpec((1, D, D_EXPERT), lambda i, be, nb: (be[i], 0, 0)),
                      pl.BlockSpec((1, D_EXPERT, D), lambda i, be, nb: (be[i], 0, 0))],
            out_specs=pl.BlockSpec((MOE_ROWS, D), lambda i, be, nb: (i, 0))),
        compiler_params=pltpu.CompilerParams(
            dimension_semantics=("arbitrary",), vmem_limit_bytes=VMEM_LIMIT),
        name="moe_experts",
    )(blk_expert, n_used, xin, w_gate, w_up, w_down)


def _row_tokens(dests, counts, pcounts, pstarts, pends, n_rows, T):
    dest_flat = jnp.concatenate(dests)
    A = dest_flat.shape[0]
    n_pad = n_rows - A
    seg_count = jnp.concatenate([pcounts - counts, n_rows - pends[-1:]])
    seg_first = jnp.concatenate([pstarts + counts, pends[-1:]])
    seg_end = jnp.cumsum(seg_count)
    i = jnp.arange(n_pad, dtype=jnp.int32)
    seg = jnp.sum(i[None, :] >= seg_end[:, None], axis=0)
    in_seg = seg[None, :] == jnp.arange(seg_end.shape[0], dtype=jnp.int32)[:, None]
    shift = (seg_first - (seg_end - seg_count)).astype(jnp.int32)
    pad_rows = i + jnp.sum(jnp.where(in_seg, shift[:, None], 0), axis=0)
    rows = jnp.concatenate([dest_flat, pad_rows])
    toks = jnp.concatenate([jnp.arange(T, dtype=jnp.int32)] * len(dests) + [i % T])
    tok_bits = max(T - 1, 1).bit_length()
    assert n_rows << tok_bits <= 1 << 32
    packed = jnp.sort((rows.astype(jnp.uint32) << tok_bits) | toks.astype(jnp.uint32))
    return (packed & jnp.uint32((1 << tok_bits) - 1)).astype(jnp.int32)


def _row_layout(info, counts, T):
    A = T * TOP_K
    counts = counts.astype(jnp.int32)
    pcounts = (counts + MOE_ROWS - 1) // MOE_ROWS * MOE_ROWS
    pends = jnp.cumsum(pcounts)
    pstarts = pends - pcounts
    experts = jnp.arange(N_EXPERTS, dtype=F32)[:, None]
    seg_start = [jnp.dot(pstarts.astype(F32), (info[k][None, :] == experts).astype(F32),
                         precision=lax.Precision.HIGHEST) for k in range(TOP_K)]
    dest = [(seg_start[k] + info[TOP_K + k]).astype(jnp.int32) for k in range(TOP_K)]
    n_blocks = -(-A // MOE_ROWS) + N_EXPERTS
    n_rows = n_blocks * MOE_ROWS
    row_tok = _row_tokens(dest, counts, pcounts, pstarts, pends, n_rows, T)
    blk_start = jnp.arange(n_blocks, dtype=jnp.int32) * MOE_ROWS
    blk_expert = jnp.minimum(jnp.sum(blk_start[:, None] >= pends[None, :], axis=1), N_EXPERTS - 1)
    n_used = (pends[N_EXPERTS - 1] // MOE_ROWS).astype(jnp.int32)[None]
    return row_tok, blk_expert.astype(jnp.int32), n_used, dest


def _combine_kernel(x1_ref, y0_ref, y1_ref, info_ref, o_ref):
    gate = 2 * TOP_K
    g0 = info_ref[:, gate:gate + 1]
    g1 = info_ref[:, gate + 1:gate + 2]
    o_ref[...] = x1_ref[...] + (g0 * y0_ref[...].astype(F32) + g1 * y1_ref[...].astype(F32))


def _combine(x1, y0, y1, info):
    T, D = x1.shape
    tm = 1024
    row_spec = pl.BlockSpec((tm, D), lambda i: (i, 0))
    return pl.pallas_call(
        _combine_kernel,
        out_shape=jax.ShapeDtypeStruct((T, D), F32),
        grid=(T // tm,),
        in_specs=[row_spec, row_spec, row_spec, pl.BlockSpec((tm, ROUTER_PAD), lambda i: (i, 0))],
        out_specs=row_spec,
        compiler_params=pltpu.CompilerParams(
            dimension_semantics=("arbitrary",), vmem_limit_bytes=VMEM_LIMIT),
        name="moe_combine",
    )(x1, y0, y1, info)


def _pad_rows(w, rows):
    return jnp.pad(w, ((0, rows - w.shape[0]), (0, 0)))


def kernel(x, norm1_gain, w_in, q_norm_gain, k_norm_gain, rel_bias_table, w_att_out, rwkv_shift_mu,
           rwkv_w0, rwkv_w_up, rwkv_a0, rwkv_a_up, rwkv_g_up, rwkv_k_k, rwkv_k_a, rwkv_r_k,
           rwkv_ln_w, rwkv_ln_b, w_rwkv_out, w_out, norm2_gain, w_group_router, b_group_router,
           w_expert_router, b_expert_router, w_expert_gate, w_expert_up, w_expert_down):
    B, S, D = x.shape
    T = B * S
    C = RWKV_WIDTH
    bias = _attention_bias(rel_bias_table)
    for l in range(norm1_gain.shape[0]):
        wi = w_in[l]
        w_qkv = wi[:, :3 * ATT_WIDTH]
        w_rw = wi[:, 3 * ATT_WIDTH:3 * ATT_WIDTH + RWKV_SHIFT_WIDTH]
        w_gt = wi[:, 3 * ATT_WIDTH + RWKV_SHIFT_WIDTH:]
        pad_cols = lambda w, n: jnp.pad(w, ((0, 0), (0, n - w.shape[1])))
        o_w, o_a, o_g = 3 * C, 3 * C + DECAY_LORA, 3 * C + DECAY_LORA + ICLR_LORA

        def rw_layout(w):
            return jnp.concatenate([w[:, :o_w], pad_cols(w[:, o_w:o_a], LANES),
                                    pad_cols(w[:, o_a:o_g], LANES), pad_cols(w[:, o_g:], 2 * LANES)], axis=1)

        w_rg = jnp.concatenate([rw_layout(w_rw), w_gt], axis=1).astype(BF16)
        mu = rw_layout(rwkv_shift_mu[l][None, :])

        p_rg, p_qkv = _inproj(x.reshape(T, D), norm1_gain[l][None, :], w_rg, w_qkv.astype(BF16))
        p3 = p_rg.reshape(B, S, RWKV_PAD_WIDTH + 2 * D)

        tile2 = lambda gmat: jnp.tile(gmat, (1, 2))[:, None, :]
        att = _attention(p_qkv.reshape(B, S, 3 * ATT_WIDTH), tile2(q_norm_gain[l]) * (HEAD_DIM ** -0.5),
                         tile2(k_norm_gain[l]),
                         bias)

        row = lambda v: v[None, :]
        prep = _rwkv_prep(
            p3, mu, row(rwkv_w0[l]), _pad_rows(rwkv_w_up[l], LANES).astype(BF16), row(rwkv_a0[l]),
            _pad_rows(rwkv_a_up[l], LANES).astype(BF16), _pad_rows(rwkv_g_up[l], 2 * LANES).astype(BF16),
            row(rwkv_k_k[l]), row(rwkv_k_a[l]), rwkv_r_k[l].reshape(1, C))
        o_g = _wkv(*prep, row(rwkv_ln_w[l]), row(rwkv_ln_b[l]))

        w_router = jnp.concatenate([w_group_router[l], w_expert_router[l]], axis=1)
        w_router = jnp.pad(w_router, ((0, 0), (0, ROUTER_PAD - w_router.shape[1])))
        wr_hi = w_router.astype(BF16)
        wr_lo = (w_router - wr_hi.astype(F32)).astype(BF16)
        b_router = jnp.concatenate([b_group_router[l], b_expert_router[l]])
        b_router = jnp.pad(b_router, (0, ROUTER_PAD - b_router.shape[0]))[None, :]
        x1, h2, info, counts, info_t = _merge(
            att, o_g, p3, x, w_att_out[l].astype(BF16),
            w_rwkv_out[l].astype(BF16), w_out[l].astype(BF16),
            norm2_gain[l][None, :], jnp.stack([wr_hi, wr_lo]), b_router)

        row_tok, blk_expert, n_used, dest = _row_layout(
            info_t, counts[0, N_EXPERT_GROUPS:N_EXPERT_GROUPS + N_EXPERTS], T)
        xin = h2.reshape(T, D // 2)[row_tok]
        yb = _experts(blk_expert, n_used, xin, w_expert_gate[l], w_expert_up[l], w_expert_down[l])
        x = _combine(x1.reshape(T, D), yb[dest[0]], yb[dest[1]],
                     info.reshape(T, ROUTER_PAD)).reshape(B, S, D)
    return x
```

```python
import functools
import math

import jax
import jax.numpy as jnp
from jax import lax
from jax.experimental import pallas as pl
from jax.experimental.pallas import tpu as pltpu

F32 = jnp.float32
BF16 = jnp.bfloat16

D_MODEL = 1024
ATT_PATTERNS = ((128, 1), (512, 4), (2048, 16))
N_ATT_GROUPS = 3
HEADS_PER_GROUP = 4
HEAD_DIM = 64
ATT_WIDTH = 768
ATT_OUT_WIDTH = 256
REL_BUCKETS = 32
REL_MAX_DIST = 2048
RWKV_HEAD_DIM = 64
RWKV_WIDTH = 512
DECAY_LORA = 64
ICLR_LORA = 64
RWKV_SHIFT_WIDTH = 1824
GN_EPS = 64e-5
N_EXPERT_GROUPS = 4
EXPERTS_PER_GROUP = 8
N_EXPERTS = 32
TOP_K = 2
D_EXPERT = 512
NORM_EPS = 1e-6
NEG_INF = -1e30

LANES = 128
ATT_BLOCK = 128
ATT_BATCH = 8
ATT_AHEAD = 1
WKV_CHUNK = 64
RWKV_PAD_WIDTH = 2048
QKV_COL_BLOCK = 0
ROUTER_PAD = 128
MOE_ROWS = 512
VMEM_LIMIT = 56 * 1024 * 1024


def _sigmoid(x):
    return 0.5 * jnp.tanh(0.5 * x) + 0.5


def _dot(a, b):
    return jnp.dot(a.astype(BF16), b.astype(BF16), preferred_element_type=F32)


def _split3(a):
    hi = a.astype(BF16)
    r1 = a - hi.astype(F32)
    mid = r1.astype(BF16)
    lo = (r1 - mid.astype(F32)).astype(BF16)
    return hi, mid, lo


def _split2(a):
    hi = a.astype(BF16)
    lo = (a - hi.astype(F32)).astype(BF16)
    return hi, lo


INPROJ_COLS = 512


def _inproj_kernel(x_ref, g_ref, wrg_ref, wqkv_ref, rg_ref, qkv_ref):
    x = x_ref[...]
    ms = jnp.mean(x * x, axis=-1, keepdims=True)
    h = (x * lax.rsqrt(ms + NORM_EPS) * g_ref[...]).astype(BF16)
    for w_ref, o_ref in ((wrg_ref, rg_ref), (wqkv_ref, qkv_ref)):
        width = w_ref.shape[1]
        for c in range(0, width, INPROJ_COLS):
            cs = slice(c, min(c + INPROJ_COLS, width))
            o_ref[:, cs] = jnp.dot(h, w_ref[:, cs], preferred_element_type=F32).astype(o_ref.dtype)


def _inproj(x2, gain, w_rg, w_qkv):
    T = x2.shape[0]
    tm = 512
    n_rg, n_qkv = w_rg.shape[1], w_qkv.shape[1]
    return pl.pallas_call(
        _inproj_kernel,
        out_shape=[jax.ShapeDtypeStruct((T, n_rg), BF16), jax.ShapeDtypeStruct((T, n_qkv), F32)],
        grid=(T // tm,),
        in_specs=[pl.BlockSpec((tm, D_MODEL), lambda i: (i, 0)),
                  pl.BlockSpec((1, D_MODEL), lambda i: (0, 0)),
                  pl.BlockSpec((D_MODEL, n_rg), lambda i: (0, 0)),
                  pl.BlockSpec((D_MODEL, n_qkv), lambda i: (0, 0))],
        out_specs=[pl.BlockSpec((tm, n_rg), lambda i: (i, 0)),
                   pl.BlockSpec((tm, n_qkv), lambda i: (i, 0))],
        compiler_params=pltpu.CompilerParams(
            dimension_semantics=("arbitrary",), vmem_limit_bytes=VMEM_LIMIT),
        name="inproj",
    )(x2, gain, w_rg, w_qkv)


def _attn_kernel(q0, q1, q2, k0, k1, k2, v0, v1, v2, qg_ref, kg_ref, bias_ref, out_ref,
                 qn_s, kn_s, acc_s, m_s, l_s):
    S = out_ref.shape[1]
    q_refs, k_refs, v_refs = (q0, q1, q2), (k0, k1, k2), (v0, v1, v2)
    lane = lax.broadcasted_iota(jnp.int32, (1, LANES), 1)
    lo = lane < HEAD_DIM
    col = lax.broadcasted_iota(jnp.int32, (1, 2 * ATT_BLOCK), 1)

    same_head = (lax.broadcasted_iota(jnp.int32, (LANES, LANES), 0) < HEAD_DIM) == (
        lax.broadcasted_iota(jnp.int32, (LANES, LANES), 1) < HEAD_DIM)
    ones_bd = jnp.where(same_head, 1.0, 0.0).astype(BF16)

    def pair_norm(x, gain):
        ss = jnp.dot((x * x).astype(BF16), ones_bd, preferred_element_type=F32)
        return x * lax.rsqrt(ss * (1.0 / HEAD_DIM) + NORM_EPS) * gain

    rows_per = 256

    def norm_body(i, c):
        rs = pl.ds(pl.multiple_of(i * rows_per, rows_per), rows_per)
        for g in range(N_ATT_GROUPS):
            qn_s[g, rs, :] = pair_norm(q_refs[g][0, rs, :], qg_ref[g])
            kn_s[g, rs, :] = pair_norm(k_refs[g][0, rs, :], kg_ref[g])
        return c

    lax.fori_loop(0, S // rows_per, norm_body, 0)

    for g, (window, d) in enumerate(ATT_PATTERNS):
        n_blk = (S // d) // ATT_BLOCK
        has_prev = n_blk > 1

        def rows_at(start, d=d):
            return pl.ds(start, ATT_BLOCK) if d == 1 else pl.ds(start, ATT_BLOCK, stride=d)

        def blk_stages(it, g=g, d=d, has_prev=has_prev, rows_at=rows_at):
            curs, q2s, kcats, vcats, pens = [], [], [], [], []
            for b in range(ATT_BATCH):
                idx = it * ATT_BATCH + b
                r = idx % d
                n = idx // d
                cur = rows_at(n * (ATT_BLOCK * d) + r)
                q = qn_s[g, cur, :]
                q2s.append(jnp.concatenate([jnp.where(lo, q, 0.0), jnp.where(lo, 0.0, q)], axis=0).astype(BF16))
                kc = kn_s[g, cur, :].astype(BF16)
                vc = v_refs[g][0, cur, :].astype(BF16)
                if has_prev:
                    prv = rows_at(max(n - 1, 0) * (ATT_BLOCK * d) + r)
                    kc = jnp.concatenate([kn_s[g, prv, :].astype(BF16), kc], axis=0)
                    vc = jnp.concatenate([v_refs[g][0, prv, :].astype(BF16), vc], axis=0)
                    pens.append(jnp.where(col < ATT_BLOCK, NEG_INF if n == 0 else 0.0, 0.0)[None])
                curs.append(cur)
                kcats.append(kc)
                vcats.append(jnp.concatenate([vc, jnp.ones_like(vc)], axis=1))
            q2, kcat, vcat = jnp.stack(q2s), jnp.stack(kcats), jnp.stack(vcats)
            s = lax.dot_general(q2, kcat, (((2,), (2,)), ((0,), (0,))), preferred_element_type=F32)
            yield
            if has_prev:
                s = s + bias_ref[g, 0] + jnp.concatenate(pens, axis=0)
            else:
                s = s + bias_ref[g, 0, :, ATT_BLOCK:]
            m = jnp.max(s, axis=-1, keepdims=True)
            p = jnp.exp(s - m).astype(BF16)
            ol = lax.dot_general(p, vcat, (((2,), (1,)), ((0,), (0,))), preferred_element_type=F32)
            for b in range(ATT_BATCH):
                cur = curs[b]
                acc_s[g, cur, :] = jnp.where(lo, ol[b, :ATT_BLOCK, :LANES], ol[b, ATT_BLOCK:, :LANES])
                l_s[g, cur, :] = jnp.where(lo, ol[b, :ATT_BLOCK, LANES:], ol[b, ATT_BLOCK:, LANES:])
                m_s[g, cur, :] = jnp.where(lo, m[b, :ATT_BLOCK], m[b, ATT_BLOCK:])

        steps = [blk_stages(it) for it in range(d * n_blk // ATT_BATCH)]
        for step in steps[:ATT_AHEAD]:
            next(step)
        for k, step in enumerate(steps):
            if k + ATT_AHEAD < len(steps):
                next(steps[k + ATT_AHEAD])
            for _ in step:
                pass

    def comb_body(i, c):
        rs = pl.ds(pl.multiple_of(i * rows_per, rows_per), rows_per)
        m = jnp.maximum(jnp.maximum(m_s[0, rs, :], m_s[1, rs, :]), m_s[2, rs, :])
        num = jnp.zeros((rows_per, LANES), F32)
        den = jnp.zeros((rows_per, LANES), F32)
        for g in range(N_ATT_GROUPS):
            e = jnp.exp(m_s[g, rs, :] - m)
            num = num + e * acc_s[g, rs, :]
            den = den + e * l_s[g, rs, :]
        out_ref[0, rs, :] = num / den
        return c

    lax.fori_loop(0, S // rows_per, comb_body, 0)


def _attention(p3, q_gain, k_gain, bias):
    B, S, _ = p3.shape
    n_pairs = HEADS_PER_GROUP // 2

    def col_spec(base):
        return [pl.BlockSpec((1, S, LANES),
                             functools.partial(lambda b, sp, off: (b, 0, off + sp), off=base + g * n_pairs))
                for g in range(N_ATT_GROUPS)]

    n_qkv_blocks = ATT_WIDTH // LANES
    in_specs = (col_spec(QKV_COL_BLOCK) + col_spec(QKV_COL_BLOCK + n_qkv_blocks)
                + col_spec(QKV_COL_BLOCK + 2 * n_qkv_blocks)
                + [pl.BlockSpec((N_ATT_GROUPS, 1, LANES), lambda b, sp: (0, 0, 0)),
                   pl.BlockSpec((N_ATT_GROUPS, 1, LANES), lambda b, sp: (0, 0, 0)),
                   pl.BlockSpec((N_ATT_GROUPS, 1, 2 * ATT_BLOCK, 2 * ATT_BLOCK),
                                lambda b, sp: (0, sp, 0, 0))])
    return pl.pallas_call(
        _attn_kernel,
        out_shape=jax.ShapeDtypeStruct((B, S, ATT_OUT_WIDTH), F32),
        grid=(B, n_pairs),
        in_specs=in_specs,
        out_specs=pl.BlockSpec((1, S, LANES), lambda b, sp: (b, 0, sp)),
        scratch_shapes=[pltpu.VMEM((N_ATT_GROUPS, S, LANES), F32) for _ in range(5)],
        compiler_params=pltpu.CompilerParams(
            dimension_semantics=("arbitrary", "arbitrary"), vmem_limit_bytes=VMEM_LIMIT),
        name="dilated_attention",
    )(*([p3] * 9), q_gain, k_gain, bias)


def _t5_causal_bucket(dist):
    max_exact = REL_BUCKETS // 2
    d = jnp.maximum(dist.astype(F32), 1.0)
    large = max_exact + (jnp.log(d / max_exact) / math.log(REL_MAX_DIST / max_exact)
                         * (REL_BUCKETS - max_exact)).astype(jnp.int32)
    large = jnp.minimum(large, REL_BUCKETS - 1)
    return jnp.where(dist < max_exact, dist, large)


def _attention_bias(rel_bias_table):
    W = ATT_BLOCK
    qi = jnp.arange(W)[:, None]
    kj = jnp.arange(2 * W)[None, :]
    rel = qi + W - kj
    valid = (rel >= 0) & (rel <= W)
    per_group = []
    for g, (_, d) in enumerate(ATT_PATTERNS):
        tab = rel_bias_table[:, g * HEADS_PER_GROUP:(g + 1) * HEADS_PER_GROUP]
        bucket = _t5_causal_bucket(jnp.clip(rel, 0, W) * d)
        onehot = (bucket[:, :, None] == jnp.arange(REL_BUCKETS)[None, None, :]).astype(F32)
        bias = jnp.einsum('qkb,bh->qkh', onehot, tab.astype(F32),
                          precision=lax.Precision.HIGHEST)
        bias = jnp.where(valid[:, :, None], bias, NEG_INF)
        per_group.append(jnp.transpose(bias, (2, 0, 1)).reshape(2, 2 * W, 2 * W))
    return jnp.stack(per_group, axis=0)


def _rwkv_prep_kernel(p_ref, mu_ref, w0_ref, wup_ref, a0_ref, aup_ref, gup_ref, kk_ref, ka_ref, rk_ref,
                      at_o, rt_o, bt_o, kt_o, bw_o, kw_o, v_o, bonus_o, g_o, wend_o, carry):
    tq = p_ref.shape[1]
    C = RWKV_WIDTH
    L = WKV_CHUNK

    @pl.when(pl.program_id(1) == 0)
    def _():
        carry[...] = jnp.zeros_like(carry)

    p = p_ref[0].astype(F32)
    row = lax.broadcasted_iota(jnp.int32, (tq, 1), 0)
    prev = jnp.where(row == 0, carry[...], pltpu.roll(p, 1, axis=0))
    carry[...] = p[tq - 1:tq, :]
    pm = p + (prev - p) * mu_ref[...]
    r, k, v = pm[:, 0:C], pm[:, C:2 * C], pm[:, 2 * C:3 * C]
    xw = pm[:, 3 * C:3 * C + LANES]
    xa = pm[:, 3 * C + LANES:3 * C + 2 * LANES]
    xg = pm[:, 3 * C + 2 * LANES:]

    z = w0_ref[...] + _dot(jnp.tanh(xw), wup_ref[...])
    softplus_neg = jnp.maximum(-z, 0.0) + jnp.log(1.0 + jnp.exp(-jnp.abs(z)))
    lw = -jnp.exp(-softplus_neg - 0.5)
    a = _sigmoid(a0_ref[...] + _dot(xa, aup_ref[...]))
    g = _dot(_sigmoid(xg), gup_ref[...])

    hr = jnp.right_shift(lax.broadcasted_iota(jnp.int32, (C, C), 0), 6)
    hc = jnp.right_shift(lax.broadcasted_iota(jnp.int32, (C, C), 1), 6)
    ones_bd = jnp.where(hr == hc, 1.0, 0.0).astype(BF16)

    def head_sums(x):
        return jnp.dot(x.astype(BF16), ones_bd, preferred_element_type=F32)

    k2 = k * (1.0 + (a - 1.0) * ka_ref[...])
    kk = k * kk_ref[...]
    kk = kk / jnp.maximum(jnp.sqrt(head_sums(kk * kk)), 1e-12)
    bonus = head_sums(r * k2 * rk_ref[...]) * v
    b = kk * a

    half = 256
    ri = lax.broadcasted_iota(jnp.int32, (half, half), 0)
    ci = lax.broadcasted_iota(jnp.int32, (half, half), 1)
    tri = jnp.where((jnp.right_shift(ri, 6) == jnp.right_shift(ci, 6)) & (ri >= ci), 1.0, 0.0).astype(BF16)
    cums = []
    for s in range(tq // half):
        hi, mid, lo3 = _split3(lw[s * half:(s + 1) * half, :])
        cums.append(jnp.dot(tri, hi, preferred_element_type=F32)
                    + (jnp.dot(tri, mid, preferred_element_type=F32)
                       + jnp.dot(tri, lo3, preferred_element_type=F32)))
    cum = jnp.concatenate(cums, axis=0)
    n_chunks = tq // L
    ends = [cum[c * L + L - 1:c * L + L, :] for c in range(n_chunks)]
    tot = jnp.concatenate([jnp.broadcast_to(e, (L, C)) for e in ends], axis=0)
    wend = jnp.exp(jnp.concatenate(ends, axis=0))

    e_pos = jnp.exp(cum)
    e_neg = jnp.exp(-cum)
    e_rem = jnp.exp(tot - cum)
    at = -kk * jnp.exp(cum - lw)
    rt = r * e_pos
    bt = b * e_neg
    kt = k2 * e_neg
    bw = b * e_rem
    kw = k2 * e_rem
    at_o[0] = at.astype(BF16)
    rt_o[0] = rt.astype(BF16)
    bt_o[0] = bt.astype(BF16)
    kt_o[0] = kt.astype(BF16)
    bw_o[0] = bw.astype(BF16)
    kw_o[0] = kw.astype(BF16)
    v_o[0] = v.astype(BF16)
    bonus_o[0] = bonus
    g_o[0] = g
    wend_o[0] = wend


def _rwkv_prep(p3, mu, w0, wup, a0, aup, gup, k_k, k_a, r_k):
    B, S, _ = p3.shape
    tq = 512
    C = RWKV_WIDTH
    tm_shape = lambda dt: jax.ShapeDtypeStruct((B, S, C), dt)
    tm_spec = pl.BlockSpec((1, tq, C), lambda b, j: (b, j, 0))
    full = lambda shape: pl.BlockSpec(shape, lambda b, j: (0,) * len(shape))
    return pl.pallas_call(
        _rwkv_prep_kernel,
        out_shape=[tm_shape(BF16)] * 7 + [tm_shape(F32)] * 2 + [jax.ShapeDtypeStruct((B, S // WKV_CHUNK, C), F32)],
        grid=(B, S // tq),
        in_specs=[pl.BlockSpec((1, tq, RWKV_PAD_WIDTH), lambda b, j: (b, j, 0)),
                  full((1, RWKV_PAD_WIDTH)), full((1, C)), full((LANES, C)), full((1, C)),
                  full((LANES, C)), full((2 * LANES, C)), full((1, C)), full((1, C)), full((1, C))],
        out_specs=[tm_spec] * 9 + [pl.BlockSpec((1, tq // WKV_CHUNK, C), lambda b, j: (b, j, 0))],
        scratch_shapes=[pltpu.VMEM((1, RWKV_PAD_WIDTH), F32)],
        compiler_params=pltpu.CompilerParams(
            dimension_semantics=("arbitrary", "arbitrary"), vmem_limit_bytes=VMEM_LIMIT),
        name="rwkv_prep",
    )(p3, mu, w0, wup, a0, aup, gup, k_k, k_a, r_k)


WKV_GROUP = 4
WKV_GROUP_WIDTH = WKV_GROUP * RWKV_HEAD_DIM


def _wkv_kernel(at_ref, rt_ref, bt_ref, kt_ref, bw_ref, kw_ref, v_ref, bonus_ref, g_ref, wend_ref,
                lnw_ref, lnb_ref, o_ref, state, m_s, u0_s, o0_s, arb_s, p_s, c_s, sh_s):
    N, C, GW = RWKV_HEAD_DIM, WKV_CHUNK, WKV_GROUP_WIDTH
    n_groups = RWKV_WIDTH // GW
    n_sub = at_ref.shape[1] // C
    j = pl.program_id(1)

    @pl.when(j == 0)
    def _():
        state[...] = jnp.zeros_like(state)

    def iota(shape, dim):
        return lax.broadcasted_iota(jnp.int32, shape, dim)

    same_head = jnp.right_shift(iota((1, GW, GW), 1), 6) == jnp.right_shift(iota((1, GW, GW), 2), 6)
    ones_bd = jnp.where(same_head, 1.0, 0.0).astype(BF16)
    g_row = iota((1, 2 * C, 2 * GW), 1)
    g_col = jnp.bitwise_and(iota((1, 2 * C, 2 * GW), 2), N - 1)
    band = ((g_row < C) & (g_row > g_col)) | ((g_row >= C) & ((g_row - C) >= g_col))
    eye = jnp.where(iota((1, C, GW), 1) == jnp.bitwise_and(iota((1, C, GW), 2), N - 1), 1.0, 0.0)

    def bmm(a, b, dims):
        return lax.dot_general(a.astype(BF16), b.astype(BF16), (dims, ((0,), (0,))),
                               preferred_element_type=F32)

    nn, nt, tn = ((2,), (1,)), ((2,), (2,)), ((1,), (1,))

    def bd(x):
        xb = x.astype(BF16)
        return jnp.where(same_head, jnp.concatenate([xb] * WKV_GROUP, axis=1), jnp.zeros((), BF16))

    def narrow(x):
        return sum(x[:, h * N:(h + 1) * N] for h in range(WKV_GROUP))

    span = 4
    rows = span * C

    def stack(x):
        x3 = x.reshape(span, C, n_groups * GW)
        return jnp.concatenate([x3[:, :, q * GW:(q + 1) * GW] for q in range(n_groups)], axis=0)

    def unstack(y):
        return jnp.concatenate([y[q * span:(q + 1) * span].reshape(rows, GW) for q in range(n_groups)],
                               axis=1)

    def chunk_terms(i):
        rs = slice(i * rows, (i + 1) * rows)
        at = stack(at_ref[0, rs, :])
        ar = jnp.concatenate([at, stack(rt_ref[0, rs, :])], axis=1)
        bk_bd = jnp.concatenate([bd(stack(bt_ref[0, rs, :])), bd(stack(kt_ref[0, rs, :]))], axis=1)
        gm = jnp.where(band, bmm(ar, bk_bd, nt), 0.0)
        yield
        a_ab = gm[:, :C, :GW]
        inv = eye + a_ab
        pw = bmm(a_ab, bd(a_ab), nn)
        yield
        for _ in range(int(math.log2(C)) - 2):
            prod = bmm(jnp.concatenate([inv, pw], axis=1), bd(pw), nn)
            yield
            inv = inv + prod[:, :C]
            pw = prod[:, C:]
        inv = inv + bmm(inv, bd(pw), nn)
        yield
        v = stack(v_ref[0, rs, :])
        kv = bmm(gm[:, :, GW:], bd(v), nn)
        yield
        mu0 = bmm(inv, jnp.concatenate([bd(at), bd(kv[:, :C])], axis=2), nn)
        yield
        m = mu0[:, :, :GW].astype(BF16)
        u0 = mu0[:, :, GW:]
        bw = stack(bw_ref[0, rs, :])
        low_rank = jnp.where(same_head, bmm(m, bw, tn), 0.0)
        const = jnp.where(same_head, bmm(jnp.concatenate([u0.astype(BF16), v], axis=1),
                                         jnp.concatenate([bw, stack(kw_ref[0, rs, :])], axis=1), tn), 0.0)
        m_s[rs, :] = unstack(m)
        u0_s[rs, :] = unstack(u0)
        o0_s[rs, :] = unstack(kv[:, C:])
        arb_s[rs, :] = unstack(gm[:, C:, :GW]).astype(BF16)
        for q in range(n_groups):
            p_s[i * span:(i + 1) * span, q] = low_rank[q * span:(q + 1) * span].astype(BF16)
            c_s[i * span:(i + 1) * span, q] = narrow(const[q * span:(q + 1) * span])

    def groups(x):
        return jnp.stack([x[:, q * GW:(q + 1) * GW] for q in range(n_groups)], axis=0)

    def chunk_state(i):
        s0 = state[...]
        sh_s[i] = s0.astype(BF16)
        w_end = groups(wend_ref[0, pl.ds(j * n_sub + i, 1), :])
        state[...] = s0 * w_end + bmm(s0, p_s[i], nn) + c_s[i]

    n_prob = n_groups * span

    ones_p = jnp.broadcast_to(ones_bd, (n_prob, GW, GW))

    def head_sums(x):
        return bmm(x, ones_p, nn)

    def chunk_outputs(i):
        rs = slice(i * rows, (i + 1) * rows)
        s0 = bd(jnp.concatenate([sh_s[i * span:(i + 1) * span, q] for q in range(n_groups)], axis=0))
        mr_s = bmm(jnp.concatenate([stack(m_s[rs, :]), stack(rt_ref[0, rs, :])], axis=1), s0, nt)
        yield
        u = mr_s[:, :C] + stack(u0_s[rs, :])
        o = mr_s[:, C:] + stack(o0_s[rs, :]) + bmm(stack(arb_s[rs, :]), bd(u), nn)
        yield
        mean = head_sums(o) * (1.0 / N)
        yield
        dlt = o - mean
        on = unstack(dlt * lax.rsqrt(head_sums(dlt * dlt) * (1.0 / N) + GN_EPS))
        on = on * lnw_ref[...] + lnb_ref[...]
        o_ref[0, rs, :] = (on + bonus_ref[0, rs, :]) * g_ref[0, rs, :]

    def run(stages, between=()):
        pending = iter(between)
        for _ in stages:
            step = next(pending, None)
            if step is not None:
                step()
        for step in pending:
            step()

    n_spans = n_sub // span
    assert n_spans >= 2
    states = lambda i: [functools.partial(chunk_state, k) for k in range(i * span, (i + 1) * span)]
    run(chunk_terms(0))
    for i in range(1, n_spans):
        run(chunk_terms(i), states(i - 1))
    run(chunk_outputs(0), states(n_spans - 1))
    for i in range(1, n_spans):
        run(chunk_outputs(i))


def _wkv(at, rt, bt, kt, bw, kw, v, bonus, g, wend, ln_w, ln_b):
    B, S, C = at.shape
    tc = 512
    tm_spec = pl.BlockSpec((1, tc, C), lambda bb, j: (bb, j, 0))
    par_spec = pl.BlockSpec((1, C), lambda bb, j: (0, 0))
    blocks = (C // WKV_GROUP_WIDTH, WKV_GROUP_WIDTH, WKV_GROUP_WIDTH)
    wide = (C // WKV_GROUP_WIDTH, RWKV_HEAD_DIM, WKV_GROUP_WIDTH)
    return pl.pallas_call(
        _wkv_kernel,
        out_shape=jax.ShapeDtypeStruct((B, S, C), F32),
        grid=(B, S // tc),
        in_specs=([tm_spec] * 9 + [pl.BlockSpec((1, S // WKV_CHUNK, C), lambda bb, j: (bb, 0, 0))]
                  + [par_spec] * 2),
        out_specs=tm_spec,
        scratch_shapes=[pltpu.VMEM(wide, F32),
                        pltpu.VMEM((tc, C), BF16), pltpu.VMEM((tc, C), F32), pltpu.VMEM((tc, C), F32),
                        pltpu.VMEM((tc, C), BF16),
                        pltpu.VMEM((tc // WKV_CHUNK,) + blocks, BF16), pltpu.VMEM((tc // WKV_CHUNK,) + wide, F32),
                        pltpu.VMEM((tc // WKV_CHUNK,) + wide, BF16)],
        compiler_params=pltpu.CompilerParams(
            dimension_semantics=("arbitrary", "arbitrary"), vmem_limit_bytes=VMEM_LIMIT),
        name="wkv7",
    )(at, rt, bt, kt, bw, kw, v, bonus, g, wend, ln_w, ln_b)


MERGE_SPLIT = 2
INFO_FIELDS = 8


def _merge_kernel(att_ref, o_ref, gate_ref, x_ref, wa_ref, wr_ref, wo_ref, g2_ref, wrt_ref, brt_ref,
                  x1_ref, h2_ref, lg_ref, cnt_ref, lt_ref):
    tm = x_ref.shape[1]
    ts = tm // MERGE_SPLIT
    parts = [slice(h * ts, (h + 1) * ts) for h in range(MERGE_SPLIT)]
    lane = lax.broadcasted_iota(jnp.int32, (1, ROUTER_PAD), 1)

    @pl.when((pl.program_id(0) == 0) & (pl.program_id(1) == 0))
    def _():
        cnt_ref[...] = jnp.zeros_like(cnt_ref)

    ys = [(_dot(att_ref[0, rs, :], wa_ref[...]), _dot(o_ref[0, rs, :], wr_ref[...])) for rs in parts]

    mixed = []
    for rs, (y_att, y_rwkv) in zip(parts, ys):
        gates = _sigmoid(gate_ref[0, rs, :].astype(F32))
        mixed.append(gates[:, :D_MODEL] * y_att + gates[:, D_MODEL:] * y_rwkv)

    x1s = []
    for rs, mx in zip(parts, mixed):
        x1 = x_ref[0, rs, :] + _dot(mx, wo_ref[...])
        x1_ref[0, rs, :] = x1
        x1s.append(x1)

    splits = []
    for rs, x1 in zip(parts, x1s):
        ms = jnp.mean(x1 * x1, axis=-1, keepdims=True)
        h2 = x1 * lax.rsqrt(ms + NORM_EPS) * g2_ref[...]
        hh, hl = _split2(h2)
        half = D_MODEL // 2
        rounded = hh.astype(F32)
        w_hi = pltpu.bitcast(rounded[:, :half], jnp.uint32)
        w_lo = pltpu.bitcast(rounded[:, half:], jnp.uint32)
        h2_ref[0, rs, :] = w_hi | jnp.right_shift(w_lo, jnp.uint32(16))
        splits.append((hh, hl))

    logits = []
    for hh, hl in splits:
        both = jnp.dot(hh, wrt_ref[...], preferred_element_type=F32)
        cross = jnp.dot(hl, wrt_ref[:, :ROUTER_PAD], preferred_element_type=F32)
        logits.append(both[:, :ROUTER_PAD] + (both[:, ROUTER_PAD:] + cross) + brt_ref[...])

    ti = lax.broadcasted_iota(jnp.int32, (ts, ts), 0)
    tj = lax.broadcasted_iota(jnp.int32, (ts, ts), 1)
    before = jnp.where(tj < ti, 1.0, 0.0).astype(BF16)
    for rs, lg in zip(parts, logits):
        gmask = lane < N_EXPERT_GROUPS
        gl = jnp.where(gmask, lg, -jnp.inf)
        gmax = jnp.max(gl, axis=-1, keepdims=True)
        grp_idx = jnp.min(jnp.where(gl == gmax, lane, ROUTER_PAD), axis=-1, keepdims=True)
        grp_p = 1.0 / jnp.sum(jnp.where(gmask, jnp.exp(lg - gmax), 0.0), axis=-1, keepdims=True)
        e_lane = lane - N_EXPERT_GROUPS
        emask = ((e_lane >= 0) & (e_lane < N_EXPERTS)
                 & (jnp.right_shift(e_lane, int(math.log2(EXPERTS_PER_GROUP))) == grp_idx))
        el = jnp.where(emask, lg, -jnp.inf)
        m1 = jnp.max(el, axis=-1, keepdims=True)
        i1 = jnp.min(jnp.where(el == m1, lane, ROUTER_PAD), axis=-1, keepdims=True)
        el2 = jnp.where(lane == i1, -jnp.inf, el)
        m2 = jnp.max(el2, axis=-1, keepdims=True)
        i2 = jnp.min(jnp.where(el2 == m2, lane, ROUTER_PAD), axis=-1, keepdims=True)
        e2 = jnp.exp(m2 - m1)
        gate1 = grp_p / (1.0 + e2)
        gate2 = gate1 * e2

        oh = jnp.where((lane == i1) | (lane == i2), 1.0, 0.0)
        pref = jnp.dot(before, oh.astype(BF16), preferred_element_type=F32) + cnt_ref[...]
        rank1 = jnp.sum(jnp.where(lane == i1, pref, 0.0), axis=-1, keepdims=True)
        rank2 = jnp.sum(jnp.where(lane == i2, pref, 0.0), axis=-1, keepdims=True)
        cnt_ref[...] = cnt_ref[...] + jnp.sum(oh, axis=0, keepdims=True)
        info = jnp.where(lane == 0, (i1 - N_EXPERT_GROUPS).astype(F32), 0.0)
        info = jnp.where(lane == 1, (i2 - N_EXPERT_GROUPS).astype(F32), info)
        info = jnp.where(lane == 2, rank1, info)
        info = jnp.where(lane == 3, rank2, info)
        info = jnp.where(lane == 4, gate1, info)
        info = jnp.where(lane == 5, gate2, info)
        lg_ref[0, rs, :] = info
        lt_ref[:, rs] = info.T[:INFO_FIELDS, :]


def _merge(att, o_g, p3, x, w_att_out, w_rwkv_out, w_out, gain2, w_router, b_router):
    B, S, D = x.shape
    tm = 512
    full = lambda shape: pl.BlockSpec(shape, lambda b, j: (0,) * len(shape))
    return pl.pallas_call(
        _merge_kernel,
        out_shape=[jax.ShapeDtypeStruct((B, S, D), F32), jax.ShapeDtypeStruct((B, S, D // 2), jnp.uint32),
                   jax.ShapeDtypeStruct((B, S, ROUTER_PAD), F32),
                   jax.ShapeDtypeStruct((1, ROUTER_PAD), F32),
                   jax.ShapeDtypeStruct((INFO_FIELDS, B * S), F32)],
        grid=(B, S // tm),
        in_specs=[pl.BlockSpec((1, tm, ATT_OUT_WIDTH), lambda b, j: (b, j, 0)),
                  pl.BlockSpec((1, tm, RWKV_WIDTH), lambda b, j: (b, j, 0)),
                  pl.BlockSpec((1, tm, 2 * D_MODEL), lambda b, j: (b, j, 1)),
                  pl.BlockSpec((1, tm, D), lambda b, j: (b, j, 0)),
                  full((ATT_OUT_WIDTH, D)), full((RWKV_WIDTH, D)), full((D, D)),
                  full((1, D)), full((D, 2 * ROUTER_PAD)), full((1, ROUTER_PAD))],
        out_specs=[pl.BlockSpec((1, tm, D), lambda b, j: (b, j, 0)),
                   pl.BlockSpec((1, tm, D // 2), lambda b, j: (b, j, 0)),
                   pl.BlockSpec((1, tm, ROUTER_PAD), lambda b, j: (b, j, 0)),
                   pl.BlockSpec((1, ROUTER_PAD), lambda b, j: (0, 0)),
                   pl.BlockSpec((INFO_FIELDS, tm), lambda b, j: (0, b * (S // tm) + j))],
        compiler_params=pltpu.CompilerParams(
            dimension_semantics=("arbitrary", "arbitrary"), vmem_limit_bytes=VMEM_LIMIT),
        name="merge_outproj_router",
    )(att, o_g, p3, x, w_att_out, w_rwkv_out, w_out, gain2, w_router, b_router)


def _expert_kernel(be_ref, nb_ref, x_ref, wg_ref, wu_ref, wd_ref, y_ref, wgu_s, wd_s):
    i = pl.program_id(0)

    @pl.when((i == 0) | (be_ref[i] != be_ref[jnp.maximum(i - 1, 0)]))
    def _():
        wgu_s[:, :D_EXPERT] = wg_ref[0].astype(BF16)
        wgu_s[:, D_EXPERT:] = wu_ref[0].astype(BF16)
        wd_s[...] = wd_ref[0].astype(BF16)

    @pl.when(i < nb_ref[0])
    def _():
        words = x_ref[...]
        x_a = pltpu.bitcast(words & jnp.uint32(0xFFFF0000), F32)
        x_b = pltpu.bitcast(jnp.left_shift(words, jnp.uint32(16)), F32)
        xb = jnp.concatenate([x_a, x_b], axis=-1).astype(BF16)
        hw = D_EXPERT // 2
        hgu = [(jnp.dot(xb, wgu_s[:, c * hw:(c + 1) * hw], preferred_element_type=F32),
                jnp.dot(xb, wgu_s[:, D_EXPERT + c * hw:D_EXPERT + (c + 1) * hw], preferred_element_type=F32))
               for c in range(2)]
        y = None
        for c, (hg, hu) in enumerate(hgu):
            hid = (hg * _sigmoid(hg) * hu).astype(BF16)
            part = jnp.dot(hid, wd_s[c * hw:(c + 1) * hw, :], preferred_element_type=F32)
            y = part if y is None else y + part
        y_ref[...] = y.astype(y_ref.dtype)

    @pl.when(i >= nb_ref[0])
    def _():
        y_ref[...] = jnp.zeros_like(y_ref)


def _experts(blk_expert, n_used, xin, w_gate, w_up, w_down):
    n_rows = xin.shape[0]
    D = D_MODEL
    n_blocks = n_rows // MOE_ROWS
    return pl.pallas_call(
        _expert_kernel,
        out_shape=jax.ShapeDtypeStruct((n_rows, D), BF16),
        grid_spec=pltpu.PrefetchScalarGridSpec(
            num_scalar_prefetch=2,
            grid=(n_blocks,),
            scratch_shapes=[pltpu.VMEM((D, 2 * D_EXPERT), BF16), pltpu.VMEM((D_EXPERT, D), BF16)],
            in_specs=[pl.BlockSpec((MOE_ROWS, D // 2), lambda i, be, nb: (i, 0)),
                      pl.BlockSpec((1, D, D_EXPERT), lambda i, be, nb: (be[i], 0, 0)),
                      pl.BlockSpec((1, D, D_EXPERT), lambda i, be, nb: (be[i], 0, 0)),
                      pl.BlockSpec((1, D_EXPERT, D), lambda i, be, nb: (be[i], 0, 0))],
            out_specs=pl.BlockSpec((MOE_ROWS, D), lambda i, be, nb: (i, 0))),
        compiler_params=pltpu.CompilerParams(
            dimension_semantics=("arbitrary",), vmem_limit_bytes=VMEM_LIMIT),
        name="moe_experts",
    )(blk_expert, n_used, xin, w_gate, w_up, w_down)


def _row_tokens(dests, counts, pcounts, pstarts, pends, n_rows, T):
    dest_flat = jnp.concatenate(dests)
    A = dest_flat.shape[0]
    n_pad = n_rows - A
    seg_count = jnp.concatenate([pcounts - counts, n_rows - pends[-1:]])
    seg_first = jnp.concatenate([pstarts + counts, pends[-1:]])
    seg_end = jnp.cumsum(seg_count)
    i = jnp.arange(n_pad, dtype=jnp.int32)
    seg = jnp.sum(i[None, :] >= seg_end[:, None], axis=0)
    in_seg = seg[None, :] == jnp.arange(seg_end.shape[0], dtype=jnp.int32)[:, None]
    shift = (seg_first - (seg_end - seg_count)).astype(jnp.int32)
    pad_rows = i + jnp.sum(jnp.where(in_seg, shift[:, None], 0), axis=0)
    rows = jnp.concatenate([dest_flat, pad_rows])
    toks = jnp.concatenate([jnp.arange(T, dtype=jnp.int32)] * len(dests) + [i % T])
    tok_bits = max(T - 1, 1).bit_length()
    assert n_rows << tok_bits <= 1 << 32
    packed = jnp.sort((rows.astype(jnp.uint32) << tok_bits) | toks.astype(jnp.uint32))
    return (packed & jnp.uint32((1 << tok_bits) - 1)).astype(jnp.int32)


def _row_layout(info, counts, T):
    A = T * TOP_K
    counts = counts.astype(jnp.int32)
    pcounts = (counts + MOE_ROWS - 1) // MOE_ROWS * MOE_ROWS
    pends = jnp.cumsum(pcounts)
    pstarts = pends - pcounts
    experts = jnp.arange(N_EXPERTS, dtype=F32)[:, None]
    seg_start = [jnp.dot(pstarts.astype(F32), (info[k][None, :] == experts).astype(F32),
                         precision=lax.Precision.HIGHEST) for k in range(TOP_K)]
    dest = [(seg_start[k] + info[TOP_K + k]).astype(jnp.int32) for k in range(TOP_K)]
    n_blocks = -(-A // MOE_ROWS) + N_EXPERTS
    n_rows = n_blocks * MOE_ROWS
    row_tok = _row_tokens(dest, counts, pcounts, pstarts, pends, n_rows, T)
    blk_start = jnp.arange(n_blocks, dtype=jnp.int32) * MOE_ROWS
    blk_expert = jnp.minimum(jnp.sum(blk_start[:, None] >= pends[None, :], axis=1), N_EXPERTS - 1)
    n_used = (pends[N_EXPERTS - 1] // MOE_ROWS).astype(jnp.int32)[None]
    return row_tok, blk_expert.astype(jnp.int32), n_used, dest


def _combine_kernel(x1_ref, y0_ref, y1_ref, info_ref, o_ref):
    gate = 2 * TOP_K
    g0 = info_ref[:, gate:gate + 1]
    g1 = info_ref[:, gate + 1:gate + 2]
    o_ref[...] = x1_ref[...] + (g0 * y0_ref[...].astype(F32) + g1 * y1_ref[...].astype(F32))


def _combine(x1, y, info):
    T, D = x1.shape
    tm = 1024
    row_spec = pl.BlockSpec((tm, D), lambda i: (i, 0))
    second = pl.BlockSpec((tm, D), lambda i: (i + T // tm, 0))
    return pl.pallas_call(
        _combine_kernel,
        out_shape=jax.ShapeDtypeStruct((T, D), F32),
        grid=(T // tm,),
        in_specs=[row_spec, row_spec, second, pl.BlockSpec((tm, ROUTER_PAD), lambda i: (i, 0))],
        out_specs=row_spec,
        compiler_params=pltpu.CompilerParams(
            dimension_semantics=("arbitrary",), vmem_limit_bytes=VMEM_LIMIT),
        name="moe_combine",
    )(x1, y, y, info)


def _pad_rows(w, rows):
    return jnp.pad(w, ((0, rows - w.shape[0]), (0, 0)))


def kernel(x, norm1_gain, w_in, q_norm_gain, k_norm_gain, rel_bias_table, w_att_out, rwkv_shift_mu,
           rwkv_w0, rwkv_w_up, rwkv_a0, rwkv_a_up, rwkv_g_up, rwkv_k_k, rwkv_k_a, rwkv_r_k,
           rwkv_ln_w, rwkv_ln_b, w_rwkv_out, w_out, norm2_gain, w_group_router, b_group_router,
           w_expert_router, b_expert_router, w_expert_gate, w_expert_up, w_expert_down):
    B, S, D = x.shape
    T = B * S
    C = RWKV_WIDTH
    bias = _attention_bias(rel_bias_table)
    for l in range(norm1_gain.shape[0]):
        wi = w_in[l]
        w_qkv = wi[:, :3 * ATT_WIDTH]
        w_rw = wi[:, 3 * ATT_WIDTH:3 * ATT_WIDTH + RWKV_SHIFT_WIDTH]
        w_gt = wi[:, 3 * ATT_WIDTH + RWKV_SHIFT_WIDTH:]
        pad_cols = lambda w, n: jnp.pad(w, ((0, 0), (0, n - w.shape[1])))
        o_w, o_a, o_g = 3 * C, 3 * C + DECAY_LORA, 3 * C + DECAY_LORA + ICLR_LORA

        def rw_layout(w):
            return jnp.concatenate([w[:, :o_w], pad_cols(w[:, o_w:o_a], LANES),
                                    pad_cols(w[:, o_a:o_g], LANES), pad_cols(w[:, o_g:], 2 * LANES)], axis=1)

        w_rg = jnp.concatenate([rw_layout(w_rw), w_gt], axis=1).astype(BF16)
        mu = rw_layout(rwkv_shift_mu[l][None, :])

        p_rg, p_qkv = _inproj(x.reshape(T, D), norm1_gain[l][None, :], w_rg, w_qkv.astype(BF16))
        p3 = p_rg.reshape(B, S, RWKV_PAD_WIDTH + 2 * D)

        tile2 = lambda gmat: jnp.tile(gmat, (1, 2))[:, None, :]
        att = _attention(p_qkv.reshape(B, S, 3 * ATT_WIDTH), tile2(q_norm_gain[l]) * (HEAD_DIM ** -0.5),
                         tile2(k_norm_gain[l]),
                         bias)

        row = lambda v: v[None, :]
        prep = _rwkv_prep(
            p3, mu, row(rwkv_w0[l]), _pad_rows(rwkv_w_up[l], LANES).astype(BF16), row(rwkv_a0[l]),
            _pad_rows(rwkv_a_up[l], LANES).astype(BF16), _pad_rows(rwkv_g_up[l], 2 * LANES).astype(BF16),
            row(rwkv_k_k[l]), row(rwkv_k_a[l]), rwkv_r_k[l].reshape(1, C))
        o_g = _wkv(*prep, row(rwkv_ln_w[l]), row(rwkv_ln_b[l]))

        w_router = jnp.concatenate([w_group_router[l], w_expert_router[l]], axis=1)
        w_router = jnp.pad(w_router, ((0, 0), (0, ROUTER_PAD - w_router.shape[1])))
        wr_hi = w_router.astype(BF16)
        wr_lo = (w_router - wr_hi.astype(F32)).astype(BF16)
        b_router = jnp.concatenate([b_group_router[l], b_expert_router[l]])
        b_router = jnp.pad(b_router, (0, ROUTER_PAD - b_router.shape[0]))[None, :]
        x1, h2, info, counts, info_t = _merge(
            att, o_g, p3, x, w_att_out[l].astype(BF16),
            w_rwkv_out[l].astype(BF16), w_out[l].astype(BF16),
            norm2_gain[l][None, :], jnp.concatenate([wr_hi, wr_lo], axis=1), b_router)

        row_tok, blk_expert, n_used, dest = _row_layout(
            info_t, counts[0, N_EXPERT_GROUPS:N_EXPERT_GROUPS + N_EXPERTS], T)
        xin = h2.reshape(T, D // 2)[row_tok]
        yb = _experts(blk_expert, n_used, xin, w_expert_gate[l], w_expert_up[l], w_expert_down[l])
        x = _combine(x1.reshape(T, D), yb[jnp.concatenate(dest)],
                     info.reshape(T, ROUTER_PAD)).reshape(B, S, D)
    return x
```

```python
import functools
import math

import jax
import jax.numpy as jnp
from jax import lax
from jax.experimental import pallas as pl
from jax.experimental.pallas import tpu as pltpu

F32 = jnp.float32
BF16 = jnp.bfloat16

D_MODEL = 1024
ATT_PATTERNS = ((128, 1), (512, 4), (2048, 16))
N_ATT_GROUPS = 3
HEADS_PER_GROUP = 4
HEAD_DIM = 64
ATT_WIDTH = 768
ATT_OUT_WIDTH = 256
REL_BUCKETS = 32
REL_MAX_DIST = 2048
RWKV_HEAD_DIM = 64
RWKV_WIDTH = 512
DECAY_LORA = 64
ICLR_LORA = 64
RWKV_SHIFT_WIDTH = 1824
GN_EPS = 64e-5
N_EXPERT_GROUPS = 4
EXPERTS_PER_GROUP = 8
N_EXPERTS = 32
TOP_K = 2
D_EXPERT = 512
NORM_EPS = 1e-6
NEG_INF = -1e30

LANES = 128
ATT_BLOCK = 128
ATT_BATCH = 8
ATT_AHEAD = 1
WKV_CHUNK = 64
RWKV_PAD_WIDTH = 2048
QKV_COL_BLOCK = 0
ROUTER_PAD = 128
MOE_ROWS = 512
VMEM_LIMIT = 56 * 1024 * 1024


def _sigmoid(x):
    return 0.5 * jnp.tanh(0.5 * x) + 0.5


def _dot(a, b):
    return jnp.dot(a.astype(BF16), b.astype(BF16), preferred_element_type=F32)


def _split3(a):
    hi = a.astype(BF16)
    r1 = a - hi.astype(F32)
    mid = r1.astype(BF16)
    lo = (r1 - mid.astype(F32)).astype(BF16)
    return hi, mid, lo


def _split2(a):
    hi = a.astype(BF16)
    lo = (a - hi.astype(F32)).astype(BF16)
    return hi, lo


INPROJ_COLS = 512


def _inproj_kernel(x_ref, g_ref, wrg_ref, wqkv_ref, rg_ref, qkv_ref):
    x = x_ref[...]
    ms = jnp.mean(x * x, axis=-1, keepdims=True)
    h = (x * lax.rsqrt(ms + NORM_EPS) * g_ref[...]).astype(BF16)
    for w_ref, o_ref in ((wrg_ref, rg_ref), (wqkv_ref, qkv_ref)):
        width = w_ref.shape[1]
        for c in range(0, width, INPROJ_COLS):
            cs = slice(c, min(c + INPROJ_COLS, width))
            o_ref[:, cs] = jnp.dot(h, w_ref[:, cs], preferred_element_type=F32).astype(o_ref.dtype)


def _inproj(x2, gain, w_rg, w_qkv):
    T = x2.shape[0]
    tm = 512
    n_rg, n_qkv = w_rg.shape[1], w_qkv.shape[1]
    return pl.pallas_call(
        _inproj_kernel,
        out_shape=[jax.ShapeDtypeStruct((T, n_rg), BF16), jax.ShapeDtypeStruct((T, n_qkv), F32)],
        grid=(T // tm,),
        in_specs=[pl.BlockSpec((tm, D_MODEL), lambda i: (i, 0)),
                  pl.BlockSpec((1, D_MODEL), lambda i: (0, 0)),
                  pl.BlockSpec((D_MODEL, n_rg), lambda i: (0, 0)),
                  pl.BlockSpec((D_MODEL, n_qkv), lambda i: (0, 0))],
        out_specs=[pl.BlockSpec((tm, n_rg), lambda i: (i, 0)),
                   pl.BlockSpec((tm, n_qkv), lambda i: (i, 0))],
        compiler_params=pltpu.CompilerParams(
            dimension_semantics=("arbitrary",), vmem_limit_bytes=VMEM_LIMIT),
        name="inproj",
    )(x2, gain, w_rg, w_qkv)


def _attn_kernel(q0, q1, q2, k0, k1, k2, v0, v1, v2, qg_ref, kg_ref, bias_ref, out_ref,
                 qn_s, kn_s, acc_s, m_s, l_s):
    S = out_ref.shape[1]
    q_refs, k_refs, v_refs = (q0, q1, q2), (k0, k1, k2), (v0, v1, v2)
    lane = lax.broadcasted_iota(jnp.int32, (1, LANES), 1)
    lo = lane < HEAD_DIM
    col = lax.broadcasted_iota(jnp.int32, (1, 2 * ATT_BLOCK), 1)

    same_head = (lax.broadcasted_iota(jnp.int32, (LANES, LANES), 0) < HEAD_DIM) == (
        lax.broadcasted_iota(jnp.int32, (LANES, LANES), 1) < HEAD_DIM)
    ones_bd = jnp.where(same_head, 1.0, 0.0).astype(BF16)

    def pair_norm(x, gain):
        ss = jnp.dot((x * x).astype(BF16), ones_bd, preferred_element_type=F32)
        return x * lax.rsqrt(ss * (1.0 / HEAD_DIM) + NORM_EPS) * gain

    rows_per = 256

    def norm_body(i, c):
        rs = pl.ds(pl.multiple_of(i * rows_per, rows_per), rows_per)
        for g in range(N_ATT_GROUPS):
            qn_s[g, rs, :] = pair_norm(q_refs[g][0, rs, :], qg_ref[g])
            kn_s[g, rs, :] = pair_norm(k_refs[g][0, rs, :], kg_ref[g])
        return c

    lax.fori_loop(0, S // rows_per, norm_body, 0)

    for g, (window, d) in enumerate(ATT_PATTERNS):
        n_blk = (S // d) // ATT_BLOCK
        has_prev = n_blk > 1

        def rows_at(start, d=d):
            return pl.ds(start, ATT_BLOCK) if d == 1 else pl.ds(start, ATT_BLOCK, stride=d)

        def blk_stages(it, g=g, d=d, has_prev=has_prev, rows_at=rows_at):
            curs, q2s, kcats, vcats, pens = [], [], [], [], []
            for b in range(ATT_BATCH):
                idx = it * ATT_BATCH + b
                r = idx % d
                n = idx // d
                cur = rows_at(n * (ATT_BLOCK * d) + r)
                q = qn_s[g, cur, :]
                q2s.append(jnp.concatenate([jnp.where(lo, q, 0.0), jnp.where(lo, 0.0, q)], axis=0).astype(BF16))
                kc = kn_s[g, cur, :].astype(BF16)
                vc = v_refs[g][0, cur, :].astype(BF16)
                if has_prev:
                    prv = rows_at(max(n - 1, 0) * (ATT_BLOCK * d) + r)
                    kc = jnp.concatenate([kn_s[g, prv, :].astype(BF16), kc], axis=0)
                    vc = jnp.concatenate([v_refs[g][0, prv, :].astype(BF16), vc], axis=0)
                    pens.append(jnp.where(col < ATT_BLOCK, NEG_INF if n == 0 else 0.0, 0.0)[None])
                curs.append(cur)
                kcats.append(kc)
                vcats.append(jnp.concatenate([vc, jnp.ones_like(vc)], axis=1))
            q2, kcat, vcat = jnp.stack(q2s), jnp.stack(kcats), jnp.stack(vcats)
            s = lax.dot_general(q2, kcat, (((2,), (2,)), ((0,), (0,))), preferred_element_type=F32)
            yield
            if has_prev:
                s = s + bias_ref[g, 0] + jnp.concatenate(pens, axis=0)
            else:
                s = s + bias_ref[g, 0, :, ATT_BLOCK:]
            m = jnp.max(s, axis=-1, keepdims=True)
            p = jnp.exp(s - m).astype(BF16)
            ol = lax.dot_general(p, vcat, (((2,), (1,)), ((0,), (0,))), preferred_element_type=F32)
            for b in range(ATT_BATCH):
                cur = curs[b]
                acc_s[g, cur, :] = jnp.where(lo, ol[b, :ATT_BLOCK, :LANES], ol[b, ATT_BLOCK:, :LANES])
                l_s[g, cur, :] = jnp.where(lo, ol[b, :ATT_BLOCK, LANES:], ol[b, ATT_BLOCK:, LANES:])
                m_s[g, cur, :] = jnp.where(lo, m[b, :ATT_BLOCK], m[b, ATT_BLOCK:])

        steps = [blk_stages(it) for it in range(d * n_blk // ATT_BATCH)]
        for step in steps[:ATT_AHEAD]:
            next(step)
        for k, step in enumerate(steps):
            if k + ATT_AHEAD < len(steps):
                next(steps[k + ATT_AHEAD])
            for _ in step:
                pass

    def comb_body(i, c):
        rs = pl.ds(pl.multiple_of(i * rows_per, rows_per), rows_per)
        m = jnp.maximum(jnp.maximum(m_s[0, rs, :], m_s[1, rs, :]), m_s[2, rs, :])
        num = jnp.zeros((rows_per, LANES), F32)
        den = jnp.zeros((rows_per, LANES), F32)
        for g in range(N_ATT_GROUPS):
            e = jnp.exp(m_s[g, rs, :] - m)
            num = num + e * acc_s[g, rs, :]
            den = den + e * l_s[g, rs, :]
        out_ref[0, rs, :] = num / den
        return c

    lax.fori_loop(0, S // rows_per, comb_body, 0)


def _attention(p3, q_gain, k_gain, bias):
    B, S, _ = p3.shape
    n_pairs = HEADS_PER_GROUP // 2

    def col_spec(base):
        return [pl.BlockSpec((1, S, LANES),
                             functools.partial(lambda b, sp, off: (b, 0, off + sp), off=base + g * n_pairs))
                for g in range(N_ATT_GROUPS)]

    n_qkv_blocks = ATT_WIDTH // LANES
    in_specs = (col_spec(QKV_COL_BLOCK) + col_spec(QKV_COL_BLOCK + n_qkv_blocks)
                + col_spec(QKV_COL_BLOCK + 2 * n_qkv_blocks)
                + [pl.BlockSpec((N_ATT_GROUPS, 1, LANES), lambda b, sp: (0, 0, 0)),
                   pl.BlockSpec((N_ATT_GROUPS, 1, LANES), lambda b, sp: (0, 0, 0)),
                   pl.BlockSpec((N_ATT_GROUPS, 1, 2 * ATT_BLOCK, 2 * ATT_BLOCK),
                                lambda b, sp: (0, sp, 0, 0))])
    return pl.pallas_call(
        _attn_kernel,
        out_shape=jax.ShapeDtypeStruct((B, S, ATT_OUT_WIDTH), F32),
        grid=(B, n_pairs),
        in_specs=in_specs,
        out_specs=pl.BlockSpec((1, S, LANES), lambda b, sp: (b, 0, sp)),
        scratch_shapes=[pltpu.VMEM((N_ATT_GROUPS, S, LANES), F32) for _ in range(5)],
        compiler_params=pltpu.CompilerParams(
            dimension_semantics=("arbitrary", "arbitrary"), vmem_limit_bytes=VMEM_LIMIT),
        name="dilated_attention",
    )(*([p3] * 9), q_gain, k_gain, bias)


def _t5_causal_bucket(dist):
    max_exact = REL_BUCKETS // 2
    d = jnp.maximum(dist.astype(F32), 1.0)
    large = max_exact + (jnp.log(d / max_exact) / math.log(REL_MAX_DIST / max_exact)
                         * (REL_BUCKETS - max_exact)).astype(jnp.int32)
    large = jnp.minimum(large, REL_BUCKETS - 1)
    return jnp.where(dist < max_exact, dist, large)


def _attention_bias(rel_bias_table):
    W = ATT_BLOCK
    qi = jnp.arange(W)[:, None]
    kj = jnp.arange(2 * W)[None, :]
    rel = qi + W - kj
    valid = (rel >= 0) & (rel <= W)
    per_group = []
    for g, (_, d) in enumerate(ATT_PATTERNS):
        tab = rel_bias_table[:, g * HEADS_PER_GROUP:(g + 1) * HEADS_PER_GROUP]
        bucket = _t5_causal_bucket(jnp.clip(rel, 0, W) * d)
        onehot = (bucket[:, :, None] == jnp.arange(REL_BUCKETS)[None, None, :]).astype(F32)
        bias = jnp.einsum('qkb,bh->qkh', onehot, tab.astype(F32),
                          precision=lax.Precision.HIGHEST)
        bias = jnp.where(valid[:, :, None], bias, NEG_INF)
        per_group.append(jnp.transpose(bias, (2, 0, 1)).reshape(2, 2 * W, 2 * W))
    return jnp.stack(per_group, axis=0)


def _rwkv_prep_kernel(p_ref, mu_ref, w0_ref, wup_ref, a0_ref, aup_ref, gup_ref, kk_ref, ka_ref, rk_ref,
                      at_o, rt_o, bt_o, kt_o, bw_o, kw_o, v_o, bonus_o, g_o, wend_o, carry):
    tq = p_ref.shape[1]
    C = RWKV_WIDTH
    L = WKV_CHUNK

    @pl.when(pl.program_id(1) == 0)
    def _():
        carry[...] = jnp.zeros_like(carry)

    p = p_ref[0].astype(F32)
    row = lax.broadcasted_iota(jnp.int32, (tq, 1), 0)
    prev = jnp.where(row == 0, carry[...], pltpu.roll(p, 1, axis=0))
    carry[...] = p[tq - 1:tq, :]
    pm = p + (prev - p) * mu_ref[...]
    r, k, v = pm[:, 0:C], pm[:, C:2 * C], pm[:, 2 * C:3 * C]
    xw = pm[:, 3 * C:3 * C + LANES]
    xa = pm[:, 3 * C + LANES:3 * C + 2 * LANES]
    xg = pm[:, 3 * C + 2 * LANES:]

    z = w0_ref[...] + _dot(jnp.tanh(xw), wup_ref[...])
    softplus_neg = jnp.maximum(-z, 0.0) + jnp.log(1.0 + jnp.exp(-jnp.abs(z)))
    lw = -jnp.exp(-softplus_neg - 0.5)
    a = _sigmoid(a0_ref[...] + _dot(xa, aup_ref[...]))
    g = _dot(_sigmoid(xg), gup_ref[...])

    hr = jnp.right_shift(lax.broadcasted_iota(jnp.int32, (C, C), 0), 6)
    hc = jnp.right_shift(lax.broadcasted_iota(jnp.int32, (C, C), 1), 6)
    ones_bd = jnp.where(hr == hc, 1.0, 0.0).astype(BF16)

    def head_sums(x):
        return jnp.dot(x.astype(BF16), ones_bd, preferred_element_type=F32)

    k2 = k * (1.0 + (a - 1.0) * ka_ref[...])
    kk = k * kk_ref[...]
    kk = kk / jnp.maximum(jnp.sqrt(head_sums(kk * kk)), 1e-12)
    bonus = head_sums(r * k2 * rk_ref[...]) * v
    b = kk * a

    half = 256
    ri = lax.broadcasted_iota(jnp.int32, (half, half), 0)
    ci = lax.broadcasted_iota(jnp.int32, (half, half), 1)
    tri = jnp.where((jnp.right_shift(ri, 6) == jnp.right_shift(ci, 6)) & (ri >= ci), 1.0, 0.0).astype(BF16)
    cums = []
    for s in range(tq // half):
        hi, mid, lo3 = _split3(lw[s * half:(s + 1) * half, :])
        cums.append(jnp.dot(tri, hi, preferred_element_type=F32)
                    + (jnp.dot(tri, mid, preferred_element_type=F32)
                       + jnp.dot(tri, lo3, preferred_element_type=F32)))
    cum = jnp.concatenate(cums, axis=0)
    n_chunks = tq // L
    ends = [cum[c * L + L - 1:c * L + L, :] for c in range(n_chunks)]
    tot = jnp.concatenate([jnp.broadcast_to(e, (L, C)) for e in ends], axis=0)
    wend = jnp.exp(jnp.concatenate(ends, axis=0))

    e_pos = jnp.exp(cum)
    e_neg = jnp.exp(-cum)
    e_rem = jnp.exp(tot - cum)
    at = -kk * jnp.exp(cum - lw)
    rt = r * e_pos
    bt = b * e_neg
    kt = k2 * e_neg
    bw = b * e_rem
    kw = k2 * e_rem
    at_o[0] = at.astype(BF16)
    rt_o[0] = rt.astype(BF16)
    bt_o[0] = bt.astype(BF16)
    kt_o[0] = kt.astype(BF16)
    bw_o[0] = bw.astype(BF16)
    kw_o[0] = kw.astype(BF16)
    v_o[0] = v.astype(BF16)
    bonus_o[0] = bonus
    g_o[0] = g
    wend_o[0] = wend


def _rwkv_prep(p3, mu, w0, wup, a0, aup, gup, k_k, k_a, r_k):
    B, S, _ = p3.shape
    tq = 512
    C = RWKV_WIDTH
    tm_shape = lambda dt: jax.ShapeDtypeStruct((B, S, C), dt)
    tm_spec = pl.BlockSpec((1, tq, C), lambda b, j: (b, j, 0))
    full = lambda shape: pl.BlockSpec(shape, lambda b, j: (0,) * len(shape))
    return pl.pallas_call(
        _rwkv_prep_kernel,
        out_shape=[tm_shape(BF16)] * 7 + [tm_shape(F32)] * 2 + [jax.ShapeDtypeStruct((B, S // WKV_CHUNK, C), F32)],
        grid=(B, S // tq),
        in_specs=[pl.BlockSpec((1, tq, RWKV_PAD_WIDTH), lambda b, j: (b, j, 0)),
                  full((1, RWKV_PAD_WIDTH)), full((1, C)), full((LANES, C)), full((1, C)),
                  full((LANES, C)), full((2 * LANES, C)), full((1, C)), full((1, C)), full((1, C))],
        out_specs=[tm_spec] * 9 + [pl.BlockSpec((1, tq // WKV_CHUNK, C), lambda b, j: (b, j, 0))],
        scratch_shapes=[pltpu.VMEM((1, RWKV_PAD_WIDTH), F32)],
        compiler_params=pltpu.CompilerParams(
            dimension_semantics=("arbitrary", "arbitrary"), vmem_limit_bytes=VMEM_LIMIT),
        name="rwkv_prep",
    )(p3, mu, w0, wup, a0, aup, gup, k_k, k_a, r_k)


WKV_GROUP = 4
WKV_GROUP_WIDTH = WKV_GROUP * RWKV_HEAD_DIM


def _wkv_kernel(at_ref, rt_ref, bt_ref, kt_ref, bw_ref, kw_ref, v_ref, bonus_ref, g_ref, wend_ref,
                lnw_ref, lnb_ref, o_ref, state, m_s, u0_s, o0_s, arb_s, p_s, c_s, sh_s):
    N, C, GW = RWKV_HEAD_DIM, WKV_CHUNK, WKV_GROUP_WIDTH
    n_groups = RWKV_WIDTH // GW
    n_sub = at_ref.shape[1] // C
    j = pl.program_id(1)

    @pl.when(j == 0)
    def _():
        state[...] = jnp.zeros_like(state)

    def iota(shape, dim):
        return lax.broadcasted_iota(jnp.int32, shape, dim)

    same_head = jnp.right_shift(iota((1, GW, GW), 1), 6) == jnp.right_shift(iota((1, GW, GW), 2), 6)
    ones_bd = jnp.where(same_head, 1.0, 0.0).astype(BF16)
    g_row = iota((1, 2 * C, 2 * GW), 1)
    g_col = jnp.bitwise_and(iota((1, 2 * C, 2 * GW), 2), N - 1)
    band = ((g_row < C) & (g_row > g_col)) | ((g_row >= C) & ((g_row - C) >= g_col))
    eye = jnp.where(iota((1, C, GW), 1) == jnp.bitwise_and(iota((1, C, GW), 2), N - 1), 1.0, 0.0)

    def bmm(a, b, dims):
        return lax.dot_general(a.astype(BF16), b.astype(BF16), (dims, ((0,), (0,))),
                               preferred_element_type=F32)

    nn, nt, tn = ((2,), (1,)), ((2,), (2,)), ((1,), (1,))

    def bd(x):
        xb = x.astype(BF16)
        return jnp.where(same_head, jnp.concatenate([xb] * WKV_GROUP, axis=1), jnp.zeros((), BF16))

    def narrow(x):
        return sum(x[:, h * N:(h + 1) * N] for h in range(WKV_GROUP))

    span = 4
    rows = span * C

    def stack(x):
        x3 = x.reshape(span, C, n_groups * GW)
        return jnp.concatenate([x3[:, :, q * GW:(q + 1) * GW] for q in range(n_groups)], axis=0)

    def unstack(y):
        return jnp.concatenate([y[q * span:(q + 1) * span].reshape(rows, GW) for q in range(n_groups)],
                               axis=1)

    def chunk_terms(i):
        rs = slice(i * rows, (i + 1) * rows)
        at = stack(at_ref[0, rs, :])
        ar = jnp.concatenate([at, stack(rt_ref[0, rs, :])], axis=1)
        bk_bd = jnp.concatenate([bd(stack(bt_ref[0, rs, :])), bd(stack(kt_ref[0, rs, :]))], axis=1)
        gm = jnp.where(band, bmm(ar, bk_bd, nt), 0.0)
        yield
        a_ab = gm[:, :C, :GW]
        inv = eye + a_ab
        pw = bmm(a_ab, bd(a_ab), nn)
        yield
        for _ in range(int(math.log2(C)) - 2):
            prod = bmm(jnp.concatenate([inv, pw], axis=1), bd(pw), nn)
            yield
            inv = inv + prod[:, :C]
            pw = prod[:, C:]
        inv = inv + bmm(inv, bd(pw), nn)
        yield
        v = stack(v_ref[0, rs, :])
        kv = bmm(gm[:, :, GW:], bd(v), nn)
        yield
        mu0 = bmm(inv, jnp.concatenate([bd(at), bd(kv[:, :C])], axis=2), nn)
        yield
        m = mu0[:, :, :GW].astype(BF16)
        u0 = mu0[:, :, GW:]
        bw = stack(bw_ref[0, rs, :])
        low_rank = jnp.where(same_head, bmm(m, bw, tn), 0.0)
        const = jnp.where(same_head, bmm(jnp.concatenate([u0.astype(BF16), v], axis=1),
                                         jnp.concatenate([bw, stack(kw_ref[0, rs, :])], axis=1), tn), 0.0)
        m_s[rs, :] = unstack(m)
        u0_s[rs, :] = unstack(u0)
        o0_s[rs, :] = unstack(kv[:, C:])
        arb_s[rs, :] = unstack(gm[:, C:, :GW]).astype(BF16)
        for q in range(n_groups):
            p_s[i * span:(i + 1) * span, q] = low_rank[q * span:(q + 1) * span].astype(BF16)
            c_s[i * span:(i + 1) * span, q] = narrow(const[q * span:(q + 1) * span])

    def groups(x):
        return jnp.stack([x[:, q * GW:(q + 1) * GW] for q in range(n_groups)], axis=0)

    def chunk_state(i):
        s0 = state[...]
        sh_s[i] = s0.astype(BF16)
        w_end = groups(wend_ref[0, pl.ds(j * n_sub + i, 1), :])
        state[...] = s0 * w_end + bmm(s0, p_s[i], nn) + c_s[i]

    n_prob = n_groups * span

    ones_p = jnp.broadcast_to(ones_bd, (n_prob, GW, GW))

    def head_sums(x):
        return bmm(x, ones_p, nn)

    def chunk_outputs(i):
        rs = slice(i * rows, (i + 1) * rows)
        s0 = bd(jnp.concatenate([sh_s[i * span:(i + 1) * span, q] for q in range(n_groups)], axis=0))
        mr_s = bmm(jnp.concatenate([stack(m_s[rs, :]), stack(rt_ref[0, rs, :])], axis=1), s0, nt)
        yield
        u = mr_s[:, :C] + stack(u0_s[rs, :])
        o = mr_s[:, C:] + stack(o0_s[rs, :]) + bmm(stack(arb_s[rs, :]), bd(u), nn)
        yield
        mean = head_sums(o) * (1.0 / N)
        yield
        dlt = o - mean
        on = unstack(dlt * lax.rsqrt(head_sums(dlt * dlt) * (1.0 / N) + GN_EPS))
        on = on * lnw_ref[...] + lnb_ref[...]
        o_ref[0, rs, :] = (on + bonus_ref[0, rs, :]) * g_ref[0, rs, :]

    def run(stages, between=()):
        pending = iter(between)
        for _ in stages:
            step = next(pending, None)
            if step is not None:
                step()
        for step in pending:
            step()

    n_spans = n_sub // span
    assert n_spans >= 2
    states = lambda i: [functools.partial(chunk_state, k) for k in range(i * span, (i + 1) * span)]
    run(chunk_terms(0))
    for i in range(1, n_spans):
        run(chunk_terms(i), states(i - 1))
    run(chunk_outputs(0), states(n_spans - 1))
    for i in range(1, n_spans):
        run(chunk_outputs(i))


def _wkv(at, rt, bt, kt, bw, kw, v, bonus, g, wend, ln_w, ln_b):
    B, S, C = at.shape
    tc = 1024
    tm_spec = pl.BlockSpec((1, tc, C), lambda bb, j: (bb, j, 0))
    par_spec = pl.BlockSpec((1, C), lambda bb, j: (0, 0))
    blocks = (C // WKV_GROUP_WIDTH, WKV_GROUP_WIDTH, WKV_GROUP_WIDTH)
    wide = (C // WKV_GROUP_WIDTH, RWKV_HEAD_DIM, WKV_GROUP_WIDTH)
    return pl.pallas_call(
        _wkv_kernel,
        out_shape=jax.ShapeDtypeStruct((B, S, C), F32),
        grid=(B, S // tc),
        in_specs=([tm_spec] * 9 + [pl.BlockSpec((1, S // WKV_CHUNK, C), lambda bb, j: (bb, 0, 0))]
                  + [par_spec] * 2),
        out_specs=tm_spec,
        scratch_shapes=[pltpu.VMEM(wide, F32),
                        pltpu.VMEM((tc, C), BF16), pltpu.VMEM((tc, C), F32), pltpu.VMEM((tc, C), F32),
                        pltpu.VMEM((tc, C), BF16),
                        pltpu.VMEM((tc // WKV_CHUNK,) + blocks, BF16), pltpu.VMEM((tc // WKV_CHUNK,) + wide, F32),
                        pltpu.VMEM((tc // WKV_CHUNK,) + wide, BF16)],
        compiler_params=pltpu.CompilerParams(
            dimension_semantics=("arbitrary", "arbitrary"), vmem_limit_bytes=VMEM_LIMIT),
        name="wkv7",
    )(at, rt, bt, kt, bw, kw, v, bonus, g, wend, ln_w, ln_b)


MERGE_SPLIT = 2
INFO_FIELDS = 8


def _merge_kernel(att_ref, o_ref, gate_ref, x_ref, wa_ref, wr_ref, wo_ref, g2_ref, wrt_ref, brt_ref,
                  x1_ref, h2_ref, lg_ref, cnt_ref, lt_ref):
    tm = x_ref.shape[1]
    ts = tm // MERGE_SPLIT
    parts = [slice(h * ts, (h + 1) * ts) for h in range(MERGE_SPLIT)]
    lane = lax.broadcasted_iota(jnp.int32, (1, ROUTER_PAD), 1)

    @pl.when((pl.program_id(0) == 0) & (pl.program_id(1) == 0))
    def _():
        cnt_ref[...] = jnp.zeros_like(cnt_ref)

    ys = [(_dot(att_ref[0, rs, :], wa_ref[...]), _dot(o_ref[0, rs, :], wr_ref[...])) for rs in parts]

    mixed = []
    for rs, (y_att, y_rwkv) in zip(parts, ys):
        gates = _sigmoid(gate_ref[0, rs, :].astype(F32))
        mixed.append(gates[:, :D_MODEL] * y_att + gates[:, D_MODEL:] * y_rwkv)

    x1s = []
    for rs, mx in zip(parts, mixed):
        x1 = x_ref[0, rs, :] + _dot(mx, wo_ref[...])
        x1_ref[0, rs, :] = x1
        x1s.append(x1)

    splits = []
    for rs, x1 in zip(parts, x1s):
        ms = jnp.mean(x1 * x1, axis=-1, keepdims=True)
        h2 = x1 * lax.rsqrt(ms + NORM_EPS) * g2_ref[...]
        hh, hl = _split2(h2)
        half = D_MODEL // 2
        rounded = hh.astype(F32)
        w_hi = pltpu.bitcast(rounded[:, :half], jnp.uint32)
        w_lo = pltpu.bitcast(rounded[:, half:], jnp.uint32)
        h2_ref[0, rs, :] = w_hi | jnp.right_shift(w_lo, jnp.uint32(16))
        splits.append((hh, hl))

    logits = []
    for hh, hl in splits:
        both = jnp.dot(hh, wrt_ref[...], preferred_element_type=F32)
        cross = jnp.dot(hl, wrt_ref[:, :ROUTER_PAD], preferred_element_type=F32)
        logits.append(both[:, :ROUTER_PAD] + (both[:, ROUTER_PAD:] + cross) + brt_ref[...])

    ti = lax.broadcasted_iota(jnp.int32, (ts, ts), 0)
    tj = lax.broadcasted_iota(jnp.int32, (ts, ts), 1)
    before = jnp.where(tj < ti, 1.0, 0.0).astype(BF16)
    for rs, lg in zip(parts, logits):
        gmask = lane < N_EXPERT_GROUPS
        gl = jnp.where(gmask, lg, -jnp.inf)
        gmax = jnp.max(gl, axis=-1, keepdims=True)
        grp_idx = jnp.min(jnp.where(gl == gmax, lane, ROUTER_PAD), axis=-1, keepdims=True)
        grp_p = 1.0 / jnp.sum(jnp.where(gmask, jnp.exp(lg - gmax), 0.0), axis=-1, keepdims=True)
        e_lane = lane - N_EXPERT_GROUPS
        emask = ((e_lane >= 0) & (e_lane < N_EXPERTS)
                 & (jnp.right_shift(e_lane, int(math.log2(EXPERTS_PER_GROUP))) == grp_idx))
        el = jnp.where(emask, lg, -jnp.inf)
        m1 = jnp.max(el, axis=-1, keepdims=True)
        i1 = jnp.min(jnp.where(el == m1, lane, ROUTER_PAD), axis=-1, keepdims=True)
        el2 = jnp.where(lane == i1, -jnp.inf, el)
        m2 = jnp.max(el2, axis=-1, keepdims=True)
        i2 = jnp.min(jnp.where(el2 == m2, lane, ROUTER_PAD), axis=-1, keepdims=True)
        e2 = jnp.exp(m2 - m1)
        gate1 = grp_p / (1.0 + e2)
        gate2 = gate1 * e2

        oh = jnp.where((lane == i1) | (lane == i2), 1.0, 0.0)
        pref = jnp.dot(before, oh.astype(BF16), preferred_element_type=F32) + cnt_ref[...]
        rank1 = jnp.sum(jnp.where(lane == i1, pref, 0.0), axis=-1, keepdims=True)
        rank2 = jnp.sum(jnp.where(lane == i2, pref, 0.0), axis=-1, keepdims=True)
        cnt_ref[...] = cnt_ref[...] + jnp.sum(oh, axis=0, keepdims=True)
        info = jnp.where(lane == 0, (i1 - N_EXPERT_GROUPS).astype(F32), 0.0)
        info = jnp.where(lane == 1, (i2 - N_EXPERT_GROUPS).astype(F32), info)
        info = jnp.where(lane == 2, rank1, info)
        info = jnp.where(lane == 3, rank2, info)
        info = jnp.where(lane == 4, gate1, info)
        info = jnp.where(lane == 5, gate2, info)
        lg_ref[0, rs, :] = info
        lt_ref[:, rs] = info.T[:INFO_FIELDS, :]


def _merge(att, o_g, p3, x, w_att_out, w_rwkv_out, w_out, gain2, w_router, b_router):
    B, S, D = x.shape
    tm = 1024
    full = lambda shape: pl.BlockSpec(shape, lambda b, j: (0,) * len(shape))
    return pl.pallas_call(
        _merge_kernel,
        out_shape=[jax.ShapeDtypeStruct((B, S, D), F32), jax.ShapeDtypeStruct((B, S, D // 2), jnp.uint32),
                   jax.ShapeDtypeStruct((B, S, ROUTER_PAD), F32),
                   jax.ShapeDtypeStruct((1, ROUTER_PAD), F32),
                   jax.ShapeDtypeStruct((INFO_FIELDS, B * S), F32)],
        grid=(B, S // tm),
        in_specs=[pl.BlockSpec((1, tm, ATT_OUT_WIDTH), lambda b, j: (b, j, 0)),
                  pl.BlockSpec((1, tm, RWKV_WIDTH), lambda b, j: (b, j, 0)),
                  pl.BlockSpec((1, tm, 2 * D_MODEL), lambda b, j: (b, j, 1)),
                  pl.BlockSpec((1, tm, D), lambda b, j: (b, j, 0)),
                  full((ATT_OUT_WIDTH, D)), full((RWKV_WIDTH, D)), full((D, D)),
                  full((1, D)), full((D, 2 * ROUTER_PAD)), full((1, ROUTER_PAD))],
        out_specs=[pl.BlockSpec((1, tm, D), lambda b, j: (b, j, 0)),
                   pl.BlockSpec((1, tm, D // 2), lambda b, j: (b, j, 0)),
                   pl.BlockSpec((1, tm, ROUTER_PAD), lambda b, j: (b, j, 0)),
                   pl.BlockSpec((1, ROUTER_PAD), lambda b, j: (0, 0)),
                   pl.BlockSpec((INFO_FIELDS, tm), lambda b, j: (0, b * (S // tm) + j))],
        compiler_params=pltpu.CompilerParams(
            dimension_semantics=("arbitrary", "arbitrary"), vmem_limit_bytes=VMEM_LIMIT),
        name="merge_outproj_router",
    )(att, o_g, p3, x, w_att_out, w_rwkv_out, w_out, gain2, w_router, b_router)


def _expert_kernel(be_ref, nb_ref, x_ref, wg_ref, wu_ref, wd_ref, y_ref, wgu_s, wd_s):
    i = pl.program_id(0)

    @pl.when((i == 0) | (be_ref[i] != be_ref[jnp.maximum(i - 1, 0)]))
    def _():
        wgu_s[:, :D_EXPERT] = wg_ref[0].astype(BF16)
        wgu_s[:, D_EXPERT:] = wu_ref[0].astype(BF16)
        wd_s[...] = wd_ref[0].astype(BF16)

    @pl.when(i < nb_ref[0])
    def _():
        words = x_ref[...]
        x_a = pltpu.bitcast(words & jnp.uint32(0xFFFF0000), F32)
        x_b = pltpu.bitcast(jnp.left_shift(words, jnp.uint32(16)), F32)
        xb = jnp.concatenate([x_a, x_b], axis=-1).astype(BF16)
        hw = D_EXPERT // 2
        hgu = [(jnp.dot(xb, wgu_s[:, c * hw:(c + 1) * hw], preferred_element_type=F32),
                jnp.dot(xb, wgu_s[:, D_EXPERT + c * hw:D_EXPERT + (c + 1) * hw], preferred_element_type=F32))
               for c in range(2)]
        y = None
        for c, (hg, hu) in enumerate(hgu):
            hid = (hg * _sigmoid(hg) * hu).astype(BF16)
            part = jnp.dot(hid, wd_s[c * hw:(c + 1) * hw, :], preferred_element_type=F32)
            y = part if y is None else y + part
        y_ref[...] = y.astype(y_ref.dtype)

    @pl.when(i >= nb_ref[0])
    def _():
        y_ref[...] = jnp.zeros_like(y_ref)


def _experts(blk_expert, n_used, xin, w_gate, w_up, w_down):
    n_rows = xin.shape[0]
    D = D_MODEL
    n_blocks = n_rows // MOE_ROWS
    return pl.pallas_call(
        _expert_kernel,
        out_shape=jax.ShapeDtypeStruct((n_rows, D), BF16),
        grid_spec=pltpu.PrefetchScalarGridSpec(
            num_scalar_prefetch=2,
            grid=(n_blocks,),
            scratch_shapes=[pltpu.VMEM((D, 2 * D_EXPERT), BF16), pltpu.VMEM((D_EXPERT, D), BF16)],
            in_specs=[pl.BlockSpec((MOE_ROWS, D // 2), lambda i, be, nb: (i, 0)),
                      pl.BlockSpec((1, D, D_EXPERT), lambda i, be, nb: (be[i], 0, 0)),
                      pl.BlockSpec((1, D, D_EXPERT), lambda i, be, nb: (be[i], 0, 0)),
                      pl.BlockSpec((1, D_EXPERT, D), lambda i, be, nb: (be[i], 0, 0))],
            out_specs=pl.BlockSpec((MOE_ROWS, D), lambda i, be, nb: (i, 0))),
        compiler_params=pltpu.CompilerParams(
            dimension_semantics=("arbitrary",), vmem_limit_bytes=VMEM_LIMIT),
        name="moe_experts",
    )(blk_expert, n_used, xin, w_gate, w_up, w_down)


def _row_tokens(dests, counts, pcounts, pstarts, pends, n_rows, T):
    dest_flat = jnp.concatenate(dests)
    A = dest_flat.shape[0]
    n_pad = n_rows - A
    seg_count = jnp.concatenate([pcounts - counts, n_rows - pends[-1:]])
    seg_first = jnp.concatenate([pstarts + counts, pends[-1:]])
    seg_end = jnp.cumsum(seg_count)
    i = jnp.arange(n_pad, dtype=jnp.int32)
    seg = jnp.sum(i[None, :] >= seg_end[:, None], axis=0)
    in_seg = seg[None, :] == jnp.arange(seg_end.shape[0], dtype=jnp.int32)[:, None]
    shift = (seg_first - (seg_end - seg_count)).astype(jnp.int32)
    pad_rows = i + jnp.sum(jnp.where(in_seg, shift[:, None], 0), axis=0)
    rows = jnp.concatenate([dest_flat, pad_rows])
    toks = jnp.concatenate([jnp.arange(T, dtype=jnp.int32)] * len(dests) + [i % T])
    tok_bits = max(T - 1, 1).bit_length()
    assert n_rows << tok_bits <= 1 << 32
    packed = jnp.sort((rows.astype(jnp.uint32) << tok_bits) | toks.astype(jnp.uint32))
    return (packed & jnp.uint32((1 << tok_bits) - 1)).astype(jnp.int32)


def _row_layout(info, counts, T):
    A = T * TOP_K
    counts = counts.astype(jnp.int32)
    pcounts = (counts + MOE_ROWS - 1) // MOE_ROWS * MOE_ROWS
    pends = jnp.cumsum(pcounts)
    pstarts = pends - pcounts
    experts = jnp.arange(N_EXPERTS, dtype=F32)[:, None]
    seg_start = [jnp.dot(pstarts.astype(F32), (info[k][None, :] == experts).astype(F32),
                         precision=lax.Precision.HIGHEST) for k in range(TOP_K)]
    dest = [(seg_start[k] + info[TOP_K + k]).astype(jnp.int32) for k in range(TOP_K)]
    n_blocks = -(-A // MOE_ROWS) + N_EXPERTS
    n_rows = n_blocks * MOE_ROWS
    row_tok = _row_tokens(dest, counts, pcounts, pstarts, pends, n_rows, T)
    blk_start = jnp.arange(n_blocks, dtype=jnp.int32) * MOE_ROWS
    blk_expert = jnp.minimum(jnp.sum(blk_start[:, None] >= pends[None, :], axis=1), N_EXPERTS - 1)
    n_used = (pends[N_EXPERTS - 1] // MOE_ROWS).astype(jnp.int32)[None]
    return row_tok, blk_expert.astype(jnp.int32), n_used, dest


def _combine_kernel(x1_ref, y0_ref, y1_ref, info_ref, o_ref):
    gate = 2 * TOP_K
    g0 = info_ref[:, gate:gate + 1]
    g1 = info_ref[:, gate + 1:gate + 2]
    o_ref[...] = x1_ref[...] + (g0 * y0_ref[...].astype(F32) + g1 * y1_ref[...].astype(F32))


def _combine(x1, y, info):
    T, D = x1.shape
    tm = 1024
    row_spec = pl.BlockSpec((tm, D), lambda i: (i, 0))
    second = pl.BlockSpec((tm, D), lambda i: (i + T // tm, 0))
    return pl.pallas_call(
        _combine_kernel,
        out_shape=jax.ShapeDtypeStruct((T, D), F32),
        grid=(T // tm,),
        in_specs=[row_spec, row_spec, second, pl.BlockSpec((tm, ROUTER_PAD), lambda i: (i, 0))],
        out_specs=row_spec,
        compiler_params=pltpu.CompilerParams(
            dimension_semantics=("arbitrary",), vmem_limit_bytes=VMEM_LIMIT),
        name="moe_combine",
    )(x1, y, y, info)


def _pad_rows(w, rows):
    return jnp.pad(w, ((0, rows - w.shape[0]), (0, 0)))


def kernel(x, norm1_gain, w_in, q_norm_gain, k_norm_gain, rel_bias_table, w_att_out, rwkv_shift_mu,
           rwkv_w0, rwkv_w_up, rwkv_a0, rwkv_a_up, rwkv_g_up, rwkv_k_k, rwkv_k_a, rwkv_r_k,
           rwkv_ln_w, rwkv_ln_b, w_rwkv_out, w_out, norm2_gain, w_group_router, b_group_router,
           w_expert_router, b_expert_router, w_expert_gate, w_expert_up, w_expert_down):
    B, S, D = x.shape
    T = B * S
    C = RWKV_WIDTH
    bias = _attention_bias(rel_bias_table)
    for l in range(norm1_gain.shape[0]):
        wi = w_in[l]
        w_qkv = wi[:, :3 * ATT_WIDTH]
        w_rw = wi[:, 3 * ATT_WIDTH:3 * ATT_WIDTH + RWKV_SHIFT_WIDTH]
        w_gt = wi[:, 3 * ATT_WIDTH + RWKV_SHIFT_WIDTH:]
        pad_cols = lambda w, n: jnp.pad(w, ((0, 0), (0, n - w.shape[1])))
        o_w, o_a, o_g = 3 * C, 3 * C + DECAY_LORA, 3 * C + DECAY_LORA + ICLR_LORA

        def rw_layout(w):
            return jnp.concatenate([w[:, :o_w], pad_cols(w[:, o_w:o_a], LANES),
                                    pad_cols(w[:, o_a:o_g], LANES), pad_cols(w[:, o_g:], 2 * LANES)], axis=1)

        w_rg = jnp.concatenate([rw_layout(w_rw), w_gt], axis=1).astype(BF16)
        mu = rw_layout(rwkv_shift_mu[l][None, :])

        p_rg, p_qkv = _inproj(x.reshape(T, D), norm1_gain[l][None, :], w_rg, w_qkv.astype(BF16))
        p3 = p_rg.reshape(B, S, RWKV_PAD_WIDTH + 2 * D)

        tile2 = lambda gmat: jnp.tile(gmat, (1, 2))[:, None, :]
        att = _attention(p_qkv.reshape(B, S, 3 * ATT_WIDTH), tile2(q_norm_gain[l]) * (HEAD_DIM ** -0.5),
                         tile2(k_norm_gain[l]),
                         bias)

        row = lambda v: v[None, :]
        prep = _rwkv_prep(
            p3, mu, row(rwkv_w0[l]), _pad_rows(rwkv_w_up[l], LANES).astype(BF16), row(rwkv_a0[l]),
            _pad_rows(rwkv_a_up[l], LANES).astype(BF16), _pad_rows(rwkv_g_up[l], 2 * LANES).astype(BF16),
            row(rwkv_k_k[l]), row(rwkv_k_a[l]), rwkv_r_k[l].reshape(1, C))
        o_g = _wkv(*prep, row(rwkv_ln_w[l]), row(rwkv_ln_b[l]))

        w_router = jnp.concatenate([w_group_router[l], w_expert_router[l]], axis=1)
        w_router = jnp.pad(w_router, ((0, 0), (0, ROUTER_PAD - w_router.shape[1])))
        wr_hi = w_router.astype(BF16)
        wr_lo = (w_router - wr_hi.astype(F32)).astype(BF16)
        b_router = jnp.concatenate([b_group_router[l], b_expert_router[l]])
        b_router = jnp.pad(b_router, (0, ROUTER_PAD - b_router.shape[0]))[None, :]
        x1, h2, info, counts, info_t = _merge(
            att, o_g, p3, x, w_att_out[l].astype(BF16),
            w_rwkv_out[l].astype(BF16), w_out[l].astype(BF16),
            norm2_gain[l][None, :], jnp.concatenate([wr_hi, wr_lo], axis=1), b_router)

        row_tok, blk_expert, n_used, dest = _row_layout(
            info_t, counts[0, N_EXPERT_GROUPS:N_EXPERT_GROUPS + N_EXPERTS], T)
        xin = h2.reshape(T, D // 2)[row_tok]
        yb = _experts(blk_expert, n_used, xin, w_expert_gate[l], w_expert_up[l], w_expert_down[l])
        x = _combine(x1.reshape(T, D), yb[jnp.concatenate(dest)],
                     info.reshape(T, ROUTER_PAD)).reshape(B, S, D)
    return x
```

```python
import functools
import math

import jax
import jax.numpy as jnp
from jax import lax
from jax.experimental import pallas as pl
from jax.experimental.pallas import tpu as pltpu

F32 = jnp.float32
BF16 = jnp.bfloat16

D_MODEL = 1024
ATT_PATTERNS = ((128, 1), (512, 4), (2048, 16))
N_ATT_GROUPS = 3
HEADS_PER_GROUP = 4
HEAD_DIM = 64
ATT_WIDTH = 768
ATT_OUT_WIDTH = 256
REL_BUCKETS = 32
REL_MAX_DIST = 2048
RWKV_HEAD_DIM = 64
RWKV_WIDTH = 512
DECAY_LORA = 64
ICLR_LORA = 64
RWKV_SHIFT_WIDTH = 1824
GN_EPS = 64e-5
N_EXPERT_GROUPS = 4
EXPERTS_PER_GROUP = 8
N_EXPERTS = 32
TOP_K = 2
D_EXPERT = 512
NORM_EPS = 1e-6
NEG_INF = -1e30

LANES = 128
ATT_BLOCK = 128
ATT_BATCH = 8
ATT_AHEAD = 1
WKV_CHUNK = 64
RWKV_PAD_WIDTH = 2048
QKV_COL_BLOCK = 0
ROUTER_PAD = 128
MOE_ROWS = 512
VMEM_LIMIT = 56 * 1024 * 1024


def _sigmoid(x):
    return 0.5 * jnp.tanh(0.5 * x) + 0.5


def _dot(a, b):
    return jnp.dot(a.astype(BF16), b.astype(BF16), preferred_element_type=F32)


def _split3(a):
    hi = a.astype(BF16)
    r1 = a - hi.astype(F32)
    mid = r1.astype(BF16)
    lo = (r1 - mid.astype(F32)).astype(BF16)
    return hi, mid, lo


def _split2(a):
    hi = a.astype(BF16)
    lo = (a - hi.astype(F32)).astype(BF16)
    return hi, lo


INPROJ_COLS = 512


def _inproj_kernel(x_ref, g_ref, wrg_ref, wqkv_ref, rg_ref, qkv_ref):
    x = x_ref[...]
    ms = jnp.mean(x * x, axis=-1, keepdims=True)
    h = (x * lax.rsqrt(ms + NORM_EPS) * g_ref[...]).astype(BF16)
    for w_ref, o_ref in ((wrg_ref, rg_ref), (wqkv_ref, qkv_ref)):
        width = w_ref.shape[1]
        for c in range(0, width, INPROJ_COLS):
            cs = slice(c, min(c + INPROJ_COLS, width))
            o_ref[:, cs] = jnp.dot(h, w_ref[:, cs], preferred_element_type=F32).astype(o_ref.dtype)


def _inproj(x2, gain, w_rg, w_qkv):
    T = x2.shape[0]
    tm = 512
    n_rg, n_qkv = w_rg.shape[1], w_qkv.shape[1]
    return pl.pallas_call(
        _inproj_kernel,
        out_shape=[jax.ShapeDtypeStruct((T, n_rg), BF16), jax.ShapeDtypeStruct((T, n_qkv), F32)],
        grid=(T // tm,),
        in_specs=[pl.BlockSpec((tm, D_MODEL), lambda i: (i, 0)),
                  pl.BlockSpec((1, D_MODEL), lambda i: (0, 0)),
                  pl.BlockSpec((D_MODEL, n_rg), lambda i: (0, 0)),
                  pl.BlockSpec((D_MODEL, n_qkv), lambda i: (0, 0))],
        out_specs=[pl.BlockSpec((tm, n_rg), lambda i: (i, 0)),
                   pl.BlockSpec((tm, n_qkv), lambda i: (i, 0))],
        compiler_params=pltpu.CompilerParams(
            dimension_semantics=("arbitrary",), vmem_limit_bytes=VMEM_LIMIT),
        name="inproj",
    )(x2, gain, w_rg, w_qkv)


def _attn_kernel(q0, q1, q2, k0, k1, k2, v0, v1, v2, qg_ref, kg_ref, bias_ref, out_ref,
                 qn_s, kn_s, acc_s, m_s, l_s):
    S = out_ref.shape[1]
    q_refs, k_refs, v_refs = (q0, q1, q2), (k0, k1, k2), (v0, v1, v2)
    lane = lax.broadcasted_iota(jnp.int32, (1, LANES), 1)
    lo = lane < HEAD_DIM
    col = lax.broadcasted_iota(jnp.int32, (1, 2 * ATT_BLOCK), 1)

    same_head = (lax.broadcasted_iota(jnp.int32, (LANES, LANES), 0) < HEAD_DIM) == (
        lax.broadcasted_iota(jnp.int32, (LANES, LANES), 1) < HEAD_DIM)
    ones_bd = jnp.where(same_head, 1.0, 0.0).astype(BF16)

    def pair_norm(x, gain):
        ss = jnp.dot((x * x).astype(BF16), ones_bd, preferred_element_type=F32)
        return x * lax.rsqrt(ss * (1.0 / HEAD_DIM) + NORM_EPS) * gain

    rows_per = 256

    def norm_body(i, c):
        rs = pl.ds(pl.multiple_of(i * rows_per, rows_per), rows_per)
        for g in range(N_ATT_GROUPS):
            qn_s[g, rs, :] = pair_norm(q_refs[g][0, rs, :], qg_ref[g])
            kn_s[g, rs, :] = pair_norm(k_refs[g][0, rs, :], kg_ref[g])
        return c

    lax.fori_loop(0, S // rows_per, norm_body, 0)

    for g, (window, d) in enumerate(ATT_PATTERNS):
        n_blk = (S // d) // ATT_BLOCK
        has_prev = n_blk > 1

        def rows_at(start, d=d):
            return pl.ds(start, ATT_BLOCK) if d == 1 else pl.ds(start, ATT_BLOCK, stride=d)

        def blk_stages(it, g=g, d=d, has_prev=has_prev, rows_at=rows_at):
            curs, q2s, kcats, vcats, pens = [], [], [], [], []
            for b in range(ATT_BATCH):
                idx = it * ATT_BATCH + b
                r = idx % d
                n = idx // d
                cur = rows_at(n * (ATT_BLOCK * d) + r)
                q = qn_s[g, cur, :]
                q2s.append(jnp.concatenate([jnp.where(lo, q, 0.0), jnp.where(lo, 0.0, q)], axis=0).astype(BF16))
                kc = kn_s[g, cur, :].astype(BF16)
                vc = v_refs[g][0, cur, :].astype(BF16)
                if has_prev:
                    prv = rows_at(max(n - 1, 0) * (ATT_BLOCK * d) + r)
                    kc = jnp.concatenate([kn_s[g, prv, :].astype(BF16), kc], axis=0)
                    vc = jnp.concatenate([v_refs[g][0, prv, :].astype(BF16), vc], axis=0)
                    pens.append(jnp.where(col < ATT_BLOCK, NEG_INF if n == 0 else 0.0, 0.0)[None])
                curs.append(cur)
                kcats.append(kc)
                vcats.append(jnp.concatenate([vc, jnp.ones_like(vc)], axis=1))
            q2, kcat, vcat = jnp.stack(q2s), jnp.stack(kcats), jnp.stack(vcats)
            s = lax.dot_general(q2, kcat, (((2,), (2,)), ((0,), (0,))), preferred_element_type=F32)
            yield
            if has_prev:
                s = s + bias_ref[g, 0] + jnp.concatenate(pens, axis=0)
            else:
                s = s + bias_ref[g, 0, :, ATT_BLOCK:]
            m = jnp.max(s, axis=-1, keepdims=True)
            p = jnp.exp(s - m).astype(BF16)
            ol = lax.dot_general(p, vcat, (((2,), (1,)), ((0,), (0,))), preferred_element_type=F32)
            for b in range(ATT_BATCH):
                cur = curs[b]
                acc_s[g, cur, :] = jnp.where(lo, ol[b, :ATT_BLOCK, :LANES], ol[b, ATT_BLOCK:, :LANES])
                l_s[g, cur, :] = jnp.where(lo, ol[b, :ATT_BLOCK, LANES:], ol[b, ATT_BLOCK:, LANES:])
                m_s[g, cur, :] = jnp.where(lo, m[b, :ATT_BLOCK], m[b, ATT_BLOCK:])

        steps = [blk_stages(it) for it in range(d * n_blk // ATT_BATCH)]
        for step in steps[:ATT_AHEAD]:
            next(step)
        for k, step in enumerate(steps):
            if k + ATT_AHEAD < len(steps):
                next(steps[k + ATT_AHEAD])
            for _ in step:
                pass

    def comb_body(i, c):
        rs = pl.ds(pl.multiple_of(i * rows_per, rows_per), rows_per)
        m = jnp.maximum(jnp.maximum(m_s[0, rs, :], m_s[1, rs, :]), m_s[2, rs, :])
        num = jnp.zeros((rows_per, LANES), F32)
        den = jnp.zeros((rows_per, LANES), F32)
        for g in range(N_ATT_GROUPS):
            e = jnp.exp(m_s[g, rs, :] - m)
            num = num + e * acc_s[g, rs, :]
            den = den + e * l_s[g, rs, :]
        out_ref[0, rs, :] = num / den
        return c

    lax.fori_loop(0, S // rows_per, comb_body, 0)


def _attention(p3, q_gain, k_gain, bias):
    B, S, _ = p3.shape
    n_pairs = HEADS_PER_GROUP // 2

    def col_spec(base):
        return [pl.BlockSpec((1, S, LANES),
                             functools.partial(lambda b, sp, off: (b, 0, off + sp), off=base + g * n_pairs))
                for g in range(N_ATT_GROUPS)]

    n_qkv_blocks = ATT_WIDTH // LANES
    in_specs = (col_spec(QKV_COL_BLOCK) + col_spec(QKV_COL_BLOCK + n_qkv_blocks)
                + col_spec(QKV_COL_BLOCK + 2 * n_qkv_blocks)
                + [pl.BlockSpec((N_ATT_GROUPS, 1, LANES), lambda b, sp: (0, 0, 0)),
                   pl.BlockSpec((N_ATT_GROUPS, 1, LANES), lambda b, sp: (0, 0, 0)),
                   pl.BlockSpec((N_ATT_GROUPS, 1, 2 * ATT_BLOCK, 2 * ATT_BLOCK),
                                lambda b, sp: (0, sp, 0, 0))])
    return pl.pallas_call(
        _attn_kernel,
        out_shape=jax.ShapeDtypeStruct((B, S, ATT_OUT_WIDTH), F32),
        grid=(B, n_pairs),
        in_specs=in_specs,
        out_specs=pl.BlockSpec((1, S, LANES), lambda b, sp: (b, 0, sp)),
        scratch_shapes=[pltpu.VMEM((N_ATT_GROUPS, S, LANES), F32) for _ in range(5)],
        compiler_params=pltpu.CompilerParams(
            dimension_semantics=("arbitrary", "arbitrary"), vmem_limit_bytes=VMEM_LIMIT),
        name="dilated_attention",
    )(*([p3] * 9), q_gain, k_gain, bias)


def _t5_causal_bucket(dist):
    max_exact = REL_BUCKETS // 2
    d = jnp.maximum(dist.astype(F32), 1.0)
    large = max_exact + (jnp.log(d / max_exact) / math.log(REL_MAX_DIST / max_exact)
                         * (REL_BUCKETS - max_exact)).astype(jnp.int32)
    large = jnp.minimum(large, REL_BUCKETS - 1)
    return jnp.where(dist < max_exact, dist, large)


def _attention_bias(rel_bias_table):
    W = ATT_BLOCK
    qi = jnp.arange(W)[:, None]
    kj = jnp.arange(2 * W)[None, :]
    rel = qi + W - kj
    valid = (rel >= 0) & (rel <= W)
    per_group = []
    for g, (_, d) in enumerate(ATT_PATTERNS):
        tab = rel_bias_table[:, g * HEADS_PER_GROUP:(g + 1) * HEADS_PER_GROUP]
        bucket = _t5_causal_bucket(jnp.clip(rel, 0, W) * d)
        onehot = (bucket[:, :, None] == jnp.arange(REL_BUCKETS)[None, None, :]).astype(F32)
        bias = jnp.einsum('qkb,bh->qkh', onehot, tab.astype(F32),
                          precision=lax.Precision.HIGHEST)
        bias = jnp.where(valid[:, :, None], bias, NEG_INF)
        per_group.append(jnp.transpose(bias, (2, 0, 1)).reshape(2, 2 * W, 2 * W))
    return jnp.stack(per_group, axis=0)


def _rwkv_prep_kernel(p_ref, mu_ref, w0_ref, wup_ref, a0_ref, aup_ref, gup_ref, kk_ref, ka_ref, rk_ref,
                      at_o, rt_o, bt_o, kt_o, bw_o, kw_o, v_o, bonus_o, g_o, wend_o, carry):
    tq = p_ref.shape[1]
    C = RWKV_WIDTH
    L = WKV_CHUNK

    @pl.when(pl.program_id(1) == 0)
    def _():
        carry[...] = jnp.zeros_like(carry)

    p = p_ref[0].astype(F32)
    row = lax.broadcasted_iota(jnp.int32, (tq, 1), 0)
    prev = jnp.where(row == 0, carry[...], pltpu.roll(p, 1, axis=0))
    carry[...] = p[tq - 1:tq, :]
    pm = p + (prev - p) * mu_ref[...]
    r, k, v = pm[:, 0:C], pm[:, C:2 * C], pm[:, 2 * C:3 * C]
    xw = pm[:, 3 * C:3 * C + LANES]
    xa = pm[:, 3 * C + LANES:3 * C + 2 * LANES]
    xg = pm[:, 3 * C + 2 * LANES:]

    z = w0_ref[...] + _dot(jnp.tanh(xw), wup_ref[...])
    softplus_neg = jnp.maximum(-z, 0.0) + jnp.log(1.0 + jnp.exp(-jnp.abs(z)))
    lw = -jnp.exp(-softplus_neg - 0.5)
    a = _sigmoid(a0_ref[...] + _dot(xa, aup_ref[...]))
    g = _dot(_sigmoid(xg), gup_ref[...])

    hr = jnp.right_shift(lax.broadcasted_iota(jnp.int32, (C, C), 0), 6)
    hc = jnp.right_shift(lax.broadcasted_iota(jnp.int32, (C, C), 1), 6)
    ones_bd = jnp.where(hr == hc, 1.0, 0.0).astype(BF16)

    def head_sums(x):
        return jnp.dot(x.astype(BF16), ones_bd, preferred_element_type=F32)

    k2 = k * (1.0 + (a - 1.0) * ka_ref[...])
    kk = k * kk_ref[...]
    kk = kk / jnp.maximum(jnp.sqrt(head_sums(kk * kk)), 1e-12)
    bonus = head_sums(r * k2 * rk_ref[...]) * v
    b = kk * a

    half = 256
    ri = lax.broadcasted_iota(jnp.int32, (half, half), 0)
    ci = lax.broadcasted_iota(jnp.int32, (half, half), 1)
    tri = jnp.where((jnp.right_shift(ri, 6) == jnp.right_shift(ci, 6)) & (ri >= ci), 1.0, 0.0).astype(BF16)
    cums = []
    for s in range(tq // half):
        hi, mid, lo3 = _split3(lw[s * half:(s + 1) * half, :])
        cums.append(jnp.dot(tri, hi, preferred_element_type=F32)
                    + (jnp.dot(tri, mid, preferred_element_type=F32)
                       + jnp.dot(tri, lo3, preferred_element_type=F32)))
    cum = jnp.concatenate(cums, axis=0)
    n_chunks = tq // L
    ends = [cum[c * L + L - 1:c * L + L, :] for c in range(n_chunks)]
    tot = jnp.concatenate([jnp.broadcast_to(e, (L, C)) for e in ends], axis=0)
    wend = jnp.exp(jnp.concatenate(ends, axis=0))

    e_pos = jnp.exp(cum)
    e_neg = jnp.exp(-cum)
    e_rem = jnp.exp(tot - cum)
    at = -kk * jnp.exp(cum - lw)
    rt = r * e_pos
    bt = b * e_neg
    kt = k2 * e_neg
    bw = b * e_rem
    kw = k2 * e_rem
    at_o[0] = at.astype(BF16)
    rt_o[0] = rt.astype(BF16)
    bt_o[0] = bt.astype(BF16)
    kt_o[0] = kt.astype(BF16)
    bw_o[0] = bw.astype(BF16)
    kw_o[0] = kw.astype(BF16)
    v_o[0] = v.astype(BF16)
    bonus_o[0] = bonus
    g_o[0] = g
    wend_o[0] = wend


def _rwkv_prep(p3, mu, w0, wup, a0, aup, gup, k_k, k_a, r_k):
    B, S, _ = p3.shape
    tq = 1024
    C = RWKV_WIDTH
    tm_shape = lambda dt: jax.ShapeDtypeStruct((B, S, C), dt)
    tm_spec = pl.BlockSpec((1, tq, C), lambda b, j: (b, j, 0))
    full = lambda shape: pl.BlockSpec(shape, lambda b, j: (0,) * len(shape))
    return pl.pallas_call(
        _rwkv_prep_kernel,
        out_shape=[tm_shape(BF16)] * 7 + [tm_shape(F32)] * 2 + [jax.ShapeDtypeStruct((B, S // WKV_CHUNK, C), F32)],
        grid=(B, S // tq),
        in_specs=[pl.BlockSpec((1, tq, RWKV_PAD_WIDTH), lambda b, j: (b, j, 0)),
                  full((1, RWKV_PAD_WIDTH)), full((1, C)), full((LANES, C)), full((1, C)),
                  full((LANES, C)), full((2 * LANES, C)), full((1, C)), full((1, C)), full((1, C))],
        out_specs=[tm_spec] * 9 + [pl.BlockSpec((1, tq // WKV_CHUNK, C), lambda b, j: (b, j, 0))],
        scratch_shapes=[pltpu.VMEM((1, RWKV_PAD_WIDTH), F32)],
        compiler_params=pltpu.CompilerParams(
            dimension_semantics=("arbitrary", "arbitrary"), vmem_limit_bytes=VMEM_LIMIT),
        name="rwkv_prep",
    )(p3, mu, w0, wup, a0, aup, gup, k_k, k_a, r_k)


WKV_GROUP = 4
WKV_GROUP_WIDTH = WKV_GROUP * RWKV_HEAD_DIM


def _wkv_kernel(at_ref, rt_ref, bt_ref, kt_ref, bw_ref, kw_ref, v_ref, bonus_ref, g_ref, wend_ref,
                lnw_ref, lnb_ref, o_ref, state, m_s, u0_s, o0_s, arb_s, p_s, c_s, sh_s):
    N, C, GW = RWKV_HEAD_DIM, WKV_CHUNK, WKV_GROUP_WIDTH
    n_groups = RWKV_WIDTH // GW
    n_sub = at_ref.shape[1] // C
    j = pl.program_id(1)

    @pl.when(j == 0)
    def _():
        state[...] = jnp.zeros_like(state)

    def iota(shape, dim):
        return lax.broadcasted_iota(jnp.int32, shape, dim)

    same_head = jnp.right_shift(iota((1, GW, GW), 1), 6) == jnp.right_shift(iota((1, GW, GW), 2), 6)
    ones_bd = jnp.where(same_head, 1.0, 0.0).astype(BF16)
    g_row = iota((1, 2 * C, 2 * GW), 1)
    g_col = jnp.bitwise_and(iota((1, 2 * C, 2 * GW), 2), N - 1)
    band = ((g_row < C) & (g_row > g_col)) | ((g_row >= C) & ((g_row - C) >= g_col))
    eye = jnp.where(iota((1, C, GW), 1) == jnp.bitwise_and(iota((1, C, GW), 2), N - 1), 1.0, 0.0)

    def bmm(a, b, dims):
        return lax.dot_general(a.astype(BF16), b.astype(BF16), (dims, ((0,), (0,))),
                               preferred_element_type=F32)

    nn, nt, tn = ((2,), (1,)), ((2,), (2,)), ((1,), (1,))

    def bd(x):
        xb = x.astype(BF16)
        return jnp.where(same_head, jnp.concatenate([xb] * WKV_GROUP, axis=1), jnp.zeros((), BF16))

    def narrow(x):
        return sum(x[:, h * N:(h + 1) * N] for h in range(WKV_GROUP))

    span = 4
    rows = span * C

    def stack(x):
        x3 = x.reshape(span, C, n_groups * GW)
        return jnp.concatenate([x3[:, :, q * GW:(q + 1) * GW] for q in range(n_groups)], axis=0)

    def unstack(y):
        return jnp.concatenate([y[q * span:(q + 1) * span].reshape(rows, GW) for q in range(n_groups)],
                               axis=1)

    def chunk_terms(i):
        rs = slice(i * rows, (i + 1) * rows)
        at = stack(at_ref[0, rs, :])
        ar = jnp.concatenate([at, stack(rt_ref[0, rs, :])], axis=1)
        bk_bd = jnp.concatenate([bd(stack(bt_ref[0, rs, :])), bd(stack(kt_ref[0, rs, :]))], axis=1)
        gm = jnp.where(band, bmm(ar, bk_bd, nt), 0.0)
        yield
        a_ab = gm[:, :C, :GW]
        inv = eye + a_ab
        pw = bmm(a_ab, bd(a_ab), nn)
        yield
        for _ in range(int(math.log2(C)) - 2):
            prod = bmm(jnp.concatenate([inv, pw], axis=1), bd(pw), nn)
            yield
            inv = inv + prod[:, :C]
            pw = prod[:, C:]
        inv = inv + bmm(inv, bd(pw), nn)
        yield
        v = stack(v_ref[0, rs, :])
        kv = bmm(gm[:, :, GW:], bd(v), nn)
        yield
        mu0 = bmm(inv, jnp.concatenate([bd(at), bd(kv[:, :C])], axis=2), nn)
        yield
        m = mu0[:, :, :GW].astype(BF16)
        u0 = mu0[:, :, GW:]
        bw = stack(bw_ref[0, rs, :])
        low_rank = jnp.where(same_head, bmm(m, bw, tn), 0.0)
        const = jnp.where(same_head, bmm(jnp.concatenate([u0.astype(BF16), v], axis=1),
                                         jnp.concatenate([bw, stack(kw_ref[0, rs, :])], axis=1), tn), 0.0)
        m_s[rs, :] = unstack(m)
        u0_s[rs, :] = unstack(u0)
        o0_s[rs, :] = unstack(kv[:, C:])
        arb_s[rs, :] = unstack(gm[:, C:, :GW]).astype(BF16)
        for q in range(n_groups):
            p_s[i * span:(i + 1) * span, q] = low_rank[q * span:(q + 1) * span].astype(BF16)
            c_s[i * span:(i + 1) * span, q] = narrow(const[q * span:(q + 1) * span])

    def groups(x):
        return jnp.stack([x[:, q * GW:(q + 1) * GW] for q in range(n_groups)], axis=0)

    def chunk_state(i):
        s0 = state[...]
        sh_s[i] = s0.astype(BF16)
        w_end = groups(wend_ref[0, pl.ds(j * n_sub + i, 1), :])
        state[...] = s0 * w_end + bmm(s0, p_s[i], nn) + c_s[i]

    n_prob = n_groups * span

    ones_p = jnp.broadcast_to(ones_bd, (n_prob, GW, GW))

    def head_sums(x):
        return bmm(x, ones_p, nn)

    def chunk_outputs(i):
        rs = slice(i * rows, (i + 1) * rows)
        s0 = bd(jnp.concatenate([sh_s[i * span:(i + 1) * span, q] for q in range(n_groups)], axis=0))
        mr_s = bmm(jnp.concatenate([stack(m_s[rs, :]), stack(rt_ref[0, rs, :])], axis=1), s0, nt)
        yield
        u = mr_s[:, :C] + stack(u0_s[rs, :])
        o = mr_s[:, C:] + stack(o0_s[rs, :]) + bmm(stack(arb_s[rs, :]), bd(u), nn)
        yield
        mean = head_sums(o) * (1.0 / N)
        yield
        dlt = o - mean
        on = unstack(dlt * lax.rsqrt(head_sums(dlt * dlt) * (1.0 / N) + GN_EPS))
        on = on * lnw_ref[...] + lnb_ref[...]
        o_ref[0, rs, :] = (on + bonus_ref[0, rs, :]) * g_ref[0, rs, :]

    def run(stages, between=()):
        pending = iter(between)
        for _ in stages:
            step = next(pending, None)
            if step is not None:
                step()
        for step in pending:
            step()

    n_spans = n_sub // span
    assert n_spans >= 2
    states = lambda i: [functools.partial(chunk_state, k) for k in range(i * span, (i + 1) * span)]
    run(chunk_terms(0))
    for i in range(1, n_spans):
        run(chunk_terms(i), states(i - 1))
    run(chunk_outputs(0), states(n_spans - 1))
    for i in range(1, n_spans):
        run(chunk_outputs(i))


def _wkv(at, rt, bt, kt, bw, kw, v, bonus, g, wend, ln_w, ln_b):
    B, S, C = at.shape
    tc = 1024
    tm_spec = pl.BlockSpec((1, tc, C), lambda bb, j: (bb, j, 0))
    par_spec = pl.BlockSpec((1, C), lambda bb, j: (0, 0))
    blocks = (C // WKV_GROUP_WIDTH, WKV_GROUP_WIDTH, WKV_GROUP_WIDTH)
    wide = (C // WKV_GROUP_WIDTH, RWKV_HEAD_DIM, WKV_GROUP_WIDTH)
    return pl.pallas_call(
        _wkv_kernel,
        out_shape=jax.ShapeDtypeStruct((B, S, C), F32),
        grid=(B, S // tc),
        in_specs=([tm_spec] * 9 + [pl.BlockSpec((1, S // WKV_CHUNK, C), lambda bb, j: (bb, 0, 0))]
                  + [par_spec] * 2),
        out_specs=tm_spec,
        scratch_shapes=[pltpu.VMEM(wide, F32),
                        pltpu.VMEM((tc, C), BF16), pltpu.VMEM((tc, C), F32), pltpu.VMEM((tc, C), F32),
                        pltpu.VMEM((tc, C), BF16),
                        pltpu.VMEM((tc // WKV_CHUNK,) + blocks, BF16), pltpu.VMEM((tc // WKV_CHUNK,) + wide, F32),
                        pltpu.VMEM((tc // WKV_CHUNK,) + wide, BF16)],
        compiler_params=pltpu.CompilerParams(
            dimension_semantics=("arbitrary", "arbitrary"), vmem_limit_bytes=VMEM_LIMIT),
        name="wkv7",
    )(at, rt, bt, kt, bw, kw, v, bonus, g, wend, ln_w, ln_b)


MERGE_SPLIT = 2
INFO_FIELDS = 8


def _merge_kernel(att_ref, o_ref, gate_ref, x_ref, wa_ref, wr_ref, wo_ref, g2_ref, wrt_ref, brt_ref,
                  x1_ref, h2_ref, lg_ref, cnt_ref, lt_ref):
    tm = x_ref.shape[1]
    ts = tm // MERGE_SPLIT
    parts = [slice(h * ts, (h + 1) * ts) for h in range(MERGE_SPLIT)]
    lane = lax.broadcasted_iota(jnp.int32, (1, ROUTER_PAD), 1)

    @pl.when((pl.program_id(0) == 0) & (pl.program_id(1) == 0))
    def _():
        cnt_ref[...] = jnp.zeros_like(cnt_ref)

    ys = [(_dot(att_ref[0, rs, :], wa_ref[...]), _dot(o_ref[0, rs, :], wr_ref[...])) for rs in parts]

    mixed = []
    for rs, (y_att, y_rwkv) in zip(parts, ys):
        gates = _sigmoid(gate_ref[0, rs, :].astype(F32))
        mixed.append(gates[:, :D_MODEL] * y_att + gates[:, D_MODEL:] * y_rwkv)

    x1s = []
    for rs, mx in zip(parts, mixed):
        x1 = x_ref[0, rs, :] + _dot(mx, wo_ref[...])
        x1_ref[0, rs, :] = x1
        x1s.append(x1)

    splits = []
    for rs, x1 in zip(parts, x1s):
        ms = jnp.mean(x1 * x1, axis=-1, keepdims=True)
        h2 = x1 * lax.rsqrt(ms + NORM_EPS) * g2_ref[...]
        hh, hl = _split2(h2)
        half = D_MODEL // 2
        rounded = hh.astype(F32)
        w_hi = pltpu.bitcast(rounded[:, :half], jnp.uint32)
        w_lo = pltpu.bitcast(rounded[:, half:], jnp.uint32)
        h2_ref[0, rs, :] = w_hi | jnp.right_shift(w_lo, jnp.uint32(16))
        splits.append((hh, hl))

    logits = []
    for hh, hl in splits:
        both = jnp.dot(hh, wrt_ref[...], preferred_element_type=F32)
        cross = jnp.dot(hl, wrt_ref[:, :ROUTER_PAD], preferred_element_type=F32)
        logits.append(both[:, :ROUTER_PAD] + (both[:, ROUTER_PAD:] + cross) + brt_ref[...])

    ti = lax.broadcasted_iota(jnp.int32, (ts, ts), 0)
    tj = lax.broadcasted_iota(jnp.int32, (ts, ts), 1)
    before = jnp.where(tj < ti, 1.0, 0.0).astype(BF16)
    for rs, lg in zip(parts, logits):
        gmask = lane < N_EXPERT_GROUPS
        gl = jnp.where(gmask, lg, -jnp.inf)
        gmax = jnp.max(gl, axis=-1, keepdims=True)
        grp_idx = jnp.min(jnp.where(gl == gmax, lane, ROUTER_PAD), axis=-1, keepdims=True)
        grp_p = 1.0 / jnp.sum(jnp.where(gmask, jnp.exp(lg - gmax), 0.0), axis=-1, keepdims=True)
        e_lane = lane - N_EXPERT_GROUPS
        emask = ((e_lane >= 0) & (e_lane < N_EXPERTS)
                 & (jnp.right_shift(e_lane, int(math.log2(EXPERTS_PER_GROUP))) == grp_idx))
        el = jnp.where(emask, lg, -jnp.inf)
        m1 = jnp.max(el, axis=-1, keepdims=True)
        i1 = jnp.min(jnp.where(el == m1, lane, ROUTER_PAD), axis=-1, keepdims=True)
        el2 = jnp.where(lane == i1, -jnp.inf, el)
        m2 = jnp.max(el2, axis=-1, keepdims=True)
        i2 = jnp.min(jnp.where(el2 == m2, lane, ROUTER_PAD), axis=-1, keepdims=True)
        e2 = jnp.exp(m2 - m1)
        gate1 = grp_p / (1.0 + e2)
        gate2 = gate1 * e2

        oh = jnp.where((lane == i1) | (lane == i2), 1.0, 0.0)
        pref = jnp.dot(before, oh.astype(BF16), preferred_element_type=F32) + cnt_ref[...]
        rank1 = jnp.sum(jnp.where(lane == i1, pref, 0.0), axis=-1, keepdims=True)
        rank2 = jnp.sum(jnp.where(lane == i2, pref, 0.0), axis=-1, keepdims=True)
        cnt_ref[...] = cnt_ref[...] + jnp.sum(oh, axis=0, keepdims=True)
        info = jnp.where(lane == 0, (i1 - N_EXPERT_GROUPS).astype(F32), 0.0)
        info = jnp.where(lane == 1, (i2 - N_EXPERT_GROUPS).astype(F32), info)
        info = jnp.where(lane == 2, rank1, info)
        info = jnp.where(lane == 3, rank2, info)
        info = jnp.where(lane == 4, gate1, info)
        info = jnp.where(lane == 5, gate2, info)
        lg_ref[0, rs, :] = info
        lt_ref[:, rs] = info.T[:INFO_FIELDS, :]


def _merge(att, o_g, p3, x, w_att_out, w_rwkv_out, w_out, gain2, w_router, b_router):
    B, S, D = x.shape
    tm = 1024
    full = lambda shape: pl.BlockSpec(shape, lambda b, j: (0,) * len(shape))
    return pl.pallas_call(
        _merge_kernel,
        out_shape=[jax.ShapeDtypeStruct((B, S, D), F32), jax.ShapeDtypeStruct((B, S, D // 2), jnp.uint32),
                   jax.ShapeDtypeStruct((B, S, ROUTER_PAD), F32),
                   jax.ShapeDtypeStruct((1, ROUTER_PAD), F32),
                   jax.ShapeDtypeStruct((INFO_FIELDS, B * S), F32)],
        grid=(B, S // tm),
        in_specs=[pl.BlockSpec((1, tm, ATT_OUT_WIDTH), lambda b, j: (b, j, 0)),
                  pl.BlockSpec((1, tm, RWKV_WIDTH), lambda b, j: (b, j, 0)),
                  pl.BlockSpec((1, tm, 2 * D_MODEL), lambda b, j: (b, j, 1)),
                  pl.BlockSpec((1, tm, D), lambda b, j: (b, j, 0)),
                  full((ATT_OUT_WIDTH, D)), full((RWKV_WIDTH, D)), full((D, D)),
                  full((1, D)), full((D, 2 * ROUTER_PAD)), full((1, ROUTER_PAD))],
        out_specs=[pl.BlockSpec((1, tm, D), lambda b, j: (b, j, 0)),
                   pl.BlockSpec((1, tm, D // 2), lambda b, j: (b, j, 0)),
                   pl.BlockSpec((1, tm, ROUTER_PAD), lambda b, j: (b, j, 0)),
                   pl.BlockSpec((1, ROUTER_PAD), lambda b, j: (0, 0)),
                   pl.BlockSpec((INFO_FIELDS, tm), lambda b, j: (0, b * (S // tm) + j))],
        compiler_params=pltpu.CompilerParams(
            dimension_semantics=("arbitrary", "arbitrary"), vmem_limit_bytes=VMEM_LIMIT),
        name="merge_outproj_router",
    )(att, o_g, p3, x, w_att_out, w_rwkv_out, w_out, gain2, w_router, b_router)


def _expert_kernel(be_ref, nb_ref, x_ref, wg_ref, wu_ref, wd_ref, y_ref, wgu_s, wd_s):
    i = pl.program_id(0)

    @pl.when((i == 0) | (be_ref[i] != be_ref[jnp.maximum(i - 1, 0)]))
    def _():
        wgu_s[:, :D_EXPERT] = wg_ref[0].astype(BF16)
        wgu_s[:, D_EXPERT:] = wu_ref[0].astype(BF16)
        wd_s[...] = wd_ref[0].astype(BF16)

    @pl.when(i < nb_ref[0])
    def _():
        words = x_ref[...]
        x_a = pltpu.bitcast(words & jnp.uint32(0xFFFF0000), F32)
        x_b = pltpu.bitcast(jnp.left_shift(words, jnp.uint32(16)), F32)
        xb = jnp.concatenate([x_a, x_b], axis=-1).astype(BF16)
        hw = D_EXPERT // 2
        hgu = [(jnp.dot(xb, wgu_s[:, c * hw:(c + 1) * hw], preferred_element_type=F32),
                jnp.dot(xb, wgu_s[:, D_EXPERT + c * hw:D_EXPERT + (c + 1) * hw], preferred_element_type=F32))
               for c in range(2)]
        y = None
        for c, (hg, hu) in enumerate(hgu):
            hid = (hg * _sigmoid(hg) * hu).astype(BF16)
            part = jnp.dot(hid, wd_s[c * hw:(c + 1) * hw, :], preferred_element_type=F32)
            y = part if y is None else y + part
        y_ref[...] = y.astype(y_ref.dtype)

    @pl.when(i >= nb_ref[0])
    def _():
        y_ref[...] = jnp.zeros_like(y_ref)


def _experts(blk_expert, n_used, xin, w_gate, w_up, w_down):
    n_rows = xin.shape[0]
    D = D_MODEL
    n_blocks = n_rows // MOE_ROWS
    return pl.pallas_call(
        _expert_kernel,
        out_shape=jax.ShapeDtypeStruct((n_rows, D), BF16),
        grid_spec=pltpu.PrefetchScalarGridSpec(
            num_scalar_prefetch=2,
            grid=(n_blocks,),
            scratch_shapes=[pltpu.VMEM((D, 2 * D_EXPERT), BF16), pltpu.VMEM((D_EXPERT, D), BF16)],
            in_specs=[pl.BlockSpec((MOE_ROWS, D // 2), lambda i, be, nb: (i, 0)),
                      pl.BlockSpec((1, D, D_EXPERT), lambda i, be, nb: (be[i], 0, 0)),
                      pl.BlockSpec((1, D, D_EXPERT), lambda i, be, nb: (be[i], 0, 0)),
                      pl.BlockSpec((1, D_EXPERT, D), lambda i, be, nb: (be[i], 0, 0))],
            out_specs=pl.BlockSpec((MOE_ROWS, D), lambda i, be, nb: (i, 0))),
        compiler_params=pltpu.CompilerParams(
            dimension_semantics=("arbitrary",), vmem_limit_bytes=VMEM_LIMIT),
        name="moe_experts",
    )(blk_expert, n_used, xin, w_gate, w_up, w_down)


def _row_tokens(dests, counts, pcounts, pstarts, pends, n_rows, T):
    dest_flat = jnp.concatenate(dests)
    A = dest_flat.shape[0]
    n_pad = n_rows - A
    seg_count = jnp.concatenate([pcounts - counts, n_rows - pends[-1:]])
    seg_first = jnp.concatenate([pstarts + counts, pends[-1:]])
    seg_end = jnp.cumsum(seg_count)
    i = jnp.arange(n_pad, dtype=jnp.int32)
    seg = jnp.sum(i[None, :] >= seg_end[:, None], axis=0)
    in_seg = seg[None, :] == jnp.arange(seg_end.shape[0], dtype=jnp.int32)[:, None]
    shift = (seg_first - (seg_end - seg_count)).astype(jnp.int32)
    pad_rows = i + jnp.sum(jnp.where(in_seg, shift[:, None], 0), axis=0)
    rows = jnp.concatenate([dest_flat, pad_rows])
    toks = jnp.concatenate([jnp.arange(T, dtype=jnp.int32)] * len(dests) + [i % T])
    tok_bits = max(T - 1, 1).bit_length()
    assert n_rows << tok_bits <= 1 << 32
    packed = jnp.sort((rows.astype(jnp.uint32) << tok_bits) | toks.astype(jnp.uint32))
    return (packed & jnp.uint32((1 << tok_bits) - 1)).astype(jnp.int32)


def _row_layout(info, counts, T):
    A = T * TOP_K
    counts = counts.astype(jnp.int32)
    pcounts = (counts + MOE_ROWS - 1) // MOE_ROWS * MOE_ROWS
    pends = jnp.cumsum(pcounts)
    pstarts = pends - pcounts
    experts = jnp.arange(N_EXPERTS, dtype=F32)[:, None]
    seg_start = [jnp.dot(pstarts.astype(F32), (info[k][None, :] == experts).astype(F32),
                         precision=lax.Precision.HIGHEST) for k in range(TOP_K)]
    dest = [(seg_start[k] + info[TOP_K + k]).astype(jnp.int32) for k in range(TOP_K)]
    n_blocks = -(-A // MOE_ROWS) + N_EXPERTS
    n_rows = n_blocks * MOE_ROWS
    row_tok = _row_tokens(dest, counts, pcounts, pstarts, pends, n_rows, T)
    blk_start = jnp.arange(n_blocks, dtype=jnp.int32) * MOE_ROWS
    blk_expert = jnp.minimum(jnp.sum(blk_start[:, None] >= pends[None, :], axis=1), N_EXPERTS - 1)
    n_used = (pends[N_EXPERTS - 1] // MOE_ROWS).astype(jnp.int32)[None]
    return row_tok, blk_expert.astype(jnp.int32), n_used, dest


def _combine_kernel(x1_ref, y0_ref, y1_ref, info_ref, o_ref):
    gate = 2 * TOP_K
    g0 = info_ref[:, gate:gate + 1]
    g1 = info_ref[:, gate + 1:gate + 2]
    o_ref[...] = x1_ref[...] + (g0 * y0_ref[...].astype(F32) + g1 * y1_ref[...].astype(F32))


def _combine(x1, y, info):
    T, D = x1.shape
    tm = 1024
    row_spec = pl.BlockSpec((tm, D), lambda i: (i, 0))
    second = pl.BlockSpec((tm, D), lambda i: (i + T // tm, 0))
    return pl.pallas_call(
        _combine_kernel,
        out_shape=jax.ShapeDtypeStruct((T, D), F32),
        grid=(T // tm,),
        in_specs=[row_spec, row_spec, second, pl.BlockSpec((tm, ROUTER_PAD), lambda i: (i, 0))],
        out_specs=row_spec,
        compiler_params=pltpu.CompilerParams(
            dimension_semantics=("arbitrary",), vmem_limit_bytes=VMEM_LIMIT),
        name="moe_combine",
    )(x1, y, y, info)


def _pad_rows(w, rows):
    return jnp.pad(w, ((0, rows - w.shape[0]), (0, 0)))


def kernel(x, norm1_gain, w_in, q_norm_gain, k_norm_gain, rel_bias_table, w_att_out, rwkv_shift_mu,
           rwkv_w0, rwkv_w_up, rwkv_a0, rwkv_a_up, rwkv_g_up, rwkv_k_k, rwkv_k_a, rwkv_r_k,
           rwkv_ln_w, rwkv_ln_b, w_rwkv_out, w_out, norm2_gain, w_group_router, b_group_router,
           w_expert_router, b_expert_router, w_expert_gate, w_expert_up, w_expert_down):
    B, S, D = x.shape
    T = B * S
    C = RWKV_WIDTH
    bias = _attention_bias(rel_bias_table)
    for l in range(norm1_gain.shape[0]):
        wi = w_in[l]
        w_qkv = wi[:, :3 * ATT_WIDTH]
        w_rw = wi[:, 3 * ATT_WIDTH:3 * ATT_WIDTH + RWKV_SHIFT_WIDTH]
        w_gt = wi[:, 3 * ATT_WIDTH + RWKV_SHIFT_WIDTH:]
        pad_cols = lambda w, n: jnp.pad(w, ((0, 0), (0, n - w.shape[1])))
        o_w, o_a, o_g = 3 * C, 3 * C + DECAY_LORA, 3 * C + DECAY_LORA + ICLR_LORA

        def rw_layout(w):
            return jnp.concatenate([w[:, :o_w], pad_cols(w[:, o_w:o_a], LANES),
                                    pad_cols(w[:, o_a:o_g], LANES), pad_cols(w[:, o_g:], 2 * LANES)], axis=1)

        w_rg = jnp.concatenate([rw_layout(w_rw), w_gt], axis=1).astype(BF16)
        mu = rw_layout(rwkv_shift_mu[l][None, :])

        p_rg, p_qkv = _inproj(x.reshape(T, D), norm1_gain[l][None, :], w_rg, w_qkv.astype(BF16))
        p3 = p_rg.reshape(B, S, RWKV_PAD_WIDTH + 2 * D)

        tile2 = lambda gmat: jnp.tile(gmat, (1, 2))[:, None, :]
        att = _attention(p_qkv.reshape(B, S, 3 * ATT_WIDTH), tile2(q_norm_gain[l]) * (HEAD_DIM ** -0.5),
                         tile2(k_norm_gain[l]),
                         bias)

        row = lambda v: v[None, :]
        prep = _rwkv_prep(
            p3, mu, row(rwkv_w0[l]), _pad_rows(rwkv_w_up[l], LANES).astype(BF16), row(rwkv_a0[l]),
            _pad_rows(rwkv_a_up[l], LANES).astype(BF16), _pad_rows(rwkv_g_up[l], 2 * LANES).astype(BF16),
            row(rwkv_k_k[l]), row(rwkv_k_a[l]), rwkv_r_k[l].reshape(1, C))
        o_g = _wkv(*prep, row(rwkv_ln_w[l]), row(rwkv_ln_b[l]))

        w_router = jnp.concatenate([w_group_router[l], w_expert_router[l]], axis=1)
        w_router = jnp.pad(w_router, ((0, 0), (0, ROUTER_PAD - w_router.shape[1])))
        wr_hi = w_router.astype(BF16)
        wr_lo = (w_router - wr_hi.astype(F32)).astype(BF16)
        b_router = jnp.concatenate([b_group_router[l], b_expert_router[l]])
        b_router = jnp.pad(b_router, (0, ROUTER_PAD - b_router.shape[0]))[None, :]
        x1, h2, info, counts, info_t = _merge(
            att, o_g, p3, x, w_att_out[l].astype(BF16),
            w_rwkv_out[l].astype(BF16), w_out[l].astype(BF16),
            norm2_gain[l][None, :], jnp.concatenate([wr_hi, wr_lo], axis=1), b_router)

        row_tok, blk_expert, n_used, dest = _row_layout(
            info_t, counts[0, N_EXPERT_GROUPS:N_EXPERT_GROUPS + N_EXPERTS], T)
        xin = h2.reshape(T, D // 2)[row_tok]
        yb = _experts(blk_expert, n_used, xin, w_expert_gate[l], w_expert_up[l], w_expert_down[l])
        x = _combine(x1.reshape(T, D), yb[jnp.concatenate(dest)],
                     info.reshape(T, ROUTER_PAD)).reshape(B, S, D)
    return x
```
